```python
import math
import jax, jax.numpy as jnp
from jax import lax
import numpy as np

D_MODEL = 1024
BATCH = 8
SEQ = 2048
DEPTH = 1
DEC_BATCH = 128
DEC_SEQ = 1
PAST_LEN = 8192
PAGE_SIZE = 128

HEAD_DIM = 64
A_HEADS = 8
A_WIDTH = A_HEADS * HEAD_DIM
A_DECAY_RANK = 64
A_ICLR_RANK = 64
A_GATE_RANK = 128
A_GN_EPS = 64e-5
B_HEADS = 8
B_KV_HEADS = 2
B_GROUP = B_HEADS // B_KV_HEADS
B_WIDTH = B_HEADS * HEAD_DIM
KV_WIDTH = B_KV_HEADS * HEAD_DIM
CMP_BLOCK = 32
CMP_HIDDEN = 128
SEL_BLOCK = 64
SEL_RATIO = SEL_BLOCK // CMP_BLOCK
TOP_N = 16
WINDOW = 512
Q_BLOCK = 128
FORCE_SCORE = 1e4
N_BUCKETS = 32
MAX_EXACT = N_BUCKETS // 2
MAX_DISTANCE = 1024
N_EXPERTS = 32
TOP_K = 4
D_FF = 1024
SWIGLU_ALPHA = 1.702
SWIGLU_LIMIT = 7.0
MOE_CHUNK = 128
NORM_EPS = 1e-6
NEG_INF = -1e30
A_COLS = (A_WIDTH, A_WIDTH, A_WIDTH, A_DECAY_RANK, A_ICLR_RANK, A_GATE_RANK)
A_PROJ = 3 * A_WIDTH + A_DECAY_RANK + A_ICLR_RANK + A_GATE_RANK
B_PROJ = B_WIDTH + 6 * KV_WIDTH + 3 * B_HEADS
GATE_PROJ = 2 * D_MODEL
MIX_COLS = (A_PROJ, B_PROJ, GATE_PROJ)
P_TOTAL = A_PROJ + B_PROJ + GATE_PROJ

kernel_name = 'hybrid_rwkv7_nsa_moe_step'


def split_cols(p, sizes):
    return jnp.split(p, [int(i) for i in np.cumsum(sizes)[:-1]], axis=-1)


def rms_norm(x, g):
    xf = x.astype(jnp.float32)
    y = xf * lax.rsqrt(jnp.mean(xf * xf, axis=-1, keepdims=True) + NORM_EPS)
    return (y * g.astype(jnp.float32)).astype(x.dtype)


def rel_bucket(dist):
    n = jnp.maximum(dist, 0)
    log_ratio = jnp.log(jnp.maximum(n, 1).astype(jnp.float32) / MAX_EXACT) / math.log(MAX_DISTANCE / MAX_EXACT)
    large = jnp.minimum(MAX_EXACT + (log_ratio * (N_BUCKETS - MAX_EXACT)).astype(jnp.int32), N_BUCKETS - 1)
    return jnp.where(n < MAX_EXACT, n, large)


def rel_bias_heads(rel_bias, dist):
    b = rel_bias.astype(jnp.float32)[rel_bucket(dist)]
    return jnp.moveaxis(b, -1, 0).reshape(B_KV_HEADS, B_GROUP, *dist.shape)


def wkv7_scan(s0, r, w, k, v, a, b):
    def step(s, z):
        r_t, w_t, k_t, v_t, a_t, b_t = z
        sa = jnp.einsum('bhij,bhj->bhi', s, a_t)
        s = s * w_t[:, :, None, :] + sa[..., None] * b_t[:, :, None, :] + v_t[..., None] * k_t[:, :, None, :]
        return s, jnp.einsum('bhij,bhj->bhi', s, r_t)
    xs = tuple(jnp.moveaxis(z, 1, 0) for z in (r, w, k, v, a, b))
    s_fin, ys = lax.scan(step, s0, xs)
    return s_fin, jnp.moveaxis(ys, 0, 1)


def rwkv_branch(p_a, prev_row, s0, mu_shift, w0, w_decay_up, a0, w_iclr_up, w_gate_up, k_k, k_a, r_k, ln_x_w, ln_x_b):
    b, t, _ = p_a.shape
    f32 = jnp.float32
    prev = jnp.concatenate([prev_row[:, None].astype(p_a.dtype), p_a[:, :-1]], axis=1)
    ps = p_a + mu_shift * (prev - p_a)
    r, k, v, xw, xa, xg = split_cols(ps, A_COLS)
    heads = lambda z: z.reshape(b, t, A_HEADS, HEAD_DIM)
    w_log = -jax.nn.softplus(-(w0.astype(f32) + jnp.tanh(xw.astype(f32)) @ w_decay_up.astype(f32))) - 0.5
    decay = jnp.exp(-jnp.exp(w_log))
    a = jax.nn.sigmoid(a0.astype(f32) + xa.astype(f32) @ w_iclr_up.astype(f32))
    gate = jax.nn.sigmoid(xg) @ w_gate_up
    k = k.astype(f32)
    kk = heads(k * k_k.astype(f32))
    kk = kk / jnp.maximum(jnp.sqrt(jnp.sum(kk * kk, axis=-1, keepdims=True)), 1e-12)
    k = heads(k * (1.0 + (a - 1.0) * k_a.astype(f32)))
    r, v, a = heads(r.astype(f32)), heads(v.astype(f32)), heads(a)
    s_fin, y = wkv7_scan(s0.astype(f32), r, heads(decay), k, v, -kk, kk * a)
    mu = jnp.mean(y, axis=-1, keepdims=True)
    var = jnp.mean(jnp.square(y - mu), axis=-1, keepdims=True)
    y = ((y - mu) * lax.rsqrt(var + A_GN_EPS)).reshape(b, t, A_WIDTH) * ln_x_w.astype(f32) + ln_x_b.astype(f32)
    y = y + (jnp.sum(r * k * r_k.astype(f32), axis=-1, keepdims=True) * v).reshape(b, t, A_WIDTH)
    return y.astype(p_a.dtype) * gate, s_fin.astype(s0.dtype), p_a[:, -1]


def nsa_split(p_b):
    b, t = p_b.shape[:2]
    q, kv, gate = split_cols(p_b, (B_WIDTH, 6 * KV_WIDTH, 3 * B_HEADS))
    return (q.reshape(b, t, B_HEADS, HEAD_DIM),
            kv.reshape(b, t, 3, 2, B_KV_HEADS, HEAD_DIM),
            jax.nn.sigmoid(gate).reshape(b, t, B_HEADS, 3))


def compress_branch(cmp_kv, pe_k, pe_v, w_k1, w_k2, w_v1, w_v2):
    b, L = cmp_kv.shape[:2]
    nc = L // CMP_BLOCK
    blk = cmp_kv[:, :nc * CMP_BLOCK].reshape(b, nc, CMP_BLOCK, 2, B_KV_HEADS, HEAD_DIM)
    def phi(x, pe, w1, w2):
        flat = (x + pe[:, None, :]).transpose(0, 1, 3, 2, 4).reshape(b, nc, B_KV_HEADS, CMP_BLOCK * HEAD_DIM)
        return jax.nn.gelu(flat @ w1) @ w2
    cmp_end = jnp.arange(nc) * CMP_BLOCK + (CMP_BLOCK - 1)
    return phi(blk[:, :, :, 0], pe_k, w_k1, w_k2), phi(blk[:, :, :, 1], pe_v, w_v1, w_v2), cmp_end


def sel_blocks(sel_kv):
    b, L = sel_kv.shape[:2]
    ns = -(-L // SEL_BLOCK)
    x = jnp.pad(sel_kv, ((0, 0), (0, ns * SEL_BLOCK - L), (0, 0), (0, 0), (0, 0)))
    x = x.reshape(b, ns, SEL_BLOCK, 2, B_KV_HEADS, HEAD_DIM).transpose(3, 0, 4, 1, 2, 5)
    return x[0], x[1]


def nsa_attend(q, pos, gates, cmp_k, cmp_v, cmp_end, sel_k, sel_v, win_k, win_v, win_pos, rel_bias):
    b, qb = q.shape[:2]
    ns = sel_k.shape[2]
    scale = HEAD_DIM ** -0.5
    qg = q.reshape(b, qb, B_KV_HEADS, B_GROUP, HEAD_DIM)
    d_c = pos[:, None] - cmp_end[None, :]
    ok_c = d_c >= 0
    s_c = jnp.einsum('bqgrd,bngd->bgrqn', qg, cmp_k).astype(jnp.float32) * scale + rel_bias_heads(rel_bias, d_c)
    p_c = jax.nn.softmax(jnp.where(ok_c, s_c, NEG_INF), axis=-1) * ok_c
    o_c = jnp.einsum('bgrqn,bngd->bqgrd', p_c.astype(q.dtype), cmp_v)
    imp = jnp.sum(p_c, axis=2)
    imp = jnp.pad(imp, ((0, 0), (0, 0), (0, 0), (0, SEL_RATIO * ns - imp.shape[-1])))
    imp = imp.reshape(b, B_KV_HEADS, qb, ns, SEL_RATIO).sum(axis=-1)
    blk = jnp.arange(ns)[None, :]
    cur = (pos // SEL_BLOCK)[:, None]
    forced = (blk == 0) | (blk == cur) | (blk == cur - 1)
    score = jnp.where(blk <= cur, imp + FORCE_SCORE * forced, NEG_INF)
    top_val, top_idx = lax.top_k(score, min(TOP_N, ns))
    top_ok = top_val > NEG_INF / 2
    bi = jnp.arange(b)[:, None, None, None]
    gi = jnp.arange(B_KV_HEADS)[None, :, None, None]
    k_sel = sel_k[bi, gi, top_idx]
    v_sel = sel_v[bi, gi, top_idx]
    d_s = pos[:, None, None] - (top_idx[..., None] * SEL_BLOCK + jnp.arange(SEL_BLOCK))
    ok_s = (top_ok[..., None] & (d_s >= 0))[:, :, None]
    tbl = rel_bias.astype(jnp.float32).reshape(N_BUCKETS, B_KV_HEADS, B_GROUP)
    b_s = jnp.moveaxis(tbl[rel_bucket(d_s), gi[..., None]], -1, 2)
    s_s = jnp.einsum('bqgrd,bgqksd->bgrqks', qg, k_sel).astype(jnp.float32) * scale + b_s
    shp = s_s.shape
    p_s = jax.nn.softmax(jnp.where(ok_s, s_s, NEG_INF).reshape(*shp[:4], -1), axis=-1).reshape(shp)
    o_s = jnp.einsum('bgrqks,bgqksd->bqgrd', p_s.astype(q.dtype), v_sel)
    d_w = pos[:, None] - win_pos[None, :]
    ok_w = (d_w >= 0) & (d_w < WINDOW) & (win_pos[None, :] >= 0)
    s_w = jnp.einsum('bqgrd,bwgd->bgrqw', qg, win_k).astype(jnp.float32) * scale + rel_bias_heads(rel_bias, d_w)
    p_w = jax.nn.softmax(jnp.where(ok_w, s_w, NEG_INF), axis=-1)
    o_w = jnp.einsum('bgrqw,bwgd->bqgrd', p_w.astype(q.dtype), win_v)
    g = gates.reshape(b, qb, B_KV_HEADS, B_GROUP, 3)
    o = g[..., 0:1] * o_c + g[..., 1:2] * o_s + g[..., 2:3] * o_w
    return o.reshape(b, qb, B_WIDTH)


def nsa_prompt(p_b, cmp_w, rel_bias):
    q, kv, gates = nsa_split(p_b)
    b, t = q.shape[:2]
    cmp_k, cmp_v, cmp_end = compress_branch(kv[:, :, 0], *cmp_w)
    sel_k, sel_v = sel_blocks(kv[:, :, 1])
    win_pad = jnp.pad(kv[:, :, 2], ((0, 0), (WINDOW, 0), (0, 0), (0, 0), (0, 0)))
    def one_block(start):
        take = lambda z, n: lax.dynamic_slice_in_dim(z, start, n, axis=1)
        w_kv = take(win_pad, Q_BLOCK + WINDOW)
        return nsa_attend(take(q, Q_BLOCK), start + jnp.arange(Q_BLOCK), take(gates, Q_BLOCK),
                          cmp_k, cmp_v, cmp_end, sel_k, sel_v, w_kv[:, :, 0], w_kv[:, :, 1],
                          start - WINDOW + jnp.arange(Q_BLOCK + WINDOW), rel_bias)
    y = lax.map(one_block, jnp.arange(0, t, Q_BLOCK))
    y = jnp.moveaxis(y, 0, 1).reshape(b, t, B_WIDTH)
    n_win = min(WINDOW, t)
    return y, kv[:, :, 0], kv[:, :, 1], kv[:, t - n_win:, 2]


def nsa_sample(p_b, past_cmp, past_sel, win_buf, cmp_w, rel_bias):
    q, kv, gates = nsa_split(p_b)
    t = q.shape[1]
    past_len = past_cmp.shape[1]
    n_buf = win_buf.shape[1]
    cmp_k, cmp_v, cmp_end = compress_branch(jnp.concatenate([past_cmp, kv[:, :, 0]], axis=1), *cmp_w)
    sel_k, sel_v = sel_blocks(jnp.concatenate([past_sel, kv[:, :, 1]], axis=1))
    w_ctx = jnp.concatenate([win_buf, kv[:, :, 2]], axis=1)
    y = nsa_attend(q, past_len + jnp.arange(t), gates, cmp_k, cmp_v, cmp_end, sel_k, sel_v,
                   w_ctx[:, :, 0], w_ctx[:, :, 1], past_len - n_buf + jnp.arange(n_buf + t), rel_bias)
    return y, kv[:, :, 0], kv[:, :, 1], w_ctx[:, t:]


def clamped_swiglu(u):
    glu = jnp.minimum(u[..., :D_FF], SWIGLU_LIMIT)
    lin = jnp.clip(u[..., D_FF:], -SWIGLU_LIMIT, SWIGLU_LIMIT)
    return glu * jax.nn.sigmoid(SWIGLU_ALPHA * glu) * (lin + 1.0)


def moe_ffn(h, w_router, b_router, w_up, b_up, w_down, b_down):
    shp = h.shape
    x = h.reshape(-1, D_MODEL)
    n = x.shape[0]
    n_assign = n * TOP_K
    logits = (x @ w_router).astype(jnp.float32) + b_router.astype(jnp.float32)
    top_val, top_e = lax.top_k(logits, TOP_K)
    top_w = jax.nn.softmax(top_val, axis=-1).astype(h.dtype)
    flat_e = top_e.reshape(-1)
    flat_tok = jnp.repeat(jnp.arange(n, dtype=jnp.int32), TOP_K)
    flat_w = top_w.reshape(-1)
    counts = jnp.bincount(flat_e, length=N_EXPERTS)
    padded = (counts + MOE_CHUNK - 1) // MOE_CHUNK * MOE_CHUNK
    pad_end = jnp.cumsum(padded)
    pad_start = pad_end - padded
    grp_start = jnp.cumsum(counts) - counts
    order = jnp.argsort(flat_e, stable=True)
    se = flat_e[order]
    dest = pad_start[se] + jnp.arange(n_assign) - grp_start[se]
    n_chunks = -(-(n_assign + N_EXPERTS * (MOE_CHUNK - 1)) // MOE_CHUNK)
    slots = n_chunks * MOE_CHUNK
    slot_tok = jnp.full((slots,), n, jnp.int32).at[dest].set(flat_tok[order])
    slot_w = jnp.zeros((slots,), h.dtype).at[dest].set(flat_w[order])
    chunk_e = jnp.minimum(jnp.searchsorted(pad_end, jnp.arange(n_chunks) * MOE_CHUNK, side='right'), N_EXPERTS - 1)
    x_pad = jnp.concatenate([x, jnp.zeros((1, D_MODEL), x.dtype)], axis=0)
    xs = x_pad[slot_tok].reshape(n_chunks, MOE_CHUNK, D_MODEL)
    def expert_chunk(args):
        xc, e = args
        return clamped_swiglu(xc @ w_up[e] + b_up[e]) @ w_down[e] + b_down[e]
    ys = lax.map(expert_chunk, (xs, chunk_e)).reshape(slots, D_MODEL)
    out = jax.ops.segment_sum(ys * slot_w[:, None], slot_tok, num_segments=n + 1)[:n]
    return out.reshape(shp)


def merge_and_ffn(x, y_a, y_b, p_g, w_br_a, w_br_b, w_out, norm_ffn, w_router, b_router, w_up, b_up, w_down, b_down):
    g_a, g_b = jnp.split(jax.nn.sigmoid(p_g), 2, axis=-1)
    x = x + (g_a * (y_a @ w_br_a) + g_b * (y_b @ w_br_b)) @ w_out
    return x + moe_ffn(rms_norm(x, norm_ffn), w_router, b_router, w_up, b_up, w_down, b_down)


def setup_inputs(seed: int = 0) -> dict:
    key = jax.random.key(seed)
    ks = iter(list(jax.random.split(key, 48)))
    f32 = jnp.float32
    def nrm(shape, scale):
        return jax.random.normal(next(ks), shape, f32) * scale
    def gain(shape):
        return 1.0 + nrm(shape, 0.05)
    def unif(shape, lo, hi):
        return jax.random.uniform(next(ks), shape, f32, lo, hi)
    n_pages = PAST_LEN // PAGE_SIZE
    n_used = DEC_BATCH * n_pages
    n_pool = n_used + n_used // 4
    row = (2, B_KV_HEADS, HEAD_DIM)
    L = DEPTH
    return {
        'x_prompt': nrm((BATCH, SEQ, D_MODEL), 1.0),
        'x_sample': nrm((DEC_BATCH, DEC_SEQ, D_MODEL), 1.0),
        'cache_cmp_kv': nrm((L, n_pool, PAGE_SIZE) + row, 1.0),
        'cache_sel_kv': nrm((L, n_pool, PAGE_SIZE) + row, 1.0),
        'state_win_kv': nrm((L, DEC_BATCH, min(WINDOW, PAST_LEN)) + row, 1.0),
        'state_rwkv': nrm((L, DEC_BATCH, A_HEADS, HEAD_DIM, HEAD_DIM), 0.3),
        'state_rwkv_shift': nrm((L, DEC_BATCH, A_PROJ), 1.0),
        'page_table': jax.random.permutation(next(ks), n_pool)[:n_used].reshape(DEC_BATCH, n_pages).astype(jnp.int32),
        'norm_attn': gain((L, D_MODEL)),
        'w_in': nrm((L, D_MODEL, P_TOTAL), D_MODEL ** -0.5),
        'mu_shift': unif((L, A_PROJ), 0.0, 1.0),
        'w0': unif((L, A_WIDTH), -6.0, -1.0),
        'w_decay_up': nrm((L, A_DECAY_RANK, A_WIDTH), 0.5 * A_DECAY_RANK ** -0.5),
        'a0': nrm((L, A_WIDTH), 0.1),
        'w_iclr_up': nrm((L, A_ICLR_RANK, A_WIDTH), A_ICLR_RANK ** -0.5),
        'w_gate_up': nrm((L, A_GATE_RANK, A_WIDTH), A_GATE_RANK ** -0.5),
        'k_k': 1.0 + nrm((L, A_WIDTH), 0.1),
        'k_a': 1.0 + nrm((L, A_WIDTH), 0.1),
        'r_k': nrm((L, A_HEADS, HEAD_DIM), 0.1),
        'ln_x_w': gain((L, A_WIDTH)),
        'ln_x_b': nrm((L, A_WIDTH), 0.02),
        'pe_cmp_k': nrm((L, CMP_BLOCK, HEAD_DIM), 0.1),
        'pe_cmp_v': nrm((L, CMP_BLOCK, HEAD_DIM), 0.1),
        'w_cmp_k1': nrm((L, CMP_BLOCK * HEAD_DIM, CMP_HIDDEN), (CMP_BLOCK * HEAD_DIM) ** -0.5),
        'w_cmp_k2': nrm((L, CMP_HIDDEN, HEAD_DIM), CMP_HIDDEN ** -0.5),
        'w_cmp_v1': nrm((L, CMP_BLOCK * HEAD_DIM, CMP_HIDDEN), (CMP_BLOCK * HEAD_DIM) ** -0.5),
        'w_cmp_v2': nrm((L, CMP_HIDDEN, HEAD_DIM), CMP_HIDDEN ** -0.5),
        'rel_bias': nrm((N_BUCKETS, B_HEADS), 0.5),
        'w_br_a': nrm((L, A_WIDTH, D_MODEL), A_WIDTH ** -0.5),
        'w_br_b': nrm((L, B_WIDTH, D_MODEL), B_WIDTH ** -0.5),
        'w_out': nrm((L, D_MODEL, D_MODEL), D_MODEL ** -0.5),
        'norm_ffn': gain((L, D_MODEL)),
        'w_router': nrm((L, D_MODEL, N_EXPERTS), D_MODEL ** -0.5),
        'b_router': nrm((L, N_EXPERTS), 0.01),
        'w_up': nrm((L, N_EXPERTS, D_MODEL, 2 * D_FF), D_MODEL ** -0.5),
        'b_up': nrm((L, N_EXPERTS, 2 * D_FF), 0.01),
        'w_down': nrm((L, N_EXPERTS, D_FF, D_MODEL), D_FF ** -0.5),
        'b_down': nrm((L, N_EXPERTS, D_MODEL), 0.01),
        'norm_final': gain((D_MODEL,)),
    }


def reference(x_prompt, x_sample, cache_cmp_kv, cache_sel_kv, state_win_kv, state_rwkv, state_rwkv_shift, page_table,
              norm_attn, w_in, mu_shift, w0, w_decay_up, a0, w_iclr_up, w_gate_up, k_k, k_a, r_k, ln_x_w, ln_x_b,
              pe_cmp_k, pe_cmp_v, w_cmp_k1, w_cmp_k2, w_cmp_v1, w_cmp_v2, rel_bias, w_br_a, w_br_b, w_out,
              norm_ffn, w_router, b_router, w_up, b_up, w_down, b_down, norm_final):
    xp, xs = x_prompt, x_sample
    n_p, n_s = x_prompt.shape[0], x_sample.shape[0]
    new = {name: [] for name in ('cmp_p', 'sel_p', 'win_p', 'wkv_p', 'shift_p', 'cmp_s', 'sel_s', 'win_s', 'wkv_s', 'shift_s')}
    for layer in range(DEPTH):
        rw = (mu_shift[layer], w0[layer], w_decay_up[layer], a0[layer], w_iclr_up[layer], w_gate_up[layer],
              k_k[layer], k_a[layer], r_k[layer], ln_x_w[layer], ln_x_b[layer])
        cw = (pe_cmp_k[layer], pe_cmp_v[layer], w_cmp_k1[layer], w_cmp_k2[layer], w_cmp_v1[layer], w_cmp_v2[layer])
        mw = (w_br_a[layer], w_br_b[layer], w_out[layer], norm_ffn[layer], w_router[layer], b_router[layer],
              w_up[layer], b_up[layer], w_down[layer], b_down[layer])
        p_a, p_b, p_g = split_cols(rms_norm(xp, norm_attn[layer]) @ w_in[layer], MIX_COLS)
        y_a, wkv_p, shift_p = rwkv_branch(p_a, jnp.zeros_like(p_a[:, 0]),
                                          jnp.zeros((n_p, A_HEADS, HEAD_DIM, HEAD_DIM), jnp.float32), *rw)
        y_b, cmp_p, sel_p, win_p = nsa_prompt(p_b, cw, rel_bias)
        xp = merge_and_ffn(xp, y_a, y_b, p_g, *mw)
        s_a, s_b, s_g = split_cols(rms_norm(xs, norm_attn[layer]) @ w_in[layer], MIX_COLS)
        y_a, wkv_s, shift_s = rwkv_branch(s_a, state_rwkv_shift[layer], state_rwkv[layer], *rw)
        past_cmp = cache_cmp_kv[layer][page_table].reshape(n_s, -1, 2, B_KV_HEADS, HEAD_DIM)
        past_sel = cache_sel_kv[layer][page_table].reshape(n_s, -1, 2, B_KV_HEADS, HEAD_DIM)
        y_b, cmp_s, sel_s, win_s = nsa_sample(s_b, past_cmp, past_sel, state_win_kv[layer], cw, rel_bias)
        xs = merge_and_ffn(xs, y_a, y_b, s_g, *mw)
        for name, val in (('cmp_p', cmp_p), ('sel_p', sel_p), ('win_p', win_p), ('wkv_p', wkv_p), ('shift_p', shift_p),
                          ('cmp_s', cmp_s), ('sel_s', sel_s), ('win_s', win_s), ('wkv_s', wkv_s), ('shift_s', shift_s)):
            new[name].append(val)
    y_prompt = rms_norm(xp, norm_final)
    y_sample = rms_norm(xs, norm_final)
    return (y_prompt, y_sample,
            jnp.stack(new['cmp_p']), jnp.stack(new['sel_p']), jnp.stack(new['win_p']),
            jnp.stack(new['wkv_p']), jnp.stack(new['shift_p']),
            jnp.stack(new['cmp_s']), jnp.stack(new['sel_s']), jnp.stack(new['win_s']),
            jnp.stack(new['wkv_s']), jnp.stack(new['shift_s']))
```

```python
import functools
import math

import jax
import jax.numpy as jnp
import numpy as np
from jax import lax
from jax.experimental import pallas as pl
from jax.experimental.pallas import tpu as pltpu

F32 = jnp.float32
BF16 = jnp.bfloat16
I32 = jnp.int32
HIGHEST = lax.Precision.HIGHEST

D_MODEL = 1024
HEAD_DIM = 64
A_HEADS = 8
A_WIDTH = A_HEADS * HEAD_DIM
A_DECAY_RANK = 64
A_ICLR_RANK = 64
A_GATE_RANK = 128
A_GN_EPS = 64e-5
A_PROJ = 3 * A_WIDTH + A_DECAY_RANK + A_ICLR_RANK + A_GATE_RANK
B_HEADS = 8
B_KV_HEADS = 2
B_GROUP = B_HEADS // B_KV_HEADS
B_WIDTH = B_HEADS * HEAD_DIM
KV_WIDTH = B_KV_HEADS * HEAD_DIM
KV_ROW = 2 * KV_WIDTH
CMP_BLOCK = 32
CMP_HIDDEN = 128
SEL_BLOCK = 64
SEL_RATIO = SEL_BLOCK // CMP_BLOCK
TOP_N = 16
WINDOW = 512
Q_BLOCK = 128
FORCE_SCORE = 1e4
N_BUCKETS = 32
MAX_EXACT = N_BUCKETS // 2
MAX_DISTANCE = 1024
N_EXPERTS = 32
TOP_K = 4
D_FF = 1024
SWIGLU_ALPHA = 1.702
SWIGLU_LIMIT = 7.0
NORM_EPS = 1e-6
NEG_INF = -1e30
SCALE = HEAD_DIM ** -0.5
GATE_PAD = 128
GATE_PROJ = 2 * D_MODEL
MOE_ROWS = 256
LANES = 128
VMEM_LIMIT = 56 * 1024 * 1024


def _cparams(sem):
    return pltpu.CompilerParams(dimension_semantics=sem, vmem_limit_bytes=VMEM_LIMIT)


def _full(shape):
    n = len(shape)
    return pl.BlockSpec(shape, lambda *_: (0,) * n)


def _dot(a, b):
    return jnp.dot(a, b, preferred_element_type=F32)


def _dot_nt(a, b):
    return lax.dot_general(a, b, (((1,), (1,)), ((), ())), preferred_element_type=F32)


def _bf(x):
    return x.astype(BF16)


PROJ_SPLITS = (A_PROJ, B_WIDTH, KV_ROW, KV_ROW, KV_ROW, GATE_PAD, GATE_PROJ)


def _proj_kernel(x_ref, g_ref, w_ref, *o_refs):
    x = x_ref[...]
    y = x * lax.rsqrt(jnp.mean(x * x, axis=-1, keepdims=True) + NORM_EPS) * g_ref[...]
    h = _bf(y)
    c = 0
    for o_ref, n in zip(o_refs, PROJ_SPLITS):
        o_ref[...] = _dot(h, w_ref[:, c:c + n])
        c += n


def _project(x, g, w, tm):
    rows = x.shape[0]
    ncol = sum(PROJ_SPLITS)
    return pl.pallas_call(
        _proj_kernel,
        grid=(rows // tm,),
        in_specs=[pl.BlockSpec((tm, D_MODEL), lambda i: (i, 0)), _full((1, D_MODEL)), _full((D_MODEL, ncol))],
        out_specs=[pl.BlockSpec((tm, n), lambda i: (i, 0)) for n in PROJ_SPLITS],
        out_shape=[jax.ShapeDtypeStruct((rows, n), F32) for n in PROJ_SPLITS],
        compiler_params=_cparams(("parallel",)),
        name="norm_proj",
    )(x, g, w)


def _pack_w_in(w_in):
    a, rest = w_in[:, :A_PROJ], w_in[:, A_PROJ:]
    q, kv, gt, pg = (rest[:, :B_WIDTH], rest[:, B_WIDTH:B_WIDTH + 3 * KV_ROW],
                     rest[:, B_WIDTH + 3 * KV_ROW:B_WIDTH + 3 * KV_ROW + 3 * B_HEADS],
                     rest[:, B_WIDTH + 3 * KV_ROW + 3 * B_HEADS:])
    gt = jnp.pad(gt, ((0, 0), (0, GATE_PAD - 3 * B_HEADS)))
    return _bf(jnp.concatenate([a, q, kv, gt, pg], axis=1))


def _softplus(z):
    return jnp.maximum(z, 0.0) + jnp.log1p(jnp.exp(-jnp.abs(z)))


def _rwkv_prep_kernel(p_ref, prev_ref, mu_ref, w0_ref, wdu_ref, a0_ref, wiu_ref, wgu_ref, kk_ref, ka_ref, rk_ref,
                      ones_ref, r_o, w_o, k_o, v_o, kk_o, kb_o, g_o, bo_o):
    p = p_ref[...]
    ps = p + mu_ref[...] * (prev_ref[...] - p)
    r = ps[:, 0:A_WIDTH]
    k = ps[:, A_WIDTH:2 * A_WIDTH]
    v = ps[:, 2 * A_WIDTH:3 * A_WIDTH]
    c = 3 * A_WIDTH
    xw = ps[:, c:c + A_DECAY_RANK]
    xa = ps[:, c + A_DECAY_RANK:c + A_DECAY_RANK + A_ICLR_RANK]
    xg = ps[:, c + A_DECAY_RANK + A_ICLR_RANK:]
    w_log = -_softplus(-(w0_ref[...] + _dot(_bf(jnp.tanh(xw)), wdu_ref[...]))) - 0.5
    decay = jnp.exp(-jnp.exp(w_log))
    a = jax.nn.sigmoid(a0_ref[...] + _dot(_bf(xa), wiu_ref[...]))
    gate = _dot(_bf(jax.nn.sigmoid(xg)), wgu_ref[...])
    ones = ones_ref[...]
    kk = k * kk_ref[...]
    ss = jnp.dot(kk * kk, ones, precision=HIGHEST, preferred_element_type=F32)
    kk = kk / jnp.maximum(jnp.sqrt(ss), 1e-12)
    k2 = k * (1.0 + (a - 1.0) * ka_ref[...])
    rk = jnp.dot(r * k2 * rk_ref[...], ones, precision=HIGHEST, preferred_element_type=F32)
    r_o[...] = r
    w_o[...] = decay
    k_o[...] = k2
    v_o[...] = v
    kk_o[...] = kk
    kb_o[...] = kk * a
    g_o[...] = gate
    bo_o[...] = rk * v


def _rwkv_prep(p, prev, mu, w0, wdu, a0, wiu, wgu, k_k, k_a, r_k, tm):
    rows = p.shape[0]
    head = np.arange(A_WIDTH) // HEAD_DIM
    ones = jnp.asarray(head[:, None] == head[None, :], F32)
    row = lambda z: z.reshape(1, -1).astype(F32)
    spec_in = pl.BlockSpec((tm, A_PROJ), lambda i: (i, 0))
    spec_o = pl.BlockSpec((tm, A_WIDTH), lambda i: (i, 0))
    return pl.pallas_call(
        _rwkv_prep_kernel,
        grid=(rows // tm,),
        in_specs=[spec_in, spec_in, _full((1, A_PROJ)), _full((1, A_WIDTH)), _full((A_DECAY_RANK, A_WIDTH)),
                  _full((1, A_WIDTH)), _full((A_ICLR_RANK, A_WIDTH)), _full((A_GATE_RANK, A_WIDTH)),
                  _full((1, A_WIDTH)), _full((1, A_WIDTH)), _full((1, A_WIDTH)), _full((A_WIDTH, A_WIDTH))],
        out_specs=[spec_o] * 8,
        out_shape=[jax.ShapeDtypeStruct((rows, A_WIDTH), F32)] * 8,
        compiler_params=_cparams(("parallel",)),
        name="rwkv_prep",
    )(p, prev, row(mu), row(w0), _bf(wdu), row(a0), _bf(wiu), _bf(wgu), row(k_k), row(k_a), row(r_k), ones)


N_IO = HEAD_DIM // 8
N_JP = HEAD_DIM // 2
BH_LANES = 64


def _wkv_kernel(r_ref, w_ref, k_ref, v_ref, a_ref, b_ref, s0_ref, y_ref, sfin_ref, s_scr, *, tc):
    t_blk = pl.program_id(1)

    @pl.when(t_blk == 0)
    def _():
        s_scr[...] = s0_ref[0]

    def bc(ref, t, jp):
        return jnp.broadcast_to(ref[0, t, jp:jp + 1, :], (8, LANES))

    def fold(x):
        return x + pltpu.roll(x, BH_LANES, 1)

    def step(t, carry):
        acc = [jnp.zeros((8, LANES), F32) for _ in range(N_IO)]
        for jp in range(N_JP):
            a_ = bc(a_ref, t, jp)
            for io in range(N_IO):
                acc[io] = acc[io] + s_scr[io, jp] * a_
        sa = [-fold(acc[io]) for io in range(N_IO)]
        vv = [v_ref[0, t, io * 8:(io + 1) * 8, :] for io in range(N_IO)]
        yacc = [jnp.zeros((8, LANES), F32) for _ in range(N_IO)]
        for jp in range(N_JP):
            w_ = bc(w_ref, t, jp)
            b_ = bc(b_ref, t, jp)
            k_ = bc(k_ref, t, jp)
            r_ = bc(r_ref, t, jp)
            for io in range(N_IO):
                s = s_scr[io, jp] * w_ + sa[io] * b_ + vv[io] * k_
                s_scr[io, jp] = s
                yacc[io] = yacc[io] + s * r_
        y = [fold(yacc[io]) for io in range(N_IO)]
        tot = y[0]
        for io in range(1, N_IO):
            tot = tot + y[io]
        mu = jnp.sum(tot, axis=0, keepdims=True) * (1.0 / HEAD_DIM)
        d = [y[io] - mu for io in range(N_IO)]
        sq = d[0] * d[0]
        for io in range(1, N_IO):
            sq = sq + d[io] * d[io]
        var = jnp.sum(sq, axis=0, keepdims=True) * (1.0 / HEAD_DIM)
        inv = lax.rsqrt(var + A_GN_EPS)
        for io in range(N_IO):
            y_ref[0, t, io * 8:(io + 1) * 8, :] = d[io] * inv
        return carry

    lax.fori_loop(0, tc, step, 0)

    @pl.when(t_blk == pl.num_programs(1) - 1)
    def _():
        sfin_ref[0] = s_scr[...]


def _wkv_scan(r, w, k, v, a, b, s0, tc):
    nb, t = r.shape[:2]
    kspec = pl.BlockSpec((1, tc, N_JP, LANES), lambda n, i: (n, i, 0, 0))
    vspec = pl.BlockSpec((1, tc, HEAD_DIM, LANES), lambda n, i: (n, i, 0, 0))
    sspec = pl.BlockSpec((1, N_IO, N_JP, 8, LANES), lambda n, i: (n, 0, 0, 0, 0))
    return pl.pallas_call(
        functools.partial(_wkv_kernel, tc=tc),
        grid=(nb, t // tc),
        in_specs=[kspec, kspec, kspec, vspec, kspec, kspec, sspec],
        out_specs=[vspec, sspec],
        out_shape=[jax.ShapeDtypeStruct((nb, t, HEAD_DIM, LANES), F32),
                   jax.ShapeDtypeStruct((nb, N_IO, N_JP, 8, LANES), F32)],
        scratch_shapes=[pltpu.VMEM((N_IO, N_JP, 8, LANES), F32)],
        compiler_params=_cparams(("parallel", "arbitrary")),
        name="wkv_scan",
    )(r, w, k, v, a, b, s0)


def _to_key_tiles(x, b, t):
    nb = b // 8
    x = x.reshape(nb, 8, t, A_HEADS, N_JP, 2).transpose(0, 2, 4, 5, 1, 3)
    return x.reshape(nb, t, N_JP, LANES)


def _to_val_tiles(x, b, t):
    nb = b // 8
    x = x.reshape(nb, 8, t, A_HEADS, HEAD_DIM).transpose(0, 2, 4, 1, 3).reshape(nb, t, HEAD_DIM, BH_LANES)
    return jnp.concatenate([x, x], axis=-1)


def _from_val_tiles(y, b, t):
    nb = b // 8
    y = y[..., :BH_LANES].reshape(nb, t, HEAD_DIM, 8, A_HEADS).transpose(0, 3, 1, 4, 2)
    return y.reshape(b * t, A_WIDTH)


def _state_to_tiles(s, b):
    nb = b // 8
    s = s.reshape(nb, 8, A_HEADS, N_IO, 8, N_JP, 2).transpose(0, 3, 5, 4, 6, 1, 2)
    return s.reshape(nb, N_IO, N_JP, 8, LANES)


def _state_from_tiles(s, b):
    nb = b // 8
    s = s.reshape(nb, N_IO, N_JP, 8, 2, 8, A_HEADS).transpose(0, 5, 6, 1, 3, 2, 4)
    return s.reshape(b, A_HEADS, HEAD_DIM, HEAD_DIM)


def _compress_kernel(xk_ref, xv_ref, pe_ref, w1_ref, w2_ref, o_ref, *, nblk):
    for c, x_ref in enumerate((xk_ref, xv_ref)):
        acc = jnp.zeros((nblk, B_KV_HEADS * CMP_HIDDEN), F32)
        for tau in range(CMP_BLOCK):
            x = x_ref[pl.ds(tau, nblk, stride=CMP_BLOCK), :] + pe_ref[c, tau:tau + 1, :]
            acc = acc + _dot(_bf(x), w1_ref[c, tau])
        o_ref[:, c * KV_WIDTH:(c + 1) * KV_WIDTH] = _dot(_bf(jax.nn.gelu(acc)), w2_ref[c])


def _compress(x, pe_k, pe_v, w_k1, w_k2, w_v1, w_v2, rows):
    n = x.shape[0]
    nblk = rows // CMP_BLOCK
    pe = jnp.stack([jnp.concatenate([pe_k, pe_k], axis=1), jnp.concatenate([pe_v, pe_v], axis=1)]).astype(F32)
    eye = jnp.eye(B_KV_HEADS, dtype=F32)
    w1 = jnp.stack([w_k1, w_v1]).reshape(2, CMP_BLOCK, HEAD_DIM, CMP_HIDDEN)
    w1 = jnp.einsum('ctdh,ge->ctgdeh', w1, eye).reshape(2, CMP_BLOCK, KV_WIDTH, B_KV_HEADS * CMP_HIDDEN)
    w2 = jnp.einsum('chd,ge->cghed', jnp.stack([w_k2, w_v2]), eye).reshape(2, B_KV_HEADS * CMP_HIDDEN, KV_WIDTH)
    return pl.pallas_call(
        functools.partial(_compress_kernel, nblk=nblk),
        grid=(n // rows,),
        in_specs=[pl.BlockSpec((rows, KV_WIDTH), lambda i: (i, 0)), pl.BlockSpec((rows, KV_WIDTH), lambda i: (i, 1)),
                  _full((2, CMP_BLOCK, KV_WIDTH)), _full((2, CMP_BLOCK, KV_WIDTH, B_KV_HEADS * CMP_HIDDEN)),
                  _full((2, B_KV_HEADS * CMP_HIDDEN, KV_WIDTH))],
        out_specs=pl.BlockSpec((nblk, KV_ROW), lambda i: (i, 0)),
        out_shape=jax.ShapeDtypeStruct((n // CMP_BLOCK, KV_ROW), F32),
        compiler_params=_cparams(("parallel",)),
        name="nsa_compress",
    )(x, x, pe, _bf(w1), _bf(w2))


def _rel_bucket(dist):
    n = jnp.maximum(dist, 0)
    log_ratio = jnp.log(jnp.maximum(n, 1).astype(F32) / MAX_EXACT) / math.log(MAX_DISTANCE / MAX_EXACT)
    large = jnp.minimum(MAX_EXACT + (log_ratio * (N_BUCKETS - MAX_EXACT)).astype(I32), N_BUCKETS - 1)
    return jnp.where(n < MAX_EXACT, n, large)


def _bias_of(rel_bias, dist):
    return jnp.moveaxis(rel_bias.astype(F32)[_rel_bucket(dist)], -1, 0)


def _masked_softmax_rows(s, ok):
    s = jnp.where(ok, s, NEG_INF)
    e = jnp.exp(s - jnp.max(s, axis=-1, keepdims=True))
    return e / jnp.sum(e, axis=-1, keepdims=True) * ok.astype(F32)


def _nsa_prompt_kernel(q_ref, gate_ref, ck_ref, sel_ref, win_ref, bct_ref, bc_ref, tz_ref, o_ref, imp_scr,
                       *, nc, ns, top_n, wtiles):
    i = pl.program_id(1)
    qb = Q_BLOCK
    gates = jax.nn.sigmoid(gate_ref[...])
    qi = lax.broadcasted_iota(I32, (qb, qb), 0)
    ki = lax.broadcasted_iota(I32, (qb, qb), 1)
    eye = _bf(qi == ki)
    pos_l = i * qb + lax.broadcasted_iota(I32, (1, qb), 1)
    pos_s = i * qb + lax.broadcasted_iota(I32, (qb, 1), 0)
    okT = pos_l >= lax.broadcasted_iota(I32, (nc, qb), 0) * CMP_BLOCK + (CMP_BLOCK - 1)
    ok = pos_s >= lax.broadcasted_iota(I32, (qb, nc), 1) * CMP_BLOCK + (CMP_BLOCK - 1)
    blk = lax.broadcasted_iota(I32, (ns, qb), 0)
    cur = pos_l // SEL_BLOCK
    forced = (blk == 0) | (blk == cur) | (blk == cur - 1)
    n_e = lax.broadcasted_iota(I32, (ns, qb), 0)
    k_e = lax.broadcasted_iota(I32, (ns, qb), 1) // SEL_BLOCK

    def flash(kv_ref, g, qg, lo, mask_fn):
        def body(j, carry):
            ms, ls, accs = carry
            kt = _bf(kv_ref[pl.ds(j * qb, qb), g * HEAD_DIM:(g + 1) * HEAD_DIM])
            vt = _bf(kv_ref[pl.ds(j * qb, qb), KV_WIDTH + g * HEAD_DIM:KV_WIDTH + (g + 1) * HEAD_DIM])
            mask = mask_fn(j)
            ms2, ls2, accs2 = [], [], []
            for r in range(B_GROUP):
                s = _dot_nt(qg[r], kt) + tz_ref[g * B_GROUP + r, i - j]
                s = jnp.where(mask, s, NEG_INF)
                m_new = jnp.maximum(ms[r], jnp.max(s, axis=-1, keepdims=True))
                alpha = jnp.exp(ms[r] - m_new)
                p = jnp.where(mask, jnp.exp(s - m_new), 0.0)
                ls2.append(alpha * ls[r] + jnp.sum(p, axis=-1, keepdims=True))
                accs2.append(alpha * accs[r] + _dot(_bf(p), vt))
                ms2.append(m_new)
            return tuple(ms2), tuple(ls2), tuple(accs2)

        init = (tuple(jnp.full((qb, 1), NEG_INF, F32) for _ in range(B_GROUP)),
                tuple(jnp.zeros((qb, 1), F32) for _ in range(B_GROUP)),
                tuple(jnp.zeros((qb, HEAD_DIM), F32) for _ in range(B_GROUP)))
        _, ls, accs = lax.fori_loop(lo, i + 1, body, init)
        return [accs[r] / ls[r] for r in range(B_GROUP)]

    outs = []
    for g in range(B_KV_HEADS):
        qg = [_bf(q_ref[:, (g * B_GROUP + r) * HEAD_DIM:(g * B_GROUP + r + 1) * HEAD_DIM] * SCALE)
              for r in range(B_GROUP)]
        kc = _bf(ck_ref[:, g * HEAD_DIM:(g + 1) * HEAD_DIM])
        vc = _bf(ck_ref[:, KV_WIDTH + g * HEAD_DIM:KV_WIDTH + (g + 1) * HEAD_DIM])
        o_c = []
        impT = jnp.zeros((nc, qb), F32)
        for r in range(B_GROUP):
            h = g * B_GROUP + r
            p = _masked_softmax_rows(_dot_nt(qg[r], kc) + bc_ref[h], ok)
            o_c.append(_dot(_bf(p), vc))
            sT = jnp.where(okT, _dot_nt(kc, qg[r]) + bct_ref[h], NEG_INF)
            eT = jnp.exp(sT - jnp.max(sT, axis=0, keepdims=True))
            impT = impT + eT / jnp.sum(eT, axis=0, keepdims=True) * okT.astype(F32)
        imp_scr[...] = impT
        imp2 = imp_scr[pl.ds(0, ns, stride=SEL_RATIO), :] + imp_scr[pl.ds(1, ns, stride=SEL_RATIO), :]
        score = jnp.where(blk <= cur, imp2 + FORCE_SCORE * forced.astype(F32), NEG_INF)
        rank = jnp.zeros((ns, qb), I32)
        for m in range(ns):
            row = score[m:m + 1, :]
            rank = rank + ((row > score) | ((row == score) & (m < blk))).astype(I32)
        selT = _bf((rank < top_n) & (score > NEG_INF / 2))
        sel = _bf(_dot_nt(eye, selT))

        def sel_mask(j):
            e_tile = _bf(n_e == 2 * j + k_e)
            m = _dot(sel, e_tile) > 0.5
            return m & ((j < i) | (qi >= ki))

        def win_mask(j):
            d = (i - j) * qb + qi - ki
            return (d >= 0) & (d < WINDOW)

        o_s = flash(sel_ref, g, qg, 0, sel_mask)
        o_w = flash(win_ref, g, qg, jnp.maximum(i - wtiles, 0), win_mask)
        for r in range(B_GROUP):
            h = g * B_GROUP + r
            outs.append(gates[:, 3 * h:3 * h + 1] * o_c[r] + gates[:, 3 * h + 1:3 * h + 2] * o_s[r]
                        + gates[:, 3 * h + 2:3 * h + 3] * o_w[r])
    o_ref[...] = jnp.concatenate(outs, axis=1)


def _nsa_prompt(q, gate, ckv, kv_sel, kv_win, rel_bias, b, t):
    nq = t // Q_BLOCK
    nc = t // CMP_BLOCK
    ns = -(-t // SEL_BLOCK)
    assert t % Q_BLOCK == 0 and nc == SEL_RATIO * ns and WINDOW % Q_BLOCK == 0
    pos = jnp.arange(t)
    cmp_end = jnp.arange(nc) * CMP_BLOCK + (CMP_BLOCK - 1)
    bc = _bias_of(rel_bias, pos[:, None] - cmp_end[None, :])
    bct = jnp.swapaxes(bc, 1, 2)
    dz = jnp.arange(nq)[:, None, None] * Q_BLOCK + jnp.arange(Q_BLOCK)[None, :, None] - jnp.arange(Q_BLOCK)[None, None, :]
    tz = _bias_of(rel_bias, dz)
    row = lambda w: pl.BlockSpec((Q_BLOCK, w), lambda bi, i: (bi * nq + i, 0))
    return pl.pallas_call(
        functools.partial(_nsa_prompt_kernel, nc=nc, ns=ns, top_n=min(TOP_N, ns), wtiles=WINDOW // Q_BLOCK),
        grid=(b, nq),
        in_specs=[row(B_WIDTH), row(GATE_PAD),
                  pl.BlockSpec((nc, KV_ROW), lambda bi, i: (bi, 0)),
                  pl.BlockSpec((t, KV_ROW), lambda bi, i: (bi, 0)),
                  pl.BlockSpec((t, KV_ROW), lambda bi, i: (bi, 0)),
                  pl.BlockSpec((B_HEADS, nc, Q_BLOCK), lambda bi, i: (0, 0, i)),
                  pl.BlockSpec((B_HEADS, Q_BLOCK, nc), lambda bi, i: (0, i, 0)),
                  _full((B_HEADS, nq, Q_BLOCK, Q_BLOCK))],
        out_specs=row(B_WIDTH),
        out_shape=jax.ShapeDtypeStruct((b * t, B_WIDTH), F32),
        scratch_shapes=[pltpu.VMEM((nc, Q_BLOCK), F32)],
        compiler_params=_cparams(("parallel", "arbitrary")),
        name="nsa_prompt",
    )(q, gate, ckv, kv_sel, kv_win, bct, bc, tz)


def _group_q(q_ref, g):
    rows = [q_ref[0, :, (g * B_GROUP + r) * HEAD_DIM:(g * B_GROUP + r + 1) * HEAD_DIM] for r in range(B_GROUP)]
    return _bf(jnp.concatenate(rows, axis=0) * SCALE)


def _nsa_sample_select_kernel(q_ref, gate_ref, ck_ref, bc_ref, pair_ref, tri_ref, oc_ref, idx_ref,
                              *, nsp, cur, top_n):
    gates = jax.nn.sigmoid(gate_ref[0])
    blk = lax.broadcasted_iota(I32, (1, nsp), 1)
    forced = (blk == 0) | (blk == cur) | (blk == cur - 1)
    mi = lax.broadcasted_iota(I32, (nsp, nsp), 0)
    ni = lax.broadcasted_iota(I32, (nsp, nsp), 1)
    kk = lax.broadcasted_iota(I32, (TOP_N, nsp), 0).astype(F32)
    nf = lax.broadcasted_iota(I32, (TOP_N, nsp), 1).astype(F32)
    outs = []
    for g in range(B_KV_HEADS):
        q4 = _group_q(q_ref, g)
        kc = _bf(ck_ref[:, g * HEAD_DIM:(g + 1) * HEAD_DIM])
        vc = _bf(ck_ref[:, KV_WIDTH + g * HEAD_DIM:KV_WIDTH + (g + 1) * HEAD_DIM])
        s = _dot_nt(q4, kc) + bc_ref[g * B_GROUP:(g + 1) * B_GROUP, :]
        e = jnp.exp(s - jnp.max(s, axis=-1, keepdims=True))
        p = e / jnp.sum(e, axis=-1, keepdims=True)
        o_c = _dot(_bf(p), vc)
        imp = ((p[0:1] + p[1:2]) + p[2:3]) + p[3:4]
        imp2 = jnp.dot(imp, pair_ref[...], precision=HIGHEST, preferred_element_type=F32)
        score = jnp.where(blk <= cur, imp2 + FORCE_SCORE * forced.astype(F32), NEG_INF)
        m1 = jnp.broadcast_to(score, (nsp, nsp))
        m2 = m1.T
        gt = (m2 > m1) | ((m2 == m1) & (mi < ni))
        rank = jnp.sum(gt.astype(F32), axis=0, keepdims=True)
        sel = (rank < top_n) & (score > NEG_INF / 2)
        before = _dot(_bf(sel), tri_ref[...])
        hit = jnp.broadcast_to(sel, (TOP_N, nsp)) & (jnp.broadcast_to(before, (TOP_N, nsp)) == kk)
        idx = jnp.sum(jnp.where(hit, nf, 0.0), axis=1, keepdims=True)
        cnt = jnp.sum(hit.astype(F32), axis=1, keepdims=True)
        idx_ref[0, g] = jnp.where(cnt > 0.5, idx, -1.0).astype(I32)
        for r in range(B_GROUP):
            h = g * B_GROUP + r
            outs.append(gates[:, 3 * h:3 * h + 1] * o_c[r:r + 1])
    oc_ref[0] = jnp.concatenate(outs, axis=1)


def _nsa_sample_select(q3, gate3, ckv, rel_bias, past_len):
    s = q3.shape[0]
    nc = past_len // CMP_BLOCK
    ns = -(-(past_len + 1) // SEL_BLOCK)
    nsp = -(-ns // LANES) * LANES
    cur = past_len // SEL_BLOCK
    bc = _bias_of(rel_bias, past_len - (jnp.arange(nc) * CMP_BLOCK + (CMP_BLOCK - 1)))
    pair = jnp.asarray(np.arange(nc)[:, None] // SEL_RATIO == np.arange(nsp)[None, :], F32)
    tri = jnp.asarray(np.arange(nsp)[:, None] < np.arange(nsp)[None, :], BF16)
    return pl.pallas_call(
        functools.partial(_nsa_sample_select_kernel, nsp=nsp, cur=cur, top_n=min(TOP_N, ns)),
        grid=(s,),
        in_specs=[pl.BlockSpec((1, 1, B_WIDTH), lambda i: (i, 0, 0)), pl.BlockSpec((1, 1, GATE_PAD), lambda i: (i, 0, 0)),
                  pl.BlockSpec((nc, KV_ROW), lambda i: (i, 0)), _full((B_HEADS, nc)), _full((nc, nsp)), _full((nsp, nsp))],
        out_specs=[pl.BlockSpec((1, 1, B_WIDTH), lambda i: (i, 0, 0)),
                   pl.BlockSpec((1, B_KV_HEADS, TOP_N, 1), lambda i: (i, 0, 0, 0))],
        out_shape=[jax.ShapeDtypeStruct((s, 1, B_WIDTH), F32), jax.ShapeDtypeStruct((s, B_KV_HEADS, TOP_N, 1), I32)],
        compiler_params=_cparams(("parallel",)),
        name="nsa_sample_select",
    )(q3, gate3, ckv, bc, pair, tri)


def _nsa_sample_attend_kernel(idx_ref, pt_ref, q_ref, gate_ref, oc_ref, ksel_ref, kwin_ref, win_ref, fblk_ref, bw_ref,
                              cache_ref, o_ref, buf, sem, *, cur, n_pages, past_len):
    i = pl.program_id(0)
    n_s = pl.num_programs(0)
    slot = i % 2
    halves = PAGE_BLOCKS
    n_slot = B_KV_HEADS * TOP_N

    def block_of(s_idx, j):
        return idx_ref[s_idx * n_slot + j]

    def copy(s_idx, sl, j):
        n = jnp.clip(block_of(s_idx, j), 0, cur - 1)
        page = pt_ref[s_idx * n_pages + n // halves]
        return pltpu.make_async_copy(cache_ref.at[page * halves + n % halves], buf.at[sl, j], sem.at[sl])

    def cached(s_idx, j):
        n = block_of(s_idx, j)
        return (n >= 0) & (n < cur)

    def fetch(s_idx, sl):
        for j in range(n_slot):
            @pl.when(cached(s_idx, j))
            def _():
                copy(s_idx, sl, j).start()

    @pl.when(i == 0)
    def _():
        fetch(0, 0)

    @pl.when(i + 1 < n_s)
    def _():
        fetch(i + 1, 1 - slot)

    for j in range(n_slot):
        @pl.when(cached(i, j))
        def _():
            copy(i, slot, j).wait()

        @pl.when(jnp.logical_not(cached(i, j)))
        def _():
            buf[slot, j] = jnp.zeros((SEL_BLOCK, KV_ROW), F32)
            buf[slot, j, 0:1, :] = ksel_ref[0]

    gates = jax.nn.sigmoid(gate_ref[0])
    t_l = lax.broadcasted_iota(I32, (1, SEL_BLOCK), 1)
    n_buf = win_ref.shape[1]
    j_w = lax.broadcasted_iota(I32, (1, n_buf), 1)
    d_w = n_buf - j_w
    ok_w = (d_w >= 0) & (d_w < WINDOW) & (past_len - d_w >= 0)
    f0 = fblk_ref[cur]
    outs = []
    for g in range(B_KV_HEADS):
        q4 = _group_q(q_ref, g)
        ksl = slice(g * HEAD_DIM, (g + 1) * HEAD_DIM)
        vsl = slice(KV_WIDTH + g * HEAD_DIM, KV_WIDTH + (g + 1) * HEAD_DIM)
        pieces = []
        for k in range(TOP_N):
            j = g * TOP_N + k
            n = block_of(i, j)
            nb = jnp.clip(n, 0, cur)
            fb = fblk_ref[nb][g * B_GROUP:(g + 1) * B_GROUP, :]
            s_k = _dot_nt(q4, _bf(buf[slot, j, :, ksl])) + fb
            ok = (n >= 0) & (n * SEL_BLOCK + t_l <= past_len)
            pieces.append(jnp.where(ok, s_k, NEG_INF))
        s = jnp.concatenate(pieces, axis=1)
        e = jnp.exp(s - jnp.max(s, axis=-1, keepdims=True))
        p = e / jnp.sum(e, axis=-1, keepdims=True)
        v_all = _bf(buf[slot, g * TOP_N:(g + 1) * TOP_N, :, vsl].reshape(TOP_N * SEL_BLOCK, HEAD_DIM))
        o_s = _dot(_bf(p), v_all)
        kw = _bf(win_ref[0, :, ksl])
        vw = _bf(win_ref[0, :, vsl])
        s_w = jnp.where(ok_w, _dot_nt(q4, kw) + bw_ref[g * B_GROUP:(g + 1) * B_GROUP, :], NEG_INF)
        k_new = _bf(kwin_ref[0, :, ksl])
        v_new = _bf(kwin_ref[0, :, vsl])
        s_n = (jnp.sum(q4.astype(F32) * k_new.astype(F32), axis=-1, keepdims=True)
               + f0[g * B_GROUP:(g + 1) * B_GROUP, 0:1])
        m = jnp.maximum(jnp.max(s_w, axis=-1, keepdims=True), s_n)
        e_w = jnp.exp(s_w - m)
        e_n = jnp.exp(s_n - m)
        den = jnp.sum(e_w, axis=-1, keepdims=True) + e_n
        o_w = _dot(_bf(e_w / den), vw) + _bf(e_n / den).astype(F32) * v_new.astype(F32)
        for r in range(B_GROUP):
            h = g * B_GROUP + r
            outs.append(gates[:, 3 * h + 1:3 * h + 2] * o_s[r:r + 1] + gates[:, 3 * h + 2:3 * h + 3] * o_w[r:r + 1])
    o_ref[0] = oc_ref[0] + jnp.concatenate(outs, axis=1)


PAGE_BLOCKS = 2


def _nsa_sample_attend(idx, page_table, q3, gate3, oc3, ksel3, kwin3, win, cache_sel, rel_bias, past_len):
    s = q3.shape[0]
    n_pages = page_table.shape[1]
    cur = past_len // SEL_BLOCK
    n_buf = win.shape[1]
    n_pool, page = cache_sel.shape[:2]
    assert page == PAGE_BLOCKS * SEL_BLOCK and past_len % page == 0
    cache = cache_sel.reshape(n_pool * PAGE_BLOCKS, SEL_BLOCK, KV_ROW)
    keypos = jnp.arange(cur + 1)[:, None] * SEL_BLOCK + jnp.arange(SEL_BLOCK)[None, :]
    fblk = jnp.moveaxis(_bias_of(rel_bias, past_len - keypos), 0, 1)
    bw = _bias_of(rel_bias, n_buf - jnp.arange(n_buf))
    row3 = lambda w: pl.BlockSpec((1, 1, w), lambda i, *_: (i, 0, 0))
    grid_spec = pltpu.PrefetchScalarGridSpec(
        num_scalar_prefetch=2,
        grid=(s,),
        in_specs=[row3(B_WIDTH), row3(GATE_PAD), row3(B_WIDTH), row3(KV_ROW), row3(KV_ROW),
                  pl.BlockSpec((1, n_buf, KV_ROW), lambda i, *_: (i, 0, 0)),
                  pl.BlockSpec((cur + 1, B_HEADS, SEL_BLOCK), lambda i, *_: (0, 0, 0)),
                  pl.BlockSpec((B_HEADS, n_buf), lambda i, *_: (0, 0)),
                  pl.BlockSpec(memory_space=pl.ANY)],
        out_specs=row3(B_WIDTH),
        scratch_shapes=[pltpu.VMEM((2, B_KV_HEADS * TOP_N, SEL_BLOCK, KV_ROW), F32), pltpu.SemaphoreType.DMA((2,))],
    )
    return pl.pallas_call(
        functools.partial(_nsa_sample_attend_kernel, cur=cur, n_pages=n_pages, past_len=past_len),
        grid_spec=grid_spec,
        out_shape=jax.ShapeDtypeStruct((s, 1, B_WIDTH), F32),
        compiler_params=_cparams(("arbitrary",)),
        name="nsa_sample_attend",
    )(idx.reshape(-1), page_table.reshape(-1), q3, gate3, oc3, ksel3, kwin3, win, fblk, bw, cache)


def _merge_kernel(x_ref, yn_ref, bo_ref, ga_ref, yb_ref, pg_ref, lnw_ref, lnb_ref, wa_ref, wb_ref, wo_ref, nf_ref,
                  wr_ref, br_ref, x1_ref, h_ref, te_ref, tw_ref):
    y_a = (yn_ref[...] * lnw_ref[...] + lnb_ref[...] + bo_ref[...]) * ga_ref[...]
    g_a = jax.nn.sigmoid(pg_ref[:, :D_MODEL])
    g_b = jax.nn.sigmoid(pg_ref[:, D_MODEL:])
    m = g_a * _dot(_bf(y_a), wa_ref[...]) + g_b * _dot(_bf(yb_ref[...]), wb_ref[...])
    x1 = x_ref[...] + _dot(_bf(m), wo_ref[...])
    x1_ref[...] = x1
    h = x1 * lax.rsqrt(jnp.mean(x1 * x1, axis=-1, keepdims=True) + NORM_EPS) * nf_ref[...]
    h_ref[...] = _bf(h)
    logits = jnp.dot(h, wr_ref[...], precision=HIGHEST, preferred_element_type=F32) + br_ref[...]
    lane = lax.broadcasted_iota(I32, logits.shape, 1).astype(F32)
    col = lax.broadcasted_iota(I32, (logits.shape[0], TOP_K), 1)
    vals, idxs = [], []
    for _ in range(TOP_K):
        m_k = jnp.max(logits, axis=-1, keepdims=True)
        i_k = jnp.min(jnp.where(logits == m_k, lane, float(N_EXPERTS)), axis=-1, keepdims=True)
        vals.append(m_k)
        idxs.append(i_k)
        logits = jnp.where(lane == i_k, -jnp.inf, logits)
    e = [jnp.exp(v - vals[0]) for v in vals]
    den = ((e[0] + e[1]) + e[2]) + e[3]
    te = jnp.zeros(col.shape, F32)
    tw = jnp.zeros(col.shape, F32)
    for k in range(TOP_K):
        te = jnp.where(col == k, idxs[k], te)
        tw = jnp.where(col == k, e[k] / den, tw)
    te_ref[...] = te.astype(I32)
    tw_ref[...] = tw


def _merge(x, yn, bonus, gate, yb, pg, ln_w, ln_b, w_a, w_b, w_o, norm_ffn, w_router, b_router, tm):
    rows = x.shape[0]
    row = lambda w: pl.BlockSpec((tm, w), lambda i: (i, 0))
    vec = lambda z: z.reshape(1, -1).astype(F32)
    return pl.pallas_call(
        _merge_kernel,
        grid=(rows // tm,),
        in_specs=[row(D_MODEL), row(A_WIDTH), row(A_WIDTH), row(A_WIDTH), row(B_WIDTH), row(GATE_PROJ),
                  _full((1, A_WIDTH)), _full((1, A_WIDTH)), _full((A_WIDTH, D_MODEL)), _full((B_WIDTH, D_MODEL)),
                  _full((D_MODEL, D_MODEL)), _full((1, D_MODEL)), _full((D_MODEL, N_EXPERTS)), _full((1, N_EXPERTS))],
        out_specs=[row(D_MODEL), row(D_MODEL), row(TOP_K), row(TOP_K)],
        out_shape=[jax.ShapeDtypeStruct((rows, D_MODEL), F32), jax.ShapeDtypeStruct((rows, D_MODEL), BF16),
                   jax.ShapeDtypeStruct((rows, TOP_K), I32), jax.ShapeDtypeStruct((rows, TOP_K), F32)],
        compiler_params=_cparams(("parallel",)),
        name="merge_router",
    )(x, yn, bonus, gate, yb, pg, vec(ln_w), vec(ln_b), _bf(w_a), _bf(w_b), _bf(w_o), vec(norm_ffn),
      w_router.astype(F32), vec(b_router))


def _moe_kernel(ce_ref, nu_ref, x_ref, sw_ref, wu_ref, bu_ref, wd_ref, bd_ref, o_ref, wu_bf, wd_bf):
    c = pl.program_id(0)
    e = ce_ref[c]
    prev = ce_ref[jnp.maximum(c - 1, 0)]

    @pl.when((c == 0) | (e != prev))
    def _():
        wu_bf[...] = _bf(wu_ref[0])
        wd_bf[...] = _bf(wd_ref[0])

    @pl.when(c < nu_ref[0])
    def _():
        u = _dot(x_ref[...], wu_bf[...]) + bu_ref[0]
        glu = jnp.minimum(u[:, :D_FF], SWIGLU_LIMIT)
        lin = jnp.clip(u[:, D_FF:], -SWIGLU_LIMIT, SWIGLU_LIMIT)
        act = glu * jax.nn.sigmoid(SWIGLU_ALPHA * glu) * (lin + 1.0)
        o_ref[...] = (_dot(_bf(act), wd_bf[...]) + bd_ref[0]) * sw_ref[...]

    @pl.when(c >= nu_ref[0])
    def _():
        o_ref[...] = jnp.zeros_like(o_ref)


def _moe_experts(chunk_e, n_used, xs, slot_w, w_up, b_up, w_down, b_down):
    slots = xs.shape[0]
    n_chunks = slots // MOE_ROWS
    grid_spec = pltpu.PrefetchScalarGridSpec(
        num_scalar_prefetch=2,
        grid=(n_chunks,),
        in_specs=[pl.BlockSpec((MOE_ROWS, D_MODEL), lambda c, ce, nu: (c, 0)),
                  pl.BlockSpec((MOE_ROWS, 1), lambda c, ce, nu: (c, 0)),
                  pl.BlockSpec((1, D_MODEL, 2 * D_FF), lambda c, ce, nu: (ce[c], 0, 0)),
                  pl.BlockSpec((1, 1, 2 * D_FF), lambda c, ce, nu: (ce[c], 0, 0)),
                  pl.BlockSpec((1, D_FF, D_MODEL), lambda c, ce, nu: (ce[c], 0, 0)),
                  pl.BlockSpec((1, 1, D_MODEL), lambda c, ce, nu: (ce[c], 0, 0))],
        out_specs=pl.BlockSpec((MOE_ROWS, D_MODEL), lambda c, ce, nu: (c, 0)),
        scratch_shapes=[pltpu.VMEM((D_MODEL, 2 * D_FF), BF16), pltpu.VMEM((D_FF, D_MODEL), BF16)],
    )
    return pl.pallas_call(
        _moe_kernel,
        grid_spec=grid_spec,
        out_shape=jax.ShapeDtypeStruct((slots, D_MODEL), F32),
        compiler_params=_cparams(("arbitrary",)),
        name="moe_experts",
    )(chunk_e, n_used, xs, slot_w, w_up, b_up.reshape(N_EXPERTS, 1, -1), w_down, b_down.reshape(N_EXPERTS, 1, -1))


def _moe(h, top_e, top_w, w_up, b_up, w_down, b_down):
    n = h.shape[0]
    n_assign = n * TOP_K
    flat_e = top_e.reshape(-1)
    counts = jnp.bincount(flat_e, length=N_EXPERTS)
    padded = (counts + MOE_ROWS - 1) // MOE_ROWS * MOE_ROWS
    pad_end = jnp.cumsum(padded)
    pad_start = pad_end - padded
    grp_start = jnp.cumsum(counts) - counts
    order = jnp.argsort(flat_e, stable=True)
    se = flat_e[order]
    dest = (pad_start[se] + jnp.arange(n_assign) - grp_start[se]).astype(I32)
    n_chunks = -(-(n_assign + N_EXPERTS * (MOE_ROWS - 1)) // MOE_ROWS)
    slots = n_chunks * MOE_ROWS
    slot_tok = jnp.full((slots,), n, I32).at[dest].set((order // TOP_K).astype(I32))
    slot_w = jnp.zeros((slots,), F32).at[dest].set(top_w.reshape(-1)[order])
    chunk_e = jnp.minimum(jnp.searchsorted(pad_end, jnp.arange(n_chunks) * MOE_ROWS, side='right'),
                          N_EXPERTS - 1).astype(I32)
    n_used = (pad_end[-1] // MOE_ROWS).astype(I32).reshape(1)
    h_pad = jnp.concatenate([h, jnp.zeros((1, D_MODEL), h.dtype)], axis=0)
    xs = h_pad[slot_tok]
    ys = _moe_experts(chunk_e, n_used, xs, slot_w.reshape(-1, 1), w_up, b_up, w_down, b_down)
    slot_of = jnp.zeros((n_assign,), I32).at[order].set(dest)
    return ys[slot_of.reshape(n, TOP_K)].sum(axis=1)


def _final_kernel(x_ref, m_ref, g_ref, o_ref):
    x = x_ref[...] + m_ref[...]
    o_ref[...] = x * lax.rsqrt(jnp.mean(x * x, axis=-1, keepdims=True) + NORM_EPS) * g_ref[...]


def _final(x1, moe, g, tm):
    rows = x1.shape[0]
    row = pl.BlockSpec((tm, D_MODEL), lambda i: (i, 0))
    return pl.pallas_call(
        _final_kernel,
        grid=(rows // tm,),
        in_specs=[row, row, _full((1, D_MODEL))],
        out_specs=row,
        out_shape=jax.ShapeDtypeStruct((rows, D_MODEL), F32),
        compiler_params=_cparams(("parallel",)),
        name="final_norm",
    )(x1, moe, g.reshape(1, -1).astype(F32))


def _row_tile(rows, cap):
    tm = cap
    while rows % tm:
        tm //= 2
    return tm


def _rwkv_group(p_a, prev, s0_tiles, b, t, rw, tc):
    mu, w0, wdu, a0, wiu, wgu, k_k, k_a, r_k = rw
    tm = _row_tile(b * t, 256)
    r, w, k2, v, kk, kb, gate, bonus = _rwkv_prep(p_a, prev, mu, w0, wdu, a0, wiu, wgu, k_k, k_a, r_k, tm)
    kt = lambda z: _to_key_tiles(z, b, t)
    yn, s_fin = _wkv_scan(kt(r), kt(w), kt(k2), _to_val_tiles(v, b, t), kt(kk), kt(kb), s0_tiles, tc)
    return _from_val_tiles(yn, b, t), gate, bonus, _state_from_tiles(s_fin, b)


def kernel(x_prompt, x_sample, cache_cmp_kv, cache_sel_kv, state_win_kv, state_rwkv, state_rwkv_shift, page_table,
           norm_attn, w_in, mu_shift, w0, w_decay_up, a0, w_iclr_up, w_gate_up, k_k, k_a, r_k, ln_x_w, ln_x_b,
           pe_cmp_k, pe_cmp_v, w_cmp_k1, w_cmp_k2, w_cmp_v1, w_cmp_v2, rel_bias, w_br_a, w_br_b, w_out,
           norm_ffn, w_router, b_router, w_up, b_up, w_down, b_down, norm_final):
    bp, tp, _ = x_prompt.shape
    bs, ts, _ = x_sample.shape
    depth = w_in.shape[0]
    past_len = page_table.shape[1] * cache_cmp_kv.shape[2]
    n_buf = state_win_kv.shape[2]
    assert ts == 1 and bp % 8 == 0 and bs % 8 == 0
    xp = x_prompt.reshape(bp * tp, D_MODEL)
    xs = x_sample.reshape(bs, D_MODEL)
    new = {name: [] for name in ('cmp_p', 'sel_p', 'win_p', 'wkv_p', 'shift_p', 'cmp_s', 'sel_s', 'win_s', 'wkv_s', 'shift_s')}
    kv6 = lambda z, b, t: z.reshape(b, t, 2, B_KV_HEADS, HEAD_DIM)
    for l in range(depth):
        rw = (mu_shift[l], w0[l], w_decay_up[l], a0[l], w_iclr_up[l], w_gate_up[l], k_k[l], k_a[l], r_k[l])
        cw = (pe_cmp_k[l], pe_cmp_v[l], w_cmp_k1[l], w_cmp_k2[l], w_cmp_v1[l], w_cmp_v2[l])
        w_pack = _pack_w_in(w_in[l])
        g_attn = norm_attn[l].reshape(1, -1).astype(F32)

        p_a, q, kv_c, kv_s, kv_w, gt, pg = _project(xp, g_attn, w_pack, _row_tile(bp * tp, 256))
        pa3 = p_a.reshape(bp, tp, A_PROJ)
        prev = jnp.concatenate([jnp.zeros((bp, 1, A_PROJ), F32), pa3[:, :-1]], axis=1).reshape(bp * tp, A_PROJ)
        s0 = jnp.zeros((bp // 8, N_IO, N_JP, 8, LANES), F32)
        yn, gate, bonus, wkv_p = _rwkv_group(p_a, prev, s0, bp, tp, rw, _row_tile(tp, 32))
        ckv = _compress(kv_c, *cw, rows=_row_tile(tp, 2048))
        y_b = _nsa_prompt(q, gt, ckv, kv_s, kv_w, rel_bias, bp, tp)
        x1, h, top_e, top_w = _merge(xp, yn, bonus, gate, y_b, pg, ln_x_w[l], ln_x_b[l], w_br_a[l], w_br_b[l], w_out[l],
                                     norm_ffn[l], w_router[l], b_router[l], _row_tile(bp * tp, 256))
        moe = _moe(h, top_e, top_w, w_up[l], b_up[l], w_down[l], b_down[l])
        xp_next = (x1, moe)
        n_win = min(WINDOW, tp)
        new['cmp_p'].append(kv6(kv_c, bp, tp))
        new['sel_p'].append(kv6(kv_s, bp, tp))
        new['win_p'].append(kv6(kv_w, bp, tp)[:, tp - n_win:])
        new['wkv_p'].append(wkv_p)
        new['shift_p'].append(pa3[:, -1])

        s_a, q, kv_c, kv_s, kv_w, gt, pg = _project(xs, g_attn, w_pack, _row_tile(bs, 256))
        yn, gate, bonus, wkv_s = _rwkv_group(s_a, state_rwkv_shift[l], _state_to_tiles(state_rwkv[l].astype(F32), bs),
                                             bs, 1, rw, 1)
        past_cmp = cache_cmp_kv[l][page_table].reshape(bs * past_len, KV_ROW)
        ckv = _compress(past_cmp, *cw, rows=past_len)
        q3, gt3 = q.reshape(bs, 1, B_WIDTH), gt.reshape(bs, 1, GATE_PAD)
        oc3, idx = _nsa_sample_select(q3, gt3, ckv, rel_bias, past_len)
        win = state_win_kv[l].reshape(bs, n_buf, KV_ROW)
        y_b = _nsa_sample_attend(idx, page_table, q3, gt3, oc3, kv_s.reshape(bs, 1, KV_ROW), kv_w.reshape(bs, 1, KV_ROW),
                                 win, cache_sel_kv[l].reshape(cache_sel_kv.shape[1], cache_sel_kv.shape[2], KV_ROW),
                                 rel_bias, past_len).reshape(bs, B_WIDTH)
        x1, h, top_e, top_w = _merge(xs, yn, bonus, gate, y_b, pg, ln_x_w[l], ln_x_b[l], w_br_a[l], w_br_b[l], w_out[l],
                                     norm_ffn[l], w_router[l], b_router[l], _row_tile(bs, 256))
        moe = _moe(h, top_e, top_w, w_up[l], b_up[l], w_down[l], b_down[l])
        xs_next = (x1, moe)
        new['cmp_s'].append(kv6(kv_c, bs, 1))
        new['sel_s'].append(kv6(kv_s, bs, 1))
        new['win_s'].append(jnp.concatenate([state_win_kv[l], kv6(kv_w, bs, 1)], axis=1)[:, 1:])
        new['wkv_s'].append(wkv_s.astype(state_rwkv.dtype))
        new['shift_s'].append(s_a)

        last = l == depth - 1
        g_fin = norm_final if last else jnp.ones((D_MODEL,), F32)
        if last:
            xp = _final(*xp_next, g_fin, _row_tile(bp * tp, 512))
            xs = _final(*xs_next, g_fin, _row_tile(bs, 512))
        else:
            xp = xp_next[0] + xp_next[1]
            xs = xs_next[0] + xs_next[1]
    return (xp.reshape(bp, tp, D_MODEL), xs.reshape(bs, ts, D_MODEL),
            jnp.stack(new['cmp_p']), jnp.stack(new['sel_p']), jnp.stack(new['win_p']),
            jnp.stack(new['wkv_p']), jnp.stack(new['shift_p']),
            jnp.stack(new['cmp_s']), jnp.stack(new['sel_s']), jnp.stack(new['win_s']),
            jnp.stack(new['wkv_s']), jnp.stack(new['shift_s']))
```

```python
import functools
import math

import jax
import jax.numpy as jnp
import numpy as np
from jax import lax
from jax.experimental import pallas as pl
from jax.experimental.pallas import tpu as pltpu

F32 = jnp.float32
BF16 = jnp.bfloat16
I32 = jnp.int32
HIGHEST = lax.Precision.HIGHEST

D_MODEL = 1024
HEAD_DIM = 64
A_HEADS = 8
A_WIDTH = A_HEADS * HEAD_DIM
A_DECAY_RANK = 64
A_ICLR_RANK = 64
A_GATE_RANK = 128
A_GN_EPS = 64e-5
A_PROJ = 3 * A_WIDTH + A_DECAY_RANK + A_ICLR_RANK + A_GATE_RANK
B_HEADS = 8
B_KV_HEADS = 2
B_GROUP = B_HEADS // B_KV_HEADS
B_WIDTH = B_HEADS * HEAD_DIM
KV_WIDTH = B_KV_HEADS * HEAD_DIM
KV_ROW = 2 * KV_WIDTH
CMP_BLOCK = 32
CMP_HIDDEN = 128
SEL_BLOCK = 64
SEL_RATIO = SEL_BLOCK // CMP_BLOCK
TOP_N = 16
WINDOW = 512
Q_BLOCK = 128
FORCE_SCORE = 1e4
N_BUCKETS = 32
MAX_EXACT = N_BUCKETS // 2
MAX_DISTANCE = 1024
N_EXPERTS = 32
TOP_K = 4
D_FF = 1024
SWIGLU_ALPHA = 1.702
SWIGLU_LIMIT = 7.0
NORM_EPS = 1e-6
NEG_INF = -1e30
SCALE = HEAD_DIM ** -0.5
GATE_PAD = 128
GATE_PROJ = 2 * D_MODEL
MOE_ROWS = 256
LANES = 128
VMEM_LIMIT = 56 * 1024 * 1024


def _cparams(sem):
    return pltpu.CompilerParams(dimension_semantics=sem, vmem_limit_bytes=VMEM_LIMIT)


def _full(shape):
    n = len(shape)
    return pl.BlockSpec(shape, lambda *_: (0,) * n)


def _dot(a, b):
    return jnp.dot(a, b, preferred_element_type=F32)


def _dot_nt(a, b):
    return lax.dot_general(a, b, (((1,), (1,)), ((), ())), preferred_element_type=F32)


def _bf(x):
    return x.astype(BF16)


PROJ_SPLITS = (B_WIDTH, KV_ROW, KV_ROW, KV_ROW, GATE_PAD, GATE_PROJ)
CH_MAJOR = (np.arange(A_HEADS)[None, :] * HEAD_DIM + np.arange(HEAD_DIM)[:, None]).reshape(-1)
A_PERM = np.concatenate([CH_MAJOR, A_WIDTH + CH_MAJOR, 2 * A_WIDTH + CH_MAJOR, np.arange(3 * A_WIDTH, A_PROJ)])


def _proj_kernel(x_ref, g_ref, wa_ref, w_ref, oa_ref, *o_refs, a_transposed):
    x = x_ref[...]
    y = x * lax.rsqrt(jnp.mean(x * x, axis=-1, keepdims=True) + NORM_EPS) * g_ref[...]
    h = _bf(y)
    if a_transposed:
        oa_ref[0] = _dot_nt(wa_ref[...], h)
    else:
        oa_ref[...] = _dot(h, wa_ref[...])
    c = 0
    for o_ref, n in zip(o_refs, PROJ_SPLITS):
        o_ref[...] = _dot(h, w_ref[:, c:c + n])
        c += n


def _project(x, g, wa, w, tm, seq=None):
    rows = x.shape[0]
    ncol = sum(PROJ_SPLITS)
    if seq is None:
        a_spec = pl.BlockSpec((tm, A_PROJ), lambda i: (i, 0))
        a_shape = jax.ShapeDtypeStruct((rows, A_PROJ), F32)
    else:
        per = seq // tm
        a_spec = pl.BlockSpec((1, A_PROJ, tm), lambda i: (i // per, 0, i % per))
        a_shape = jax.ShapeDtypeStruct((rows // seq, A_PROJ, seq), F32)
    return pl.pallas_call(
        functools.partial(_proj_kernel, a_transposed=seq is not None),
        grid=(rows // tm,),
        in_specs=[pl.BlockSpec((tm, D_MODEL), lambda i: (i, 0)), _full((1, D_MODEL)), _full(wa.shape),
                  _full((D_MODEL, ncol))],
        out_specs=[a_spec] + [pl.BlockSpec((tm, n), lambda i: (i, 0)) for n in PROJ_SPLITS],
        out_shape=[a_shape] + [jax.ShapeDtypeStruct((rows, n), F32) for n in PROJ_SPLITS],
        compiler_params=_cparams(("parallel",)),
        name="norm_proj",
    )(x, g, wa, w)


def _pack_w_in(w_in):
    a, rest = w_in[:, :A_PROJ], w_in[:, A_PROJ:]
    q, kv, gt, pg = (rest[:, :B_WIDTH], rest[:, B_WIDTH:B_WIDTH + 3 * KV_ROW],
                     rest[:, B_WIDTH + 3 * KV_ROW:B_WIDTH + 3 * KV_ROW + 3 * B_HEADS],
                     rest[:, B_WIDTH + 3 * KV_ROW + 3 * B_HEADS:])
    gt = jnp.pad(gt, ((0, 0), (0, GATE_PAD - 3 * B_HEADS)))
    return _bf(a), _bf(a[:, A_PERM].T), _bf(jnp.concatenate([q, kv, gt, pg], axis=1))


def _softplus(z):
    return jnp.maximum(z, 0.0) + jnp.log1p(jnp.exp(-jnp.abs(z)))


def _rwkv_prep_kernel(p_ref, prev_ref, mu_ref, w0_ref, wdu_ref, a0_ref, wiu_ref, wgu_ref, kk_ref, ka_ref, rk_ref,
                      ones_ref, r_o, w_o, k_o, v_o, kk_o, kb_o, g_o, bo_o):
    p = p_ref[...]
    ps = p + mu_ref[...] * (prev_ref[...] - p)
    r = ps[:, 0:A_WIDTH]
    k = ps[:, A_WIDTH:2 * A_WIDTH]
    v = ps[:, 2 * A_WIDTH:3 * A_WIDTH]
    c = 3 * A_WIDTH
    xw = ps[:, c:c + A_DECAY_RANK]
    xa = ps[:, c + A_DECAY_RANK:c + A_DECAY_RANK + A_ICLR_RANK]
    xg = ps[:, c + A_DECAY_RANK + A_ICLR_RANK:]
    w_log = -_softplus(-(w0_ref[...] + _dot(_bf(jnp.tanh(xw)), wdu_ref[...]))) - 0.5
    decay = jnp.exp(-jnp.exp(w_log))
    a = jax.nn.sigmoid(a0_ref[...] + _dot(_bf(xa), wiu_ref[...]))
    gate = _dot(_bf(jax.nn.sigmoid(xg)), wgu_ref[...])
    ones = ones_ref[...]
    kk = k * kk_ref[...]
    ss = jnp.dot(kk * kk, ones, precision=HIGHEST, preferred_element_type=F32)
    kk = kk / jnp.maximum(jnp.sqrt(ss), 1e-12)
    k2 = k * (1.0 + (a - 1.0) * ka_ref[...])
    rk = jnp.dot(r * k2 * rk_ref[...], ones, precision=HIGHEST, preferred_element_type=F32)
    r_o[...] = r
    w_o[...] = decay
    k_o[...] = k2
    v_o[...] = v
    kk_o[...] = kk
    kb_o[...] = kk * a
    g_o[...] = gate
    bo_o[...] = rk * v


def _rwkv_prep(p, prev, mu, w0, wdu, a0, wiu, wgu, k_k, k_a, r_k, tm):
    rows = p.shape[0]
    head = np.arange(A_WIDTH) // HEAD_DIM
    ones = jnp.asarray(head[:, None] == head[None, :], F32)
    row = lambda z: z.reshape(1, -1).astype(F32)
    spec_in = pl.BlockSpec((tm, A_PROJ), lambda i: (i, 0))
    spec_o = pl.BlockSpec((tm, A_WIDTH), lambda i: (i, 0))
    return pl.pallas_call(
        _rwkv_prep_kernel,
        grid=(rows // tm,),
        in_specs=[spec_in, spec_in, _full((1, A_PROJ)), _full((1, A_WIDTH)), _full((A_DECAY_RANK, A_WIDTH)),
                  _full((1, A_WIDTH)), _full((A_ICLR_RANK, A_WIDTH)), _full((A_GATE_RANK, A_WIDTH)),
                  _full((1, A_WIDTH)), _full((1, A_WIDTH)), _full((1, A_WIDTH)), _full((A_WIDTH, A_WIDTH))],
        out_specs=[spec_o] * 8,
        out_shape=[jax.ShapeDtypeStruct((rows, A_WIDTH), F32)] * 8,
        compiler_params=_cparams(("parallel",)),
        name="rwkv_prep",
    )(p, prev, row(mu), row(w0), _bf(wdu), row(a0), _bf(wiu), _bf(wgu), row(k_k), row(k_a), row(r_k), ones)


def _rwkv_prep_t_kernel(p_ref, mu_ref, w0_ref, wdu_ref, a0_ref, wiu_ref, wgu_ref, kk_ref, ka_ref, rk_ref,
                        r_o, w_o, k_o, v_o, kk_o, kb_o, g_o, bo_o, carry):
    tt = p_ref.shape[2]

    @pl.when(pl.program_id(1) == 0)
    def _():
        carry[...] = jnp.zeros_like(carry)

    p = p_ref[0]
    lane = lax.broadcasted_iota(I32, (1, tt), 1)
    prev = jnp.where(lane == 0, carry[:, 0:1], pltpu.roll(p, 1, 1))
    carry[:, 0:1] = p[:, tt - 1:tt]
    ps = p + mu_ref[...] * (prev - p)
    r = ps[0:A_WIDTH]
    k = ps[A_WIDTH:2 * A_WIDTH]
    v = ps[2 * A_WIDTH:3 * A_WIDTH]
    c = 3 * A_WIDTH
    xw = ps[c:c + A_DECAY_RANK]
    xa = ps[c + A_DECAY_RANK:c + A_DECAY_RANK + A_ICLR_RANK]
    xg = ps[c + A_DECAY_RANK + A_ICLR_RANK:]

    def head_sum(x):
        s = jnp.sum(x.reshape(HEAD_DIM, A_HEADS, tt), axis=0)
        return jnp.broadcast_to(s[None], (HEAD_DIM, A_HEADS, tt)).reshape(A_WIDTH, tt)

    w_log = -_softplus(-(w0_ref[...] + _dot(wdu_ref[...], _bf(jnp.tanh(xw))))) - 0.5
    decay = jnp.exp(-jnp.exp(w_log))
    a = jax.nn.sigmoid(a0_ref[...] + _dot(wiu_ref[...], _bf(xa)))
    gate = _dot(wgu_ref[...], _bf(jax.nn.sigmoid(xg)))
    kk = k * kk_ref[...]
    kk = kk / jnp.maximum(jnp.sqrt(head_sum(kk * kk)), 1e-12)
    k2 = k * (1.0 + (a - 1.0) * ka_ref[...])
    rk = head_sum(r * k2 * rk_ref[...])
    for o_ref, val in ((r_o, r), (w_o, decay), (k_o, k2), (kk_o, kk), (kb_o, kk * a)):
        o_ref[...] = val.reshape(N_JP, 2, 1, A_HEADS, tt)
    v_o[...] = v.reshape(HEAD_DIM, 1, A_HEADS, tt)
    g_o[0] = gate
    bo_o[0] = rk * v


def _rwkv_prep_t(p_t, mu, w0, wdu, a0, wiu, wgu, k_k, k_a, r_k, tt):
    b, _, t = p_t.shape
    col = lambda z, perm: z.reshape(-1)[perm].reshape(-1, 1).astype(F32)
    up = lambda w: _bf(w[:, CH_MAJOR].T)
    kshape = jax.ShapeDtypeStruct((N_JP, 2, b, A_HEADS, t), F32)
    kspec = pl.BlockSpec((N_JP, 2, 1, A_HEADS, tt), lambda bi, i: (0, 0, bi, 0, i))
    vspec = pl.BlockSpec((HEAD_DIM, 1, A_HEADS, tt), lambda bi, i: (0, bi, 0, i))
    nspec = pl.BlockSpec((1, A_WIDTH, tt), lambda bi, i: (bi, 0, i))
    nshape = jax.ShapeDtypeStruct((b, A_WIDTH, t), F32)
    return pl.pallas_call(
        _rwkv_prep_t_kernel,
        grid=(b, t // tt),
        in_specs=[pl.BlockSpec((1, A_PROJ, tt), lambda bi, i: (bi, 0, i)), _full((A_PROJ, 1)), _full((A_WIDTH, 1)),
                  _full((A_WIDTH, A_DECAY_RANK)), _full((A_WIDTH, 1)), _full((A_WIDTH, A_ICLR_RANK)),
                  _full((A_WIDTH, A_GATE_RANK)), _full((A_WIDTH, 1)), _full((A_WIDTH, 1)), _full((A_WIDTH, 1))],
        out_specs=[kspec, kspec, kspec, vspec, kspec, kspec, nspec, nspec],
        out_shape=[kshape, kshape, kshape, jax.ShapeDtypeStruct((HEAD_DIM, b, A_HEADS, t), F32), kshape, kshape,
                   nshape, nshape],
        scratch_shapes=[pltpu.VMEM((A_PROJ, LANES), F32)],
        compiler_params=_cparams(("parallel", "arbitrary")),
        name="rwkv_prep_t",
    )(p_t, col(mu, A_PERM), col(w0, CH_MAJOR), up(wdu), col(a0, CH_MAJOR), up(wiu), up(wgu),
      col(k_k, CH_MAJOR), col(k_a, CH_MAJOR), col(r_k, CH_MAJOR))


def _key_tiles_kernel(*refs):
    n = len(refs) // 2
    for z_ref, o_ref in zip(refs[:n], refs[n:]):
        for jp in range(N_JP):
            o_ref[0, jp] = z_ref[jp].T


def _key_tiles(zs, tt):
    t = zs[0].shape[2]
    return pl.pallas_call(
        _key_tiles_kernel,
        grid=(t // tt,),
        in_specs=[pl.BlockSpec((N_JP, LANES, tt), lambda i: (0, 0, i))] * len(zs),
        out_specs=[pl.BlockSpec((1, N_JP, tt, LANES), lambda i: (0, 0, i, 0))] * len(zs),
        out_shape=[jax.ShapeDtypeStruct((1, N_JP, t, LANES), F32)] * len(zs),
        compiler_params=_cparams(("parallel",)),
        name="wkv_key_tiles",
    )(*zs)


def _val_tiles_kernel(z_ref, o_ref, *, tt):
    for i in range(HEAD_DIM):
        x = z_ref[i]
        o_ref[pl.ds(i, tt, stride=HEAD_DIM), :] = jnp.concatenate([x, x], axis=0).T


def _val_tiles(z, tt):
    t = z.shape[2]
    return pl.pallas_call(
        functools.partial(_val_tiles_kernel, tt=tt),
        grid=(t // tt,),
        in_specs=[pl.BlockSpec((HEAD_DIM, BH_LANES, tt), lambda i: (0, 0, i))],
        out_specs=pl.BlockSpec((tt * HEAD_DIM, LANES), lambda i: (i, 0)),
        out_shape=jax.ShapeDtypeStruct((t * HEAD_DIM, LANES), F32),
        compiler_params=_cparams(("parallel",)),
        name="wkv_val_tiles",
    )(z)


def _val_untile_kernel(y_ref, o_ref, *, tt):
    for i in range(HEAD_DIM):
        o_ref[i] = y_ref[pl.ds(i, tt, stride=HEAD_DIM), :].T[0:BH_LANES]


def _val_untile(y, tt):
    t = y.shape[0] // HEAD_DIM
    return pl.pallas_call(
        functools.partial(_val_untile_kernel, tt=tt),
        grid=(t // tt,),
        in_specs=[pl.BlockSpec((tt * HEAD_DIM, LANES), lambda i: (i, 0))],
        out_specs=pl.BlockSpec((HEAD_DIM, BH_LANES, tt), lambda i: (0, 0, i)),
        out_shape=jax.ShapeDtypeStruct((HEAD_DIM, BH_LANES, t), F32),
        compiler_params=_cparams(("parallel",)),
        name="wkv_val_untile",
    )(y)


N_IO = HEAD_DIM // 8
N_JP = HEAD_DIM // 2
BH_LANES = 64


def _wkv_kernel(r_ref, w_ref, k_ref, v_ref, a_ref, b_ref, s0_ref, y_ref, sfin_ref, s_scr, *, tc):
    t_blk = pl.program_id(1)

    @pl.when(t_blk == 0)
    def _():
        s_scr[...] = s0_ref[0]

    def bc(ref, t, jp):
        return jnp.broadcast_to(ref[0, jp, pl.ds(t, 1), :], (8, LANES))

    def fold(x):
        return x + pltpu.roll(x, BH_LANES, 1)

    def step(t, carry):
        acc = [jnp.zeros((8, LANES), F32) for _ in range(N_IO)]
        for jp in range(N_JP):
            a_ = bc(a_ref, t, jp)
            for io in range(N_IO):
                acc[io] = acc[io] + s_scr[io, jp] * a_
        sa = [-fold(acc[io]) for io in range(N_IO)]
        vv = [v_ref[0, t, io * 8:(io + 1) * 8, :] for io in range(N_IO)]
        yacc = [jnp.zeros((8, LANES), F32) for _ in range(N_IO)]
        for jp in range(N_JP):
            w_ = bc(w_ref, t, jp)
            b_ = bc(b_ref, t, jp)
            k_ = bc(k_ref, t, jp)
            r_ = bc(r_ref, t, jp)
            for io in range(N_IO):
                s = s_scr[io, jp] * w_ + sa[io] * b_ + vv[io] * k_
                s_scr[io, jp] = s
                yacc[io] = yacc[io] + s * r_
        y = [fold(yacc[io]) for io in range(N_IO)]
        tot = y[0]
        for io in range(1, N_IO):
            tot = tot + y[io]
        mu = jnp.sum(tot, axis=0, keepdims=True) * (1.0 / HEAD_DIM)
        d = [y[io] - mu for io in range(N_IO)]
        sq = d[0] * d[0]
        for io in range(1, N_IO):
            sq = sq + d[io] * d[io]
        var = jnp.sum(sq, axis=0, keepdims=True) * (1.0 / HEAD_DIM)
        inv = lax.rsqrt(var + A_GN_EPS)
        for io in range(N_IO):
            y_ref[0, t, io * 8:(io + 1) * 8, :] = d[io] * inv
        return carry

    lax.fori_loop(0, tc, step, 0)

    @pl.when(t_blk == pl.num_programs(1) - 1)
    def _():
        sfin_ref[0] = s_scr[...]


def _wkv_scan(r, w, k, v, a, b, s0, tc):
    nb, t = v.shape[:2]
    kspec = pl.BlockSpec((1, N_JP, tc, LANES), lambda n, i: (n, 0, i, 0))
    vspec = pl.BlockSpec((1, tc, HEAD_DIM, LANES), lambda n, i: (n, i, 0, 0))
    sspec = pl.BlockSpec((1, N_IO, N_JP, 8, LANES), lambda n, i: (n, 0, 0, 0, 0))
    return pl.pallas_call(
        functools.partial(_wkv_kernel, tc=tc),
        grid=(nb, t // tc),
        in_specs=[kspec, kspec, kspec, vspec, kspec, kspec, sspec],
        out_specs=[vspec, sspec],
        out_shape=[jax.ShapeDtypeStruct((nb, t, HEAD_DIM, LANES), F32),
                   jax.ShapeDtypeStruct((nb, N_IO, N_JP, 8, LANES), F32)],
        scratch_shapes=[pltpu.VMEM((N_IO, N_JP, 8, LANES), F32)],
        compiler_params=_cparams(("parallel", "arbitrary")),
        name="wkv_scan",
    )(r, w, k, v, a, b, s0)


def _to_key_tiles(x, b, t):
    nb = b // 8
    x = x.reshape(nb, 8, t, A_HEADS, N_JP, 2).transpose(0, 4, 2, 5, 1, 3)
    return x.reshape(nb, N_JP, t, LANES)


def _to_val_tiles(x, b, t):
    nb = b // 8
    x = x.reshape(nb, 8, t, A_HEADS, HEAD_DIM).transpose(0, 2, 4, 1, 3).reshape(nb, t, HEAD_DIM, BH_LANES)
    return jnp.concatenate([x, x], axis=-1)


def _from_val_tiles(y, b, t):
    nb = b // 8
    y = y[..., :BH_LANES].reshape(nb, t, HEAD_DIM, 8, A_HEADS).transpose(0, 3, 1, 4, 2)
    return y.reshape(b * t, A_WIDTH)


def _state_to_tiles(s, b):
    nb = b // 8
    s = s.reshape(nb, 8, A_HEADS, N_IO, 8, N_JP, 2).transpose(0, 3, 5, 4, 6, 1, 2)
    return s.reshape(nb, N_IO, N_JP, 8, LANES)


def _state_from_tiles(s, b):
    nb = b // 8
    s = s.reshape(nb, N_IO, N_JP, 8, 2, 8, A_HEADS).transpose(0, 5, 6, 1, 3, 2, 4)
    return s.reshape(b, A_HEADS, HEAD_DIM, HEAD_DIM)


def _compress_kernel(xk_ref, xv_ref, pe_ref, w1_ref, w2_ref, o_ref, *, nblk):
    for c, x_ref in enumerate((xk_ref, xv_ref)):
        acc = jnp.zeros((nblk, B_KV_HEADS * CMP_HIDDEN), F32)
        for tau in range(CMP_BLOCK):
            x = x_ref[pl.ds(tau, nblk, stride=CMP_BLOCK), :] + pe_ref[c, tau:tau + 1, :]
            acc = acc + _dot(_bf(x), w1_ref[c, tau])
        o_ref[:, c * KV_WIDTH:(c + 1) * KV_WIDTH] = _dot(_bf(jax.nn.gelu(acc)), w2_ref[c])


def _compress(x, pe_k, pe_v, w_k1, w_k2, w_v1, w_v2, rows):
    n = x.shape[0]
    nblk = rows // CMP_BLOCK
    pe = jnp.stack([jnp.concatenate([pe_k, pe_k], axis=1), jnp.concatenate([pe_v, pe_v], axis=1)]).astype(F32)
    eye = jnp.eye(B_KV_HEADS, dtype=F32)
    w1 = jnp.stack([w_k1, w_v1]).reshape(2, CMP_BLOCK, HEAD_DIM, CMP_HIDDEN)
    w1 = jnp.einsum('ctdh,ge->ctgdeh', w1, eye).reshape(2, CMP_BLOCK, KV_WIDTH, B_KV_HEADS * CMP_HIDDEN)
    w2 = jnp.einsum('chd,ge->cghed', jnp.stack([w_k2, w_v2]), eye).reshape(2, B_KV_HEADS * CMP_HIDDEN, KV_WIDTH)
    return pl.pallas_call(
        functools.partial(_compress_kernel, nblk=nblk),
        grid=(n // rows,),
        in_specs=[pl.BlockSpec((rows, KV_WIDTH), lambda i: (i, 0)), pl.BlockSpec((rows, KV_WIDTH), lambda i: (i, 1)),
                  _full((2, CMP_BLOCK, KV_WIDTH)), _full((2, CMP_BLOCK, KV_WIDTH, B_KV_HEADS * CMP_HIDDEN)),
                  _full((2, B_KV_HEADS * CMP_HIDDEN, KV_WIDTH))],
        out_specs=pl.BlockSpec((nblk, KV_ROW), lambda i: (i, 0)),
        out_shape=jax.ShapeDtypeStruct((n // CMP_BLOCK, KV_ROW), F32),
        compiler_params=_cparams(("parallel",)),
        name="nsa_compress",
    )(x, x, pe, _bf(w1), _bf(w2))


def _rel_bucket(dist):
    n = jnp.maximum(dist, 0)
    log_ratio = jnp.log(jnp.maximum(n, 1).astype(F32) / MAX_EXACT) / math.log(MAX_DISTANCE / MAX_EXACT)
    large = jnp.minimum(MAX_EXACT + (log_ratio * (N_BUCKETS - MAX_EXACT)).astype(I32), N_BUCKETS - 1)
    return jnp.where(n < MAX_EXACT, n, large)


def _bias_of(rel_bias, dist):
    return jnp.moveaxis(rel_bias.astype(F32)[_rel_bucket(dist)], -1, 0)


def _masked_softmax_rows(s, ok):
    s = jnp.where(ok, s, NEG_INF)
    e = jnp.exp(s - jnp.max(s, axis=-1, keepdims=True))
    return e / jnp.sum(e, axis=-1, keepdims=True) * ok.astype(F32)


def _nsa_prompt_kernel(q_ref, gate_ref, ck_ref, sel_ref, win_ref, bct_ref, bc_ref, tz_ref, o_ref, imp_scr,
                       *, nc, ns, top_n, wtiles):
    i = pl.program_id(1)
    qb = Q_BLOCK
    gates = jax.nn.sigmoid(gate_ref[...])
    qi = lax.broadcasted_iota(I32, (qb, qb), 0)
    ki = lax.broadcasted_iota(I32, (qb, qb), 1)
    eye = _bf(qi == ki)
    pos_l = i * qb + lax.broadcasted_iota(I32, (1, qb), 1)
    pos_s = i * qb + lax.broadcasted_iota(I32, (qb, 1), 0)
    okT = pos_l >= lax.broadcasted_iota(I32, (nc, qb), 0) * CMP_BLOCK + (CMP_BLOCK - 1)
    ok = pos_s >= lax.broadcasted_iota(I32, (qb, nc), 1) * CMP_BLOCK + (CMP_BLOCK - 1)
    blk = lax.broadcasted_iota(I32, (ns, qb), 0)
    cur = pos_l // SEL_BLOCK
    forced = (blk == 0) | (blk == cur) | (blk == cur - 1)
    n_e = lax.broadcasted_iota(I32, (ns, qb), 0)
    k_e = lax.broadcasted_iota(I32, (ns, qb), 1) // SEL_BLOCK

    def flash(kv_ref, g, qg, lo, mask_fn):
        def body(j, carry):
            ms, ls, accs = carry
            kt = _bf(kv_ref[pl.ds(j * qb, qb), g * HEAD_DIM:(g + 1) * HEAD_DIM])
            vt = _bf(kv_ref[pl.ds(j * qb, qb), KV_WIDTH + g * HEAD_DIM:KV_WIDTH + (g + 1) * HEAD_DIM])
            mask = mask_fn(j)
            ms2, ls2, accs2 = [], [], []
            for r in range(B_GROUP):
                s = _dot_nt(qg[r], kt) + tz_ref[g * B_GROUP + r, i - j]
                s = jnp.where(mask, s, NEG_INF)
                m_new = jnp.maximum(ms[r], jnp.max(s, axis=-1, keepdims=True))
                alpha = jnp.exp(ms[r] - m_new)
                p = jnp.where(mask, jnp.exp(s - m_new), 0.0)
                ls2.append(alpha * ls[r] + jnp.sum(p, axis=-1, keepdims=True))
                accs2.append(alpha * accs[r] + _dot(_bf(p), vt))
                ms2.append(m_new)
            return tuple(ms2), tuple(ls2), tuple(accs2)

        init = (tuple(jnp.full((qb, 1), NEG_INF, F32) for _ in range(B_GROUP)),
                tuple(jnp.zeros((qb, 1), F32) for _ in range(B_GROUP)),
                tuple(jnp.zeros((qb, HEAD_DIM), F32) for _ in range(B_GROUP)))
        _, ls, accs = lax.fori_loop(lo, i + 1, body, init)
        return [accs[r] / ls[r] for r in range(B_GROUP)]

    outs = []
    for g in range(B_KV_HEADS):
        qg = [_bf(q_ref[:, (g * B_GROUP + r) * HEAD_DIM:(g * B_GROUP + r + 1) * HEAD_DIM] * SCALE)
              for r in range(B_GROUP)]
        kc = _bf(ck_ref[:, g * HEAD_DIM:(g + 1) * HEAD_DIM])
        vc = _bf(ck_ref[:, KV_WIDTH + g * HEAD_DIM:KV_WIDTH + (g + 1) * HEAD_DIM])
        o_c = []
        impT = jnp.zeros((nc, qb), F32)
        for r in range(B_GROUP):
            h = g * B_GROUP + r
            p = _masked_softmax_rows(_dot_nt(qg[r], kc) + bc_ref[h], ok)
            o_c.append(_dot(_bf(p), vc))
            sT = jnp.where(okT, _dot_nt(kc, qg[r]) + bct_ref[h], NEG_INF)
            eT = jnp.exp(sT - jnp.max(sT, axis=0, keepdims=True))
            impT = impT + eT / jnp.sum(eT, axis=0, keepdims=True) * okT.astype(F32)
        imp_scr[...] = impT
        imp2 = imp_scr[pl.ds(0, ns, stride=SEL_RATIO), :] + imp_scr[pl.ds(1, ns, stride=SEL_RATIO), :]
        score = jnp.where(blk <= cur, imp2 + FORCE_SCORE * forced.astype(F32), NEG_INF)
        rank = jnp.zeros((ns, qb), I32)
        for m in range(ns):
            row = score[m:m + 1, :]
            rank = rank + ((row > score) | ((row == score) & (m < blk))).astype(I32)
        selT = _bf((rank < top_n) & (score > NEG_INF / 2))
        sel = _bf(_dot_nt(eye, selT))

        def sel_mask(j):
            e_tile = _bf(n_e == 2 * j + k_e)
            m = _dot(sel, e_tile) > 0.5
            return m & ((j < i) | (qi >= ki))

        def win_mask(j):
            d = (i - j) * qb + qi - ki
            return (d >= 0) & (d < WINDOW)

        o_s = flash(sel_ref, g, qg, 0, sel_mask)
        o_w = flash(win_ref, g, qg, jnp.maximum(i - wtiles, 0), win_mask)
        for r in range(B_GROUP):
            h = g * B_GROUP + r
            outs.append(gates[:, 3 * h:3 * h + 1] * o_c[r] + gates[:, 3 * h + 1:3 * h + 2] * o_s[r]
                        + gates[:, 3 * h + 2:3 * h + 3] * o_w[r])
    o_ref[...] = jnp.concatenate(outs, axis=1)


def _nsa_prompt(q, gate, ckv, kv_sel, kv_win, rel_bias, b, t):
    nq = t // Q_BLOCK
    nc = t // CMP_BLOCK
    ns = -(-t // SEL_BLOCK)
    assert t % Q_BLOCK == 0 and nc == SEL_RATIO * ns and WINDOW % Q_BLOCK == 0
    pos = jnp.arange(t)
    cmp_end = jnp.arange(nc) * CMP_BLOCK + (CMP_BLOCK - 1)
    bc = _bias_of(rel_bias, pos[:, None] - cmp_end[None, :])
    bct = jnp.swapaxes(bc, 1, 2)
    dz = jnp.arange(nq)[:, None, None] * Q_BLOCK + jnp.arange(Q_BLOCK)[None, :, None] - jnp.arange(Q_BLOCK)[None, None, :]
    tz = _bias_of(rel_bias, dz)
    row = lambda w: pl.BlockSpec((Q_BLOCK, w), lambda bi, i: (bi * nq + i, 0))
    return pl.pallas_call(
        functools.partial(_nsa_prompt_kernel, nc=nc, ns=ns, top_n=min(TOP_N, ns), wtiles=WINDOW // Q_BLOCK),
        grid=(b, nq),
        in_specs=[row(B_WIDTH), row(GATE_PAD),
                  pl.BlockSpec((nc, KV_ROW), lambda bi, i: (bi, 0)),
                  pl.BlockSpec((t, KV_ROW), lambda bi, i: (bi, 0)),
                  pl.BlockSpec((t, KV_ROW), lambda bi, i: (bi, 0)),
                  pl.BlockSpec((B_HEADS, nc, Q_BLOCK), lambda bi, i: (0, 0, i)),
                  pl.BlockSpec((B_HEADS, Q_BLOCK, nc), lambda bi, i: (0, i, 0)),
                  _full((B_HEADS, nq, Q_BLOCK, Q_BLOCK))],
        out_specs=row(B_WIDTH),
        out_shape=jax.ShapeDtypeStruct((b * t, B_WIDTH), F32),
        scratch_shapes=[pltpu.VMEM((nc, Q_BLOCK), F32)],
        compiler_params=_cparams(("parallel", "arbitrary")),
        name="nsa_prompt",
    )(q, gate, ckv, kv_sel, kv_win, bct, bc, tz)


def _group_q(q_ref, g):
    rows = [q_ref[0, :, (g * B_GROUP + r) * HEAD_DIM:(g * B_GROUP + r + 1) * HEAD_DIM] for r in range(B_GROUP)]
    return _bf(jnp.concatenate(rows, axis=0) * SCALE)


def _nsa_sample_select_kernel(q_ref, gate_ref, ck_ref, bc_ref, pair_ref, tri_ref, oc_ref, idx_ref,
                              *, nsp, cur, top_n):
    gates = jax.nn.sigmoid(gate_ref[0])
    blk = lax.broadcasted_iota(I32, (1, nsp), 1)
    forced = (blk == 0) | (blk == cur) | (blk == cur - 1)
    mi = lax.broadcasted_iota(I32, (nsp, nsp), 0)
    ni = lax.broadcasted_iota(I32, (nsp, nsp), 1)
    kk = lax.broadcasted_iota(I32, (TOP_N, nsp), 0).astype(F32)
    nf = lax.broadcasted_iota(I32, (TOP_N, nsp), 1).astype(F32)
    outs = []
    for g in range(B_KV_HEADS):
        q4 = _group_q(q_ref, g)
        kc = _bf(ck_ref[:, g * HEAD_DIM:(g + 1) * HEAD_DIM])
        vc = _bf(ck_ref[:, KV_WIDTH + g * HEAD_DIM:KV_WIDTH + (g + 1) * HEAD_DIM])
        s = _dot_nt(q4, kc) + bc_ref[g * B_GROUP:(g + 1) * B_GROUP, :]
        e = jnp.exp(s - jnp.max(s, axis=-1, keepdims=True))
        p = e / jnp.sum(e, axis=-1, keepdims=True)
        o_c = _dot(_bf(p), vc)
        imp = ((p[0:1] + p[1:2]) + p[2:3]) + p[3:4]
        imp2 = jnp.dot(imp, pair_ref[...], precision=HIGHEST, preferred_element_type=F32)
        score = jnp.where(blk <= cur, imp2 + FORCE_SCORE * forced.astype(F32), NEG_INF)
        m1 = jnp.broadcast_to(score, (nsp, nsp))
        m2 = m1.T
        gt = (m2 > m1) | ((m2 == m1) & (mi < ni))
        rank = jnp.sum(gt.astype(F32), axis=0, keepdims=True)
        sel = (rank < top_n) & (score > NEG_INF / 2)
        before = _dot(_bf(sel), tri_ref[...])
        hit = jnp.broadcast_to(sel, (TOP_N, nsp)) & (jnp.broadcast_to(before, (TOP_N, nsp)) == kk)
        idx = jnp.sum(jnp.where(hit, nf, 0.0), axis=1, keepdims=True)
        cnt = jnp.sum(hit.astype(F32), axis=1, keepdims=True)
        idx_ref[0, g] = jnp.where(cnt > 0.5, idx, -1.0).astype(I32)
        for r in range(B_GROUP):
            h = g * B_GROUP + r
            outs.append(gates[:, 3 * h:3 * h + 1] * o_c[r:r + 1])
    oc_ref[0] = jnp.concatenate(outs, axis=1)


def _nsa_sample_select(q3, gate3, ckv, rel_bias, past_len):
    s = q3.shape[0]
    nc = past_len // CMP_BLOCK
    ns = -(-(past_len + 1) // SEL_BLOCK)
    nsp = -(-ns // LANES) * LANES
    cur = past_len // SEL_BLOCK
    bc = _bias_of(rel_bias, past_len - (jnp.arange(nc) * CMP_BLOCK + (CMP_BLOCK - 1)))
    pair = jnp.asarray(np.arange(nc)[:, None] // SEL_RATIO == np.arange(nsp)[None, :], F32)
    tri = jnp.asarray(np.arange(nsp)[:, None] < np.arange(nsp)[None, :], BF16)
    return pl.pallas_call(
        functools.partial(_nsa_sample_select_kernel, nsp=nsp, cur=cur, top_n=min(TOP_N, ns)),
        grid=(s,),
        in_specs=[pl.BlockSpec((1, 1, B_WIDTH), lambda i: (i, 0, 0)), pl.BlockSpec((1, 1, GATE_PAD), lambda i: (i, 0, 0)),
                  pl.BlockSpec((nc, KV_ROW), lambda i: (i, 0)), _full((B_HEADS, nc)), _full((nc, nsp)), _full((nsp, nsp))],
        out_specs=[pl.BlockSpec((1, 1, B_WIDTH), lambda i: (i, 0, 0)),
                   pl.BlockSpec((1, B_KV_HEADS, TOP_N, 1), lambda i: (i, 0, 0, 0))],
        out_shape=[jax.ShapeDtypeStruct((s, 1, B_WIDTH), F32), jax.ShapeDtypeStruct((s, B_KV_HEADS, TOP_N, 1), I32)],
        compiler_params=_cparams(("parallel",)),
        name="nsa_sample_select",
    )(q3, gate3, ckv, bc, pair, tri)


def _nsa_sample_attend_kernel(idx_ref, pt_ref, q_ref, gate_ref, oc_ref, ksel_ref, kwin_ref, win_ref, fblk_ref, bw_ref,
                              cache_ref, o_ref, buf, sem, *, cur, n_pages, past_len):
    i = pl.program_id(0)
    n_s = pl.num_programs(0)
    slot = i % 2
    halves = PAGE_BLOCKS
    n_slot = B_KV_HEADS * TOP_N

    def block_of(s_idx, j):
        return idx_ref[s_idx * n_slot + j]

    def copy(s_idx, sl, j):
        n = jnp.clip(block_of(s_idx, j), 0, cur - 1)
        page = pt_ref[s_idx * n_pages + n // halves]
        return pltpu.make_async_copy(cache_ref.at[page * halves + n % halves], buf.at[sl, j], sem.at[sl])

    def cached(s_idx, j):
        n = block_of(s_idx, j)
        return (n >= 0) & (n < cur)

    def fetch(s_idx, sl):
        for j in range(n_slot):
            @pl.when(cached(s_idx, j))
            def _():
                copy(s_idx, sl, j).start()

    @pl.when(i == 0)
    def _():
        fetch(0, 0)

    @pl.when(i + 1 < n_s)
    def _():
        fetch(i + 1, 1 - slot)

    for j in range(n_slot):
        @pl.when(cached(i, j))
        def _():
            copy(i, slot, j).wait()

        @pl.when(jnp.logical_not(cached(i, j)))
        def _():
            buf[slot, j] = jnp.zeros((SEL_BLOCK, KV_ROW), F32)
            buf[slot, j, 0:1, :] = ksel_ref[0]

    gates = jax.nn.sigmoid(gate_ref[0])
    t_l = lax.broadcasted_iota(I32, (1, SEL_BLOCK), 1)
    n_buf = win_ref.shape[1]
    j_w = lax.broadcasted_iota(I32, (1, n_buf), 1)
    d_w = n_buf - j_w
    ok_w = (d_w >= 0) & (d_w < WINDOW) & (past_len - d_w >= 0)
    f0 = fblk_ref[cur]
    outs = []
    for g in range(B_KV_HEADS):
        q4 = _group_q(q_ref, g)
        ksl = slice(g * HEAD_DIM, (g + 1) * HEAD_DIM)
        vsl = slice(KV_WIDTH + g * HEAD_DIM, KV_WIDTH + (g + 1) * HEAD_DIM)
        pieces = []
        for k in range(TOP_N):
            j = g * TOP_N + k
            n = block_of(i, j)
            nb = jnp.clip(n, 0, cur)
            fb = fblk_ref[nb][g * B_GROUP:(g + 1) * B_GROUP, :]
            s_k = _dot_nt(q4, _bf(buf[slot, j, :, ksl])) + fb
            ok = (n >= 0) & (n * SEL_BLOCK + t_l <= past_len)
            pieces.append(jnp.where(ok, s_k, NEG_INF))
        s = jnp.concatenate(pieces, axis=1)
        e = jnp.exp(s - jnp.max(s, axis=-1, keepdims=True))
        p = e / jnp.sum(e, axis=-1, keepdims=True)
        v_all = _bf(buf[slot, g * TOP_N:(g + 1) * TOP_N, :, vsl].reshape(TOP_N * SEL_BLOCK, HEAD_DIM))
        o_s = _dot(_bf(p), v_all)
        kw = _bf(win_ref[0, :, ksl])
        vw = _bf(win_ref[0, :, vsl])
        s_w = jnp.where(ok_w, _dot_nt(q4, kw) + bw_ref[g * B_GROUP:(g + 1) * B_GROUP, :], NEG_INF)
        k_new = _bf(kwin_ref[0, :, ksl])
        v_new = _bf(kwin_ref[0, :, vsl])
        s_n = (jnp.sum(q4.astype(F32) * k_new.astype(F32), axis=-1, keepdims=True)
               + f0[g * B_GROUP:(g + 1) * B_GROUP, 0:1])
        m = jnp.maximum(jnp.max(s_w, axis=-1, keepdims=True), s_n)
        e_w = jnp.exp(s_w - m)
        e_n = jnp.exp(s_n - m)
        den = jnp.sum(e_w, axis=-1, keepdims=True) + e_n
        o_w = _dot(_bf(e_w / den), vw) + _bf(e_n / den).astype(F32) * v_new.astype(F32)
        for r in range(B_GROUP):
            h = g * B_GROUP + r
            outs.append(gates[:, 3 * h + 1:3 * h + 2] * o_s[r:r + 1] + gates[:, 3 * h + 2:3 * h + 3] * o_w[r:r + 1])
    o_ref[0] = oc_ref[0] + jnp.concatenate(outs, axis=1)


PAGE_BLOCKS = 2


def _nsa_sample_attend(idx, page_table, q3, gate3, oc3, ksel3, kwin3, win, cache_sel, rel_bias, past_len):
    s = q3.shape[0]
    n_pages = page_table.shape[1]
    cur = past_len // SEL_BLOCK
    n_buf = win.shape[1]
    n_pool, page = cache_sel.shape[:2]
    assert page == PAGE_BLOCKS * SEL_BLOCK and past_len % page == 0
    cache = cache_sel.reshape(n_pool * PAGE_BLOCKS, SEL_BLOCK, KV_ROW)
    keypos = jnp.arange(cur + 1)[:, None] * SEL_BLOCK + jnp.arange(SEL_BLOCK)[None, :]
    fblk = jnp.moveaxis(_bias_of(rel_bias, past_len - keypos), 0, 1)
    bw = _bias_of(rel_bias, n_buf - jnp.arange(n_buf))
    row3 = lambda w: pl.BlockSpec((1, 1, w), lambda i, *_: (i, 0, 0))
    grid_spec = pltpu.PrefetchScalarGridSpec(
        num_scalar_prefetch=2,
        grid=(s,),
        in_specs=[row3(B_WIDTH), row3(GATE_PAD), row3(B_WIDTH), row3(KV_ROW), row3(KV_ROW),
                  pl.BlockSpec((1, n_buf, KV_ROW), lambda i, *_: (i, 0, 0)),
                  pl.BlockSpec((cur + 1, B_HEADS, SEL_BLOCK), lambda i, *_: (0, 0, 0)),
                  pl.BlockSpec((B_HEADS, n_buf), lambda i, *_: (0, 0)),
                  pl.BlockSpec(memory_space=pl.ANY)],
        out_specs=row3(B_WIDTH),
        scratch_shapes=[pltpu.VMEM((2, B_KV_HEADS * TOP_N, SEL_BLOCK, KV_ROW), F32), pltpu.SemaphoreType.DMA((2,))],
    )
    return pl.pallas_call(
        functools.partial(_nsa_sample_attend_kernel, cur=cur, n_pages=n_pages, past_len=past_len),
        grid_spec=grid_spec,
        out_shape=jax.ShapeDtypeStruct((s, 1, B_WIDTH), F32),
        compiler_params=_cparams(("arbitrary",)),
        name="nsa_sample_attend",
    )(idx.reshape(-1), page_table.reshape(-1), q3, gate3, oc3, ksel3, kwin3, win, fblk, bw, cache)


def _merge_kernel(x_ref, yn_ref, bo_ref, ga_ref, yb_ref, pg_ref, lnw_ref, lnb_ref, wa_ref, wb_ref, wo_ref, nf_ref,
                  wr_ref, br_ref, x1_ref, h_ref, te_ref, tw_ref, *, a_transposed):
    if a_transposed:
        tm = x_ref.shape[0]
        yn = yn_ref[...].reshape(A_WIDTH, tm)
        y_a = ((yn * lnw_ref[...] + lnb_ref[...] + bo_ref[0]) * ga_ref[0]).T
    else:
        y_a = (yn_ref[...] * lnw_ref[...] + lnb_ref[...] + bo_ref[...]) * ga_ref[...]
    g_a = jax.nn.sigmoid(pg_ref[:, :D_MODEL])
    g_b = jax.nn.sigmoid(pg_ref[:, D_MODEL:])
    m = g_a * _dot(_bf(y_a), wa_ref[...]) + g_b * _dot(_bf(yb_ref[...]), wb_ref[...])
    x1 = x_ref[...] + _dot(_bf(m), wo_ref[...])
    x1_ref[...] = x1
    h = x1 * lax.rsqrt(jnp.mean(x1 * x1, axis=-1, keepdims=True) + NORM_EPS) * nf_ref[...]
    h_ref[...] = _bf(h)
    logits = jnp.dot(h, wr_ref[...], precision=HIGHEST, preferred_element_type=F32) + br_ref[...]
    lane = lax.broadcasted_iota(I32, logits.shape, 1).astype(F32)
    col = lax.broadcasted_iota(I32, (logits.shape[0], TOP_K), 1)
    vals, idxs = [], []
    for _ in range(TOP_K):
        m_k = jnp.max(logits, axis=-1, keepdims=True)
        i_k = jnp.min(jnp.where(logits == m_k, lane, float(N_EXPERTS)), axis=-1, keepdims=True)
        vals.append(m_k)
        idxs.append(i_k)
        logits = jnp.where(lane == i_k, -jnp.inf, logits)
    e = [jnp.exp(v - vals[0]) for v in vals]
    den = ((e[0] + e[1]) + e[2]) + e[3]
    te = jnp.zeros(col.shape, F32)
    tw = jnp.zeros(col.shape, F32)
    for k in range(TOP_K):
        te = jnp.where(col == k, idxs[k], te)
        tw = jnp.where(col == k, e[k] / den, tw)
    te_ref[...] = te.astype(I32)
    tw_ref[...] = tw


def _merge(x, yn, bonus, gate, yb, pg, ln_w, ln_b, w_a, w_b, w_o, norm_ffn, w_router, b_router, tm, seq=None):
    rows = x.shape[0]
    row = lambda w: pl.BlockSpec((tm, w), lambda i: (i, 0))
    vec = lambda z: z.reshape(1, -1).astype(F32)
    if seq is None:
        a_specs = [row(A_WIDTH)] * 3 + [_full((1, A_WIDTH))] * 2
        lnw, lnb = vec(ln_w), vec(ln_b)
    else:
        per = seq // tm
        nspec = pl.BlockSpec((1, A_WIDTH, tm), lambda i: (i // per, 0, i % per))
        a_specs = [pl.BlockSpec((HEAD_DIM, 1, A_HEADS, tm), lambda i: (0, i // per, 0, i % per)), nspec, nspec,
                   _full((A_WIDTH, 1)), _full((A_WIDTH, 1))]
        lnw, lnb = (z[CH_MAJOR].reshape(-1, 1).astype(F32) for z in (ln_w, ln_b))
        w_a = w_a[CH_MAJOR]
    return pl.pallas_call(
        functools.partial(_merge_kernel, a_transposed=seq is not None),
        grid=(rows // tm,),
        in_specs=[row(D_MODEL)] + a_specs[:3] + [row(B_WIDTH), row(GATE_PROJ)] + a_specs[3:]
                 + [_full((A_WIDTH, D_MODEL)), _full((B_WIDTH, D_MODEL)),
                  _full((D_MODEL, D_MODEL)), _full((1, D_MODEL)), _full((D_MODEL, N_EXPERTS)), _full((1, N_EXPERTS))],
        out_specs=[row(D_MODEL), row(D_MODEL), row(TOP_K), row(TOP_K)],
        out_shape=[jax.ShapeDtypeStruct((rows, D_MODEL), F32), jax.ShapeDtypeStruct((rows, D_MODEL), BF16),
                   jax.ShapeDtypeStruct((rows, TOP_K), I32), jax.ShapeDtypeStruct((rows, TOP_K), F32)],
        compiler_params=_cparams(("parallel",)),
        name="merge_router",
    )(x, yn, bonus, gate, yb, pg, lnw, lnb, _bf(w_a), _bf(w_b), _bf(w_o), vec(norm_ffn),
      w_router.astype(F32), vec(b_router))


def _moe_kernel(ce_ref, nu_ref, x_ref, sw_ref, wu_ref, bu_ref, wd_ref, bd_ref, o_ref, wu_bf, wd_bf):
    c = pl.program_id(0)
    e = ce_ref[c]
    prev = ce_ref[jnp.maximum(c - 1, 0)]

    @pl.when((c == 0) | (e != prev))
    def _():
        wu_bf[...] = _bf(wu_ref[0])
        wd_bf[...] = _bf(wd_ref[0])

    @pl.when(c < nu_ref[0])
    def _():
        u = _dot(x_ref[...], wu_bf[...]) + bu_ref[0]
        glu = jnp.minimum(u[:, :D_FF], SWIGLU_LIMIT)
        lin = jnp.clip(u[:, D_FF:], -SWIGLU_LIMIT, SWIGLU_LIMIT)
        act = glu * jax.nn.sigmoid(SWIGLU_ALPHA * glu) * (lin + 1.0)
        o_ref[...] = (_dot(_bf(act), wd_bf[...]) + bd_ref[0]) * sw_ref[...]

    @pl.when(c >= nu_ref[0])
    def _():
        o_ref[...] = jnp.zeros_like(o_ref)


def _moe_experts(chunk_e, n_used, xs, slot_w, w_up, b_up, w_down, b_down):
    slots = xs.shape[0]
    n_chunks = slots // MOE_ROWS
    grid_spec = pltpu.PrefetchScalarGridSpec(
        num_scalar_prefetch=2,
        grid=(n_chunks,),
        in_specs=[pl.BlockSpec((MOE_ROWS, D_MODEL), lambda c, ce, nu: (c, 0)),
                  pl.BlockSpec((MOE_ROWS, 1), lambda c, ce, nu: (c, 0)),
                  pl.BlockSpec((1, D_MODEL, 2 * D_FF), lambda c, ce, nu: (ce[c], 0, 0)),
                  pl.BlockSpec((1, 1, 2 * D_FF), lambda c, ce, nu: (ce[c], 0, 0)),
                  pl.BlockSpec((1, D_FF, D_MODEL), lambda c, ce, nu: (ce[c], 0, 0)),
                  pl.BlockSpec((1, 1, D_MODEL), lambda c, ce, nu: (ce[c], 0, 0))],
        out_specs=pl.BlockSpec((MOE_ROWS, D_MODEL), lambda c, ce, nu: (c, 0)),
        scratch_shapes=[pltpu.VMEM((D_MODEL, 2 * D_FF), BF16), pltpu.VMEM((D_FF, D_MODEL), BF16)],
    )
    return pl.pallas_call(
        _moe_kernel,
        grid_spec=grid_spec,
        out_shape=jax.ShapeDtypeStruct((slots, D_MODEL), F32),
        compiler_params=_cparams(("arbitrary",)),
        name="moe_experts",
    )(chunk_e, n_used, xs, slot_w, w_up, b_up.reshape(N_EXPERTS, 1, -1), w_down, b_down.reshape(N_EXPERTS, 1, -1))


def _moe(h, top_e, top_w, w_up, b_up, w_down, b_down):
    n = h.shape[0]
    n_assign = n * TOP_K
    flat_e = top_e.reshape(-1)
    counts = jnp.bincount(flat_e, length=N_EXPERTS)
    padded = (counts + MOE_ROWS - 1) // MOE_ROWS * MOE_ROWS
    pad_end = jnp.cumsum(padded)
    pad_start = pad_end - padded
    grp_start = jnp.cumsum(counts) - counts
    order = jnp.argsort(flat_e, stable=True)
    se = flat_e[order]
    dest = (pad_start[se] + jnp.arange(n_assign) - grp_start[se]).astype(I32)
    n_chunks = -(-(n_assign + N_EXPERTS * (MOE_ROWS - 1)) // MOE_ROWS)
    slots = n_chunks * MOE_ROWS
    slot_tok = jnp.full((slots,), n, I32).at[dest].set((order // TOP_K).astype(I32))
    slot_w = jnp.zeros((slots,), F32).at[dest].set(top_w.reshape(-1)[order])
    chunk_e = jnp.minimum(jnp.searchsorted(pad_end, jnp.arange(n_chunks) * MOE_ROWS, side='right'),
                          N_EXPERTS - 1).astype(I32)
    n_used = (pad_end[-1] // MOE_ROWS).astype(I32).reshape(1)
    h_pad = jnp.concatenate([h, jnp.zeros((1, D_MODEL), h.dtype)], axis=0)
    xs = h_pad[slot_tok]
    ys = _moe_experts(chunk_e, n_used, xs, slot_w.reshape(-1, 1), w_up, b_up, w_down, b_down)
    slot_of = jnp.zeros((n_assign,), I32).at[order].set(dest)
    return ys[slot_of.reshape(n, TOP_K)].sum(axis=1)


def _final_kernel(x_ref, m_ref, g_ref, o_ref):
    x = x_ref[...] + m_ref[...]
    o_ref[...] = x * lax.rsqrt(jnp.mean(x * x, axis=-1, keepdims=True) + NORM_EPS) * g_ref[...]


def _final(x1, moe, g, tm):
    rows = x1.shape[0]
    row = pl.BlockSpec((tm, D_MODEL), lambda i: (i, 0))
    return pl.pallas_call(
        _final_kernel,
        grid=(rows // tm,),
        in_specs=[row, row, _full((1, D_MODEL))],
        out_specs=row,
        out_shape=jax.ShapeDtypeStruct((rows, D_MODEL), F32),
        compiler_params=_cparams(("parallel",)),
        name="final_norm",
    )(x1, moe, g.reshape(1, -1).astype(F32))


def _row_tile(rows, cap):
    tm = cap
    while rows % tm:
        tm //= 2
    return tm


def _rwkv_group(p_a, prev, s0_tiles, b, t, rw, tc):
    mu, w0, wdu, a0, wiu, wgu, k_k, k_a, r_k = rw
    tm = _row_tile(b * t, 256)
    r, w, k2, v, kk, kb, gate, bonus = _rwkv_prep(p_a, prev, mu, w0, wdu, a0, wiu, wgu, k_k, k_a, r_k, tm)
    kt = lambda z: _to_key_tiles(z, b, t)
    yn, s_fin = _wkv_scan(kt(r), kt(w), kt(k2), _to_val_tiles(v, b, t), kt(kk), kt(kb), s0_tiles, tc)
    return _from_val_tiles(yn, b, t), gate, bonus, _state_from_tiles(s_fin, b)


def kernel(x_prompt, x_sample, cache_cmp_kv, cache_sel_kv, state_win_kv, state_rwkv, state_rwkv_shift, page_table,
           norm_attn, w_in, mu_shift, w0, w_decay_up, a0, w_iclr_up, w_gate_up, k_k, k_a, r_k, ln_x_w, ln_x_b,
           pe_cmp_k, pe_cmp_v, w_cmp_k1, w_cmp_k2, w_cmp_v1, w_cmp_v2, rel_bias, w_br_a, w_br_b, w_out,
           norm_ffn, w_router, b_router, w_up, b_up, w_down, b_down, norm_final):
    bp, tp, _ = x_prompt.shape
    bs, ts, _ = x_sample.shape
    depth = w_in.shape[0]
    past_len = page_table.shape[1] * cache_cmp_kv.shape[2]
    n_buf = state_win_kv.shape[2]
    assert ts == 1 and bp * A_HEADS == BH_LANES and bs % 8 == 0 and tp % LANES == 0
    xp = x_prompt.reshape(bp * tp, D_MODEL)
    xs = x_sample.reshape(bs, D_MODEL)
    new = {name: [] for name in ('cmp_p', 'sel_p', 'win_p', 'wkv_p', 'shift_p', 'cmp_s', 'sel_s', 'win_s', 'wkv_s', 'shift_s')}
    kv6 = lambda z, b, t: z.reshape(b, t, 2, B_KV_HEADS, HEAD_DIM)
    for l in range(depth):
        rw = (mu_shift[l], w0[l], w_decay_up[l], a0[l], w_iclr_up[l], w_gate_up[l], k_k[l], k_a[l], r_k[l])
        cw = (pe_cmp_k[l], pe_cmp_v[l], w_cmp_k1[l], w_cmp_k2[l], w_cmp_v1[l], w_cmp_v2[l])
        wa_nat, wa_t, w_rest = _pack_w_in(w_in[l])
        g_attn = norm_attn[l].reshape(1, -1).astype(F32)

        tq = _row_tile(tp, 256)
        pa_t, q, kv_c, kv_s, kv_w, gt, pg = _project(xp, g_attn, wa_t, w_rest, tq, seq=tp)
        r, w, k2, v, kk, kb, gate, bonus = _rwkv_prep_t(pa_t, *rw, tq)
        r, w, k2, kk, kb = _key_tiles([z.reshape(N_JP, LANES, tp) for z in (r, w, k2, kk, kb)], LANES)
        v = _val_tiles(v.reshape(HEAD_DIM, BH_LANES, tp), LANES).reshape(1, tp, HEAD_DIM, LANES)
        s0 = jnp.zeros((1, N_IO, N_JP, 8, LANES), F32)
        yn, s_fin = _wkv_scan(r, w, k2, v, kk, kb, s0, _row_tile(tp, 32))
        yn = _val_untile(yn.reshape(tp * HEAD_DIM, LANES), LANES).reshape(HEAD_DIM, bp, A_HEADS, tp)
        ckv = _compress(kv_c, *cw, rows=_row_tile(tp, 2048))
        y_b = _nsa_prompt(q, gt, ckv, kv_s, kv_w, rel_bias, bp, tp)
        x1, h, top_e, top_w = _merge(xp, yn, bonus, gate, y_b, pg, ln_x_w[l], ln_x_b[l], w_br_a[l], w_br_b[l], w_out[l],
                                     norm_ffn[l], w_router[l], b_router[l], tq, seq=tp)
        moe = _moe(h, top_e, top_w, w_up[l], b_up[l], w_down[l], b_down[l])
        xp_next = (x1, moe)
        n_win = min(WINDOW, tp)
        new['cmp_p'].append(kv6(kv_c, bp, tp))
        new['sel_p'].append(kv6(kv_s, bp, tp))
        new['win_p'].append(kv6(kv_w, bp, tp)[:, tp - n_win:])
        new['wkv_p'].append(_state_from_tiles(s_fin, bp))
        new['shift_p'].append(jnp.zeros((bp, A_PROJ), F32).at[:, A_PERM].set(pa_t[:, :, tp - 1]))

        s_a, q, kv_c, kv_s, kv_w, gt, pg = _project(xs, g_attn, wa_nat, w_rest, _row_tile(bs, 256))
        yn, gate, bonus, wkv_s = _rwkv_group(s_a, state_rwkv_shift[l], _state_to_tiles(state_rwkv[l].astype(F32), bs),
                                             bs, 1, rw, 1)
        past_cmp = cache_cmp_kv[l][page_table].reshape(bs * past_len, KV_ROW)
        ckv = _compress(past_cmp, *cw, rows=past_len)
        q3, gt3 = q.reshape(bs, 1, B_WIDTH), gt.reshape(bs, 1, GATE_PAD)
        oc3, idx = _nsa_sample_select(q3, gt3, ckv, rel_bias, past_len)
        win = state_win_kv[l].reshape(bs, n_buf, KV_ROW)
        y_b = _nsa_sample_attend(idx, page_table, q3, gt3, oc3, kv_s.reshape(bs, 1, KV_ROW), kv_w.reshape(bs, 1, KV_ROW),
                                 win, cache_sel_kv[l].reshape(cache_sel_kv.shape[1], cache_sel_kv.shape[2], KV_ROW),
                                 rel_bias, past_len).reshape(bs, B_WIDTH)
        x1, h, top_e, top_w = _merge(xs, yn, bonus, gate, y_b, pg, ln_x_w[l], ln_x_b[l], w_br_a[l], w_br_b[l], w_out[l],
                                     norm_ffn[l], w_router[l], b_router[l], _row_tile(bs, 256))
        moe = _moe(h, top_e, top_w, w_up[l], b_up[l], w_down[l], b_down[l])
        xs_next = (x1, moe)
        new['cmp_s'].append(kv6(kv_c, bs, 1))
        new['sel_s'].append(kv6(kv_s, bs, 1))
        new['win_s'].append(jnp.concatenate([state_win_kv[l], kv6(kv_w, bs, 1)], axis=1)[:, 1:])
        new['wkv_s'].append(wkv_s.astype(state_rwkv.dtype))
        new['shift_s'].append(s_a)

        last = l == depth - 1
        g_fin = norm_final if last else jnp.ones((D_MODEL,), F32)
        if last:
            xp = _final(*xp_next, g_fin, _row_tile(bp * tp, 512))
            xs = _final(*xs_next, g_fin, _row_tile(bs, 512))
        else:
            xp = xp_next[0] + xp_next[1]
            xs = xs_next[0] + xs_next[1]
    return (xp.reshape(bp, tp, D_MODEL), xs.reshape(bs, ts, D_MODEL),
            jnp.stack(new['cmp_p']), jnp.stack(new['sel_p']), jnp.stack(new['win_p']),
            jnp.stack(new['wkv_p']), jnp.stack(new['shift_p']),
            jnp.stack(new['cmp_s']), jnp.stack(new['sel_s']), jnp.stack(new['win_s']),
            jnp.stack(new['wkv_s']), jnp.stack(new['shift_s']))
```

```python
import functools
import math

import jax
import jax.numpy as jnp
import numpy as np
from jax import lax
from jax.experimental import pallas as pl
from jax.experimental.pallas import tpu as pltpu

F32 = jnp.float32
BF16 = jnp.bfloat16
I32 = jnp.int32
HIGHEST = lax.Precision.HIGHEST

D_MODEL = 1024
HEAD_DIM = 64
A_HEADS = 8
A_WIDTH = A_HEADS * HEAD_DIM
A_DECAY_RANK = 64
A_ICLR_RANK = 64
A_GATE_RANK = 128
A_GN_EPS = 64e-5
A_PROJ = 3 * A_WIDTH + A_DECAY_RANK + A_ICLR_RANK + A_GATE_RANK
B_HEADS = 8
B_KV_HEADS = 2
B_GROUP = B_HEADS // B_KV_HEADS
B_WIDTH = B_HEADS * HEAD_DIM
KV_WIDTH = B_KV_HEADS * HEAD_DIM
KV_ROW = 2 * KV_WIDTH
CMP_BLOCK = 32
CMP_HIDDEN = 128
SEL_BLOCK = 64
SEL_RATIO = SEL_BLOCK // CMP_BLOCK
TOP_N = 16
WINDOW = 512
Q_BLOCK = 128
FORCE_SCORE = 1e4
N_BUCKETS = 32
MAX_EXACT = N_BUCKETS // 2
MAX_DISTANCE = 1024
N_EXPERTS = 32
TOP_K = 4
D_FF = 1024
SWIGLU_ALPHA = 1.702
SWIGLU_LIMIT = 7.0
NORM_EPS = 1e-6
NEG_INF = -1e30
SCALE = HEAD_DIM ** -0.5
GATE_PAD = 128
GATE_PROJ = 2 * D_MODEL
MOE_ROWS = 256
LANES = 128
VMEM_LIMIT = 56 * 1024 * 1024


def _cparams(sem):
    return pltpu.CompilerParams(dimension_semantics=sem, vmem_limit_bytes=VMEM_LIMIT)


def _full(shape):
    n = len(shape)
    return pl.BlockSpec(shape, lambda *_: (0,) * n)


def _dot(a, b):
    return jnp.dot(a, b, preferred_element_type=F32)


def _dot_nt(a, b):
    return lax.dot_general(a, b, (((1,), (1,)), ((), ())), preferred_element_type=F32)


def _bf(x):
    return x.astype(BF16)


PROJ_SPLITS = (B_WIDTH, KV_ROW, KV_ROW, KV_ROW, GATE_PAD, GATE_PROJ)
CH_MAJOR = (np.arange(A_HEADS)[None, :] * HEAD_DIM + np.arange(HEAD_DIM)[:, None]).reshape(-1)
A_PERM = np.concatenate([CH_MAJOR, A_WIDTH + CH_MAJOR, 2 * A_WIDTH + CH_MAJOR, np.arange(3 * A_WIDTH, A_PROJ)])


def _proj_kernel(x_ref, g_ref, wa_ref, w_ref, oa_ref, *o_refs, a_transposed):
    x = x_ref[...]
    y = x * lax.rsqrt(jnp.mean(x * x, axis=-1, keepdims=True) + NORM_EPS) * g_ref[...]
    h = _bf(y)
    if a_transposed:
        oa_ref[0] = _dot_nt(wa_ref[...], h)
    else:
        oa_ref[...] = _dot(h, wa_ref[...])
    c = 0
    for o_ref, n in zip(o_refs, PROJ_SPLITS):
        o_ref[...] = _dot(h, w_ref[:, c:c + n])
        c += n


def _project(x, g, wa, w, tm, seq=None):
    rows = x.shape[0]
    ncol = sum(PROJ_SPLITS)
    if seq is None:
        a_spec = pl.BlockSpec((tm, A_PROJ), lambda i: (i, 0))
        a_shape = jax.ShapeDtypeStruct((rows, A_PROJ), F32)
    else:
        per = seq // tm
        a_spec = pl.BlockSpec((1, A_PROJ, tm), lambda i: (i // per, 0, i % per))
        a_shape = jax.ShapeDtypeStruct((rows // seq, A_PROJ, seq), F32)
    return pl.pallas_call(
        functools.partial(_proj_kernel, a_transposed=seq is not None),
        grid=(rows // tm,),
        in_specs=[pl.BlockSpec((tm, D_MODEL), lambda i: (i, 0)), _full((1, D_MODEL)), _full(wa.shape),
                  _full((D_MODEL, ncol))],
        out_specs=[a_spec] + [pl.BlockSpec((tm, n), lambda i: (i, 0)) for n in PROJ_SPLITS],
        out_shape=[a_shape] + [jax.ShapeDtypeStruct((rows, n), F32) for n in PROJ_SPLITS],
        compiler_params=_cparams(("parallel",)),
        name="norm_proj",
    )(x, g, wa, w)


def _pack_w_in(w_in):
    a, rest = w_in[:, :A_PROJ], w_in[:, A_PROJ:]
    q, kv, gt, pg = (rest[:, :B_WIDTH], rest[:, B_WIDTH:B_WIDTH + 3 * KV_ROW],
                     rest[:, B_WIDTH + 3 * KV_ROW:B_WIDTH + 3 * KV_ROW + 3 * B_HEADS],
                     rest[:, B_WIDTH + 3 * KV_ROW + 3 * B_HEADS:])
    gt = jnp.pad(gt, ((0, 0), (0, GATE_PAD - 3 * B_HEADS)))
    return _bf(a), _bf(a[:, A_PERM].T), _bf(jnp.concatenate([q, kv, gt, pg], axis=1))


def _softplus(z):
    return jnp.maximum(z, 0.0) + jnp.log1p(jnp.exp(-jnp.abs(z)))


def _rwkv_prep_kernel(p_ref, prev_ref, mu_ref, w0_ref, wdu_ref, a0_ref, wiu_ref, wgu_ref, kk_ref, ka_ref, rk_ref,
                      ones_ref, r_o, w_o, k_o, v_o, kk_o, kb_o, g_o, bo_o):
    p = p_ref[...]
    ps = p + mu_ref[...] * (prev_ref[...] - p)
    r = ps[:, 0:A_WIDTH]
    k = ps[:, A_WIDTH:2 * A_WIDTH]
    v = ps[:, 2 * A_WIDTH:3 * A_WIDTH]
    c = 3 * A_WIDTH
    xw = ps[:, c:c + A_DECAY_RANK]
    xa = ps[:, c + A_DECAY_RANK:c + A_DECAY_RANK + A_ICLR_RANK]
    xg = ps[:, c + A_DECAY_RANK + A_ICLR_RANK:]
    w_log = -_softplus(-(w0_ref[...] + _dot(_bf(jnp.tanh(xw)), wdu_ref[...]))) - 0.5
    decay = jnp.exp(-jnp.exp(w_log))
    a = jax.nn.sigmoid(a0_ref[...] + _dot(_bf(xa), wiu_ref[...]))
    gate = _dot(_bf(jax.nn.sigmoid(xg)), wgu_ref[...])
    ones = ones_ref[...]
    kk = k * kk_ref[...]
    ss = jnp.dot(kk * kk, ones, precision=HIGHEST, preferred_element_type=F32)
    kk = kk / jnp.maximum(jnp.sqrt(ss), 1e-12)
    k2 = k * (1.0 + (a - 1.0) * ka_ref[...])
    rk = jnp.dot(r * k2 * rk_ref[...], ones, precision=HIGHEST, preferred_element_type=F32)
    r_o[...] = r
    w_o[...] = decay
    k_o[...] = k2
    v_o[...] = v
    kk_o[...] = kk
    kb_o[...] = kk * a
    g_o[...] = gate
    bo_o[...] = rk * v


def _rwkv_prep(p, prev, mu, w0, wdu, a0, wiu, wgu, k_k, k_a, r_k, tm):
    rows = p.shape[0]
    head = np.arange(A_WIDTH) // HEAD_DIM
    ones = jnp.asarray(head[:, None] == head[None, :], F32)
    row = lambda z: z.reshape(1, -1).astype(F32)
    spec_in = pl.BlockSpec((tm, A_PROJ), lambda i: (i, 0))
    spec_o = pl.BlockSpec((tm, A_WIDTH), lambda i: (i, 0))
    return pl.pallas_call(
        _rwkv_prep_kernel,
        grid=(rows // tm,),
        in_specs=[spec_in, spec_in, _full((1, A_PROJ)), _full((1, A_WIDTH)), _full((A_DECAY_RANK, A_WIDTH)),
                  _full((1, A_WIDTH)), _full((A_ICLR_RANK, A_WIDTH)), _full((A_GATE_RANK, A_WIDTH)),
                  _full((1, A_WIDTH)), _full((1, A_WIDTH)), _full((1, A_WIDTH)), _full((A_WIDTH, A_WIDTH))],
        out_specs=[spec_o] * 8,
        out_shape=[jax.ShapeDtypeStruct((rows, A_WIDTH), F32)] * 8,
        compiler_params=_cparams(("parallel",)),
        name="rwkv_prep",
    )(p, prev, row(mu), row(w0), _bf(wdu), row(a0), _bf(wiu), _bf(wgu), row(k_k), row(k_a), row(r_k), ones)


def _rwkv_prep_t_kernel(p_ref, mu_ref, w0_ref, wdu_ref, a0_ref, wiu_ref, wgu_ref, kk_ref, ka_ref, rk_ref,
                        r_o, w_o, k_o, v_o, kk_o, kb_o, g_o, bo_o, carry):
    tt = p_ref.shape[2]

    @pl.when(pl.program_id(1) == 0)
    def _():
        carry[...] = jnp.zeros_like(carry)

    p = p_ref[0]
    lane = lax.broadcasted_iota(I32, (1, tt), 1)
    prev = jnp.where(lane == 0, carry[:, 0:1], pltpu.roll(p, 1, 1))
    carry[:, 0:1] = p[:, tt - 1:tt]
    ps = p + mu_ref[...] * (prev - p)
    r = ps[0:A_WIDTH]
    k = ps[A_WIDTH:2 * A_WIDTH]
    v = ps[2 * A_WIDTH:3 * A_WIDTH]
    c = 3 * A_WIDTH
    xw = ps[c:c + A_DECAY_RANK]
    xa = ps[c + A_DECAY_RANK:c + A_DECAY_RANK + A_ICLR_RANK]
    xg = ps[c + A_DECAY_RANK + A_ICLR_RANK:]

    def head_sum(x):
        s = jnp.sum(x.reshape(HEAD_DIM, A_HEADS, tt), axis=0)
        return jnp.broadcast_to(s[None], (HEAD_DIM, A_HEADS, tt)).reshape(A_WIDTH, tt)

    w_log = -_softplus(-(w0_ref[...] + _dot(wdu_ref[...], _bf(jnp.tanh(xw))))) - 0.5
    decay = jnp.exp(-jnp.exp(w_log))
    a = jax.nn.sigmoid(a0_ref[...] + _dot(wiu_ref[...], _bf(xa)))
    gate = _dot(wgu_ref[...], _bf(jax.nn.sigmoid(xg)))
    kk = k * kk_ref[...]
    kk = kk / jnp.maximum(jnp.sqrt(head_sum(kk * kk)), 1e-12)
    k2 = k * (1.0 + (a - 1.0) * ka_ref[...])
    rk = head_sum(r * k2 * rk_ref[...])
    for o_ref, val in ((r_o, r), (w_o, decay), (k_o, k2), (kk_o, kk), (kb_o, kk * a)):
        o_ref[...] = val.reshape(N_JP, 2, 1, A_HEADS, tt)
    v_o[...] = v.reshape(HEAD_DIM, 1, A_HEADS, tt)
    g_o[0] = gate
    bo_o[0] = rk * v


def _rwkv_prep_t(p_t, mu, w0, wdu, a0, wiu, wgu, k_k, k_a, r_k, tt):
    b, _, t = p_t.shape
    col = lambda z, perm: z.reshape(-1)[perm].reshape(-1, 1).astype(F32)
    up = lambda w: _bf(w[:, CH_MAJOR].T)
    kshape = jax.ShapeDtypeStruct((N_JP, 2, b, A_HEADS, t), F32)
    kspec = pl.BlockSpec((N_JP, 2, 1, A_HEADS, tt), lambda bi, i: (0, 0, bi, 0, i))
    vspec = pl.BlockSpec((HEAD_DIM, 1, A_HEADS, tt), lambda bi, i: (0, bi, 0, i))
    nspec = pl.BlockSpec((1, A_WIDTH, tt), lambda bi, i: (bi, 0, i))
    nshape = jax.ShapeDtypeStruct((b, A_WIDTH, t), F32)
    return pl.pallas_call(
        _rwkv_prep_t_kernel,
        grid=(b, t // tt),
        in_specs=[pl.BlockSpec((1, A_PROJ, tt), lambda bi, i: (bi, 0, i)), _full((A_PROJ, 1)), _full((A_WIDTH, 1)),
                  _full((A_WIDTH, A_DECAY_RANK)), _full((A_WIDTH, 1)), _full((A_WIDTH, A_ICLR_RANK)),
                  _full((A_WIDTH, A_GATE_RANK)), _full((A_WIDTH, 1)), _full((A_WIDTH, 1)), _full((A_WIDTH, 1))],
        out_specs=[kspec, kspec, kspec, vspec, kspec, kspec, nspec, nspec],
        out_shape=[kshape, kshape, kshape, jax.ShapeDtypeStruct((HEAD_DIM, b, A_HEADS, t), F32), kshape, kshape,
                   nshape, nshape],
        scratch_shapes=[pltpu.VMEM((A_PROJ, LANES), F32)],
        compiler_params=_cparams(("parallel", "arbitrary")),
        name="rwkv_prep_t",
    )(p_t, col(mu, A_PERM), col(w0, CH_MAJOR), up(wdu), col(a0, CH_MAJOR), up(wiu), up(wgu),
      col(k_k, CH_MAJOR), col(k_a, CH_MAJOR), col(r_k, CH_MAJOR))


def _key_tiles_kernel(*refs):
    n = len(refs) // 2
    for z_ref, o_ref in zip(refs[:n], refs[n:]):
        for jp in range(N_JP):
            o_ref[0, jp] = z_ref[jp].T


def _key_tiles(zs, tt):
    t = zs[0].shape[2]
    return pl.pallas_call(
        _key_tiles_kernel,
        grid=(t // tt,),
        in_specs=[pl.BlockSpec((N_JP, LANES, tt), lambda i: (0, 0, i))] * len(zs),
        out_specs=[pl.BlockSpec((1, N_JP, tt, LANES), lambda i: (0, 0, i, 0))] * len(zs),
        out_shape=[jax.ShapeDtypeStruct((1, N_JP, t, LANES), F32)] * len(zs),
        compiler_params=_cparams(("parallel",)),
        name="wkv_key_tiles",
    )(*zs)


def _val_tiles_kernel(z_ref, o_ref, *, tt):
    for i in range(HEAD_DIM):
        x = z_ref[i]
        o_ref[pl.ds(i, tt, stride=HEAD_DIM), :] = jnp.concatenate([x, x], axis=0).T


def _val_tiles(z, tt):
    t = z.shape[2]
    return pl.pallas_call(
        functools.partial(_val_tiles_kernel, tt=tt),
        grid=(t // tt,),
        in_specs=[pl.BlockSpec((HEAD_DIM, BH_LANES, tt), lambda i: (0, 0, i))],
        out_specs=pl.BlockSpec((tt * HEAD_DIM, LANES), lambda i: (i, 0)),
        out_shape=jax.ShapeDtypeStruct((t * HEAD_DIM, LANES), F32),
        compiler_params=_cparams(("parallel",)),
        name="wkv_val_tiles",
    )(z)


def _val_untile_kernel(y_ref, o_ref, *, tt):
    for i in range(HEAD_DIM):
        o_ref[i] = y_ref[pl.ds(i, tt, stride=HEAD_DIM), :].T[0:BH_LANES]


def _val_untile(y, tt):
    t = y.shape[0] // HEAD_DIM
    return pl.pallas_call(
        functools.partial(_val_untile_kernel, tt=tt),
        grid=(t // tt,),
        in_specs=[pl.BlockSpec((tt * HEAD_DIM, LANES), lambda i: (i, 0))],
        out_specs=pl.BlockSpec((HEAD_DIM, BH_LANES, tt), lambda i: (0, 0, i)),
        out_shape=jax.ShapeDtypeStruct((HEAD_DIM, BH_LANES, t), F32),
        compiler_params=_cparams(("parallel",)),
        name="wkv_val_untile",
    )(y)


N_IO = HEAD_DIM // 8
N_JP = HEAD_DIM // 2
BH_LANES = 64


def _wkv_kernel(r_ref, w_ref, k_ref, v_ref, a_ref, b_ref, s0_ref, y_ref, sfin_ref, s_scr, *, tc):
    t_blk = pl.program_id(1)

    @pl.when(t_blk == 0)
    def _():
        s_scr[...] = s0_ref[0]

    def bc(ref, t, jp):
        return jnp.broadcast_to(ref[0, jp, pl.ds(t, 1), :], (8, LANES))

    def fold(x):
        return x + pltpu.roll(x, BH_LANES, 1)

    def step(t, carry):
        acc = [jnp.zeros((8, LANES), F32) for _ in range(N_IO)]
        for jp in range(N_JP):
            a_ = bc(a_ref, t, jp)
            for io in range(N_IO):
                acc[io] = acc[io] + s_scr[io, jp] * a_
        sa = [-fold(acc[io]) for io in range(N_IO)]
        vv = [v_ref[0, t, io * 8:(io + 1) * 8, :] for io in range(N_IO)]
        yacc = [jnp.zeros((8, LANES), F32) for _ in range(N_IO)]
        for jp in range(N_JP):
            w_ = bc(w_ref, t, jp)
            b_ = bc(b_ref, t, jp)
            k_ = bc(k_ref, t, jp)
            r_ = bc(r_ref, t, jp)
            for io in range(N_IO):
                s = s_scr[io, jp] * w_ + sa[io] * b_ + vv[io] * k_
                s_scr[io, jp] = s
                yacc[io] = yacc[io] + s * r_
        y = [fold(yacc[io]) for io in range(N_IO)]
        tot = y[0]
        for io in range(1, N_IO):
            tot = tot + y[io]
        mu = jnp.sum(tot, axis=0, keepdims=True) * (1.0 / HEAD_DIM)
        d = [y[io] - mu for io in range(N_IO)]
        sq = d[0] * d[0]
        for io in range(1, N_IO):
            sq = sq + d[io] * d[io]
        var = jnp.sum(sq, axis=0, keepdims=True) * (1.0 / HEAD_DIM)
        inv = lax.rsqrt(var + A_GN_EPS)
        for io in range(N_IO):
            y_ref[0, t, io * 8:(io + 1) * 8, :] = d[io] * inv
        return carry

    lax.fori_loop(0, tc, step, 0)

    @pl.when(t_blk == pl.num_programs(1) - 1)
    def _():
        sfin_ref[0] = s_scr[...]


def _wkv_scan(r, w, k, v, a, b, s0, tc):
    nb, t = v.shape[:2]
    kspec = pl.BlockSpec((1, N_JP, tc, LANES), lambda n, i: (n, 0, i, 0))
    vspec = pl.BlockSpec((1, tc, HEAD_DIM, LANES), lambda n, i: (n, i, 0, 0))
    sspec = pl.BlockSpec((1, N_IO, N_JP, 8, LANES), lambda n, i: (n, 0, 0, 0, 0))
    return pl.pallas_call(
        functools.partial(_wkv_kernel, tc=tc),
        grid=(nb, t // tc),
        in_specs=[kspec, kspec, kspec, vspec, kspec, kspec, sspec],
        out_specs=[vspec, sspec],
        out_shape=[jax.ShapeDtypeStruct((nb, t, HEAD_DIM, LANES), F32),
                   jax.ShapeDtypeStruct((nb, N_IO, N_JP, 8, LANES), F32)],
        scratch_shapes=[pltpu.VMEM((N_IO, N_JP, 8, LANES), F32)],
        compiler_params=_cparams(("parallel", "arbitrary")),
        name="wkv_scan",
    )(r, w, k, v, a, b, s0)


def _to_key_tiles(x, b, t):
    nb = b // 8
    x = x.reshape(nb, 8, t, A_HEADS, N_JP, 2).transpose(0, 4, 2, 5, 1, 3)
    return x.reshape(nb, N_JP, t, LANES)


def _to_val_tiles(x, b, t):
    nb = b // 8
    x = x.reshape(nb, 8, t, A_HEADS, HEAD_DIM).transpose(0, 2, 4, 1, 3).reshape(nb, t, HEAD_DIM, BH_LANES)
    return jnp.concatenate([x, x], axis=-1)


def _from_val_tiles(y, b, t):
    nb = b // 8
    y = y[..., :BH_LANES].reshape(nb, t, HEAD_DIM, 8, A_HEADS).transpose(0, 3, 1, 4, 2)
    return y.reshape(b * t, A_WIDTH)


def _state_to_tiles(s, b):
    nb = b // 8
    s = s.reshape(nb, 8, A_HEADS, N_IO, 8, N_JP, 2).transpose(0, 3, 5, 4, 6, 1, 2)
    return s.reshape(nb, N_IO, N_JP, 8, LANES)


def _state_from_tiles(s, b):
    nb = b // 8
    s = s.reshape(nb, N_IO, N_JP, 8, 2, 8, A_HEADS).transpose(0, 5, 6, 1, 3, 2, 4)
    return s.reshape(b, A_HEADS, HEAD_DIM, HEAD_DIM)


def _compress_paged_kernel(pt_ref, pe_ref, w1_ref, w2_ref, cache_ref, o_ref, buf, xk, xv, sem, *, n_pages, page_len):
    i = pl.program_id(0)
    slot = i % 2

    def copy(s_idx, sl, p):
        return pltpu.make_async_copy(cache_ref.at[pt_ref[s_idx * n_pages + p]], buf.at[sl, p], sem.at[sl, p])

    def fetch(s_idx, sl):
        for p in range(n_pages):
            copy(s_idx, sl, p).start()

    @pl.when(i == 0)
    def _():
        fetch(0, 0)

    @pl.when(i + 1 < pl.num_programs(0))
    def _():
        fetch(i + 1, 1 - slot)

    for p in range(n_pages):
        copy(i, slot, p).wait()
        t = buf[slot, p].T
        xk[p * page_len:(p + 1) * page_len, :] = t[:, :KV_WIDTH]
        xv[p * page_len:(p + 1) * page_len, :] = t[:, KV_WIDTH:]
    _compress_kernel(xk, xv, pe_ref, w1_ref, w2_ref, o_ref, nblk=n_pages * page_len // CMP_BLOCK)


def _compress_kernel(xk_ref, xv_ref, pe_ref, w1_ref, w2_ref, o_ref, *, nblk):
    for c, x_ref in enumerate((xk_ref, xv_ref)):
        acc = jnp.zeros((nblk, B_KV_HEADS * CMP_HIDDEN), F32)
        for tau in range(CMP_BLOCK):
            x = x_ref[pl.ds(tau, nblk, stride=CMP_BLOCK), :] + pe_ref[c, tau:tau + 1, :]
            acc = acc + _dot(_bf(x), w1_ref[c, tau])
        o_ref[:, c * KV_WIDTH:(c + 1) * KV_WIDTH] = _dot(_bf(jax.nn.gelu(acc)), w2_ref[c])


def _compress_weights(pe_k, pe_v, w_k1, w_k2, w_v1, w_v2):
    pe = jnp.stack([jnp.concatenate([pe_k, pe_k], axis=1), jnp.concatenate([pe_v, pe_v], axis=1)]).astype(F32)
    eye = jnp.eye(B_KV_HEADS, dtype=F32)
    w1 = jnp.stack([w_k1, w_v1]).reshape(2, CMP_BLOCK, HEAD_DIM, CMP_HIDDEN)
    w1 = jnp.einsum('ctdh,ge->ctgdeh', w1, eye).reshape(2, CMP_BLOCK, KV_WIDTH, B_KV_HEADS * CMP_HIDDEN)
    w2 = jnp.einsum('chd,ge->cghed', jnp.stack([w_k2, w_v2]), eye).reshape(2, B_KV_HEADS * CMP_HIDDEN, KV_WIDTH)
    return pe, _bf(w1), _bf(w2)


COMPRESS_W_SPECS = ((2, CMP_BLOCK, KV_WIDTH), (2, CMP_BLOCK, KV_WIDTH, B_KV_HEADS * CMP_HIDDEN),
                    (2, B_KV_HEADS * CMP_HIDDEN, KV_WIDTH))


def _compress_paged(cache_t, page_table, cw):
    s, n_pages = page_table.shape
    page = cache_t.shape[2]
    past = n_pages * page
    nblk = past // CMP_BLOCK
    grid_spec = pltpu.PrefetchScalarGridSpec(
        num_scalar_prefetch=1,
        grid=(s,),
        in_specs=[pl.BlockSpec(shp, lambda i, pt, n=len(shp): (0,) * n) for shp in COMPRESS_W_SPECS]
                 + [pl.BlockSpec(memory_space=pl.ANY)],
        out_specs=pl.BlockSpec((nblk, KV_ROW), lambda i, pt: (i, 0)),
        scratch_shapes=[pltpu.VMEM((2, n_pages, KV_ROW, page), F32), pltpu.VMEM((past, KV_WIDTH), F32),
                        pltpu.VMEM((past, KV_WIDTH), F32), pltpu.SemaphoreType.DMA((2, n_pages))],
    )
    return pl.pallas_call(
        functools.partial(_compress_paged_kernel, n_pages=n_pages, page_len=page),
        grid_spec=grid_spec,
        out_shape=jax.ShapeDtypeStruct((s * nblk, KV_ROW), F32),
        compiler_params=_cparams(("arbitrary",)),
        name="nsa_compress_paged",
    )(page_table.reshape(-1), *_compress_weights(*cw), cache_t)


def _compress(x, pe_k, pe_v, w_k1, w_k2, w_v1, w_v2, rows):
    n = x.shape[0]
    nblk = rows // CMP_BLOCK
    pe, w1, w2 = _compress_weights(pe_k, pe_v, w_k1, w_k2, w_v1, w_v2)
    return pl.pallas_call(
        functools.partial(_compress_kernel, nblk=nblk),
        grid=(n // rows,),
        in_specs=[pl.BlockSpec((rows, KV_WIDTH), lambda i: (i, 0)), pl.BlockSpec((rows, KV_WIDTH), lambda i: (i, 1)),
                  ] + [_full(shp) for shp in COMPRESS_W_SPECS],
        out_specs=pl.BlockSpec((nblk, KV_ROW), lambda i: (i, 0)),
        out_shape=jax.ShapeDtypeStruct((n // CMP_BLOCK, KV_ROW), F32),
        compiler_params=_cparams(("parallel",)),
        name="nsa_compress",
    )(x, x, pe, w1, w2)


def _rel_bucket(dist):
    n = jnp.maximum(dist, 0)
    log_ratio = jnp.log(jnp.maximum(n, 1).astype(F32) / MAX_EXACT) / math.log(MAX_DISTANCE / MAX_EXACT)
    large = jnp.minimum(MAX_EXACT + (log_ratio * (N_BUCKETS - MAX_EXACT)).astype(I32), N_BUCKETS - 1)
    return jnp.where(n < MAX_EXACT, n, large)


def _bias_of(rel_bias, dist):
    return jnp.moveaxis(rel_bias.astype(F32)[_rel_bucket(dist)], -1, 0)


def _masked_softmax_rows(s, ok):
    s = jnp.where(ok, s, NEG_INF)
    e = jnp.exp(s - jnp.max(s, axis=-1, keepdims=True))
    return e / jnp.sum(e, axis=-1, keepdims=True) * ok.astype(F32)


def _nsa_prompt_kernel(q_ref, gate_ref, ck_ref, sel_ref, win_ref, bct_ref, bc_ref, tz_ref, o_ref, imp_scr,
                       *, nc, ns, top_n, wtiles):
    i = pl.program_id(1)
    qb = Q_BLOCK
    gates = jax.nn.sigmoid(gate_ref[...])
    qi = lax.broadcasted_iota(I32, (qb, qb), 0)
    ki = lax.broadcasted_iota(I32, (qb, qb), 1)
    eye = _bf(qi == ki)
    pos_l = i * qb + lax.broadcasted_iota(I32, (1, qb), 1)
    pos_s = i * qb + lax.broadcasted_iota(I32, (qb, 1), 0)
    okT = pos_l >= lax.broadcasted_iota(I32, (nc, qb), 0) * CMP_BLOCK + (CMP_BLOCK - 1)
    ok = pos_s >= lax.broadcasted_iota(I32, (qb, nc), 1) * CMP_BLOCK + (CMP_BLOCK - 1)
    blk = lax.broadcasted_iota(I32, (ns, qb), 0)
    cur = pos_l // SEL_BLOCK
    forced = (blk == 0) | (blk == cur) | (blk == cur - 1)
    n_e = lax.broadcasted_iota(I32, (ns, qb), 0)
    k_e = lax.broadcasted_iota(I32, (ns, qb), 1) // SEL_BLOCK

    def flash(kv_ref, g, qg, lo, mask_fn):
        def body(j, carry):
            ms, ls, accs = carry
            kt = _bf(kv_ref[pl.ds(j * qb, qb), g * HEAD_DIM:(g + 1) * HEAD_DIM])
            vt = _bf(kv_ref[pl.ds(j * qb, qb), KV_WIDTH + g * HEAD_DIM:KV_WIDTH + (g + 1) * HEAD_DIM])
            mask = mask_fn(j)
            ms2, ls2, accs2 = [], [], []
            for r in range(B_GROUP):
                s = _dot_nt(qg[r], kt) + tz_ref[g * B_GROUP + r, i - j]
                s = jnp.where(mask, s, NEG_INF)
                m_new = jnp.maximum(ms[r], jnp.max(s, axis=-1, keepdims=True))
                alpha = jnp.exp(ms[r] - m_new)
                p = jnp.where(mask, jnp.exp(s - m_new), 0.0)
                ls2.append(alpha * ls[r] + jnp.sum(p, axis=-1, keepdims=True))
                accs2.append(alpha * accs[r] + _dot(_bf(p), vt))
                ms2.append(m_new)
            return tuple(ms2), tuple(ls2), tuple(accs2)

        init = (tuple(jnp.full((qb, 1), NEG_INF, F32) for _ in range(B_GROUP)),
                tuple(jnp.zeros((qb, 1), F32) for _ in range(B_GROUP)),
                tuple(jnp.zeros((qb, HEAD_DIM), F32) for _ in range(B_GROUP)))
        _, ls, accs = lax.fori_loop(lo, i + 1, body, init)
        return [accs[r] / ls[r] for r in range(B_GROUP)]

    outs = []
    for g in range(B_KV_HEADS):
        qg = [_bf(q_ref[:, (g * B_GROUP + r) * HEAD_DIM:(g * B_GROUP + r + 1) * HEAD_DIM] * SCALE)
              for r in range(B_GROUP)]
        kc = _bf(ck_ref[:, g * HEAD_DIM:(g + 1) * HEAD_DIM])
        vc = _bf(ck_ref[:, KV_WIDTH + g * HEAD_DIM:KV_WIDTH + (g + 1) * HEAD_DIM])
        o_c = []
        impT = jnp.zeros((nc, qb), F32)
        for r in range(B_GROUP):
            h = g * B_GROUP + r
            p = _masked_softmax_rows(_dot_nt(qg[r], kc) + bc_ref[h], ok)
            o_c.append(_dot(_bf(p), vc))
            sT = jnp.where(okT, _dot_nt(kc, qg[r]) + bct_ref[h], NEG_INF)
            eT = jnp.exp(sT - jnp.max(sT, axis=0, keepdims=True))
            impT = impT + eT / jnp.sum(eT, axis=0, keepdims=True) * okT.astype(F32)
        imp_scr[...] = impT
        imp2 = imp_scr[pl.ds(0, ns, stride=SEL_RATIO), :] + imp_scr[pl.ds(1, ns, stride=SEL_RATIO), :]
        score = jnp.where(blk <= cur, imp2 + FORCE_SCORE * forced.astype(F32), NEG_INF)
        rank = jnp.zeros((ns, qb), I32)
        for m in range(ns):
            row = score[m:m + 1, :]
            rank = rank + ((row > score) | ((row == score) & (m < blk))).astype(I32)
        selT = _bf((rank < top_n) & (score > NEG_INF / 2))
        sel = _bf(_dot_nt(eye, selT))

        def sel_mask(j):
            e_tile = _bf(n_e == 2 * j + k_e)
            m = _dot(sel, e_tile) > 0.5
            return m & ((j < i) | (qi >= ki))

        def win_mask(j):
            d = (i - j) * qb + qi - ki
            return (d >= 0) & (d < WINDOW)

        o_s = flash(sel_ref, g, qg, 0, sel_mask)
        o_w = flash(win_ref, g, qg, jnp.maximum(i - wtiles, 0), win_mask)
        for r in range(B_GROUP):
            h = g * B_GROUP + r
            outs.append(gates[:, 3 * h:3 * h + 1] * o_c[r] + gates[:, 3 * h + 1:3 * h + 2] * o_s[r]
                        + gates[:, 3 * h + 2:3 * h + 3] * o_w[r])
    o_ref[...] = jnp.concatenate(outs, axis=1)


def _nsa_prompt(q, gate, ckv, kv_sel, kv_win, rel_bias, b, t):
    nq = t // Q_BLOCK
    nc = t // CMP_BLOCK
    ns = -(-t // SEL_BLOCK)
    assert t % Q_BLOCK == 0 and nc == SEL_RATIO * ns and WINDOW % Q_BLOCK == 0
    pos = jnp.arange(t)
    cmp_end = jnp.arange(nc) * CMP_BLOCK + (CMP_BLOCK - 1)
    bc = _bias_of(rel_bias, pos[:, None] - cmp_end[None, :])
    bct = jnp.swapaxes(bc, 1, 2)
    dz = jnp.arange(nq)[:, None, None] * Q_BLOCK + jnp.arange(Q_BLOCK)[None, :, None] - jnp.arange(Q_BLOCK)[None, None, :]
    tz = _bias_of(rel_bias, dz)
    row = lambda w: pl.BlockSpec((Q_BLOCK, w), lambda bi, i: (bi * nq + i, 0))
    return pl.pallas_call(
        functools.partial(_nsa_prompt_kernel, nc=nc, ns=ns, top_n=min(TOP_N, ns), wtiles=WINDOW // Q_BLOCK),
        grid=(b, nq),
        in_specs=[row(B_WIDTH), row(GATE_PAD),
                  pl.BlockSpec((nc, KV_ROW), lambda bi, i: (bi, 0)),
                  pl.BlockSpec((t, KV_ROW), lambda bi, i: (bi, 0)),
                  pl.BlockSpec((t, KV_ROW), lambda bi, i: (bi, 0)),
                  pl.BlockSpec((B_HEADS, nc, Q_BLOCK), lambda bi, i: (0, 0, i)),
                  pl.BlockSpec((B_HEADS, Q_BLOCK, nc), lambda bi, i: (0, i, 0)),
                  _full((B_HEADS, nq, Q_BLOCK, Q_BLOCK))],
        out_specs=row(B_WIDTH),
        out_shape=jax.ShapeDtypeStruct((b * t, B_WIDTH), F32),
        scratch_shapes=[pltpu.VMEM((nc, Q_BLOCK), F32)],
        compiler_params=_cparams(("parallel", "arbitrary")),
        name="nsa_prompt",
    )(q, gate, ckv, kv_sel, kv_win, bct, bc, tz)


def _group_q(q_ref, g):
    rows = [q_ref[0, :, (g * B_GROUP + r) * HEAD_DIM:(g * B_GROUP + r + 1) * HEAD_DIM] for r in range(B_GROUP)]
    return _bf(jnp.concatenate(rows, axis=0) * SCALE)


def _nsa_sample_select_kernel(q_ref, gate_ref, ck_ref, bc_ref, pair_ref, tri_ref, oc_ref, idx_ref,
                              *, nsp, cur, top_n):
    gates = jax.nn.sigmoid(gate_ref[0])
    blk = lax.broadcasted_iota(I32, (1, nsp), 1)
    forced = (blk == 0) | (blk == cur) | (blk == cur - 1)
    mi = lax.broadcasted_iota(I32, (nsp, nsp), 0)
    ni = lax.broadcasted_iota(I32, (nsp, nsp), 1)
    kk = lax.broadcasted_iota(I32, (TOP_N, nsp), 0).astype(F32)
    nf = lax.broadcasted_iota(I32, (TOP_N, nsp), 1).astype(F32)
    outs = []
    for g in range(B_KV_HEADS):
        q4 = _group_q(q_ref, g)
        kc = _bf(ck_ref[:, g * HEAD_DIM:(g + 1) * HEAD_DIM])
        vc = _bf(ck_ref[:, KV_WIDTH + g * HEAD_DIM:KV_WIDTH + (g + 1) * HEAD_DIM])
        s = _dot_nt(q4, kc) + bc_ref[g * B_GROUP:(g + 1) * B_GROUP, :]
        e = jnp.exp(s - jnp.max(s, axis=-1, keepdims=True))
        p = e / jnp.sum(e, axis=-1, keepdims=True)
        o_c = _dot(_bf(p), vc)
        imp = ((p[0:1] + p[1:2]) + p[2:3]) + p[3:4]
        imp2 = jnp.dot(imp, pair_ref[...], precision=HIGHEST, preferred_element_type=F32)
        score = jnp.where(blk <= cur, imp2 + FORCE_SCORE * forced.astype(F32), NEG_INF)
        m1 = jnp.broadcast_to(score, (nsp, nsp))
        m2 = m1.T
        gt = (m2 > m1) | ((m2 == m1) & (mi < ni))
        rank = jnp.sum(gt.astype(F32), axis=0, keepdims=True)
        sel = (rank < top_n) & (score > NEG_INF / 2)
        before = _dot(_bf(sel), tri_ref[...])
        hit = jnp.broadcast_to(sel, (TOP_N, nsp)) & (jnp.broadcast_to(before, (TOP_N, nsp)) == kk)
        idx = jnp.sum(jnp.where(hit, nf, 0.0), axis=1, keepdims=True)
        cnt = jnp.sum(hit.astype(F32), axis=1, keepdims=True)
        idx_ref[0, g] = jnp.where(cnt > 0.5, idx, -1.0).astype(I32)
        for r in range(B_GROUP):
            h = g * B_GROUP + r
            outs.append(gates[:, 3 * h:3 * h + 1] * o_c[r:r + 1])
    oc_ref[0] = jnp.concatenate(outs, axis=1)


def _nsa_sample_select(q3, gate3, ckv, rel_bias, past_len):
    s = q3.shape[0]
    nc = past_len // CMP_BLOCK
    ns = -(-(past_len + 1) // SEL_BLOCK)
    nsp = -(-ns // LANES) * LANES
    cur = past_len // SEL_BLOCK
    bc = _bias_of(rel_bias, past_len - (jnp.arange(nc) * CMP_BLOCK + (CMP_BLOCK - 1)))
    pair = jnp.asarray(np.arange(nc)[:, None] // SEL_RATIO == np.arange(nsp)[None, :], F32)
    tri = jnp.asarray(np.arange(nsp)[:, None] < np.arange(nsp)[None, :], BF16)
    return pl.pallas_call(
        functools.partial(_nsa_sample_select_kernel, nsp=nsp, cur=cur, top_n=min(TOP_N, ns)),
        grid=(s,),
        in_specs=[pl.BlockSpec((1, 1, B_WIDTH), lambda i: (i, 0, 0)), pl.BlockSpec((1, 1, GATE_PAD), lambda i: (i, 0, 0)),
                  pl.BlockSpec((nc, KV_ROW), lambda i: (i, 0)), _full((B_HEADS, nc)), _full((nc, nsp)), _full((nsp, nsp))],
        out_specs=[pl.BlockSpec((1, 1, B_WIDTH), lambda i: (i, 0, 0)),
                   pl.BlockSpec((1, B_KV_HEADS, TOP_N, 1), lambda i: (i, 0, 0, 0))],
        out_shape=[jax.ShapeDtypeStruct((s, 1, B_WIDTH), F32), jax.ShapeDtypeStruct((s, B_KV_HEADS, TOP_N, 1), I32)],
        compiler_params=_cparams(("parallel",)),
        name="nsa_sample_select",
    )(q3, gate3, ckv, bc, pair, tri)


def _nsa_sample_attend_kernel(idx_ref, pt_ref, q_ref, gate_ref, oc_ref, ksel_ref, kwin_ref, win_ref, fpg_ref, bw_ref,
                              cache_ref, o_ref, buf, sem, *, cur, n_pages, past_len, page_len):
    i = pl.program_id(0)
    n_s = pl.num_programs(0)
    slot = i % 2
    halves = PAGE_BLOCKS
    n_slot = B_KV_HEADS * TOP_N

    def block_of(s_idx, j):
        return idx_ref[s_idx * n_slot + j]

    def copy(s_idx, sl, j):
        n = jnp.clip(block_of(s_idx, j), 0, cur - 1)
        page = pt_ref[s_idx * n_pages + n // halves]
        return pltpu.make_async_copy(cache_ref.at[page], buf.at[sl, j], sem.at[sl])

    def cached(s_idx, j):
        n = block_of(s_idx, j)
        return (n >= 0) & (n < cur)

    def fetch(s_idx, sl):
        for j in range(n_slot):
            @pl.when(cached(s_idx, j))
            def _():
                copy(s_idx, sl, j).start()

    @pl.when(i == 0)
    def _():
        fetch(0, 0)

    @pl.when(i + 1 < n_s)
    def _():
        fetch(i + 1, 1 - slot)

    for j in range(n_slot):
        @pl.when(cached(i, j))
        def _():
            copy(i, slot, j).wait()

        @pl.when(jnp.logical_not(cached(i, j)))
        def _():
            buf[slot, j] = jnp.zeros((KV_ROW, page_len), F32)
            buf[slot, j, :, 0:1] = ksel_ref[0]

    gates = jax.nn.sigmoid(gate_ref[0])
    t_l = lax.broadcasted_iota(I32, (1, page_len), 1)
    n_buf = win_ref.shape[2]
    j_w = lax.broadcasted_iota(I32, (1, n_buf), 1)
    d_w = n_buf - j_w
    ok_w = (d_w >= 0) & (d_w < WINDOW) & (past_len - d_w >= 0)
    f0 = fpg_ref[n_pages]
    outs = []
    for g in range(B_KV_HEADS):
        q4 = _group_q(q_ref, g)
        ksl = slice(g * HEAD_DIM, (g + 1) * HEAD_DIM)
        vsl = slice(KV_WIDTH + g * HEAD_DIM, KV_WIDTH + (g + 1) * HEAD_DIM)
        pieces = []
        for k in range(TOP_N):
            j = g * TOP_N + k
            n = block_of(i, j)
            pg = jnp.clip(n, 0, cur) // halves
            fb = fpg_ref[pg][g * B_GROUP:(g + 1) * B_GROUP, :]
            s_k = _dot(q4, _bf(buf[slot, j, ksl, :])) + fb
            ok = (n >= 0) & (t_l // SEL_BLOCK == n % halves) & (pg * page_len + t_l <= past_len)
            pieces.append(jnp.where(ok, s_k, NEG_INF))
        s = jnp.concatenate(pieces, axis=1)
        e = jnp.exp(s - jnp.max(s, axis=-1, keepdims=True))
        p = _bf(e / jnp.sum(e, axis=-1, keepdims=True))
        o_s = jnp.zeros((B_GROUP, HEAD_DIM), F32)
        for k in range(TOP_N):
            o_s = o_s + _dot_nt(p[:, k * page_len:(k + 1) * page_len], _bf(buf[slot, g * TOP_N + k, vsl, :]))
        kw = _bf(win_ref[0, ksl, :])
        vw = _bf(win_ref[0, vsl, :])
        s_w = jnp.where(ok_w, _dot(q4, kw) + bw_ref[g * B_GROUP:(g + 1) * B_GROUP, :], NEG_INF)
        k_new = _bf(kwin_ref[0, :, ksl])
        v_new = _bf(kwin_ref[0, :, vsl])
        s_n = (jnp.sum(q4.astype(F32) * k_new.astype(F32), axis=-1, keepdims=True)
               + f0[g * B_GROUP:(g + 1) * B_GROUP, 0:1])
        m = jnp.maximum(jnp.max(s_w, axis=-1, keepdims=True), s_n)
        e_w = jnp.exp(s_w - m)
        e_n = jnp.exp(s_n - m)
        den = jnp.sum(e_w, axis=-1, keepdims=True) + e_n
        o_w = _dot_nt(_bf(e_w / den), vw) + _bf(e_n / den).astype(F32) * v_new.astype(F32)
        for r in range(B_GROUP):
            h = g * B_GROUP + r
            outs.append(gates[:, 3 * h + 1:3 * h + 2] * o_s[r:r + 1] + gates[:, 3 * h + 2:3 * h + 3] * o_w[r:r + 1])
    o_ref[0] = oc_ref[0] + jnp.concatenate(outs, axis=1)


PAGE_BLOCKS = 2


def _nsa_sample_attend(idx, page_table, q3, gate3, oc3, ksel_t, kwin3, win_t, cache_t, rel_bias, past_len):
    s = q3.shape[0]
    n_pages = page_table.shape[1]
    cur = past_len // SEL_BLOCK
    n_buf = win_t.shape[2]
    page = cache_t.shape[2]
    assert page == PAGE_BLOCKS * SEL_BLOCK and past_len == n_pages * page
    keypos = jnp.arange(n_pages + 1)[:, None] * page + jnp.arange(page)[None, :]
    fpg = jnp.moveaxis(_bias_of(rel_bias, past_len - keypos), 0, 1)
    bw = _bias_of(rel_bias, n_buf - jnp.arange(n_buf))
    row3 = lambda w: pl.BlockSpec((1, 1, w), lambda i, *_: (i, 0, 0))
    grid_spec = pltpu.PrefetchScalarGridSpec(
        num_scalar_prefetch=2,
        grid=(s,),
        in_specs=[row3(B_WIDTH), row3(GATE_PAD), row3(B_WIDTH),
                  pl.BlockSpec((1, KV_ROW, 1), lambda i, *_: (i, 0, 0)), row3(KV_ROW),
                  pl.BlockSpec((1, KV_ROW, n_buf), lambda i, *_: (i, 0, 0)),
                  pl.BlockSpec((n_pages + 1, B_HEADS, page), lambda i, *_: (0, 0, 0)),
                  pl.BlockSpec((B_HEADS, n_buf), lambda i, *_: (0, 0)),
                  pl.BlockSpec(memory_space=pl.ANY)],
        out_specs=row3(B_WIDTH),
        scratch_shapes=[pltpu.VMEM((2, B_KV_HEADS * TOP_N, KV_ROW, page), F32), pltpu.SemaphoreType.DMA((2,))],
    )
    return pl.pallas_call(
        functools.partial(_nsa_sample_attend_kernel, cur=cur, n_pages=n_pages, past_len=past_len, page_len=page),
        grid_spec=grid_spec,
        out_shape=jax.ShapeDtypeStruct((s, 1, B_WIDTH), F32),
        compiler_params=_cparams(("arbitrary",)),
        name="nsa_sample_attend",
    )(idx.reshape(-1), page_table.reshape(-1), q3, gate3, oc3, ksel_t, kwin3, win_t, fpg, bw, cache_t)


def _merge_kernel(x_ref, yn_ref, bo_ref, ga_ref, yb_ref, pg_ref, lnw_ref, lnb_ref, wa_ref, wb_ref, wo_ref, nf_ref,
                  wr_ref, br_ref, x1_ref, h_ref, te_ref, tw_ref, *, a_transposed):
    if a_transposed:
        tm = x_ref.shape[0]
        yn = yn_ref[...].reshape(A_WIDTH, tm)
        y_a = ((yn * lnw_ref[...] + lnb_ref[...] + bo_ref[0]) * ga_ref[0]).T
    else:
        y_a = (yn_ref[...] * lnw_ref[...] + lnb_ref[...] + bo_ref[...]) * ga_ref[...]
    g_a = jax.nn.sigmoid(pg_ref[:, :D_MODEL])
    g_b = jax.nn.sigmoid(pg_ref[:, D_MODEL:])
    m = g_a * _dot(_bf(y_a), wa_ref[...]) + g_b * _dot(_bf(yb_ref[...]), wb_ref[...])
    x1 = x_ref[...] + _dot(_bf(m), wo_ref[...])
    x1_ref[...] = x1
    h = x1 * lax.rsqrt(jnp.mean(x1 * x1, axis=-1, keepdims=True) + NORM_EPS) * nf_ref[...]
    h_ref[...] = _bf(h)
    logits = jnp.dot(h, wr_ref[...], precision=HIGHEST, preferred_element_type=F32) + br_ref[...]
    lane = lax.broadcasted_iota(I32, logits.shape, 1).astype(F32)
    col = lax.broadcasted_iota(I32, (logits.shape[0], TOP_K), 1)
    vals, idxs = [], []
    for _ in range(TOP_K):
        m_k = jnp.max(logits, axis=-1, keepdims=True)
        i_k = jnp.min(jnp.where(logits == m_k, lane, float(N_EXPERTS)), axis=-1, keepdims=True)
        vals.append(m_k)
        idxs.append(i_k)
        logits = jnp.where(lane == i_k, -jnp.inf, logits)
    e = [jnp.exp(v - vals[0]) for v in vals]
    den = ((e[0] + e[1]) + e[2]) + e[3]
    te = jnp.zeros(col.shape, F32)
    tw = jnp.zeros(col.shape, F32)
    for k in range(TOP_K):
        te = jnp.where(col == k, idxs[k], te)
        tw = jnp.where(col == k, e[k] / den, tw)
    te_ref[...] = te.astype(I32)
    tw_ref[...] = tw


def _merge(x, yn, bonus, gate, yb, pg, ln_w, ln_b, w_a, w_b, w_o, norm_ffn, w_router, b_router, tm, seq=None):
    rows = x.shape[0]
    row = lambda w: pl.BlockSpec((tm, w), lambda i: (i, 0))
    vec = lambda z: z.reshape(1, -1).astype(F32)
    if seq is None:
        a_specs = [row(A_WIDTH)] * 3 + [_full((1, A_WIDTH))] * 2
        lnw, lnb = vec(ln_w), vec(ln_b)
    else:
        per = seq // tm
        nspec = pl.BlockSpec((1, A_WIDTH, tm), lambda i: (i // per, 0, i % per))
        a_specs = [pl.BlockSpec((HEAD_DIM, 1, A_HEADS, tm), lambda i: (0, i // per, 0, i % per)), nspec, nspec,
                   _full((A_WIDTH, 1)), _full((A_WIDTH, 1))]
        lnw, lnb = (z[CH_MAJOR].reshape(-1, 1).astype(F32) for z in (ln_w, ln_b))
        w_a = w_a[CH_MAJOR]
    return pl.pallas_call(
        functools.partial(_merge_kernel, a_transposed=seq is not None),
        grid=(rows // tm,),
        in_specs=[row(D_MODEL)] + a_specs[:3] + [row(B_WIDTH), row(GATE_PROJ)] + a_specs[3:]
                 + [_full((A_WIDTH, D_MODEL)), _full((B_WIDTH, D_MODEL)),
                  _full((D_MODEL, D_MODEL)), _full((1, D_MODEL)), _full((D_MODEL, N_EXPERTS)), _full((1, N_EXPERTS))],
        out_specs=[row(D_MODEL), row(D_MODEL), row(TOP_K), row(TOP_K)],
        out_shape=[jax.ShapeDtypeStruct((rows, D_MODEL), F32), jax.ShapeDtypeStruct((rows, D_MODEL), BF16),
                   jax.ShapeDtypeStruct((rows, TOP_K), I32), jax.ShapeDtypeStruct((rows, TOP_K), F32)],
        compiler_params=_cparams(("parallel",)),
        name="merge_router",
    )(x, yn, bonus, gate, yb, pg, lnw, lnb, _bf(w_a), _bf(w_b), _bf(w_o), vec(norm_ffn),
      w_router.astype(F32), vec(b_router))


def _moe_kernel(ce_ref, nu_ref, x_ref, sw_ref, wu_ref, bu_ref, wd_ref, bd_ref, o_ref, wu_bf, wd_bf):
    c = pl.program_id(0)
    e = ce_ref[c]
    prev = ce_ref[jnp.maximum(c - 1, 0)]

    @pl.when((c == 0) | (e != prev))
    def _():
        wu_bf[...] = _bf(wu_ref[0])
        wd_bf[...] = _bf(wd_ref[0])

    @pl.when(c < nu_ref[0])
    def _():
        u = _dot(x_ref[...], wu_bf[...]) + bu_ref[0]
        glu = jnp.minimum(u[:, :D_FF], SWIGLU_LIMIT)
        lin = jnp.clip(u[:, D_FF:], -SWIGLU_LIMIT, SWIGLU_LIMIT)
        act = glu * jax.nn.sigmoid(SWIGLU_ALPHA * glu) * (lin + 1.0)
        o_ref[...] = (_dot(_bf(act), wd_bf[...]) + bd_ref[0]) * sw_ref[...]

    @pl.when(c >= nu_ref[0])
    def _():
        o_ref[...] = jnp.zeros_like(o_ref)


def _moe_experts(chunk_e, n_used, xs, slot_w, w_up, b_up, w_down, b_down):
    slots = xs.shape[0]
    n_chunks = slots // MOE_ROWS
    grid_spec = pltpu.PrefetchScalarGridSpec(
        num_scalar_prefetch=2,
        grid=(n_chunks,),
        in_specs=[pl.BlockSpec((MOE_ROWS, D_MODEL), lambda c, ce, nu: (c, 0)),
                  pl.BlockSpec((MOE_ROWS, 1), lambda c, ce, nu: (c, 0)),
                  pl.BlockSpec((1, D_MODEL, 2 * D_FF), lambda c, ce, nu: (ce[c], 0, 0)),
                  pl.BlockSpec((1, 1, 2 * D_FF), lambda c, ce, nu: (ce[c], 0, 0)),
                  pl.BlockSpec((1, D_FF, D_MODEL), lambda c, ce, nu: (ce[c], 0, 0)),
                  pl.BlockSpec((1, 1, D_MODEL), lambda c, ce, nu: (ce[c], 0, 0))],
        out_specs=pl.BlockSpec((MOE_ROWS, D_MODEL), lambda c, ce, nu: (c, 0)),
        scratch_shapes=[pltpu.VMEM((D_MODEL, 2 * D_FF), BF16), pltpu.VMEM((D_FF, D_MODEL), BF16)],
    )
    return pl.pallas_call(
        _moe_kernel,
        grid_spec=grid_spec,
        out_shape=jax.ShapeDtypeStruct((slots, D_MODEL), F32),
        compiler_params=_cparams(("arbitrary",)),
        name="moe_experts",
    )(chunk_e, n_used, xs, slot_w, w_up, b_up.reshape(N_EXPERTS, 1, -1), w_down, b_down.reshape(N_EXPERTS, 1, -1))


def _moe(h, top_e, top_w, w_up, b_up, w_down, b_down):
    n = h.shape[0]
    n_assign = n * TOP_K
    flat_e = top_e.reshape(-1)
    counts = jnp.bincount(flat_e, length=N_EXPERTS)
    padded = (counts + MOE_ROWS - 1) // MOE_ROWS * MOE_ROWS
    pad_end = jnp.cumsum(padded)
    pad_start = pad_end - padded
    grp_start = jnp.cumsum(counts) - counts
    order = jnp.argsort(flat_e, stable=True)
    se = flat_e[order]
    dest = (pad_start[se] + jnp.arange(n_assign) - grp_start[se]).astype(I32)
    n_chunks = -(-(n_assign + N_EXPERTS * (MOE_ROWS - 1)) // MOE_ROWS)
    slots = n_chunks * MOE_ROWS
    slot_tok = jnp.full((slots,), n, I32).at[dest].set((order // TOP_K).astype(I32))
    slot_w = jnp.zeros((slots,), F32).at[dest].set(top_w.reshape(-1)[order])
    chunk_e = jnp.minimum(jnp.searchsorted(pad_end, jnp.arange(n_chunks) * MOE_ROWS, side='right'),
                          N_EXPERTS - 1).astype(I32)
    n_used = (pad_end[-1] // MOE_ROWS).astype(I32).reshape(1)
    h_pad = jnp.concatenate([h, jnp.zeros((1, D_MODEL), h.dtype)], axis=0)
    xs = h_pad[slot_tok]
    ys = _moe_experts(chunk_e, n_used, xs, slot_w.reshape(-1, 1), w_up, b_up, w_down, b_down)
    slot_of = jnp.zeros((n_assign,), I32).at[order].set(dest)
    return ys[slot_of.reshape(n, TOP_K)].sum(axis=1)


def _final_kernel(x_ref, m_ref, g_ref, o_ref):
    x = x_ref[...] + m_ref[...]
    o_ref[...] = x * lax.rsqrt(jnp.mean(x * x, axis=-1, keepdims=True) + NORM_EPS) * g_ref[...]


def _final(x1, moe, g, tm):
    rows = x1.shape[0]
    row = pl.BlockSpec((tm, D_MODEL), lambda i: (i, 0))
    return pl.pallas_call(
        _final_kernel,
        grid=(rows // tm,),
        in_specs=[row, row, _full((1, D_MODEL))],
        out_specs=row,
        out_shape=jax.ShapeDtypeStruct((rows, D_MODEL), F32),
        compiler_params=_cparams(("parallel",)),
        name="final_norm",
    )(x1, moe, g.reshape(1, -1).astype(F32))


def _row_tile(rows, cap):
    tm = cap
    while rows % tm:
        tm //= 2
    return tm


def _rwkv_group(p_a, prev, s0_tiles, b, t, rw, tc):
    mu, w0, wdu, a0, wiu, wgu, k_k, k_a, r_k = rw
    tm = _row_tile(b * t, 256)
    r, w, k2, v, kk, kb, gate, bonus = _rwkv_prep(p_a, prev, mu, w0, wdu, a0, wiu, wgu, k_k, k_a, r_k, tm)
    kt = lambda z: _to_key_tiles(z, b, t)
    yn, s_fin = _wkv_scan(kt(r), kt(w), kt(k2), _to_val_tiles(v, b, t), kt(kk), kt(kb), s0_tiles, tc)
    return _from_val_tiles(yn, b, t), gate, bonus, _state_from_tiles(s_fin, b)


def kernel(x_prompt, x_sample, cache_cmp_kv, cache_sel_kv, state_win_kv, state_rwkv, state_rwkv_shift, page_table,
           norm_attn, w_in, mu_shift, w0, w_decay_up, a0, w_iclr_up, w_gate_up, k_k, k_a, r_k, ln_x_w, ln_x_b,
           pe_cmp_k, pe_cmp_v, w_cmp_k1, w_cmp_k2, w_cmp_v1, w_cmp_v2, rel_bias, w_br_a, w_br_b, w_out,
           norm_ffn, w_router, b_router, w_up, b_up, w_down, b_down, norm_final):
    bp, tp, _ = x_prompt.shape
    bs, ts, _ = x_sample.shape
    depth = w_in.shape[0]
    past_len = page_table.shape[1] * cache_cmp_kv.shape[2]
    n_buf = state_win_kv.shape[2]
    assert ts == 1 and bp * A_HEADS == BH_LANES and bs % 8 == 0 and tp % LANES == 0
    xp = x_prompt.reshape(bp * tp, D_MODEL)
    xs = x_sample.reshape(bs, D_MODEL)
    new = {name: [] for name in ('cmp_p', 'sel_p', 'win_p', 'wkv_p', 'shift_p', 'cmp_s', 'sel_s', 'win_s', 'wkv_s', 'shift_s')}
    kv6 = lambda z, b, t: z.reshape(b, t, 2, B_KV_HEADS, HEAD_DIM)
    for l in range(depth):
        rw = (mu_shift[l], w0[l], w_decay_up[l], a0[l], w_iclr_up[l], w_gate_up[l], k_k[l], k_a[l], r_k[l])
        cw = (pe_cmp_k[l], pe_cmp_v[l], w_cmp_k1[l], w_cmp_k2[l], w_cmp_v1[l], w_cmp_v2[l])
        wa_nat, wa_t, w_rest = _pack_w_in(w_in[l])
        g_attn = norm_attn[l].reshape(1, -1).astype(F32)

        tq = _row_tile(tp, 256)
        pa_t, q, kv_c, kv_s, kv_w, gt, pg = _project(xp, g_attn, wa_t, w_rest, tq, seq=tp)
        r, w, k2, v, kk, kb, gate, bonus = _rwkv_prep_t(pa_t, *rw, tq)
        r, w, k2, kk, kb = _key_tiles([z.reshape(N_JP, LANES, tp) for z in (r, w, k2, kk, kb)], LANES)
        v = _val_tiles(v.reshape(HEAD_DIM, BH_LANES, tp), LANES).reshape(1, tp, HEAD_DIM, LANES)
        s0 = jnp.zeros((1, N_IO, N_JP, 8, LANES), F32)
        yn, s_fin = _wkv_scan(r, w, k2, v, kk, kb, s0, _row_tile(tp, 32))
        yn = _val_untile(yn.reshape(tp * HEAD_DIM, LANES), LANES).reshape(HEAD_DIM, bp, A_HEADS, tp)
        ckv = _compress(kv_c, *cw, rows=_row_tile(tp, 2048))
        y_b = _nsa_prompt(q, gt, ckv, kv_s, kv_w, rel_bias, bp, tp)
        x1, h, top_e, top_w = _merge(xp, yn, bonus, gate, y_b, pg, ln_x_w[l], ln_x_b[l], w_br_a[l], w_br_b[l], w_out[l],
                                     norm_ffn[l], w_router[l], b_router[l], tq, seq=tp)
        moe = _moe(h, top_e, top_w, w_up[l], b_up[l], w_down[l], b_down[l])
        xp_next = (x1, moe)
        n_win = min(WINDOW, tp)
        new['cmp_p'].append(kv6(kv_c, bp, tp))
        new['sel_p'].append(kv6(kv_s, bp, tp))
        new['win_p'].append(kv6(kv_w, bp, tp)[:, tp - n_win:])
        new['wkv_p'].append(_state_from_tiles(s_fin, bp))
        new['shift_p'].append(jnp.zeros((bp, A_PROJ), F32).at[:, A_PERM].set(pa_t[:, :, tp - 1]))

        s_a, q, kv_c, kv_s, kv_w, gt, pg = _project(xs, g_attn, wa_nat, w_rest, _row_tile(bs, 256))
        yn, gate, bonus, wkv_s = _rwkv_group(s_a, state_rwkv_shift[l], _state_to_tiles(state_rwkv[l].astype(F32), bs),
                                             bs, 1, rw, 1)
        tok_minor = lambda z: jnp.moveaxis(z, 1, -1).reshape(z.shape[0], KV_ROW, z.shape[1])
        ckv = _compress_paged(tok_minor(cache_cmp_kv[l]), page_table, cw)
        q3, gt3 = q.reshape(bs, 1, B_WIDTH), gt.reshape(bs, 1, GATE_PAD)
        oc3, idx = _nsa_sample_select(q3, gt3, ckv, rel_bias, past_len)
        y_b = _nsa_sample_attend(idx, page_table, q3, gt3, oc3, kv_s.reshape(bs, KV_ROW, 1), kv_w.reshape(bs, 1, KV_ROW),
                                 tok_minor(state_win_kv[l]), tok_minor(cache_sel_kv[l]),
                                 rel_bias, past_len).reshape(bs, B_WIDTH)
        x1, h, top_e, top_w = _merge(xs, yn, bonus, gate, y_b, pg, ln_x_w[l], ln_x_b[l], w_br_a[l], w_br_b[l], w_out[l],
                                     norm_ffn[l], w_router[l], b_router[l], _row_tile(bs, 256))
        moe = _moe(h, top_e, top_w, w_up[l], b_up[l], w_down[l], b_down[l])
        xs_next = (x1, moe)
        new['cmp_s'].append(kv6(kv_c, bs, 1))
        new['sel_s'].append(kv6(kv_s, bs, 1))
        new['win_s'].append(jnp.concatenate([state_win_kv[l], kv6(kv_w, bs, 1)], axis=1)[:, 1:])
        new['wkv_s'].append(wkv_s.astype(state_rwkv.dtype))
        new['shift_s'].append(s_a)

        last = l == depth - 1
        g_fin = norm_final if last else jnp.ones((D_MODEL,), F32)
        if last:
            xp = _final(*xp_next, g_fin, _row_tile(bp * tp, 512))
            xs = _final(*xs_next, g_fin, _row_tile(bs, 512))
        else:
            xp = xp_next[0] + xp_next[1]
            xs = xs_next[0] + xs_next[1]
    return (xp.reshape(bp, tp, D_MODEL), xs.reshape(bs, ts, D_MODEL),
            jnp.stack(new['cmp_p']), jnp.stack(new['sel_p']), jnp.stack(new['win_p']),
            jnp.stack(new['wkv_p']), jnp.stack(new['shift_p']),
            jnp.stack(new['cmp_s']), jnp.stack(new['sel_s']), jnp.stack(new['win_s']),
            jnp.stack(new['wkv_s']), jnp.stack(new['shift_s']))
```

```python
import functools
import math

import jax
import jax.numpy as jnp
import numpy as np
from jax import lax
from jax.experimental import pallas as pl
from jax.experimental.pallas import tpu as pltpu

F32 = jnp.float32
BF16 = jnp.bfloat16
I32 = jnp.int32
HIGHEST = lax.Precision.HIGHEST

D_MODEL = 1024
HEAD_DIM = 64
A_HEADS = 8
A_WIDTH = A_HEADS * HEAD_DIM
A_DECAY_RANK = 64
A_ICLR_RANK = 64
A_GATE_RANK = 128
A_GN_EPS = 64e-5
A_PROJ = 3 * A_WIDTH + A_DECAY_RANK + A_ICLR_RANK + A_GATE_RANK
B_HEADS = 8
B_KV_HEADS = 2
B_GROUP = B_HEADS // B_KV_HEADS
B_WIDTH = B_HEADS * HEAD_DIM
KV_WIDTH = B_KV_HEADS * HEAD_DIM
KV_ROW = 2 * KV_WIDTH
CMP_BLOCK = 32
CMP_HIDDEN = 128
SEL_BLOCK = 64
SEL_RATIO = SEL_BLOCK // CMP_BLOCK
TOP_N = 16
WINDOW = 512
Q_BLOCK = 128
FORCE_SCORE = 1e4
N_BUCKETS = 32
MAX_EXACT = N_BUCKETS // 2
MAX_DISTANCE = 1024
N_EXPERTS = 32
TOP_K = 4
D_FF = 1024
SWIGLU_ALPHA = 1.702
SWIGLU_LIMIT = 7.0
NORM_EPS = 1e-6
NEG_INF = -1e30
SCALE = HEAD_DIM ** -0.5
GATE_PAD = 128
GATE_PROJ = 2 * D_MODEL
MOE_ROWS = 256
LANES = 128
VMEM_LIMIT = 56 * 1024 * 1024


def _cparams(sem):
    return pltpu.CompilerParams(dimension_semantics=sem, vmem_limit_bytes=VMEM_LIMIT)


def _full(shape):
    n = len(shape)
    return pl.BlockSpec(shape, lambda *_: (0,) * n)


def _dot(a, b):
    return jnp.dot(a, b, preferred_element_type=F32)


def _dot_nt(a, b):
    return lax.dot_general(a, b, (((1,), (1,)), ((), ())), preferred_element_type=F32)


def _bf(x):
    return x.astype(BF16)


PROJ_SPLITS = (B_WIDTH, KV_ROW, KV_ROW, KV_ROW, GATE_PAD, GATE_PROJ)
CH_MAJOR = (np.arange(A_HEADS)[None, :] * HEAD_DIM + np.arange(HEAD_DIM)[:, None]).reshape(-1)
A_PERM = np.concatenate([CH_MAJOR, A_WIDTH + CH_MAJOR, 2 * A_WIDTH + CH_MAJOR, np.arange(3 * A_WIDTH, A_PROJ)])


def _rms_bf16(x_ref, g_ref):
    x = x_ref[...]
    return _bf(x * lax.rsqrt(jnp.mean(x * x, axis=-1, keepdims=True) + NORM_EPS) * g_ref[...])


def _proj_kernel(x_ref, g_ref, wa_ref, w_ref, oa_ref, *o_refs):
    h = _rms_bf16(x_ref, g_ref)
    oa_ref[...] = _dot(h, wa_ref[...])
    c = 0
    for o_ref, n in zip(o_refs, PROJ_SPLITS):
        o_ref[...] = _dot(h, w_ref[:, c:c + n])
        c += n


def _project(x, g, wa, w, tm):
    rows = x.shape[0]
    splits = (A_PROJ,) + PROJ_SPLITS
    return pl.pallas_call(
        _proj_kernel,
        grid=(rows // tm,),
        in_specs=[pl.BlockSpec((tm, D_MODEL), lambda i: (i, 0)), _full((1, D_MODEL)), _full(wa.shape), _full(w.shape)],
        out_specs=[pl.BlockSpec((tm, n), lambda i: (i, 0)) for n in splits],
        out_shape=[jax.ShapeDtypeStruct((rows, n), F32) for n in splits],
        compiler_params=_cparams(("parallel",)),
        name="norm_proj",
    )(x, g, wa, w)


PROJ_T_SPLITS = (A_PROJ, B_WIDTH, KV_ROW, KV_ROW, KV_ROW, GATE_PAD)
PROJ_N_SPLITS = (KV_ROW, KV_ROW, KV_ROW, GATE_PROJ)


def _proj_t_kernel(x_ref, g_ref, wt_ref, wn_ref, *o_refs):
    h = _rms_bf16(x_ref, g_ref)
    t = _dot_nt(wt_ref[...], h)
    c = 0
    for o_ref, n in zip(o_refs[:len(PROJ_T_SPLITS)], PROJ_T_SPLITS):
        o_ref[0] = t[c:c + n]
        c += n
    c = 0
    for o_ref, n in zip(o_refs[len(PROJ_T_SPLITS):], PROJ_N_SPLITS):
        o_ref[...] = _dot(h, wn_ref[:, c:c + n])
        c += n


def _project_t(x, g, wt, wn, tm, seq):
    rows = x.shape[0]
    per = seq // tm
    return pl.pallas_call(
        _proj_t_kernel,
        grid=(rows // tm,),
        in_specs=[pl.BlockSpec((tm, D_MODEL), lambda i: (i, 0)), _full((1, D_MODEL)), _full(wt.shape), _full(wn.shape)],
        out_specs=[pl.BlockSpec((1, n, tm), lambda i: (i // per, 0, i % per)) for n in PROJ_T_SPLITS]
                  + [pl.BlockSpec((tm, n), lambda i: (i, 0)) for n in PROJ_N_SPLITS],
        out_shape=[jax.ShapeDtypeStruct((rows // seq, n, seq), F32) for n in PROJ_T_SPLITS]
                  + [jax.ShapeDtypeStruct((rows, n), F32) for n in PROJ_N_SPLITS],
        compiler_params=_cparams(("parallel",)),
        name="norm_proj_t",
    )(x, g, wt, wn)


def _pack_w_in(w_in):
    a, rest = w_in[:, :A_PROJ], w_in[:, A_PROJ:]
    q, kv, gt, pg = (rest[:, :B_WIDTH], rest[:, B_WIDTH:B_WIDTH + 3 * KV_ROW],
                     rest[:, B_WIDTH + 3 * KV_ROW:B_WIDTH + 3 * KV_ROW + 3 * B_HEADS],
                     rest[:, B_WIDTH + 3 * KV_ROW + 3 * B_HEADS:])
    gt = jnp.pad(gt, ((0, 0), (0, GATE_PAD - 3 * B_HEADS)))
    w_t = jnp.concatenate([a[:, A_PERM], q, kv, gt], axis=1).T
    return _bf(a), _bf(jnp.concatenate([q, kv, gt, pg], axis=1)), _bf(w_t), _bf(jnp.concatenate([kv, pg], axis=1))


def _softplus(z):
    return jnp.maximum(z, 0.0) + jnp.log1p(jnp.exp(-jnp.abs(z)))


def _rwkv_prep_kernel(p_ref, prev_ref, mu_ref, w0_ref, wdu_ref, a0_ref, wiu_ref, wgu_ref, kk_ref, ka_ref, rk_ref,
                      ones_ref, r_o, w_o, k_o, v_o, kk_o, kb_o, g_o, bo_o):
    p = p_ref[...]
    ps = p + mu_ref[...] * (prev_ref[...] - p)
    r = ps[:, 0:A_WIDTH]
    k = ps[:, A_WIDTH:2 * A_WIDTH]
    v = ps[:, 2 * A_WIDTH:3 * A_WIDTH]
    c = 3 * A_WIDTH
    xw = ps[:, c:c + A_DECAY_RANK]
    xa = ps[:, c + A_DECAY_RANK:c + A_DECAY_RANK + A_ICLR_RANK]
    xg = ps[:, c + A_DECAY_RANK + A_ICLR_RANK:]
    w_log = -_softplus(-(w0_ref[...] + _dot(_bf(jnp.tanh(xw)), wdu_ref[...]))) - 0.5
    decay = jnp.exp(-jnp.exp(w_log))
    a = jax.nn.sigmoid(a0_ref[...] + _dot(_bf(xa), wiu_ref[...]))
    gate = _dot(_bf(jax.nn.sigmoid(xg)), wgu_ref[...])
    ones = ones_ref[...]
    kk = k * kk_ref[...]
    ss = jnp.dot(kk * kk, ones, precision=HIGHEST, preferred_element_type=F32)
    kk = kk / jnp.maximum(jnp.sqrt(ss), 1e-12)
    k2 = k * (1.0 + (a - 1.0) * ka_ref[...])
    rk = jnp.dot(r * k2 * rk_ref[...], ones, precision=HIGHEST, preferred_element_type=F32)
    r_o[...] = r
    w_o[...] = decay
    k_o[...] = k2
    v_o[...] = v
    kk_o[...] = kk
    kb_o[...] = kk * a
    g_o[...] = gate
    bo_o[...] = rk * v


def _rwkv_prep(p, prev, mu, w0, wdu, a0, wiu, wgu, k_k, k_a, r_k, tm):
    rows = p.shape[0]
    head = np.arange(A_WIDTH) // HEAD_DIM
    ones = jnp.asarray(head[:, None] == head[None, :], F32)
    row = lambda z: z.reshape(1, -1).astype(F32)
    spec_in = pl.BlockSpec((tm, A_PROJ), lambda i: (i, 0))
    spec_o = pl.BlockSpec((tm, A_WIDTH), lambda i: (i, 0))
    return pl.pallas_call(
        _rwkv_prep_kernel,
        grid=(rows // tm,),
        in_specs=[spec_in, spec_in, _full((1, A_PROJ)), _full((1, A_WIDTH)), _full((A_DECAY_RANK, A_WIDTH)),
                  _full((1, A_WIDTH)), _full((A_ICLR_RANK, A_WIDTH)), _full((A_GATE_RANK, A_WIDTH)),
                  _full((1, A_WIDTH)), _full((1, A_WIDTH)), _full((1, A_WIDTH)), _full((A_WIDTH, A_WIDTH))],
        out_specs=[spec_o] * 8,
        out_shape=[jax.ShapeDtypeStruct((rows, A_WIDTH), F32)] * 8,
        compiler_params=_cparams(("parallel",)),
        name="rwkv_prep",
    )(p, prev, row(mu), row(w0), _bf(wdu), row(a0), _bf(wiu), _bf(wgu), row(k_k), row(k_a), row(r_k), ones)


def _rwkv_prep_t_kernel(p_ref, mu_ref, w0_ref, wdu_ref, a0_ref, wiu_ref, wgu_ref, kk_ref, ka_ref, rk_ref,
                        r_o, w_o, k_o, v_o, kk_o, kb_o, g_o, bo_o, carry):
    tt = p_ref.shape[2]

    @pl.when(pl.program_id(1) == 0)
    def _():
        carry[...] = jnp.zeros_like(carry)

    p = p_ref[0]
    lane = lax.broadcasted_iota(I32, (1, tt), 1)
    prev = jnp.where(lane == 0, carry[:, 0:1], pltpu.roll(p, 1, 1))
    carry[:, 0:1] = p[:, tt - 1:tt]
    ps = p + mu_ref[...] * (prev - p)
    r = ps[0:A_WIDTH]
    k = ps[A_WIDTH:2 * A_WIDTH]
    v = ps[2 * A_WIDTH:3 * A_WIDTH]
    c = 3 * A_WIDTH
    xw = ps[c:c + A_DECAY_RANK]
    xa = ps[c + A_DECAY_RANK:c + A_DECAY_RANK + A_ICLR_RANK]
    xg = ps[c + A_DECAY_RANK + A_ICLR_RANK:]

    def head_sum(x):
        s = jnp.sum(x.reshape(HEAD_DIM, A_HEADS, tt), axis=0)
        return jnp.broadcast_to(s[None], (HEAD_DIM, A_HEADS, tt)).reshape(A_WIDTH, tt)

    w_log = -_softplus(-(w0_ref[...] + _dot(wdu_ref[...], _bf(jnp.tanh(xw))))) - 0.5
    decay = jnp.exp(-jnp.exp(w_log))
    a = jax.nn.sigmoid(a0_ref[...] + _dot(wiu_ref[...], _bf(xa)))
    gate = _dot(wgu_ref[...], _bf(jax.nn.sigmoid(xg)))
    kk = k * kk_ref[...]
    kk = kk / jnp.maximum(jnp.sqrt(head_sum(kk * kk)), 1e-12)
    k2 = k * (1.0 + (a - 1.0) * ka_ref[...])
    rk = head_sum(r * k2 * rk_ref[...])
    for o_ref, val in ((r_o, r), (w_o, decay), (k_o, k2), (kk_o, kk), (kb_o, kk * a)):
        o_ref[...] = val.reshape(N_JP, 2, 1, A_HEADS, tt)
    v_o[...] = v.reshape(HEAD_DIM, 1, A_HEADS, tt)
    g_o[0] = gate
    bo_o[0] = rk * v


def _rwkv_prep_t(p_t, mu, w0, wdu, a0, wiu, wgu, k_k, k_a, r_k, tt):
    b, _, t = p_t.shape
    col = lambda z, perm: z.reshape(-1)[perm].reshape(-1, 1).astype(F32)
    up = lambda w: _bf(w[:, CH_MAJOR].T)
    kshape = jax.ShapeDtypeStruct((N_JP, 2, b, A_HEADS, t), F32)
    kspec = pl.BlockSpec((N_JP, 2, 1, A_HEADS, tt), lambda bi, i: (0, 0, bi, 0, i))
    vspec = pl.BlockSpec((HEAD_DIM, 1, A_HEADS, tt), lambda bi, i: (0, bi, 0, i))
    nspec = pl.BlockSpec((1, A_WIDTH, tt), lambda bi, i: (bi, 0, i))
    nshape = jax.ShapeDtypeStruct((b, A_WIDTH, t), F32)
    return pl.pallas_call(
        _rwkv_prep_t_kernel,
        grid=(b, t // tt),
        in_specs=[pl.BlockSpec((1, A_PROJ, tt), lambda bi, i: (bi, 0, i)), _full((A_PROJ, 1)), _full((A_WIDTH, 1)),
                  _full((A_WIDTH, A_DECAY_RANK)), _full((A_WIDTH, 1)), _full((A_WIDTH, A_ICLR_RANK)),
                  _full((A_WIDTH, A_GATE_RANK)), _full((A_WIDTH, 1)), _full((A_WIDTH, 1)), _full((A_WIDTH, 1))],
        out_specs=[kspec, kspec, kspec, vspec, kspec, kspec, nspec, nspec],
        out_shape=[kshape, kshape, kshape, jax.ShapeDtypeStruct((HEAD_DIM, b, A_HEADS, t), F32), kshape, kshape,
                   nshape, nshape],
        scratch_shapes=[pltpu.VMEM((A_PROJ, LANES), F32)],
        compiler_params=_cparams(("parallel", "arbitrary")),
        name="rwkv_prep_t",
    )(p_t, col(mu, A_PERM), col(w0, CH_MAJOR), up(wdu), col(a0, CH_MAJOR), up(wiu), up(wgu),
      col(k_k, CH_MAJOR), col(k_a, CH_MAJOR), col(r_k, CH_MAJOR))


def _key_tiles_kernel(*refs):
    n = len(refs) // 2
    for z_ref, o_ref in zip(refs[:n], refs[n:]):
        for jp in range(N_JP):
            o_ref[0, jp] = z_ref[jp].T


def _key_tiles(zs, tt):
    t = zs[0].shape[2]
    return pl.pallas_call(
        _key_tiles_kernel,
        grid=(t // tt,),
        in_specs=[pl.BlockSpec((N_JP, LANES, tt), lambda i: (0, 0, i))] * len(zs),
        out_specs=[pl.BlockSpec((1, N_JP, tt, LANES), lambda i: (0, 0, i, 0))] * len(zs),
        out_shape=[jax.ShapeDtypeStruct((1, N_JP, t, LANES), F32)] * len(zs),
        compiler_params=_cparams(("parallel",)),
        name="wkv_key_tiles",
    )(*zs)


def _val_tiles_kernel(z_ref, o_ref, *, tt):
    for i in range(HEAD_DIM):
        x = z_ref[i]
        o_ref[pl.ds(i, tt, stride=HEAD_DIM), :] = jnp.concatenate([x, x], axis=0).T


def _val_tiles(z, tt):
    t = z.shape[2]
    return pl.pallas_call(
        functools.partial(_val_tiles_kernel, tt=tt),
        grid=(t // tt,),
        in_specs=[pl.BlockSpec((HEAD_DIM, BH_LANES, tt), lambda i: (0, 0, i))],
        out_specs=pl.BlockSpec((tt * HEAD_DIM, LANES), lambda i: (i, 0)),
        out_shape=jax.ShapeDtypeStruct((t * HEAD_DIM, LANES), F32),
        compiler_params=_cparams(("parallel",)),
        name="wkv_val_tiles",
    )(z)


def _val_untile_kernel(y_ref, o_ref, *, tt):
    for i in range(HEAD_DIM):
        o_ref[i] = y_ref[pl.ds(i, tt, stride=HEAD_DIM), :].T[0:BH_LANES]


def _val_untile(y, tt):
    t = y.shape[0] // HEAD_DIM
    return pl.pallas_call(
        functools.partial(_val_untile_kernel, tt=tt),
        grid=(t // tt,),
        in_specs=[pl.BlockSpec((tt * HEAD_DIM, LANES), lambda i: (i, 0))],
        out_specs=pl.BlockSpec((HEAD_DIM, BH_LANES, tt), lambda i: (0, 0, i)),
        out_shape=jax.ShapeDtypeStruct((HEAD_DIM, BH_LANES, t), F32),
        compiler_params=_cparams(("parallel",)),
        name="wkv_val_untile",
    )(y)


N_IO = HEAD_DIM // 8
N_JP = HEAD_DIM // 2
BH_LANES = 64


def _wkv_kernel(r_ref, w_ref, k_ref, v_ref, a_ref, b_ref, s0_ref, y_ref, sfin_ref, s_scr, *, tc):
    t_blk = pl.program_id(1)

    @pl.when(t_blk == 0)
    def _():
        s_scr[...] = s0_ref[0]

    def bc(ref, t, jp):
        return jnp.broadcast_to(ref[0, jp, pl.ds(t, 1), :], (8, LANES))

    def fold(x):
        return x + pltpu.roll(x, BH_LANES, 1)

    def step(t, carry):
        acc = [jnp.zeros((8, LANES), F32) for _ in range(N_IO)]
        for jp in range(N_JP):
            a_ = bc(a_ref, t, jp)
            for io in range(N_IO):
                acc[io] = acc[io] + s_scr[io, jp] * a_
        sa = [-fold(acc[io]) for io in range(N_IO)]
        vv = [v_ref[0, t, io * 8:(io + 1) * 8, :] for io in range(N_IO)]
        yacc = [jnp.zeros((8, LANES), F32) for _ in range(N_IO)]
        for jp in range(N_JP):
            w_ = bc(w_ref, t, jp)
            b_ = bc(b_ref, t, jp)
            k_ = bc(k_ref, t, jp)
            r_ = bc(r_ref, t, jp)
            for io in range(N_IO):
                s = s_scr[io, jp] * w_ + sa[io] * b_ + vv[io] * k_
                s_scr[io, jp] = s
                yacc[io] = yacc[io] + s * r_
        y = [fold(yacc[io]) for io in range(N_IO)]
        tot = y[0]
        for io in range(1, N_IO):
            tot = tot + y[io]
        mu = jnp.sum(tot, axis=0, keepdims=True) * (1.0 / HEAD_DIM)
        d = [y[io] - mu for io in range(N_IO)]
        sq = d[0] * d[0]
        for io in range(1, N_IO):
            sq = sq + d[io] * d[io]
        var = jnp.sum(sq, axis=0, keepdims=True) * (1.0 / HEAD_DIM)
        inv = lax.rsqrt(var + A_GN_EPS)
        for io in range(N_IO):
            y_ref[0, t, io * 8:(io + 1) * 8, :] = d[io] * inv
        return carry

    lax.fori_loop(0, tc, step, 0)

    @pl.when(t_blk == pl.num_programs(1) - 1)
    def _():
        sfin_ref[0] = s_scr[...]


def _wkv_scan(r, w, k, v, a, b, s0, tc):
    nb, t = v.shape[:2]
    kspec = pl.BlockSpec((1, N_JP, tc, LANES), lambda n, i: (n, 0, i, 0))
    vspec = pl.BlockSpec((1, tc, HEAD_DIM, LANES), lambda n, i: (n, i, 0, 0))
    sspec = pl.BlockSpec((1, N_IO, N_JP, 8, LANES), lambda n, i: (n, 0, 0, 0, 0))
    return pl.pallas_call(
        functools.partial(_wkv_kernel, tc=tc),
        grid=(nb, t // tc),
        in_specs=[kspec, kspec, kspec, vspec, kspec, kspec, sspec],
        out_specs=[vspec, sspec],
        out_shape=[jax.ShapeDtypeStruct((nb, t, HEAD_DIM, LANES), F32),
                   jax.ShapeDtypeStruct((nb, N_IO, N_JP, 8, LANES), F32)],
        scratch_shapes=[pltpu.VMEM((N_IO, N_JP, 8, LANES), F32)],
        compiler_params=_cparams(("parallel", "arbitrary")),
        name="wkv_scan",
    )(r, w, k, v, a, b, s0)


def _to_key_tiles(x, b, t):
    nb = b // 8
    x = x.reshape(nb, 8, t, A_HEADS, N_JP, 2).transpose(0, 4, 2, 5, 1, 3)
    return x.reshape(nb, N_JP, t, LANES)


def _to_val_tiles(x, b, t):
    nb = b // 8
    x = x.reshape(nb, 8, t, A_HEADS, HEAD_DIM).transpose(0, 2, 4, 1, 3).reshape(nb, t, HEAD_DIM, BH_LANES)
    return jnp.concatenate([x, x], axis=-1)


def _from_val_tiles(y, b, t):
    nb = b // 8
    y = y[..., :BH_LANES].reshape(nb, t, HEAD_DIM, 8, A_HEADS).transpose(0, 3, 1, 4, 2)
    return y.reshape(b * t, A_WIDTH)


def _state_to_tiles(s, b):
    nb = b // 8
    s = s.reshape(nb, 8, A_HEADS, N_IO, 8, N_JP, 2).transpose(0, 3, 5, 4, 6, 1, 2)
    return s.reshape(nb, N_IO, N_JP, 8, LANES)


def _state_from_tiles(s, b):
    nb = b // 8
    s = s.reshape(nb, N_IO, N_JP, 8, 2, 8, A_HEADS).transpose(0, 5, 6, 1, 3, 2, 4)
    return s.reshape(b, A_HEADS, HEAD_DIM, HEAD_DIM)


def _compress_paged_kernel(pt_ref, pe_ref, w1_ref, w2_ref, cache_ref, o_ref, buf, xk, xv, sem, *, n_pages, page_len):
    i = pl.program_id(0)
    slot = i % 2

    def copy(s_idx, sl, p):
        return pltpu.make_async_copy(cache_ref.at[pt_ref[s_idx * n_pages + p]], buf.at[sl, p], sem.at[sl, p])

    def fetch(s_idx, sl):
        for p in range(n_pages):
            copy(s_idx, sl, p).start()

    @pl.when(i == 0)
    def _():
        fetch(0, 0)

    @pl.when(i + 1 < pl.num_programs(0))
    def _():
        fetch(i + 1, 1 - slot)

    for p in range(n_pages):
        copy(i, slot, p).wait()
        t = buf[slot, p].T
        xk[p * page_len:(p + 1) * page_len, :] = t[:, :KV_WIDTH]
        xv[p * page_len:(p + 1) * page_len, :] = t[:, KV_WIDTH:]
    _compress_kernel(xk, xv, pe_ref, w1_ref, w2_ref, o_ref, nblk=n_pages * page_len // CMP_BLOCK)


def _compress_kernel(xk_ref, xv_ref, pe_ref, w1_ref, w2_ref, o_ref, *, nblk):
    for c, x_ref in enumerate((xk_ref, xv_ref)):
        acc = jnp.zeros((nblk, B_KV_HEADS * CMP_HIDDEN), F32)
        for tau in range(CMP_BLOCK):
            x = x_ref[pl.ds(tau, nblk, stride=CMP_BLOCK), :] + pe_ref[c, tau:tau + 1, :]
            acc = acc + _dot(_bf(x), w1_ref[c, tau])
        o_ref[:, c * KV_WIDTH:(c + 1) * KV_WIDTH] = _dot(_bf(jax.nn.gelu(acc)), w2_ref[c])


def _compress_weights(pe_k, pe_v, w_k1, w_k2, w_v1, w_v2):
    pe = jnp.stack([jnp.concatenate([pe_k, pe_k], axis=1), jnp.concatenate([pe_v, pe_v], axis=1)]).astype(F32)
    eye = jnp.eye(B_KV_HEADS, dtype=F32)
    w1 = jnp.stack([w_k1, w_v1]).reshape(2, CMP_BLOCK, HEAD_DIM, CMP_HIDDEN)
    w1 = jnp.einsum('ctdh,ge->ctgdeh', w1, eye).reshape(2, CMP_BLOCK, KV_WIDTH, B_KV_HEADS * CMP_HIDDEN)
    w2 = jnp.einsum('chd,ge->cghed', jnp.stack([w_k2, w_v2]), eye).reshape(2, B_KV_HEADS * CMP_HIDDEN, KV_WIDTH)
    return pe, _bf(w1), _bf(w2)


COMPRESS_W_SPECS = ((2, CMP_BLOCK, KV_WIDTH), (2, CMP_BLOCK, KV_WIDTH, B_KV_HEADS * CMP_HIDDEN),
                    (2, B_KV_HEADS * CMP_HIDDEN, KV_WIDTH))


def _compress_paged(cache_t, page_table, cw):
    s, n_pages = page_table.shape
    page = cache_t.shape[2]
    past = n_pages * page
    nblk = past // CMP_BLOCK
    grid_spec = pltpu.PrefetchScalarGridSpec(
        num_scalar_prefetch=1,
        grid=(s,),
        in_specs=[pl.BlockSpec(shp, lambda i, pt, n=len(shp): (0,) * n) for shp in COMPRESS_W_SPECS]
                 + [pl.BlockSpec(memory_space=pl.ANY)],
        out_specs=pl.BlockSpec((nblk, KV_ROW), lambda i, pt: (i, 0)),
        scratch_shapes=[pltpu.VMEM((2, n_pages, KV_ROW, page), F32), pltpu.VMEM((past, KV_WIDTH), F32),
                        pltpu.VMEM((past, KV_WIDTH), F32), pltpu.SemaphoreType.DMA((2, n_pages))],
    )
    return pl.pallas_call(
        functools.partial(_compress_paged_kernel, n_pages=n_pages, page_len=page),
        grid_spec=grid_spec,
        out_shape=jax.ShapeDtypeStruct((s * nblk, KV_ROW), F32),
        compiler_params=_cparams(("arbitrary",)),
        name="nsa_compress_paged",
    )(page_table.reshape(-1), *_compress_weights(*cw), cache_t)


def _compress(x, pe_k, pe_v, w_k1, w_k2, w_v1, w_v2, rows):
    n = x.shape[0]
    nblk = rows // CMP_BLOCK
    pe, w1, w2 = _compress_weights(pe_k, pe_v, w_k1, w_k2, w_v1, w_v2)
    return pl.pallas_call(
        functools.partial(_compress_kernel, nblk=nblk),
        grid=(n // rows,),
        in_specs=[pl.BlockSpec((rows, KV_WIDTH), lambda i: (i, 0)), pl.BlockSpec((rows, KV_WIDTH), lambda i: (i, 1)),
                  ] + [_full(shp) for shp in COMPRESS_W_SPECS],
        out_specs=pl.BlockSpec((nblk, KV_ROW), lambda i: (i, 0)),
        out_shape=jax.ShapeDtypeStruct((n // CMP_BLOCK, KV_ROW), F32),
        compiler_params=_cparams(("parallel",)),
        name="nsa_compress",
    )(x, x, pe, w1, w2)


def _rel_bucket(dist):
    n = jnp.maximum(dist, 0)
    log_ratio = jnp.log(jnp.maximum(n, 1).astype(F32) / MAX_EXACT) / math.log(MAX_DISTANCE / MAX_EXACT)
    large = jnp.minimum(MAX_EXACT + (log_ratio * (N_BUCKETS - MAX_EXACT)).astype(I32), N_BUCKETS - 1)
    return jnp.where(n < MAX_EXACT, n, large)


def _bias_of(rel_bias, dist):
    return jnp.moveaxis(rel_bias.astype(F32)[_rel_bucket(dist)], -1, 0)


KEY_TILE = 512


def _nsa_prompt_kernel(q_ref, gate_ref, ck_ref, ckt_ref, ksel_ref, vsel_ref, kwin_ref, vwin_ref, bct_ref, tz_ref,
                       o_ref, imp_scr, mask_scr, *, nc, ns, nq, top_n, wtiles):
    i = pl.program_id(1)
    qb = Q_BLOCK
    gates = jax.nn.sigmoid(gate_ref[0])
    pos_l = i * qb + lax.broadcasted_iota(I32, (1, qb), 1)
    okT = pos_l >= lax.broadcasted_iota(I32, (nc, qb), 0) * CMP_BLOCK + (CMP_BLOCK - 1)
    blk = lax.broadcasted_iota(I32, (ns, qb), 0)
    cur = pos_l // SEL_BLOCK
    forced = (blk == 0) | (blk == cur) | (blk == cur - 1)
    k_s = lax.broadcasted_iota(I32, (qb, qb), 0)
    q_l = lax.broadcasted_iota(I32, (qb, qb), 1)
    n_e = lax.broadcasted_iota(I32, (qb, ns), 1)
    k_e = lax.broadcasted_iota(I32, (qb, ns), 0) // SEL_BLOCK
    wk = (wtiles + 1) * qb
    wb = jnp.maximum(i - wtiles, 0)
    d_w = pos_l - (wb * qb + lax.broadcasted_iota(I32, (wk, qb), 0))
    madd_w = jnp.where((d_w >= 0) & (d_w < WINDOW), 0.0, NEG_INF)
    n_tiles = (i * qb + qb + KEY_TILE - 1) // KEY_TILE
    per = KEY_TILE // qb

    def bias_tiles(h, first_blk, count):
        return jnp.concatenate([tz_ref[h, jnp.clip(i - (first_blk + c), 0, nq - 1)] for c in range(count)], axis=0)

    for g in range(B_KV_HEADS):
        ksl = slice(g * HEAD_DIM, (g + 1) * HEAD_DIM)
        vsl = slice(KV_WIDTH + g * HEAD_DIM, KV_WIDTH + (g + 1) * HEAD_DIM)
        qg = [_bf(q_ref[0, (g * B_GROUP + r) * HEAD_DIM:(g * B_GROUP + r + 1) * HEAD_DIM, :] * SCALE)
              for r in range(B_GROUP)]
        kc = _bf(ck_ref[:, ksl])
        vct = _bf(ckt_ref[0, vsl, :])
        o_c = []
        impT = jnp.zeros((nc, qb), F32)
        for r in range(B_GROUP):
            sT = jnp.where(okT, _dot(kc, qg[r]) + bct_ref[g * B_GROUP + r], NEG_INF)
            eT = jnp.exp(sT - jnp.max(sT, axis=0, keepdims=True))
            pT = eT / jnp.sum(eT, axis=0, keepdims=True) * okT.astype(F32)
            impT = impT + pT
            o_c.append(_dot(vct, _bf(pT)))
        imp_scr[...] = impT
        imp2 = imp_scr[pl.ds(0, ns, stride=SEL_RATIO), :] + imp_scr[pl.ds(1, ns, stride=SEL_RATIO), :]
        score = jnp.where(blk <= cur, imp2 + FORCE_SCORE * forced.astype(F32), NEG_INF)
        rank = jnp.zeros((ns, qb), I32)
        for m in range(ns):
            row = score[m:m + 1, :]
            rank = rank + ((row > score) | ((row == score) & (m < blk))).astype(I32)
        selT = _bf((rank < top_n) & (score > NEG_INF / 2))
        mask_scr[...] = jnp.full(mask_scr.shape, NEG_INF, F32)

        def fill(j, carry):
            m = _dot(_bf(n_e == 2 * j + k_e), selT) > 0.5
            m = m & ((j < i) | (q_l >= k_s))
            mask_scr[pl.ds(pl.multiple_of(j * qb, qb), qb), :] = jnp.where(m, 0.0, NEG_INF)
            return carry

        lax.fori_loop(0, i + 1, fill, 0)

        def body(jt, carry):
            ms, ls, accs = carry
            k0 = pl.multiple_of(jt * KEY_TILE, KEY_TILE)
            kt = _bf(ksel_ref[pl.ds(k0, KEY_TILE), ksl])
            vt = _bf(vsel_ref[0, vsl, pl.ds(k0, KEY_TILE)])
            madd = mask_scr[pl.ds(k0, KEY_TILE), :]
            ms2, ls2, accs2 = [], [], []
            for r in range(B_GROUP):
                s = _dot(kt, qg[r]) + bias_tiles(g * B_GROUP + r, jt * per, per) + madd
                m_new = jnp.maximum(ms[r], jnp.max(s, axis=0, keepdims=True))
                alpha = jnp.exp(ms[r] - m_new)
                p = jnp.exp(s - m_new)
                ls2.append(alpha * ls[r] + jnp.sum(p, axis=0, keepdims=True))
                accs2.append(alpha * accs[r] + _dot(vt, _bf(p)))
                ms2.append(m_new)
            return tuple(ms2), tuple(ls2), tuple(accs2)

        init = (tuple(jnp.full((1, qb), NEG_INF, F32) for _ in range(B_GROUP)),
                tuple(jnp.zeros((1, qb), F32) for _ in range(B_GROUP)),
                tuple(jnp.zeros((HEAD_DIM, qb), F32) for _ in range(B_GROUP)))
        _, ls, accs = lax.fori_loop(0, n_tiles, body, init)
        w0 = pl.multiple_of(wb * qb, qb)
        ktw = _bf(kwin_ref[pl.ds(w0, wk), ksl])
        vtw = _bf(vwin_ref[0, vsl, pl.ds(w0, wk)])
        for r in range(B_GROUP):
            h = g * B_GROUP + r
            s = _dot(ktw, qg[r]) + bias_tiles(h, wb, wtiles + 1) + madd_w
            p = jnp.exp(s - jnp.max(s, axis=0, keepdims=True))
            o_w = _dot(vtw, _bf(p)) / jnp.sum(p, axis=0, keepdims=True)
            o_s = accs[r] / ls[r]
            o_ref[0, h * HEAD_DIM:(h + 1) * HEAD_DIM, :] = (
                gates[3 * h:3 * h + 1] * o_c[r] + gates[3 * h + 1:3 * h + 2] * o_s + gates[3 * h + 2:3 * h + 3] * o_w)


def _nsa_prompt(q_t, gate_t, ckv, kv_sel, kvt_sel, kv_win, kvt_win, rel_bias):
    b, _, t = q_t.shape
    nq = t // Q_BLOCK
    nc = t // CMP_BLOCK
    ns = -(-t // SEL_BLOCK)
    wtiles = WINDOW // Q_BLOCK
    assert t % KEY_TILE == 0 and nc == SEL_RATIO * ns and WINDOW % Q_BLOCK == 0 and (wtiles + 1) * Q_BLOCK <= t
    cmp_end = jnp.arange(nc) * CMP_BLOCK + (CMP_BLOCK - 1)
    bct = _bias_of(rel_bias, jnp.arange(t)[None, :] - cmp_end[:, None])
    dz = (jnp.arange(nq)[:, None, None] * Q_BLOCK + jnp.arange(Q_BLOCK)[None, None, :]
          - jnp.arange(Q_BLOCK)[None, :, None])
    tz = _bias_of(rel_bias, dz)
    ckt = jnp.swapaxes(ckv.reshape(b, nc, KV_ROW), 1, 2)
    chan = lambda w: pl.BlockSpec((1, w, Q_BLOCK), lambda bi, i: (bi, 0, i))
    seq_rows = pl.BlockSpec((t, KV_ROW), lambda bi, i: (bi, 0))
    seq_chan = pl.BlockSpec((1, KV_ROW, t), lambda bi, i: (bi, 0, 0))
    return pl.pallas_call(
        functools.partial(_nsa_prompt_kernel, nc=nc, ns=ns, nq=nq, top_n=min(TOP_N, ns), wtiles=wtiles),
        grid=(b, nq),
        in_specs=[chan(B_WIDTH), chan(GATE_PAD),
                  pl.BlockSpec((nc, KV_ROW), lambda bi, i: (bi, 0)),
                  pl.BlockSpec((1, KV_ROW, nc), lambda bi, i: (bi, 0, 0)),
                  seq_rows, seq_chan, seq_rows, seq_chan,
                  pl.BlockSpec((B_HEADS, nc, Q_BLOCK), lambda bi, i: (0, 0, i)),
                  _full((B_HEADS, nq, Q_BLOCK, Q_BLOCK))],
        out_specs=chan(B_WIDTH),
        out_shape=jax.ShapeDtypeStruct((b, B_WIDTH, t), F32),
        scratch_shapes=[pltpu.VMEM((nc, Q_BLOCK), F32), pltpu.VMEM((t, Q_BLOCK), F32)],
        compiler_params=_cparams(("parallel", "arbitrary")),
        name="nsa_prompt",
    )(q_t, gate_t, ckv, ckt, kv_sel, kvt_sel, kv_win, kvt_win, bct, tz)


def _group_q(q_ref, g):
    rows = [q_ref[0, :, (g * B_GROUP + r) * HEAD_DIM:(g * B_GROUP + r + 1) * HEAD_DIM] for r in range(B_GROUP)]
    return _bf(jnp.concatenate(rows, axis=0) * SCALE)


def _nsa_sample_select_kernel(q_ref, gate_ref, ck_ref, bc_ref, pair_ref, tri_ref, oc_ref, idx_ref,
                              *, nsp, cur, top_n):
    gates = jax.nn.sigmoid(gate_ref[0])
    blk = lax.broadcasted_iota(I32, (1, nsp), 1)
    forced = (blk == 0) | (blk == cur) | (blk == cur - 1)
    mi = lax.broadcasted_iota(I32, (nsp, nsp), 0)
    ni = lax.broadcasted_iota(I32, (nsp, nsp), 1)
    kk = lax.broadcasted_iota(I32, (TOP_N, nsp), 0).astype(F32)
    nf = lax.broadcasted_iota(I32, (TOP_N, nsp), 1).astype(F32)
    outs = []
    for g in range(B_KV_HEADS):
        q4 = _group_q(q_ref, g)
        kc = _bf(ck_ref[:, g * HEAD_DIM:(g + 1) * HEAD_DIM])
        vc = _bf(ck_ref[:, KV_WIDTH + g * HEAD_DIM:KV_WIDTH + (g + 1) * HEAD_DIM])
        s = _dot_nt(q4, kc) + bc_ref[g * B_GROUP:(g + 1) * B_GROUP, :]
        e = jnp.exp(s - jnp.max(s, axis=-1, keepdims=True))
        p = e / jnp.sum(e, axis=-1, keepdims=True)
        o_c = _dot(_bf(p), vc)
        imp = ((p[0:1] + p[1:2]) + p[2:3]) + p[3:4]
        imp2 = jnp.dot(imp, pair_ref[...], precision=HIGHEST, preferred_element_type=F32)
        score = jnp.where(blk <= cur, imp2 + FORCE_SCORE * forced.astype(F32), NEG_INF)
        m1 = jnp.broadcast_to(score, (nsp, nsp))
        m2 = m1.T
        gt = (m2 > m1) | ((m2 == m1) & (mi < ni))
        rank = jnp.sum(gt.astype(F32), axis=0, keepdims=True)
        sel = (rank < top_n) & (score > NEG_INF / 2)
        before = _dot(_bf(sel), tri_ref[...])
        hit = jnp.broadcast_to(sel, (TOP_N, nsp)) & (jnp.broadcast_to(before, (TOP_N, nsp)) == kk)
        idx = jnp.sum(jnp.where(hit, nf, 0.0), axis=1, keepdims=True)
        cnt = jnp.sum(hit.astype(F32), axis=1, keepdims=True)
        idx_ref[0, g] = jnp.where(cnt > 0.5, idx, -1.0).astype(I32)
        for r in range(B_GROUP):
            h = g * B_GROUP + r
            outs.append(gates[:, 3 * h:3 * h + 1] * o_c[r:r + 1])
    oc_ref[0] = jnp.concatenate(outs, axis=1)


def _nsa_sample_select(q3, gate3, ckv, rel_bias, past_len):
    s = q3.shape[0]
    nc = past_len // CMP_BLOCK
    ns = -(-(past_len + 1) // SEL_BLOCK)
    nsp = -(-ns // LANES) * LANES
    cur = past_len // SEL_BLOCK
    bc = _bias_of(rel_bias, past_len - (jnp.arange(nc) * CMP_BLOCK + (CMP_BLOCK - 1)))
    pair = jnp.asarray(np.arange(nc)[:, None] // SEL_RATIO == np.arange(nsp)[None, :], F32)
    tri = jnp.asarray(np.arange(nsp)[:, None] < np.arange(nsp)[None, :], BF16)
    return pl.pallas_call(
        functools.partial(_nsa_sample_select_kernel, nsp=nsp, cur=cur, top_n=min(TOP_N, ns)),
        grid=(s,),
        in_specs=[pl.BlockSpec((1, 1, B_WIDTH), lambda i: (i, 0, 0)), pl.BlockSpec((1, 1, GATE_PAD), lambda i: (i, 0, 0)),
                  pl.BlockSpec((nc, KV_ROW), lambda i: (i, 0)), _full((B_HEADS, nc)), _full((nc, nsp)), _full((nsp, nsp))],
        out_specs=[pl.BlockSpec((1, 1, B_WIDTH), lambda i: (i, 0, 0)),
                   pl.BlockSpec((1, B_KV_HEADS, TOP_N, 1), lambda i: (i, 0, 0, 0))],
        out_shape=[jax.ShapeDtypeStruct((s, 1, B_WIDTH), F32), jax.ShapeDtypeStruct((s, B_KV_HEADS, TOP_N, 1), I32)],
        compiler_params=_cparams(("parallel",)),
        name="nsa_sample_select",
    )(q3, gate3, ckv, bc, pair, tri)


def _nsa_sample_attend_kernel(idx_ref, pt_ref, q_ref, gate_ref, oc_ref, ksel_ref, kwin_ref, win_ref, fpg_ref, bw_ref,
                              cache_ref, o_ref, buf, sem, *, cur, n_pages, past_len, page_len):
    i = pl.program_id(0)
    n_s = pl.num_programs(0)
    slot = i % 2
    halves = PAGE_BLOCKS
    n_slot = B_KV_HEADS * TOP_N

    def block_of(s_idx, j):
        return idx_ref[s_idx * n_slot + j]

    def copy(s_idx, sl, j):
        n = jnp.clip(block_of(s_idx, j), 0, cur - 1)
        page = pt_ref[s_idx * n_pages + n // halves]
        return pltpu.make_async_copy(cache_ref.at[page], buf.at[sl, j], sem.at[sl])

    def cached(s_idx, j):
        n = block_of(s_idx, j)
        return (n >= 0) & (n < cur)

    def fetch(s_idx, sl):
        for j in range(n_slot):
            @pl.when(cached(s_idx, j))
            def _():
                copy(s_idx, sl, j).start()

    @pl.when(i == 0)
    def _():
        fetch(0, 0)

    @pl.when(i + 1 < n_s)
    def _():
        fetch(i + 1, 1 - slot)

    for j in range(n_slot):
        @pl.when(cached(i, j))
        def _():
            copy(i, slot, j).wait()

        @pl.when(jnp.logical_not(cached(i, j)))
        def _():
            buf[slot, j] = jnp.zeros((KV_ROW, page_len), F32)
            buf[slot, j, :, 0:1] = ksel_ref[0]

    gates = jax.nn.sigmoid(gate_ref[0])
    t_l = lax.broadcasted_iota(I32, (1, page_len), 1)
    n_buf = win_ref.shape[2]
    j_w = lax.broadcasted_iota(I32, (1, n_buf), 1)
    d_w = n_buf - j_w
    ok_w = (d_w >= 0) & (d_w < WINDOW) & (past_len - d_w >= 0)
    f0 = fpg_ref[n_pages]
    outs = []
    for g in range(B_KV_HEADS):
        q4 = _group_q(q_ref, g)
        ksl = slice(g * HEAD_DIM, (g + 1) * HEAD_DIM)
        vsl = slice(KV_WIDTH + g * HEAD_DIM, KV_WIDTH + (g + 1) * HEAD_DIM)
        pieces = []
        for k in range(TOP_N):
            j = g * TOP_N + k
            n = block_of(i, j)
            pg = jnp.clip(n, 0, cur) // halves
            fb = fpg_ref[pg][g * B_GROUP:(g + 1) * B_GROUP, :]
            s_k = _dot(q4, _bf(buf[slot, j, ksl, :])) + fb
            ok = (n >= 0) & (t_l // SEL_BLOCK == n % halves) & (pg * page_len + t_l <= past_len)
            pieces.append(jnp.where(ok, s_k, NEG_INF))
        s = jnp.concatenate(pieces, axis=1)
        e = jnp.exp(s - jnp.max(s, axis=-1, keepdims=True))
        p = _bf(e / jnp.sum(e, axis=-1, keepdims=True))
        o_s = jnp.zeros((B_GROUP, HEAD_DIM), F32)
        for k in range(TOP_N):
            o_s = o_s + _dot_nt(p[:, k * page_len:(k + 1) * page_len], _bf(buf[slot, g * TOP_N + k, vsl, :]))
        kw = _bf(win_ref[0, ksl, :])
        vw = _bf(win_ref[0, vsl, :])
        s_w = jnp.where(ok_w, _dot(q4, kw) + bw_ref[g * B_GROUP:(g + 1) * B_GROUP, :], NEG_INF)
        k_new = _bf(kwin_ref[0, :, ksl])
        v_new = _bf(kwin_ref[0, :, vsl])
        s_n = (jnp.sum(q4.astype(F32) * k_new.astype(F32), axis=-1, keepdims=True)
               + f0[g * B_GROUP:(g + 1) * B_GROUP, 0:1])
        m = jnp.maximum(jnp.max(s_w, axis=-1, keepdims=True), s_n)
        e_w = jnp.exp(s_w - m)
        e_n = jnp.exp(s_n - m)
        den = jnp.sum(e_w, axis=-1, keepdims=True) + e_n
        o_w = _dot_nt(_bf(e_w / den), vw) + _bf(e_n / den).astype(F32) * v_new.astype(F32)
        for r in range(B_GROUP):
            h = g * B_GROUP + r
            outs.append(gates[:, 3 * h + 1:3 * h + 2] * o_s[r:r + 1] + gates[:, 3 * h + 2:3 * h + 3] * o_w[r:r + 1])
    o_ref[0] = oc_ref[0] + jnp.concatenate(outs, axis=1)


PAGE_BLOCKS = 2


def _nsa_sample_attend(idx, page_table, q3, gate3, oc3, ksel_t, kwin3, win_t, cache_t, rel_bias, past_len):
    s = q3.shape[0]
    n_pages = page_table.shape[1]
    cur = past_len // SEL_BLOCK
    n_buf = win_t.shape[2]
    page = cache_t.shape[2]
    assert page == PAGE_BLOCKS * SEL_BLOCK and past_len == n_pages * page
    keypos = jnp.arange(n_pages + 1)[:, None] * page + jnp.arange(page)[None, :]
    fpg = jnp.moveaxis(_bias_of(rel_bias, past_len - keypos), 0, 1)
    bw = _bias_of(rel_bias, n_buf - jnp.arange(n_buf))
    row3 = lambda w: pl.BlockSpec((1, 1, w), lambda i, *_: (i, 0, 0))
    grid_spec = pltpu.PrefetchScalarGridSpec(
        num_scalar_prefetch=2,
        grid=(s,),
        in_specs=[row3(B_WIDTH), row3(GATE_PAD), row3(B_WIDTH),
                  pl.BlockSpec((1, KV_ROW, 1), lambda i, *_: (i, 0, 0)), row3(KV_ROW),
                  pl.BlockSpec((1, KV_ROW, n_buf), lambda i, *_: (i, 0, 0)),
                  pl.BlockSpec((n_pages + 1, B_HEADS, page), lambda i, *_: (0, 0, 0)),
                  pl.BlockSpec((B_HEADS, n_buf), lambda i, *_: (0, 0)),
                  pl.BlockSpec(memory_space=pl.ANY)],
        out_specs=row3(B_WIDTH),
        scratch_shapes=[pltpu.VMEM((2, B_KV_HEADS * TOP_N, KV_ROW, page), F32), pltpu.SemaphoreType.DMA((2,))],
    )
    return pl.pallas_call(
        functools.partial(_nsa_sample_attend_kernel, cur=cur, n_pages=n_pages, past_len=past_len, page_len=page),
        grid_spec=grid_spec,
        out_shape=jax.ShapeDtypeStruct((s, 1, B_WIDTH), F32),
        compiler_params=_cparams(("arbitrary",)),
        name="nsa_sample_attend",
    )(idx.reshape(-1), page_table.reshape(-1), q3, gate3, oc3, ksel_t, kwin3, win_t, fpg, bw, cache_t)


def _merge_kernel(x_ref, yn_ref, bo_ref, ga_ref, yb_ref, pg_ref, lnw_ref, lnb_ref, wa_ref, wb_ref, wo_ref, nf_ref,
                  wr_ref, br_ref, x1_ref, h_ref, te_ref, tw_ref, *, a_transposed):
    if a_transposed:
        tm = x_ref.shape[0]
        yn = yn_ref[...].reshape(A_WIDTH, tm)
        y_a = ((yn * lnw_ref[...] + lnb_ref[...] + bo_ref[0]) * ga_ref[0]).T
        y_b = yb_ref[0].T
    else:
        y_a = (yn_ref[...] * lnw_ref[...] + lnb_ref[...] + bo_ref[...]) * ga_ref[...]
        y_b = yb_ref[...]
    g_a = jax.nn.sigmoid(pg_ref[:, :D_MODEL])
    g_b = jax.nn.sigmoid(pg_ref[:, D_MODEL:])
    m = g_a * _dot(_bf(y_a), wa_ref[...]) + g_b * _dot(_bf(y_b), wb_ref[...])
    x1 = x_ref[...] + _dot(_bf(m), wo_ref[...])
    x1_ref[...] = x1
    h = x1 * lax.rsqrt(jnp.mean(x1 * x1, axis=-1, keepdims=True) + NORM_EPS) * nf_ref[...]
    h_ref[...] = _bf(h)
    logits = jnp.dot(h, wr_ref[...], precision=HIGHEST, preferred_element_type=F32) + br_ref[...]
    lane = lax.broadcasted_iota(I32, logits.shape, 1).astype(F32)
    col = lax.broadcasted_iota(I32, (logits.shape[0], TOP_K), 1)
    vals, idxs = [], []
    for _ in range(TOP_K):
        m_k = jnp.max(logits, axis=-1, keepdims=True)
        i_k = jnp.min(jnp.where(logits == m_k, lane, float(N_EXPERTS)), axis=-1, keepdims=True)
        vals.append(m_k)
        idxs.append(i_k)
        logits = jnp.where(lane == i_k, -jnp.inf, logits)
    e = [jnp.exp(v - vals[0]) for v in vals]
    den = ((e[0] + e[1]) + e[2]) + e[3]
    te = jnp.zeros(col.shape, F32)
    tw = jnp.zeros(col.shape, F32)
    for k in range(TOP_K):
        te = jnp.where(col == k, idxs[k], te)
        tw = jnp.where(col == k, e[k] / den, tw)
    te_ref[...] = te.astype(I32)
    tw_ref[...] = tw


def _merge(x, yn, bonus, gate, yb, pg, ln_w, ln_b, w_a, w_b, w_o, norm_ffn, w_router, b_router, tm, seq=None):
    rows = x.shape[0]
    row = lambda w: pl.BlockSpec((tm, w), lambda i: (i, 0))
    vec = lambda z: z.reshape(1, -1).astype(F32)
    if seq is None:
        a_specs = [row(A_WIDTH)] * 3 + [row(B_WIDTH)] + [_full((1, A_WIDTH))] * 2
        lnw, lnb = vec(ln_w), vec(ln_b)
    else:
        per = seq // tm
        nspec = pl.BlockSpec((1, A_WIDTH, tm), lambda i: (i // per, 0, i % per))
        a_specs = [pl.BlockSpec((HEAD_DIM, 1, A_HEADS, tm), lambda i: (0, i // per, 0, i % per)), nspec, nspec, nspec,
                   _full((A_WIDTH, 1)), _full((A_WIDTH, 1))]
        lnw, lnb = (z[CH_MAJOR].reshape(-1, 1).astype(F32) for z in (ln_w, ln_b))
        w_a = w_a[CH_MAJOR]
    return pl.pallas_call(
        functools.partial(_merge_kernel, a_transposed=seq is not None),
        grid=(rows // tm,),
        in_specs=[row(D_MODEL)] + a_specs[:4] + [row(GATE_PROJ)] + a_specs[4:]
                 + [_full((A_WIDTH, D_MODEL)), _full((B_WIDTH, D_MODEL)),
                  _full((D_MODEL, D_MODEL)), _full((1, D_MODEL)), _full((D_MODEL, N_EXPERTS)), _full((1, N_EXPERTS))],
        out_specs=[row(D_MODEL), row(D_MODEL), row(TOP_K), row(TOP_K)],
        out_shape=[jax.ShapeDtypeStruct((rows, D_MODEL), F32), jax.ShapeDtypeStruct((rows, D_MODEL), BF16),
                   jax.ShapeDtypeStruct((rows, TOP_K), I32), jax.ShapeDtypeStruct((rows, TOP_K), F32)],
        compiler_params=_cparams(("parallel",)),
        name="merge_router",
    )(x, yn, bonus, gate, yb, pg, lnw, lnb, _bf(w_a), _bf(w_b), _bf(w_o), vec(norm_ffn),
      w_router.astype(F32), vec(b_router))


def _moe_kernel(ce_ref, nu_ref, x_ref, sw_ref, wu_ref, bu_ref, wd_ref, bd_ref, o_ref, wu_bf, wd_bf):
    c = pl.program_id(0)
    e = ce_ref[c]
    prev = ce_ref[jnp.maximum(c - 1, 0)]

    @pl.when((c == 0) | (e != prev))
    def _():
        wu_bf[...] = _bf(wu_ref[0])
        wd_bf[...] = _bf(wd_ref[0])

    @pl.when(c < nu_ref[0])
    def _():
        u = _dot(x_ref[...], wu_bf[...]) + bu_ref[0]
        glu = jnp.minimum(u[:, :D_FF], SWIGLU_LIMIT)
        lin = jnp.clip(u[:, D_FF:], -SWIGLU_LIMIT, SWIGLU_LIMIT)
        act = glu * jax.nn.sigmoid(SWIGLU_ALPHA * glu) * (lin + 1.0)
        o_ref[...] = (_dot(_bf(act), wd_bf[...]) + bd_ref[0]) * sw_ref[...]

    @pl.when(c >= nu_ref[0])
    def _():
        o_ref[...] = jnp.zeros_like(o_ref)


def _moe_experts(chunk_e, n_used, xs, slot_w, w_up, b_up, w_down, b_down):
    slots = xs.shape[0]
    n_chunks = slots // MOE_ROWS
    grid_spec = pltpu.PrefetchScalarGridSpec(
        num_scalar_prefetch=2,
        grid=(n_chunks,),
        in_specs=[pl.BlockSpec((MOE_ROWS, D_MODEL), lambda c, ce, nu: (c, 0)),
                  pl.BlockSpec((MOE_ROWS, 1), lambda c, ce, nu: (c, 0)),
                  pl.BlockSpec((1, D_MODEL, 2 * D_FF), lambda c, ce, nu: (ce[c], 0, 0)),
                  pl.BlockSpec((1, 1, 2 * D_FF), lambda c, ce, nu: (ce[c], 0, 0)),
                  pl.BlockSpec((1, D_FF, D_MODEL), lambda c, ce, nu: (ce[c], 0, 0)),
                  pl.BlockSpec((1, 1, D_MODEL), lambda c, ce, nu: (ce[c], 0, 0))],
        out_specs=pl.BlockSpec((MOE_ROWS, D_MODEL), lambda c, ce, nu: (c, 0)),
        scratch_shapes=[pltpu.VMEM((D_MODEL, 2 * D_FF), BF16), pltpu.VMEM((D_FF, D_MODEL), BF16)],
    )
    return pl.pallas_call(
        _moe_kernel,
        grid_spec=grid_spec,
        out_shape=jax.ShapeDtypeStruct((slots, D_MODEL), F32),
        compiler_params=_cparams(("arbitrary",)),
        name="moe_experts",
    )(chunk_e, n_used, xs, slot_w, w_up, b_up.reshape(N_EXPERTS, 1, -1), w_down, b_down.reshape(N_EXPERTS, 1, -1))


def _moe(h, top_e, top_w, w_up, b_up, w_down, b_down):
    n = h.shape[0]
    n_assign = n * TOP_K
    flat_e = top_e.reshape(-1)
    counts = jnp.bincount(flat_e, length=N_EXPERTS)
    padded = (counts + MOE_ROWS - 1) // MOE_ROWS * MOE_ROWS
    pad_end = jnp.cumsum(padded)
    pad_start = pad_end - padded
    grp_start = jnp.cumsum(counts) - counts
    order = jnp.argsort(flat_e, stable=True)
    se = flat_e[order]
    dest = (pad_start[se] + jnp.arange(n_assign) - grp_start[se]).astype(I32)
    n_chunks = -(-(n_assign + N_EXPERTS * (MOE_ROWS - 1)) // MOE_ROWS)
    slots = n_chunks * MOE_ROWS
    slot_tok = jnp.full((slots,), n, I32).at[dest].set((order // TOP_K).astype(I32))
    slot_w = jnp.zeros((slots,), F32).at[dest].set(top_w.reshape(-1)[order])
    chunk_e = jnp.minimum(jnp.searchsorted(pad_end, jnp.arange(n_chunks) * MOE_ROWS, side='right'),
                          N_EXPERTS - 1).astype(I32)
    n_used = (pad_end[-1] // MOE_ROWS).astype(I32).reshape(1)
    h_pad = jnp.concatenate([h, jnp.zeros((1, D_MODEL), h.dtype)], axis=0)
    xs = h_pad[slot_tok]
    ys = _moe_experts(chunk_e, n_used, xs, slot_w.reshape(-1, 1), w_up, b_up, w_down, b_down)
    slot_of = jnp.zeros((n_assign,), I32).at[order].set(dest)
    return ys[slot_of.reshape(n, TOP_K)].sum(axis=1)


def _final_kernel(x_ref, m_ref, g_ref, o_ref):
    x = x_ref[...] + m_ref[...]
    o_ref[...] = x * lax.rsqrt(jnp.mean(x * x, axis=-1, keepdims=True) + NORM_EPS) * g_ref[...]


def _final(x1, moe, g, tm):
    rows = x1.shape[0]
    row = pl.BlockSpec((tm, D_MODEL), lambda i: (i, 0))
    return pl.pallas_call(
        _final_kernel,
        grid=(rows // tm,),
        in_specs=[row, row, _full((1, D_MODEL))],
        out_specs=row,
        out_shape=jax.ShapeDtypeStruct((rows, D_MODEL), F32),
        compiler_params=_cparams(("parallel",)),
        name="final_norm",
    )(x1, moe, g.reshape(1, -1).astype(F32))


def _row_tile(rows, cap):
    tm = cap
    while rows % tm:
        tm //= 2
    return tm


def _rwkv_group(p_a, prev, s0_tiles, b, t, rw, tc):
    mu, w0, wdu, a0, wiu, wgu, k_k, k_a, r_k = rw
    tm = _row_tile(b * t, 256)
    r, w, k2, v, kk, kb, gate, bonus = _rwkv_prep(p_a, prev, mu, w0, wdu, a0, wiu, wgu, k_k, k_a, r_k, tm)
    kt = lambda z: _to_key_tiles(z, b, t)
    yn, s_fin = _wkv_scan(kt(r), kt(w), kt(k2), _to_val_tiles(v, b, t), kt(kk), kt(kb), s0_tiles, tc)
    return _from_val_tiles(yn, b, t), gate, bonus, _state_from_tiles(s_fin, b)


def kernel(x_prompt, x_sample, cache_cmp_kv, cache_sel_kv, state_win_kv, state_rwkv, state_rwkv_shift, page_table,
           norm_attn, w_in, mu_shift, w0, w_decay_up, a0, w_iclr_up, w_gate_up, k_k, k_a, r_k, ln_x_w, ln_x_b,
           pe_cmp_k, pe_cmp_v, w_cmp_k1, w_cmp_k2, w_cmp_v1, w_cmp_v2, rel_bias, w_br_a, w_br_b, w_out,
           norm_ffn, w_router, b_router, w_up, b_up, w_down, b_down, norm_final):
    bp, tp, _ = x_prompt.shape
    bs, ts, _ = x_sample.shape
    depth = w_in.shape[0]
    past_len = page_table.shape[1] * cache_cmp_kv.shape[2]
    n_buf = state_win_kv.shape[2]
    assert ts == 1 and bp * A_HEADS == BH_LANES and bs % 8 == 0 and tp % LANES == 0
    xp = x_prompt.reshape(bp * tp, D_MODEL)
    xs = x_sample.reshape(bs, D_MODEL)
    new = {name: [] for name in ('cmp_p', 'sel_p', 'win_p', 'wkv_p', 'shift_p', 'cmp_s', 'sel_s', 'win_s', 'wkv_s', 'shift_s')}
    kv6 = lambda z, b, t: z.reshape(b, t, 2, B_KV_HEADS, HEAD_DIM)
    for l in range(depth):
        rw = (mu_shift[l], w0[l], w_decay_up[l], a0[l], w_iclr_up[l], w_gate_up[l], k_k[l], k_a[l], r_k[l])
        cw = (pe_cmp_k[l], pe_cmp_v[l], w_cmp_k1[l], w_cmp_k2[l], w_cmp_v1[l], w_cmp_v2[l])
        wa_nat, w_rest, w_t, w_n = _pack_w_in(w_in[l])
        g_attn = norm_attn[l].reshape(1, -1).astype(F32)

        tq = _row_tile(tp, 256)
        pa_t, q_t, kvt_c, kvt_s, kvt_w, gt_t, kv_c, kv_s, kv_w, pg = _project_t(xp, g_attn, w_t, w_n, tq, tp)
        r, w, k2, v, kk, kb, gate, bonus = _rwkv_prep_t(pa_t, *rw, tq)
        r, w, k2, kk, kb = _key_tiles([z.reshape(N_JP, LANES, tp) for z in (r, w, k2, kk, kb)], LANES)
        v = _val_tiles(v.reshape(HEAD_DIM, BH_LANES, tp), LANES).reshape(1, tp, HEAD_DIM, LANES)
        s0 = jnp.zeros((1, N_IO, N_JP, 8, LANES), F32)
        yn, s_fin = _wkv_scan(r, w, k2, v, kk, kb, s0, _row_tile(tp, 32))
        yn = _val_untile(yn.reshape(tp * HEAD_DIM, LANES), LANES).reshape(HEAD_DIM, bp, A_HEADS, tp)
        ckv = _compress(kv_c, *cw, rows=_row_tile(tp, 2048))
        y_b = _nsa_prompt(q_t, gt_t, ckv, kv_s, kvt_s, kv_w, kvt_w, rel_bias)
        x1, h, top_e, top_w = _merge(xp, yn, bonus, gate, y_b, pg, ln_x_w[l], ln_x_b[l], w_br_a[l], w_br_b[l], w_out[l],
                                     norm_ffn[l], w_router[l], b_router[l], tq, seq=tp)
        moe = _moe(h, top_e, top_w, w_up[l], b_up[l], w_down[l], b_down[l])
        xp_next = (x1, moe)
        n_win = min(WINDOW, tp)
        kv6t = lambda z: jnp.moveaxis(z.reshape(bp, 2, B_KV_HEADS, HEAD_DIM, tp), -1, 1)
        new['cmp_p'].append(kv6t(kvt_c))
        new['sel_p'].append(kv6t(kvt_s))
        new['win_p'].append(kv6t(kvt_w)[:, tp - n_win:])
        new['wkv_p'].append(_state_from_tiles(s_fin, bp))
        new['shift_p'].append(jnp.zeros((bp, A_PROJ), F32).at[:, A_PERM].set(pa_t[:, :, tp - 1]))

        s_a, q, kv_c, kv_s, kv_w, gt, pg = _project(xs, g_attn, wa_nat, w_rest, _row_tile(bs, 256))
        yn, gate, bonus, wkv_s = _rwkv_group(s_a, state_rwkv_shift[l], _state_to_tiles(state_rwkv[l].astype(F32), bs),
                                             bs, 1, rw, 1)
        tok_minor = lambda z: jnp.moveaxis(z, 1, -1).reshape(z.shape[0], KV_ROW, z.shape[1])
        ckv = _compress_paged(tok_minor(cache_cmp_kv[l]), page_table, cw)
        q3, gt3 = q.reshape(bs, 1, B_WIDTH), gt.reshape(bs, 1, GATE_PAD)
        oc3, idx = _nsa_sample_select(q3, gt3, ckv, rel_bias, past_len)
        y_b = _nsa_sample_attend(idx, page_table, q3, gt3, oc3, kv_s.reshape(bs, KV_ROW, 1), kv_w.reshape(bs, 1, KV_ROW),
                                 tok_minor(state_win_kv[l]), tok_minor(cache_sel_kv[l]),
                                 rel_bias, past_len).reshape(bs, B_WIDTH)
        x1, h, top_e, top_w = _merge(xs, yn, bonus, gate, y_b, pg, ln_x_w[l], ln_x_b[l], w_br_a[l], w_br_b[l], w_out[l],
                                     norm_ffn[l], w_router[l], b_router[l], _row_tile(bs, 256))
        moe = _moe(h, top_e, top_w, w_up[l], b_up[l], w_down[l], b_down[l])
        xs_next = (x1, moe)
        new['cmp_s'].append(kv6(kv_c, bs, 1))
        new['sel_s'].append(kv6(kv_s, bs, 1))
        new['win_s'].append(jnp.concatenate([state_win_kv[l], kv6(kv_w, bs, 1)], axis=1)[:, 1:])
        new['wkv_s'].append(wkv_s.astype(state_rwkv.dtype))
        new['shift_s'].append(s_a)

        last = l == depth - 1
        g_fin = norm_final if last else jnp.ones((D_MODEL,), F32)
        if last:
            xp = _final(*xp_next, g_fin, _row_tile(bp * tp, 512))
            xs = _final(*xs_next, g_fin, _row_tile(bs, 512))
        else:
            xp = xp_next[0] + xp_next[1]
            xs = xs_next[0] + xs_next[1]
    return (xp.reshape(bp, tp, D_MODEL), xs.reshape(bs, ts, D_MODEL),
            jnp.stack(new['cmp_p']), jnp.stack(new['sel_p']), jnp.stack(new['win_p']),
            jnp.stack(new['wkv_p']), jnp.stack(new['shift_p']),
            jnp.stack(new['cmp_s']), jnp.stack(new['sel_s']), jnp.stack(new['win_s']),
            jnp.stack(new['wkv_s']), jnp.stack(new['shift_s']))
```

```python
import functools
import math

import jax
import jax.numpy as jnp
import numpy as np
from jax import lax
from jax.experimental import pallas as pl
from jax.experimental.pallas import tpu as pltpu

F32 = jnp.float32
BF16 = jnp.bfloat16
I32 = jnp.int32
HIGHEST = lax.Precision.HIGHEST

D_MODEL = 1024
HEAD_DIM = 64
A_HEADS = 8
A_WIDTH = A_HEADS * HEAD_DIM
A_DECAY_RANK = 64
A_ICLR_RANK = 64
A_GATE_RANK = 128
A_GN_EPS = 64e-5
A_PROJ = 3 * A_WIDTH + A_DECAY_RANK + A_ICLR_RANK + A_GATE_RANK
B_HEADS = 8
B_KV_HEADS = 2
B_GROUP = B_HEADS // B_KV_HEADS
B_WIDTH = B_HEADS * HEAD_DIM
KV_WIDTH = B_KV_HEADS * HEAD_DIM
KV_ROW = 2 * KV_WIDTH
CMP_BLOCK = 32
CMP_HIDDEN = 128
SEL_BLOCK = 64
SEL_RATIO = SEL_BLOCK // CMP_BLOCK
TOP_N = 16
WINDOW = 512
Q_BLOCK = 128
FORCE_SCORE = 1e4
N_BUCKETS = 32
MAX_EXACT = N_BUCKETS // 2
MAX_DISTANCE = 1024
N_EXPERTS = 32
TOP_K = 4
D_FF = 1024
SWIGLU_ALPHA = 1.702
SWIGLU_LIMIT = 7.0
NORM_EPS = 1e-6
NEG_INF = -1e30
SCALE = HEAD_DIM ** -0.5
GATE_PAD = 128
GATE_PROJ = 2 * D_MODEL
MOE_ROWS = 256
LANES = 128
VMEM_LIMIT = 56 * 1024 * 1024


def _cparams(sem):
    return pltpu.CompilerParams(dimension_semantics=sem, vmem_limit_bytes=VMEM_LIMIT)


def _full(shape):
    n = len(shape)
    return pl.BlockSpec(shape, lambda *_: (0,) * n)


def _dot(a, b):
    return jnp.dot(a, b, preferred_element_type=F32)


def _dot_nt(a, b):
    return lax.dot_general(a, b, (((1,), (1,)), ((), ())), preferred_element_type=F32)


def _bf(x):
    return x.astype(BF16)


PROJ_SPLITS = (B_WIDTH, KV_ROW, KV_ROW, KV_ROW, GATE_PAD, GATE_PROJ)
CH_MAJOR = (np.arange(A_HEADS)[None, :] * HEAD_DIM + np.arange(HEAD_DIM)[:, None]).reshape(-1)
A_PERM = np.concatenate([CH_MAJOR, A_WIDTH + CH_MAJOR, 2 * A_WIDTH + CH_MAJOR, np.arange(3 * A_WIDTH, A_PROJ)])


def _rms_bf16(x_ref, g_ref):
    x = x_ref[...]
    return _bf(x * lax.rsqrt(jnp.mean(x * x, axis=-1, keepdims=True) + NORM_EPS) * g_ref[...])


def _proj_kernel(x_ref, g_ref, wa_ref, w_ref, oa_ref, *o_refs):
    h = _rms_bf16(x_ref, g_ref)
    oa_ref[...] = _dot(h, wa_ref[...])
    c = 0
    for o_ref, n in zip(o_refs, PROJ_SPLITS):
        o_ref[...] = _dot(h, w_ref[:, c:c + n])
        c += n


def _project(x, g, wa, w, tm):
    rows = x.shape[0]
    splits = (A_PROJ,) + PROJ_SPLITS
    return pl.pallas_call(
        _proj_kernel,
        grid=(rows // tm,),
        in_specs=[pl.BlockSpec((tm, D_MODEL), lambda i: (i, 0)), _full((1, D_MODEL)), _full(wa.shape), _full(w.shape)],
        out_specs=[pl.BlockSpec((tm, n), lambda i: (i, 0)) for n in splits],
        out_shape=[jax.ShapeDtypeStruct((rows, n), F32) for n in splits],
        compiler_params=_cparams(("parallel",)),
        name="norm_proj",
    )(x, g, wa, w)


PROJ_T_SPLITS = (A_PROJ, B_WIDTH, KV_ROW, KV_ROW, KV_ROW, GATE_PAD)
PROJ_N_SPLITS = (KV_ROW, KV_ROW, KV_ROW, GATE_PROJ)


def _proj_t_kernel(x_ref, g_ref, wt_ref, wn_ref, *o_refs):
    h = _rms_bf16(x_ref, g_ref)
    t = _dot_nt(wt_ref[...], h)
    c = 0
    for o_ref, n in zip(o_refs[:len(PROJ_T_SPLITS)], PROJ_T_SPLITS):
        o_ref[0] = t[c:c + n]
        c += n
    c = 0
    for o_ref, n in zip(o_refs[len(PROJ_T_SPLITS):], PROJ_N_SPLITS):
        o_ref[...] = _dot(h, wn_ref[:, c:c + n])
        c += n


def _project_t(x, g, wt, wn, tm, seq):
    rows = x.shape[0]
    per = seq // tm
    return pl.pallas_call(
        _proj_t_kernel,
        grid=(rows // tm,),
        in_specs=[pl.BlockSpec((tm, D_MODEL), lambda i: (i, 0)), _full((1, D_MODEL)), _full(wt.shape), _full(wn.shape)],
        out_specs=[pl.BlockSpec((1, n, tm), lambda i: (i // per, 0, i % per)) for n in PROJ_T_SPLITS]
                  + [pl.BlockSpec((tm, n), lambda i: (i, 0)) for n in PROJ_N_SPLITS],
        out_shape=[jax.ShapeDtypeStruct((rows // seq, n, seq), F32) for n in PROJ_T_SPLITS]
                  + [jax.ShapeDtypeStruct((rows, n), F32) for n in PROJ_N_SPLITS],
        compiler_params=_cparams(("parallel",)),
        name="norm_proj_t",
    )(x, g, wt, wn)


def _pack_w_in(w_in):
    a, rest = w_in[:, :A_PROJ], w_in[:, A_PROJ:]
    q, kv, gt, pg = (rest[:, :B_WIDTH], rest[:, B_WIDTH:B_WIDTH + 3 * KV_ROW],
                     rest[:, B_WIDTH + 3 * KV_ROW:B_WIDTH + 3 * KV_ROW + 3 * B_HEADS],
                     rest[:, B_WIDTH + 3 * KV_ROW + 3 * B_HEADS:])
    gt = jnp.pad(gt, ((0, 0), (0, GATE_PAD - 3 * B_HEADS)))
    w_t = jnp.concatenate([a[:, A_PERM], q, kv, gt], axis=1).T
    return _bf(a), _bf(jnp.concatenate([q, kv, gt, pg], axis=1)), _bf(w_t), _bf(jnp.concatenate([kv, pg], axis=1))


def _softplus(z):
    return jnp.maximum(z, 0.0) + jnp.log1p(jnp.exp(-jnp.abs(z)))


def _rwkv_prep_kernel(p_ref, prev_ref, mu_ref, w0_ref, wdu_ref, a0_ref, wiu_ref, wgu_ref, kk_ref, ka_ref, rk_ref,
                      ones_ref, r_o, w_o, k_o, v_o, kk_o, kb_o, g_o, bo_o):
    p = p_ref[...]
    ps = p + mu_ref[...] * (prev_ref[...] - p)
    r = ps[:, 0:A_WIDTH]
    k = ps[:, A_WIDTH:2 * A_WIDTH]
    v = ps[:, 2 * A_WIDTH:3 * A_WIDTH]
    c = 3 * A_WIDTH
    xw = ps[:, c:c + A_DECAY_RANK]
    xa = ps[:, c + A_DECAY_RANK:c + A_DECAY_RANK + A_ICLR_RANK]
    xg = ps[:, c + A_DECAY_RANK + A_ICLR_RANK:]
    w_log = -_softplus(-(w0_ref[...] + _dot(_bf(jnp.tanh(xw)), wdu_ref[...]))) - 0.5
    decay = jnp.exp(-jnp.exp(w_log))
    a = jax.nn.sigmoid(a0_ref[...] + _dot(_bf(xa), wiu_ref[...]))
    gate = _dot(_bf(jax.nn.sigmoid(xg)), wgu_ref[...])
    ones = ones_ref[...]
    kk = k * kk_ref[...]
    ss = jnp.dot(kk * kk, ones, precision=HIGHEST, preferred_element_type=F32)
    kk = kk / jnp.maximum(jnp.sqrt(ss), 1e-12)
    k2 = k * (1.0 + (a - 1.0) * ka_ref[...])
    rk = jnp.dot(r * k2 * rk_ref[...], ones, precision=HIGHEST, preferred_element_type=F32)
    r_o[...] = r
    w_o[...] = decay
    k_o[...] = k2
    v_o[...] = v
    kk_o[...] = kk
    kb_o[...] = kk * a
    g_o[...] = gate
    bo_o[...] = rk * v


def _rwkv_prep(p, prev, mu, w0, wdu, a0, wiu, wgu, k_k, k_a, r_k, tm):
    rows = p.shape[0]
    head = np.arange(A_WIDTH) // HEAD_DIM
    ones = jnp.asarray(head[:, None] == head[None, :], F32)
    row = lambda z: z.reshape(1, -1).astype(F32)
    spec_in = pl.BlockSpec((tm, A_PROJ), lambda i: (i, 0))
    spec_o = pl.BlockSpec((tm, A_WIDTH), lambda i: (i, 0))
    return pl.pallas_call(
        _rwkv_prep_kernel,
        grid=(rows // tm,),
        in_specs=[spec_in, spec_in, _full((1, A_PROJ)), _full((1, A_WIDTH)), _full((A_DECAY_RANK, A_WIDTH)),
                  _full((1, A_WIDTH)), _full((A_ICLR_RANK, A_WIDTH)), _full((A_GATE_RANK, A_WIDTH)),
                  _full((1, A_WIDTH)), _full((1, A_WIDTH)), _full((1, A_WIDTH)), _full((A_WIDTH, A_WIDTH))],
        out_specs=[spec_o] * 8,
        out_shape=[jax.ShapeDtypeStruct((rows, A_WIDTH), F32)] * 8,
        compiler_params=_cparams(("parallel",)),
        name="rwkv_prep",
    )(p, prev, row(mu), row(w0), _bf(wdu), row(a0), _bf(wiu), _bf(wgu), row(k_k), row(k_a), row(r_k), ones)


def _rwkv_prep_t_kernel(p_ref, mu_ref, w0_ref, wdu_ref, a0_ref, wiu_ref, wgu_ref, kk_ref, ka_ref, rk_ref,
                        r_o, w_o, k_o, v_o, kk_o, kb_o, g_o, bo_o, carry):
    tt = p_ref.shape[2]

    @pl.when(pl.program_id(1) == 0)
    def _():
        carry[...] = jnp.zeros_like(carry)

    p = p_ref[0]
    lane = lax.broadcasted_iota(I32, (1, tt), 1)
    prev = jnp.where(lane == 0, carry[:, 0:1], pltpu.roll(p, 1, 1))
    carry[:, 0:1] = p[:, tt - 1:tt]
    ps = p + mu_ref[...] * (prev - p)
    r = ps[0:A_WIDTH]
    k = ps[A_WIDTH:2 * A_WIDTH]
    v = ps[2 * A_WIDTH:3 * A_WIDTH]
    c = 3 * A_WIDTH
    xw = ps[c:c + A_DECAY_RANK]
    xa = ps[c + A_DECAY_RANK:c + A_DECAY_RANK + A_ICLR_RANK]
    xg = ps[c + A_DECAY_RANK + A_ICLR_RANK:]

    def head_sum(x):
        s = jnp.sum(x.reshape(HEAD_DIM, A_HEADS, tt), axis=0)
        return jnp.broadcast_to(s[None], (HEAD_DIM, A_HEADS, tt)).reshape(A_WIDTH, tt)

    w_log = -_softplus(-(w0_ref[...] + _dot(wdu_ref[...], _bf(jnp.tanh(xw))))) - 0.5
    decay = jnp.exp(-jnp.exp(w_log))
    a = jax.nn.sigmoid(a0_ref[...] + _dot(wiu_ref[...], _bf(xa)))
    gate = _dot(wgu_ref[...], _bf(jax.nn.sigmoid(xg)))
    kk = k * kk_ref[...]
    kk = kk / jnp.maximum(jnp.sqrt(head_sum(kk * kk)), 1e-12)
    k2 = k * (1.0 + (a - 1.0) * ka_ref[...])
    rk = head_sum(r * k2 * rk_ref[...])
    for o_ref, val in ((r_o, r), (w_o, decay), (k_o, k2), (kk_o, kk), (kb_o, kk * a)):
        o_ref[...] = val.reshape(N_JP, 2, 1, A_HEADS, tt)
    v_o[...] = v.reshape(HEAD_DIM, 1, A_HEADS, tt)
    g_o[0] = gate
    bo_o[0] = rk * v


def _rwkv_prep_t(p_t, mu, w0, wdu, a0, wiu, wgu, k_k, k_a, r_k, tt):
    b, _, t = p_t.shape
    col = lambda z, perm: z.reshape(-1)[perm].reshape(-1, 1).astype(F32)
    up = lambda w: _bf(w[:, CH_MAJOR].T)
    kshape = jax.ShapeDtypeStruct((N_JP, 2, b, A_HEADS, t), F32)
    kspec = pl.BlockSpec((N_JP, 2, 1, A_HEADS, tt), lambda bi, i: (0, 0, bi, 0, i))
    vspec = pl.BlockSpec((HEAD_DIM, 1, A_HEADS, tt), lambda bi, i: (0, bi, 0, i))
    nspec = pl.BlockSpec((1, A_WIDTH, tt), lambda bi, i: (bi, 0, i))
    nshape = jax.ShapeDtypeStruct((b, A_WIDTH, t), F32)
    return pl.pallas_call(
        _rwkv_prep_t_kernel,
        grid=(b, t // tt),
        in_specs=[pl.BlockSpec((1, A_PROJ, tt), lambda bi, i: (bi, 0, i)), _full((A_PROJ, 1)), _full((A_WIDTH, 1)),
                  _full((A_WIDTH, A_DECAY_RANK)), _full((A_WIDTH, 1)), _full((A_WIDTH, A_ICLR_RANK)),
                  _full((A_WIDTH, A_GATE_RANK)), _full((A_WIDTH, 1)), _full((A_WIDTH, 1)), _full((A_WIDTH, 1))],
        out_specs=[kspec, kspec, kspec, vspec, kspec, kspec, nspec, nspec],
        out_shape=[kshape, kshape, kshape, jax.ShapeDtypeStruct((HEAD_DIM, b, A_HEADS, t), F32), kshape, kshape,
                   nshape, nshape],
        scratch_shapes=[pltpu.VMEM((A_PROJ, LANES), F32)],
        compiler_params=_cparams(("parallel", "arbitrary")),
        name="rwkv_prep_t",
    )(p_t, col(mu, A_PERM), col(w0, CH_MAJOR), up(wdu), col(a0, CH_MAJOR), up(wiu), up(wgu),
      col(k_k, CH_MAJOR), col(k_a, CH_MAJOR), col(r_k, CH_MAJOR))


def _key_tiles_kernel(*refs):
    n = len(refs) // 2
    for z_ref, o_ref in zip(refs[:n], refs[n:]):
        for jp in range(N_JP):
            o_ref[0, jp] = z_ref[jp].T


def _key_tiles(zs, tt):
    t = zs[0].shape[2]
    return pl.pallas_call(
        _key_tiles_kernel,
        grid=(t // tt,),
        in_specs=[pl.BlockSpec((N_JP, LANES, tt), lambda i: (0, 0, i))] * len(zs),
        out_specs=[pl.BlockSpec((1, N_JP, tt, LANES), lambda i: (0, 0, i, 0))] * len(zs),
        out_shape=[jax.ShapeDtypeStruct((1, N_JP, t, LANES), F32)] * len(zs),
        compiler_params=_cparams(("parallel",)),
        name="wkv_key_tiles",
    )(*zs)


def _val_tiles_kernel(z_ref, o_ref, *, tt):
    for i in range(HEAD_DIM):
        x = z_ref[i]
        o_ref[pl.ds(i, tt, stride=HEAD_DIM), :] = jnp.concatenate([x, x], axis=0).T


def _val_tiles(z, tt):
    t = z.shape[2]
    return pl.pallas_call(
        functools.partial(_val_tiles_kernel, tt=tt),
        grid=(t // tt,),
        in_specs=[pl.BlockSpec((HEAD_DIM, BH_LANES, tt), lambda i: (0, 0, i))],
        out_specs=pl.BlockSpec((tt * HEAD_DIM, LANES), lambda i: (i, 0)),
        out_shape=jax.ShapeDtypeStruct((t * HEAD_DIM, LANES), F32),
        compiler_params=_cparams(("parallel",)),
        name="wkv_val_tiles",
    )(z)


def _val_untile_kernel(y_ref, o_ref, *, tt):
    for i in range(HEAD_DIM):
        o_ref[i] = y_ref[pl.ds(i, tt, stride=HEAD_DIM), :].T[0:BH_LANES]


def _val_untile(y, tt):
    t = y.shape[0] // HEAD_DIM
    return pl.pallas_call(
        functools.partial(_val_untile_kernel, tt=tt),
        grid=(t // tt,),
        in_specs=[pl.BlockSpec((tt * HEAD_DIM, LANES), lambda i: (i, 0))],
        out_specs=pl.BlockSpec((HEAD_DIM, BH_LANES, tt), lambda i: (0, 0, i)),
        out_shape=jax.ShapeDtypeStruct((HEAD_DIM, BH_LANES, t), F32),
        compiler_params=_cparams(("parallel",)),
        name="wkv_val_untile",
    )(y)


N_IO = HEAD_DIM // 8
N_JP = HEAD_DIM // 2
BH_LANES = 64


def _wkv_kernel(r_ref, w_ref, k_ref, v_ref, a_ref, b_ref, s0_ref, y_ref, sfin_ref, s_scr, *, tc):
    t_blk = pl.program_id(1)

    @pl.when(t_blk == 0)
    def _():
        s_scr[...] = s0_ref[0]

    def bc(ref, t, jp):
        return jnp.broadcast_to(ref[0, jp, pl.ds(t, 1), :], (8, LANES))

    def fold(x):
        return x + pltpu.roll(x, BH_LANES, 1)

    def step(t, carry):
        acc = [jnp.zeros((8, LANES), F32) for _ in range(N_IO)]
        for jp in range(N_JP):
            a_ = bc(a_ref, t, jp)
            for io in range(N_IO):
                acc[io] = acc[io] + s_scr[io, jp] * a_
        sa = [-fold(acc[io]) for io in range(N_IO)]
        vv = [v_ref[0, t, io * 8:(io + 1) * 8, :] for io in range(N_IO)]
        yacc = [jnp.zeros((8, LANES), F32) for _ in range(N_IO)]
        for jp in range(N_JP):
            w_ = bc(w_ref, t, jp)
            b_ = bc(b_ref, t, jp)
            k_ = bc(k_ref, t, jp)
            r_ = bc(r_ref, t, jp)
            for io in range(N_IO):
                s = s_scr[io, jp] * w_ + sa[io] * b_ + vv[io] * k_
                s_scr[io, jp] = s
                yacc[io] = yacc[io] + s * r_
        y = [fold(yacc[io]) for io in range(N_IO)]
        tot = y[0]
        for io in range(1, N_IO):
            tot = tot + y[io]
        mu = jnp.sum(tot, axis=0, keepdims=True) * (1.0 / HEAD_DIM)
        d = [y[io] - mu for io in range(N_IO)]
        sq = d[0] * d[0]
        for io in range(1, N_IO):
            sq = sq + d[io] * d[io]
        var = jnp.sum(sq, axis=0, keepdims=True) * (1.0 / HEAD_DIM)
        inv = lax.rsqrt(var + A_GN_EPS)
        for io in range(N_IO):
            y_ref[0, t, io * 8:(io + 1) * 8, :] = d[io] * inv
        return carry

    lax.fori_loop(0, tc, step, 0)

    @pl.when(t_blk == pl.num_programs(1) - 1)
    def _():
        sfin_ref[0] = s_scr[...]


def _wkv_scan(r, w, k, v, a, b, s0, tc):
    nb, t = v.shape[:2]
    kspec = pl.BlockSpec((1, N_JP, tc, LANES), lambda n, i: (n, 0, i, 0))
    vspec = pl.BlockSpec((1, tc, HEAD_DIM, LANES), lambda n, i: (n, i, 0, 0))
    sspec = pl.BlockSpec((1, N_IO, N_JP, 8, LANES), lambda n, i: (n, 0, 0, 0, 0))
    return pl.pallas_call(
        functools.partial(_wkv_kernel, tc=tc),
        grid=(nb, t // tc),
        in_specs=[kspec, kspec, kspec, vspec, kspec, kspec, sspec],
        out_specs=[vspec, sspec],
        out_shape=[jax.ShapeDtypeStruct((nb, t, HEAD_DIM, LANES), F32),
                   jax.ShapeDtypeStruct((nb, N_IO, N_JP, 8, LANES), F32)],
        scratch_shapes=[pltpu.VMEM((N_IO, N_JP, 8, LANES), F32)],
        compiler_params=_cparams(("parallel", "arbitrary")),
        name="wkv_scan",
    )(r, w, k, v, a, b, s0)


def _to_key_tiles(x, b, t):
    nb = b // 8
    x = x.reshape(nb, 8, t, A_HEADS, N_JP, 2).transpose(0, 4, 2, 5, 1, 3)
    return x.reshape(nb, N_JP, t, LANES)


def _to_val_tiles(x, b, t):
    nb = b // 8
    x = x.reshape(nb, 8, t, A_HEADS, HEAD_DIM).transpose(0, 2, 4, 1, 3).reshape(nb, t, HEAD_DIM, BH_LANES)
    return jnp.concatenate([x, x], axis=-1)


def _from_val_tiles(y, b, t):
    nb = b // 8
    y = y[..., :BH_LANES].reshape(nb, t, HEAD_DIM, 8, A_HEADS).transpose(0, 3, 1, 4, 2)
    return y.reshape(b * t, A_WIDTH)


def _state_to_tiles(s, b):
    nb = b // 8
    s = s.reshape(nb, 8, A_HEADS, N_IO, 8, N_JP, 2).transpose(0, 3, 5, 4, 6, 1, 2)
    return s.reshape(nb, N_IO, N_JP, 8, LANES)


def _state_from_tiles(s, b):
    nb = b // 8
    s = s.reshape(nb, N_IO, N_JP, 8, 2, 8, A_HEADS).transpose(0, 5, 6, 1, 3, 2, 4)
    return s.reshape(b, A_HEADS, HEAD_DIM, HEAD_DIM)


def _compress_paged_kernel(pt_ref, pe_ref, w1_ref, w2_ref, cache_ref, o_ref, buf, xk, xv, sem, *, n_pages, page_len):
    i = pl.program_id(0)
    slot = i % 2

    def copy(s_idx, sl, p):
        return pltpu.make_async_copy(cache_ref.at[pt_ref[s_idx * n_pages + p]], buf.at[sl, p], sem.at[sl, p])

    def fetch(s_idx, sl):
        for p in range(n_pages):
            copy(s_idx, sl, p).start()

    @pl.when(i == 0)
    def _():
        fetch(0, 0)

    @pl.when(i + 1 < pl.num_programs(0))
    def _():
        fetch(i + 1, 1 - slot)

    for p in range(n_pages):
        copy(i, slot, p).wait()
        t = buf[slot, p].T
        xk[p * page_len:(p + 1) * page_len, :] = t[:, :KV_WIDTH]
        xv[p * page_len:(p + 1) * page_len, :] = t[:, KV_WIDTH:]
    _compress_kernel(xk, xv, pe_ref, w1_ref, w2_ref, o_ref, nblk=n_pages * page_len // CMP_BLOCK)


def _compress_kernel(xk_ref, xv_ref, pe_ref, w1_ref, w2_ref, o_ref, *, nblk):
    for c, x_ref in enumerate((xk_ref, xv_ref)):
        acc = jnp.zeros((nblk, B_KV_HEADS * CMP_HIDDEN), F32)
        for tau in range(CMP_BLOCK):
            x = x_ref[pl.ds(tau, nblk, stride=CMP_BLOCK), :] + pe_ref[c, tau:tau + 1, :]
            acc = acc + _dot(_bf(x), w1_ref[c, tau])
        o_ref[:, c * KV_WIDTH:(c + 1) * KV_WIDTH] = _dot(_bf(jax.nn.gelu(acc)), w2_ref[c])


def _compress_weights(pe_k, pe_v, w_k1, w_k2, w_v1, w_v2):
    pe = jnp.stack([jnp.concatenate([pe_k, pe_k], axis=1), jnp.concatenate([pe_v, pe_v], axis=1)]).astype(F32)
    eye = jnp.eye(B_KV_HEADS, dtype=F32)
    w1 = jnp.stack([w_k1, w_v1]).reshape(2, CMP_BLOCK, HEAD_DIM, CMP_HIDDEN)
    w1 = jnp.einsum('ctdh,ge->ctgdeh', w1, eye).reshape(2, CMP_BLOCK, KV_WIDTH, B_KV_HEADS * CMP_HIDDEN)
    w2 = jnp.einsum('chd,ge->cghed', jnp.stack([w_k2, w_v2]), eye).reshape(2, B_KV_HEADS * CMP_HIDDEN, KV_WIDTH)
    return pe, _bf(w1), _bf(w2)


COMPRESS_W_SPECS = ((2, CMP_BLOCK, KV_WIDTH), (2, CMP_BLOCK, KV_WIDTH, B_KV_HEADS * CMP_HIDDEN),
                    (2, B_KV_HEADS * CMP_HIDDEN, KV_WIDTH))


def _compress_paged(cache_t, page_table, cw):
    s, n_pages = page_table.shape
    page = cache_t.shape[2]
    past = n_pages * page
    nblk = past // CMP_BLOCK
    grid_spec = pltpu.PrefetchScalarGridSpec(
        num_scalar_prefetch=1,
        grid=(s,),
        in_specs=[pl.BlockSpec(shp, lambda i, pt, n=len(shp): (0,) * n) for shp in COMPRESS_W_SPECS]
                 + [pl.BlockSpec(memory_space=pl.ANY)],
        out_specs=pl.BlockSpec((nblk, KV_ROW), lambda i, pt: (i, 0)),
        scratch_shapes=[pltpu.VMEM((2, n_pages, KV_ROW, page), F32), pltpu.VMEM((past, KV_WIDTH), F32),
                        pltpu.VMEM((past, KV_WIDTH), F32), pltpu.SemaphoreType.DMA((2, n_pages))],
    )
    return pl.pallas_call(
        functools.partial(_compress_paged_kernel, n_pages=n_pages, page_len=page),
        grid_spec=grid_spec,
        out_shape=jax.ShapeDtypeStruct((s * nblk, KV_ROW), F32),
        compiler_params=_cparams(("arbitrary",)),
        name="nsa_compress_paged",
    )(page_table.reshape(-1), *_compress_weights(*cw), cache_t)


def _compress(x, pe_k, pe_v, w_k1, w_k2, w_v1, w_v2, rows):
    n = x.shape[0]
    nblk = rows // CMP_BLOCK
    pe, w1, w2 = _compress_weights(pe_k, pe_v, w_k1, w_k2, w_v1, w_v2)
    return pl.pallas_call(
        functools.partial(_compress_kernel, nblk=nblk),
        grid=(n // rows,),
        in_specs=[pl.BlockSpec((rows, KV_WIDTH), lambda i: (i, 0)), pl.BlockSpec((rows, KV_WIDTH), lambda i: (i, 1)),
                  ] + [_full(shp) for shp in COMPRESS_W_SPECS],
        out_specs=pl.BlockSpec((nblk, KV_ROW), lambda i: (i, 0)),
        out_shape=jax.ShapeDtypeStruct((n // CMP_BLOCK, KV_ROW), F32),
        compiler_params=_cparams(("parallel",)),
        name="nsa_compress",
    )(x, x, pe, w1, w2)


def _rel_bucket(dist):
    n = jnp.maximum(dist, 0)
    log_ratio = jnp.log(jnp.maximum(n, 1).astype(F32) / MAX_EXACT) / math.log(MAX_DISTANCE / MAX_EXACT)
    large = jnp.minimum(MAX_EXACT + (log_ratio * (N_BUCKETS - MAX_EXACT)).astype(I32), N_BUCKETS - 1)
    return jnp.where(n < MAX_EXACT, n, large)


def _bias_of(rel_bias, dist):
    return jnp.moveaxis(rel_bias.astype(F32)[_rel_bucket(dist)], -1, 0)


KEY_TILE = 512


def _nsa_prompt_kernel(q_ref, gate_ref, ck_ref, ckt_ref, ksel_ref, vsel_ref, kwin_ref, vwin_ref, bct_ref, tz_ref,
                       o_ref, imp_scr, mask_scr, *, nc, ns, nq, top_n, wtiles):
    i = pl.program_id(1)
    qb = Q_BLOCK
    gates = jax.nn.sigmoid(gate_ref[0])
    pos_l = i * qb + lax.broadcasted_iota(I32, (1, qb), 1)
    okT = pos_l >= lax.broadcasted_iota(I32, (nc, qb), 0) * CMP_BLOCK + (CMP_BLOCK - 1)
    blk = lax.broadcasted_iota(I32, (ns, qb), 0)
    cur = pos_l // SEL_BLOCK
    forced = (blk == 0) | (blk == cur) | (blk == cur - 1)
    k_s = lax.broadcasted_iota(I32, (qb, qb), 0)
    q_l = lax.broadcasted_iota(I32, (qb, qb), 1)
    n_e = lax.broadcasted_iota(I32, (qb, ns), 1)
    k_e = lax.broadcasted_iota(I32, (qb, ns), 0) // SEL_BLOCK
    wk = (wtiles + 1) * qb
    wb = jnp.maximum(i - wtiles, 0)
    d_w = pos_l - (wb * qb + lax.broadcasted_iota(I32, (wk, qb), 0))
    madd_w = jnp.where((d_w >= 0) & (d_w < WINDOW), 0.0, NEG_INF)
    n_tiles = (i * qb + qb + KEY_TILE - 1) // KEY_TILE
    per = KEY_TILE // qb

    def bias_tiles(h, first_blk, count):
        return jnp.concatenate([tz_ref[h, jnp.clip(i - (first_blk + c), 0, nq - 1)] for c in range(count)], axis=0)

    for g in range(B_KV_HEADS):
        ksl = slice(g * HEAD_DIM, (g + 1) * HEAD_DIM)
        vsl = slice(KV_WIDTH + g * HEAD_DIM, KV_WIDTH + (g + 1) * HEAD_DIM)
        qg = [_bf(q_ref[0, (g * B_GROUP + r) * HEAD_DIM:(g * B_GROUP + r + 1) * HEAD_DIM, :] * SCALE)
              for r in range(B_GROUP)]
        kc = _bf(ck_ref[:, ksl])
        vct = _bf(ckt_ref[0, vsl, :])
        o_c = []
        impT = jnp.zeros((nc, qb), F32)
        for r in range(B_GROUP):
            sT = jnp.where(okT, _dot(kc, qg[r]) + bct_ref[g * B_GROUP + r], NEG_INF)
            eT = jnp.exp(sT - jnp.max(sT, axis=0, keepdims=True))
            pT = eT / jnp.sum(eT, axis=0, keepdims=True) * okT.astype(F32)
            impT = impT + pT
            o_c.append(_dot(vct, _bf(pT)))
        imp_scr[...] = impT
        imp2 = imp_scr[pl.ds(0, ns, stride=SEL_RATIO), :] + imp_scr[pl.ds(1, ns, stride=SEL_RATIO), :]
        score = jnp.where(blk <= cur, imp2 + FORCE_SCORE * forced.astype(F32), NEG_INF)
        rank = jnp.zeros((ns, qb), I32)
        for m in range(ns):
            row = score[m:m + 1, :]
            rank = rank + ((row > score) | ((row == score) & (m < blk))).astype(I32)
        selT = _bf((rank < top_n) & (score > NEG_INF / 2))
        mask_scr[...] = jnp.full(mask_scr.shape, NEG_INF, F32)

        def fill(j, carry):
            m = _dot(_bf(n_e == 2 * j + k_e), selT) > 0.5
            m = m & ((j < i) | (q_l >= k_s))
            mask_scr[pl.ds(pl.multiple_of(j * qb, qb), qb), :] = jnp.where(m, 0.0, NEG_INF)
            return carry

        lax.fori_loop(0, i + 1, fill, 0)

        def body(jt, carry):
            ms, ls, accs = carry
            k0 = pl.multiple_of(jt * KEY_TILE, KEY_TILE)
            kt = _bf(ksel_ref[pl.ds(k0, KEY_TILE), ksl])
            vt = _bf(vsel_ref[0, vsl, pl.ds(k0, KEY_TILE)])
            madd = mask_scr[pl.ds(k0, KEY_TILE), :]
            ms2, ls2, accs2 = [], [], []
            for r in range(B_GROUP):
                s = _dot(kt, qg[r]) + bias_tiles(g * B_GROUP + r, jt * per, per) + madd
                m_new = jnp.maximum(ms[r], jnp.max(s, axis=0, keepdims=True))
                alpha = jnp.exp(ms[r] - m_new)
                p = jnp.exp(s - m_new)
                ls2.append(alpha * ls[r] + jnp.sum(p, axis=0, keepdims=True))
                accs2.append(alpha * accs[r] + _dot(vt, _bf(p)))
                ms2.append(m_new)
            return tuple(ms2), tuple(ls2), tuple(accs2)

        init = (tuple(jnp.full((1, qb), NEG_INF, F32) for _ in range(B_GROUP)),
                tuple(jnp.zeros((1, qb), F32) for _ in range(B_GROUP)),
                tuple(jnp.zeros((HEAD_DIM, qb), F32) for _ in range(B_GROUP)))
        _, ls, accs = lax.fori_loop(0, n_tiles, body, init)
        w0 = pl.multiple_of(wb * qb, qb)
        ktw = _bf(kwin_ref[pl.ds(w0, wk), ksl])
        vtw = _bf(vwin_ref[0, vsl, pl.ds(w0, wk)])
        for r in range(B_GROUP):
            h = g * B_GROUP + r
            s = _dot(ktw, qg[r]) + bias_tiles(h, wb, wtiles + 1) + madd_w
            p = jnp.exp(s - jnp.max(s, axis=0, keepdims=True))
            o_w = _dot(vtw, _bf(p)) / jnp.sum(p, axis=0, keepdims=True)
            o_s = accs[r] / ls[r]
            o_ref[0, h * HEAD_DIM:(h + 1) * HEAD_DIM, :] = (
                gates[3 * h:3 * h + 1] * o_c[r] + gates[3 * h + 1:3 * h + 2] * o_s + gates[3 * h + 2:3 * h + 3] * o_w)


def _nsa_prompt(q_t, gate_t, ckv, kv_sel, kvt_sel, kv_win, kvt_win, rel_bias):
    b, _, t = q_t.shape
    nq = t // Q_BLOCK
    nc = t // CMP_BLOCK
    ns = -(-t // SEL_BLOCK)
    wtiles = WINDOW // Q_BLOCK
    assert t % KEY_TILE == 0 and nc == SEL_RATIO * ns and WINDOW % Q_BLOCK == 0 and (wtiles + 1) * Q_BLOCK <= t
    cmp_end = jnp.arange(nc) * CMP_BLOCK + (CMP_BLOCK - 1)
    bct = _bias_of(rel_bias, jnp.arange(t)[None, :] - cmp_end[:, None])
    dz = (jnp.arange(nq)[:, None, None] * Q_BLOCK + jnp.arange(Q_BLOCK)[None, None, :]
          - jnp.arange(Q_BLOCK)[None, :, None])
    tz = _bias_of(rel_bias, dz)
    ckt = jnp.swapaxes(ckv.reshape(b, nc, KV_ROW), 1, 2)
    chan = lambda w: pl.BlockSpec((1, w, Q_BLOCK), lambda bi, i: (bi, 0, i))
    seq_rows = pl.BlockSpec((t, KV_ROW), lambda bi, i: (bi, 0))
    seq_chan = pl.BlockSpec((1, KV_ROW, t), lambda bi, i: (bi, 0, 0))
    return pl.pallas_call(
        functools.partial(_nsa_prompt_kernel, nc=nc, ns=ns, nq=nq, top_n=min(TOP_N, ns), wtiles=wtiles),
        grid=(b, nq),
        in_specs=[chan(B_WIDTH), chan(GATE_PAD),
                  pl.BlockSpec((nc, KV_ROW), lambda bi, i: (bi, 0)),
                  pl.BlockSpec((1, KV_ROW, nc), lambda bi, i: (bi, 0, 0)),
                  seq_rows, seq_chan, seq_rows, seq_chan,
                  pl.BlockSpec((B_HEADS, nc, Q_BLOCK), lambda bi, i: (0, 0, i)),
                  _full((B_HEADS, nq, Q_BLOCK, Q_BLOCK))],
        out_specs=chan(B_WIDTH),
        out_shape=jax.ShapeDtypeStruct((b, B_WIDTH, t), F32),
        scratch_shapes=[pltpu.VMEM((nc, Q_BLOCK), F32), pltpu.VMEM((t, Q_BLOCK), F32)],
        compiler_params=_cparams(("parallel", "arbitrary")),
        name="nsa_prompt",
    )(q_t, gate_t, ckv, ckt, kv_sel, kvt_sel, kv_win, kvt_win, bct, tz)


def _group_q(q_ref, g):
    rows = [q_ref[0, :, (g * B_GROUP + r) * HEAD_DIM:(g * B_GROUP + r + 1) * HEAD_DIM] for r in range(B_GROUP)]
    return _bf(jnp.concatenate(rows, axis=0) * SCALE)


def _nsa_sample_select_kernel(q_ref, gate_ref, ck_ref, bc_ref, pair_ref, tri_ref, oc_ref, idx_ref,
                              *, nsp, cur, top_n):
    gates = jax.nn.sigmoid(gate_ref[0])
    blk = lax.broadcasted_iota(I32, (1, nsp), 1)
    forced = (blk == 0) | (blk == cur) | (blk == cur - 1)
    mi = lax.broadcasted_iota(I32, (nsp, nsp), 0)
    ni = lax.broadcasted_iota(I32, (nsp, nsp), 1)
    kk = lax.broadcasted_iota(I32, (TOP_N, nsp), 0).astype(F32)
    nf = lax.broadcasted_iota(I32, (TOP_N, nsp), 1).astype(F32)
    outs = []
    for g in range(B_KV_HEADS):
        q4 = _group_q(q_ref, g)
        kc = _bf(ck_ref[:, g * HEAD_DIM:(g + 1) * HEAD_DIM])
        vc = _bf(ck_ref[:, KV_WIDTH + g * HEAD_DIM:KV_WIDTH + (g + 1) * HEAD_DIM])
        s = _dot_nt(q4, kc) + bc_ref[g * B_GROUP:(g + 1) * B_GROUP, :]
        e = jnp.exp(s - jnp.max(s, axis=-1, keepdims=True))
        p = e / jnp.sum(e, axis=-1, keepdims=True)
        o_c = _dot(_bf(p), vc)
        imp = ((p[0:1] + p[1:2]) + p[2:3]) + p[3:4]
        imp2 = jnp.dot(imp, pair_ref[...], precision=HIGHEST, preferred_element_type=F32)
        score = jnp.where(blk <= cur, imp2 + FORCE_SCORE * forced.astype(F32), NEG_INF)
        m1 = jnp.broadcast_to(score, (nsp, nsp))
        m2 = m1.T
        gt = (m2 > m1) | ((m2 == m1) & (mi < ni))
        rank = jnp.sum(gt.astype(F32), axis=0, keepdims=True)
        sel = (rank < top_n) & (score > NEG_INF / 2)
        before = _dot(_bf(sel), tri_ref[...])
        hit = jnp.broadcast_to(sel, (TOP_N, nsp)) & (jnp.broadcast_to(before, (TOP_N, nsp)) == kk)
        idx = jnp.sum(jnp.where(hit, nf, 0.0), axis=1, keepdims=True)
        cnt = jnp.sum(hit.astype(F32), axis=1, keepdims=True)
        idx_ref[0, g] = jnp.where(cnt > 0.5, idx, -1.0).astype(I32)
        for r in range(B_GROUP):
            h = g * B_GROUP + r
            outs.append(gates[:, 3 * h:3 * h + 1] * o_c[r:r + 1])
    oc_ref[0] = jnp.concatenate(outs, axis=1)


def _nsa_sample_select(q3, gate3, ckv, rel_bias, past_len):
    s = q3.shape[0]
    nc = past_len // CMP_BLOCK
    ns = -(-(past_len + 1) // SEL_BLOCK)
    nsp = -(-ns // LANES) * LANES
    cur = past_len // SEL_BLOCK
    bc = _bias_of(rel_bias, past_len - (jnp.arange(nc) * CMP_BLOCK + (CMP_BLOCK - 1)))
    pair = jnp.asarray(np.arange(nc)[:, None] // SEL_RATIO == np.arange(nsp)[None, :], F32)
    tri = jnp.asarray(np.arange(nsp)[:, None] < np.arange(nsp)[None, :], BF16)
    return pl.pallas_call(
        functools.partial(_nsa_sample_select_kernel, nsp=nsp, cur=cur, top_n=min(TOP_N, ns)),
        grid=(s,),
        in_specs=[pl.BlockSpec((1, 1, B_WIDTH), lambda i: (i, 0, 0)), pl.BlockSpec((1, 1, GATE_PAD), lambda i: (i, 0, 0)),
                  pl.BlockSpec((nc, KV_ROW), lambda i: (i, 0)), _full((B_HEADS, nc)), _full((nc, nsp)), _full((nsp, nsp))],
        out_specs=[pl.BlockSpec((1, 1, B_WIDTH), lambda i: (i, 0, 0)),
                   pl.BlockSpec((1, B_KV_HEADS, TOP_N, 1), lambda i: (i, 0, 0, 0))],
        out_shape=[jax.ShapeDtypeStruct((s, 1, B_WIDTH), F32), jax.ShapeDtypeStruct((s, B_KV_HEADS, TOP_N, 1), I32)],
        compiler_params=_cparams(("parallel",)),
        name="nsa_sample_select",
    )(q3, gate3, ckv, bc, pair, tri)


def _nsa_sample_attend_kernel(idx_ref, pt_ref, q_ref, gate_ref, oc_ref, ksel_ref, kwin_ref, win_ref, fpg_ref, bw_ref,
                              cache_ref, o_ref, buf, sem, *, cur, n_pages, past_len, page_len):
    i = pl.program_id(0)
    n_s = pl.num_programs(0)
    slot = i % 2
    halves = PAGE_BLOCKS
    n_slot = B_KV_HEADS * TOP_N

    def block_of(s_idx, j):
        return idx_ref[s_idx * n_slot + j]

    def copy(s_idx, sl, j):
        n = jnp.clip(block_of(s_idx, j), 0, cur - 1)
        page = pt_ref[s_idx * n_pages + n // halves]
        return pltpu.make_async_copy(cache_ref.at[page], buf.at[sl, j], sem.at[sl])

    def cached(s_idx, j):
        n = block_of(s_idx, j)
        return (n >= 0) & (n < cur)

    def fetch(s_idx, sl):
        for j in range(n_slot):
            @pl.when(cached(s_idx, j))
            def _():
                copy(s_idx, sl, j).start()

    @pl.when(i == 0)
    def _():
        fetch(0, 0)

    @pl.when(i + 1 < n_s)
    def _():
        fetch(i + 1, 1 - slot)

    for j in range(n_slot):
        @pl.when(cached(i, j))
        def _():
            copy(i, slot, j).wait()

        @pl.when(jnp.logical_not(cached(i, j)))
        def _():
            buf[slot, j] = jnp.zeros((KV_ROW, page_len), F32)
            buf[slot, j, :, 0:1] = ksel_ref[0]

    gates = jax.nn.sigmoid(gate_ref[0])
    t_l = lax.broadcasted_iota(I32, (1, page_len), 1)
    n_buf = win_ref.shape[2]
    j_w = lax.broadcasted_iota(I32, (1, n_buf), 1)
    d_w = n_buf - j_w
    ok_w = (d_w >= 0) & (d_w < WINDOW) & (past_len - d_w >= 0)
    f0 = fpg_ref[n_pages]
    outs = []
    for g in range(B_KV_HEADS):
        q4 = _group_q(q_ref, g)
        ksl = slice(g * HEAD_DIM, (g + 1) * HEAD_DIM)
        vsl = slice(KV_WIDTH + g * HEAD_DIM, KV_WIDTH + (g + 1) * HEAD_DIM)
        pieces = []
        for k in range(TOP_N):
            j = g * TOP_N + k
            n = block_of(i, j)
            pg = jnp.clip(n, 0, cur) // halves
            fb = fpg_ref[pg][g * B_GROUP:(g + 1) * B_GROUP, :]
            s_k = _dot(q4, _bf(buf[slot, j, ksl, :])) + fb
            ok = (n >= 0) & (t_l // SEL_BLOCK == n % halves) & (pg * page_len + t_l <= past_len)
            pieces.append(jnp.where(ok, s_k, NEG_INF))
        s = jnp.concatenate(pieces, axis=1)
        e = jnp.exp(s - jnp.max(s, axis=-1, keepdims=True))
        p = _bf(e / jnp.sum(e, axis=-1, keepdims=True))
        o_s = jnp.zeros((B_GROUP, HEAD_DIM), F32)
        for k in range(TOP_N):
            o_s = o_s + _dot_nt(p[:, k * page_len:(k + 1) * page_len], _bf(buf[slot, g * TOP_N + k, vsl, :]))
        kw = _bf(win_ref[0, ksl, :])
        vw = _bf(win_ref[0, vsl, :])
        s_w = jnp.where(ok_w, _dot(q4, kw) + bw_ref[g * B_GROUP:(g + 1) * B_GROUP, :], NEG_INF)
        k_new = _bf(kwin_ref[0, :, ksl])
        v_new = _bf(kwin_ref[0, :, vsl])
        s_n = (jnp.sum(q4.astype(F32) * k_new.astype(F32), axis=-1, keepdims=True)
               + f0[g * B_GROUP:(g + 1) * B_GROUP, 0:1])
        m = jnp.maximum(jnp.max(s_w, axis=-1, keepdims=True), s_n)
        e_w = jnp.exp(s_w - m)
        e_n = jnp.exp(s_n - m)
        den = jnp.sum(e_w, axis=-1, keepdims=True) + e_n
        o_w = _dot_nt(_bf(e_w / den), vw) + _bf(e_n / den).astype(F32) * v_new.astype(F32)
        for r in range(B_GROUP):
            h = g * B_GROUP + r
            outs.append(gates[:, 3 * h + 1:3 * h + 2] * o_s[r:r + 1] + gates[:, 3 * h + 2:3 * h + 3] * o_w[r:r + 1])
    o_ref[0] = oc_ref[0] + jnp.concatenate(outs, axis=1)


PAGE_BLOCKS = 2


def _nsa_sample_attend(idx, page_table, q3, gate3, oc3, ksel_t, kwin3, win_t, cache_t, rel_bias, past_len):
    s = q3.shape[0]
    n_pages = page_table.shape[1]
    cur = past_len // SEL_BLOCK
    n_buf = win_t.shape[2]
    page = cache_t.shape[2]
    assert page == PAGE_BLOCKS * SEL_BLOCK and past_len == n_pages * page
    keypos = jnp.arange(n_pages + 1)[:, None] * page + jnp.arange(page)[None, :]
    fpg = jnp.moveaxis(_bias_of(rel_bias, past_len - keypos), 0, 1)
    bw = _bias_of(rel_bias, n_buf - jnp.arange(n_buf))
    row3 = lambda w: pl.BlockSpec((1, 1, w), lambda i, *_: (i, 0, 0))
    grid_spec = pltpu.PrefetchScalarGridSpec(
        num_scalar_prefetch=2,
        grid=(s,),
        in_specs=[row3(B_WIDTH), row3(GATE_PAD), row3(B_WIDTH),
                  pl.BlockSpec((1, KV_ROW, 1), lambda i, *_: (i, 0, 0)), row3(KV_ROW),
                  pl.BlockSpec((1, KV_ROW, n_buf), lambda i, *_: (i, 0, 0)),
                  pl.BlockSpec((n_pages + 1, B_HEADS, page), lambda i, *_: (0, 0, 0)),
                  pl.BlockSpec((B_HEADS, n_buf), lambda i, *_: (0, 0)),
                  pl.BlockSpec(memory_space=pl.ANY)],
        out_specs=row3(B_WIDTH),
        scratch_shapes=[pltpu.VMEM((2, B_KV_HEADS * TOP_N, KV_ROW, page), F32), pltpu.SemaphoreType.DMA((2,))],
    )
    return pl.pallas_call(
        functools.partial(_nsa_sample_attend_kernel, cur=cur, n_pages=n_pages, past_len=past_len, page_len=page),
        grid_spec=grid_spec,
        out_shape=jax.ShapeDtypeStruct((s, 1, B_WIDTH), F32),
        compiler_params=_cparams(("arbitrary",)),
        name="nsa_sample_attend",
    )(idx.reshape(-1), page_table.reshape(-1), q3, gate3, oc3, ksel_t, kwin3, win_t, fpg, bw, cache_t)


def _merge_kernel(x_ref, yn_ref, bo_ref, ga_ref, yb_ref, pg_ref, lnw_ref, lnb_ref, wa_ref, wb_ref, wo_ref, nf_ref,
                  wr_ref, br_ref, x1_ref, h_ref, te_ref, tw_ref, *, a_transposed):
    if a_transposed:
        tm = x_ref.shape[0]
        yn = yn_ref[...].reshape(A_WIDTH, tm)
        y_a = ((yn * lnw_ref[...] + lnb_ref[...] + bo_ref[0]) * ga_ref[0]).T
        y_b = yb_ref[0].T
    else:
        y_a = (yn_ref[...] * lnw_ref[...] + lnb_ref[...] + bo_ref[...]) * ga_ref[...]
        y_b = yb_ref[...]
    g_a = jax.nn.sigmoid(pg_ref[:, :D_MODEL])
    g_b = jax.nn.sigmoid(pg_ref[:, D_MODEL:])
    m = g_a * _dot(_bf(y_a), wa_ref[...]) + g_b * _dot(_bf(y_b), wb_ref[...])
    x1 = x_ref[...] + _dot(_bf(m), wo_ref[...])
    x1_ref[...] = x1
    h = x1 * lax.rsqrt(jnp.mean(x1 * x1, axis=-1, keepdims=True) + NORM_EPS) * nf_ref[...]
    _store_rows(h_ref, h)
    logits = jnp.dot(h, wr_ref[...], precision=HIGHEST, preferred_element_type=F32) + br_ref[...]
    lane = lax.broadcasted_iota(I32, logits.shape, 1).astype(F32)
    col = lax.broadcasted_iota(I32, (logits.shape[0], TOP_K), 1)
    vals, idxs = [], []
    for _ in range(TOP_K):
        m_k = jnp.max(logits, axis=-1, keepdims=True)
        i_k = jnp.min(jnp.where(logits == m_k, lane, float(N_EXPERTS)), axis=-1, keepdims=True)
        vals.append(m_k)
        idxs.append(i_k)
        logits = jnp.where(lane == i_k, -jnp.inf, logits)
    e = [jnp.exp(v - vals[0]) for v in vals]
    den = ((e[0] + e[1]) + e[2]) + e[3]
    te = jnp.zeros(col.shape, F32)
    tw = jnp.zeros(col.shape, F32)
    for k in range(TOP_K):
        te = jnp.where(col == k, idxs[k], te)
        tw = jnp.where(col == k, e[k] / den, tw)
    te_ref[...] = te.astype(I32)
    tw_ref[...] = tw


def _merge(x, yn, bonus, gate, yb, pg, ln_w, ln_b, w_a, w_b, w_o, norm_ffn, w_router, b_router, tm, seq=None):
    rows = x.shape[0]
    row = lambda w: pl.BlockSpec((tm, w), lambda i: (i, 0))
    vec = lambda z: z.reshape(1, -1).astype(F32)
    if seq is None:
        a_specs = [row(A_WIDTH)] * 3 + [row(B_WIDTH)] + [_full((1, A_WIDTH))] * 2
        lnw, lnb = vec(ln_w), vec(ln_b)
    else:
        per = seq // tm
        nspec = pl.BlockSpec((1, A_WIDTH, tm), lambda i: (i // per, 0, i % per))
        a_specs = [pl.BlockSpec((HEAD_DIM, 1, A_HEADS, tm), lambda i: (0, i // per, 0, i % per)), nspec, nspec, nspec,
                   _full((A_WIDTH, 1)), _full((A_WIDTH, 1))]
        lnw, lnb = (z[CH_MAJOR].reshape(-1, 1).astype(F32) for z in (ln_w, ln_b))
        w_a = w_a[CH_MAJOR]
    return pl.pallas_call(
        functools.partial(_merge_kernel, a_transposed=seq is not None),
        grid=(rows // tm,),
        in_specs=[row(D_MODEL)] + a_specs[:4] + [row(GATE_PROJ)] + a_specs[4:]
                 + [_full((A_WIDTH, D_MODEL)), _full((B_WIDTH, D_MODEL)),
                  _full((D_MODEL, D_MODEL)), _full((1, D_MODEL)), _full((D_MODEL, N_EXPERTS)), _full((1, N_EXPERTS))],
        out_specs=[row(D_MODEL), pl.BlockSpec((tm * ROW_TILE, LANES), lambda i: (i, 0)), row(TOP_K), row(TOP_K)],
        out_shape=[jax.ShapeDtypeStruct((rows, D_MODEL), F32), jax.ShapeDtypeStruct((rows * ROW_TILE, LANES), F32),
                   jax.ShapeDtypeStruct((rows, TOP_K), I32), jax.ShapeDtypeStruct((rows, TOP_K), F32)],
        compiler_params=_cparams(("parallel",)),
        name="merge_router",
    )(x, yn, bonus, gate, yb, pg, lnw, lnb, _bf(w_a), _bf(w_b), _bf(w_o), vec(norm_ffn),
      w_router.astype(F32), vec(b_router))


ROW_TILE = D_MODEL // LANES


def _load_rows(ref, n, first=0, every=1):
    return jnp.concatenate([ref[pl.ds(first * ROW_TILE + s, n, stride=every * ROW_TILE), :] for s in range(ROW_TILE)],
                           axis=1)


def _store_rows(ref, x):
    for s in range(ROW_TILE):
        ref[pl.ds(s, x.shape[0], stride=ROW_TILE), :] = x[:, s * LANES:(s + 1) * LANES]


def _dispatch_kernel(dest_ref, h_ref, xs_in_ref, xs_ref, sem, *, tm):
    del xs_in_ref
    i = pl.program_id(0)
    for t in range(tm):
        for k in range(TOP_K):
            pltpu.make_async_copy(h_ref.at[i * tm + t], xs_ref.at[dest_ref[0, 0, t * TOP_K + k]], sem).start()
    for t in range(tm * TOP_K):
        pltpu.make_async_copy(h_ref.at[0], xs_ref.at[0], sem).wait()


def _dispatch(dest, h_tiles, slots, tm):
    n = h_tiles.shape[0]
    return pl.pallas_call(
        functools.partial(_dispatch_kernel, tm=tm),
        grid=(n // tm,),
        in_specs=[pl.BlockSpec((1, 1, tm * TOP_K), lambda i: (i, 0, 0), memory_space=pltpu.SMEM),
                  pl.BlockSpec(memory_space=pl.ANY), pl.BlockSpec(memory_space=pl.ANY)],
        out_specs=pl.BlockSpec(memory_space=pl.ANY),
        out_shape=jax.ShapeDtypeStruct((slots, ROW_TILE, LANES), F32),
        scratch_shapes=[pltpu.SemaphoreType.DMA(())],
        input_output_aliases={2: 0},
        compiler_params=_cparams(("arbitrary",)),
        name="moe_dispatch",
    )(dest.reshape(n // tm, 1, tm * TOP_K), h_tiles, jnp.zeros((slots, ROW_TILE, LANES), F32))


def _moe_kernel(ce_ref, nu_ref, x_ref, wu_ref, bu_ref, wd_ref, bd_ref, o_ref, wu_bf, wd_bf):
    c = pl.program_id(0)
    e = ce_ref[c]
    prev = ce_ref[jnp.maximum(c - 1, 0)]

    @pl.when((c == 0) | (e != prev))
    def _():
        wu_bf[...] = _bf(wu_ref[0])
        wd_bf[...] = _bf(wd_ref[0])

    @pl.when(c < nu_ref[0])
    def _():
        u = _dot(_bf(_load_rows(x_ref, MOE_ROWS)), wu_bf[...]) + bu_ref[0]
        glu = jnp.minimum(u[:, :D_FF], SWIGLU_LIMIT)
        lin = jnp.clip(u[:, D_FF:], -SWIGLU_LIMIT, SWIGLU_LIMIT)
        act = glu * jax.nn.sigmoid(SWIGLU_ALPHA * glu) * (lin + 1.0)
        _store_rows(o_ref, _dot(_bf(act), wd_bf[...]) + bd_ref[0])

    @pl.when(c >= nu_ref[0])
    def _():
        o_ref[...] = jnp.zeros_like(o_ref)


def _moe_experts(chunk_e, n_used, xs, w_up, b_up, w_down, b_down):
    n_chunks = xs.shape[0] // (MOE_ROWS * ROW_TILE)
    rows = pl.BlockSpec((MOE_ROWS * ROW_TILE, LANES), lambda c, ce, nu: (c, 0))
    grid_spec = pltpu.PrefetchScalarGridSpec(
        num_scalar_prefetch=2,
        grid=(n_chunks,),
        in_specs=[rows,
                  pl.BlockSpec((1, D_MODEL, 2 * D_FF), lambda c, ce, nu: (ce[c], 0, 0)),
                  pl.BlockSpec((1, 1, 2 * D_FF), lambda c, ce, nu: (ce[c], 0, 0)),
                  pl.BlockSpec((1, D_FF, D_MODEL), lambda c, ce, nu: (ce[c], 0, 0)),
                  pl.BlockSpec((1, 1, D_MODEL), lambda c, ce, nu: (ce[c], 0, 0))],
        out_specs=rows,
        scratch_shapes=[pltpu.VMEM((D_MODEL, 2 * D_FF), BF16), pltpu.VMEM((D_FF, D_MODEL), BF16)],
    )
    return pl.pallas_call(
        _moe_kernel,
        grid_spec=grid_spec,
        out_shape=jax.ShapeDtypeStruct(xs.shape, F32),
        compiler_params=_cparams(("arbitrary",)),
        name="moe_experts",
    )(chunk_e, n_used, xs, w_up, b_up.reshape(N_EXPERTS, 1, -1), w_down, b_down.reshape(N_EXPERTS, 1, -1))


def _combine_kernel(dcur_ref, dnext_ref, x_ref, w_ref, g_ref, ys_ref, o_ref, buf, sem, *, tm, normalize):
    i = pl.program_id(0)
    slot = i % 2
    n_a = tm * TOP_K

    def copy(d_ref, sl, a):
        return pltpu.make_async_copy(ys_ref.at[d_ref[0, 0, a]], buf.at[sl, pl.ds(a * ROW_TILE, ROW_TILE)], sem.at[sl])

    def fetch(d_ref, sl):
        for a in range(n_a):
            copy(d_ref, sl, a).start()

    @pl.when(i == 0)
    def _():
        fetch(dcur_ref, 0)

    @pl.when(i + 1 < pl.num_programs(0))
    def _():
        fetch(dnext_ref, 1 - slot)

    for a in range(n_a):
        copy(dcur_ref, slot, a).wait()
    x = x_ref[...]
    for k in range(TOP_K):
        x = x + w_ref[:, k:k + 1] * _load_rows(buf.at[slot], tm, first=k, every=TOP_K)
    if normalize:
        x = x * lax.rsqrt(jnp.mean(x * x, axis=-1, keepdims=True) + NORM_EPS) * g_ref[...]
    o_ref[...] = x


def _combine(dest, x1, top_w, ys_tiles, g, tm, normalize):
    n = x1.shape[0]
    steps = n // tm
    d2 = dest.reshape(steps, 1, tm * TOP_K)
    row = lambda w: pl.BlockSpec((tm, w), lambda i: (i, 0))
    return pl.pallas_call(
        functools.partial(_combine_kernel, tm=tm, normalize=normalize),
        grid=(steps,),
        in_specs=[pl.BlockSpec((1, 1, tm * TOP_K), lambda i: (i, 0, 0), memory_space=pltpu.SMEM),
                  pl.BlockSpec((1, 1, tm * TOP_K), lambda i: (jnp.minimum(i + 1, steps - 1), 0, 0),
                               memory_space=pltpu.SMEM),
                  row(D_MODEL), row(TOP_K), _full((1, D_MODEL)), pl.BlockSpec(memory_space=pl.ANY)],
        out_specs=row(D_MODEL),
        out_shape=jax.ShapeDtypeStruct((n, D_MODEL), F32),
        scratch_shapes=[pltpu.VMEM((2, tm * TOP_K * ROW_TILE, LANES), F32), pltpu.SemaphoreType.DMA((2,))],
        compiler_params=_cparams(("arbitrary",)),
        name="moe_combine",
    )(d2, d2, x1, top_w, g.reshape(1, -1).astype(F32), ys_tiles)


def _moe_residual(x1, h_tiles, top_e, top_w, w_up, b_up, w_down, b_down, g, normalize):
    n = x1.shape[0]
    n_assign = n * TOP_K
    onehot = (top_e.reshape(-1, 1) == jnp.arange(N_EXPERTS, dtype=I32)[None, :]).astype(I32)
    csum = jnp.cumsum(onehot, axis=0)
    counts = csum[-1]
    padded = (counts + MOE_ROWS - 1) // MOE_ROWS * MOE_ROWS
    pad_end = jnp.cumsum(padded)
    dest = jnp.sum(onehot * (csum - 1 + (pad_end - padded)[None, :]), axis=1).astype(I32)
    n_chunks = -(-(n_assign + N_EXPERTS * (MOE_ROWS - 1)) // MOE_ROWS)
    slots = n_chunks * MOE_ROWS
    chunk_e = jnp.minimum(jnp.searchsorted(pad_end, jnp.arange(n_chunks) * MOE_ROWS, side='right'),
                          N_EXPERTS - 1).astype(I32)
    n_used = (pad_end[-1] // MOE_ROWS).astype(I32).reshape(1)
    tm = _row_tile(n, 128)
    xs = _dispatch(dest, h_tiles.reshape(n, ROW_TILE, LANES), slots, tm)
    ys = _moe_experts(chunk_e, n_used, xs.reshape(slots * ROW_TILE, LANES), w_up, b_up, w_down, b_down)
    return _combine(dest, x1, top_w, ys.reshape(slots, ROW_TILE, LANES), g, tm, normalize)


def _row_tile(rows, cap):
    tm = cap
    while rows % tm:
        tm //= 2
    return tm


def _rwkv_group(p_a, prev, s0_tiles, b, t, rw, tc):
    mu, w0, wdu, a0, wiu, wgu, k_k, k_a, r_k = rw
    tm = _row_tile(b * t, 256)
    r, w, k2, v, kk, kb, gate, bonus = _rwkv_prep(p_a, prev, mu, w0, wdu, a0, wiu, wgu, k_k, k_a, r_k, tm)
    kt = lambda z: _to_key_tiles(z, b, t)
    yn, s_fin = _wkv_scan(kt(r), kt(w), kt(k2), _to_val_tiles(v, b, t), kt(kk), kt(kb), s0_tiles, tc)
    return _from_val_tiles(yn, b, t), gate, bonus, _state_from_tiles(s_fin, b)


def kernel(x_prompt, x_sample, cache_cmp_kv, cache_sel_kv, state_win_kv, state_rwkv, state_rwkv_shift, page_table,
           norm_attn, w_in, mu_shift, w0, w_decay_up, a0, w_iclr_up, w_gate_up, k_k, k_a, r_k, ln_x_w, ln_x_b,
           pe_cmp_k, pe_cmp_v, w_cmp_k1, w_cmp_k2, w_cmp_v1, w_cmp_v2, rel_bias, w_br_a, w_br_b, w_out,
           norm_ffn, w_router, b_router, w_up, b_up, w_down, b_down, norm_final):
    bp, tp, _ = x_prompt.shape
    bs, ts, _ = x_sample.shape
    depth = w_in.shape[0]
    past_len = page_table.shape[1] * cache_cmp_kv.shape[2]
    n_buf = state_win_kv.shape[2]
    assert ts == 1 and bp * A_HEADS == BH_LANES and bs % 8 == 0 and tp % LANES == 0
    xp = x_prompt.reshape(bp * tp, D_MODEL)
    xs = x_sample.reshape(bs, D_MODEL)
    new = {name: [] for name in ('cmp_p', 'sel_p', 'win_p', 'wkv_p', 'shift_p', 'cmp_s', 'sel_s', 'win_s', 'wkv_s', 'shift_s')}
    kv6 = lambda z, b, t: z.reshape(b, t, 2, B_KV_HEADS, HEAD_DIM)
    for l in range(depth):
        rw = (mu_shift[l], w0[l], w_decay_up[l], a0[l], w_iclr_up[l], w_gate_up[l], k_k[l], k_a[l], r_k[l])
        cw = (pe_cmp_k[l], pe_cmp_v[l], w_cmp_k1[l], w_cmp_k2[l], w_cmp_v1[l], w_cmp_v2[l])
        wa_nat, w_rest, w_t, w_n = _pack_w_in(w_in[l])
        g_attn = norm_attn[l].reshape(1, -1).astype(F32)
        last = l == depth - 1
        g_fin = norm_final if last else jnp.ones((D_MODEL,), F32)

        tq = _row_tile(tp, 256)
        pa_t, q_t, kvt_c, kvt_s, kvt_w, gt_t, kv_c, kv_s, kv_w, pg = _project_t(xp, g_attn, w_t, w_n, tq, tp)
        r, w, k2, v, kk, kb, gate, bonus = _rwkv_prep_t(pa_t, *rw, tq)
        r, w, k2, kk, kb = _key_tiles([z.reshape(N_JP, LANES, tp) for z in (r, w, k2, kk, kb)], LANES)
        v = _val_tiles(v.reshape(HEAD_DIM, BH_LANES, tp), LANES).reshape(1, tp, HEAD_DIM, LANES)
        s0 = jnp.zeros((1, N_IO, N_JP, 8, LANES), F32)
        yn, s_fin = _wkv_scan(r, w, k2, v, kk, kb, s0, _row_tile(tp, 32))
        yn = _val_untile(yn.reshape(tp * HEAD_DIM, LANES), LANES).reshape(HEAD_DIM, bp, A_HEADS, tp)
        ckv = _compress(kv_c, *cw, rows=_row_tile(tp, 2048))
        y_b = _nsa_prompt(q_t, gt_t, ckv, kv_s, kvt_s, kv_w, kvt_w, rel_bias)
        x1, h, top_e, top_w = _merge(xp, yn, bonus, gate, y_b, pg, ln_x_w[l], ln_x_b[l], w_br_a[l], w_br_b[l], w_out[l],
                                     norm_ffn[l], w_router[l], b_router[l], tq, seq=tp)
        xp_next = _moe_residual(x1, h, top_e, top_w, w_up[l], b_up[l], w_down[l], b_down[l], g_fin, last)
        n_win = min(WINDOW, tp)
        kv6t = lambda z: jnp.moveaxis(z.reshape(bp, 2, B_KV_HEADS, HEAD_DIM, tp), -1, 1)
        new['cmp_p'].append(kv6t(kvt_c))
        new['sel_p'].append(kv6t(kvt_s))
        new['win_p'].append(kv6t(kvt_w)[:, tp - n_win:])
        new['wkv_p'].append(_state_from_tiles(s_fin, bp))
        new['shift_p'].append(jnp.zeros((bp, A_PROJ), F32).at[:, A_PERM].set(pa_t[:, :, tp - 1]))

        s_a, q, kv_c, kv_s, kv_w, gt, pg = _project(xs, g_attn, wa_nat, w_rest, _row_tile(bs, 256))
        yn, gate, bonus, wkv_s = _rwkv_group(s_a, state_rwkv_shift[l], _state_to_tiles(state_rwkv[l].astype(F32), bs),
                                             bs, 1, rw, 1)
        tok_minor = lambda z: jnp.moveaxis(z, 1, -1).reshape(z.shape[0], KV_ROW, z.shape[1])
        ckv = _compress_paged(tok_minor(cache_cmp_kv[l]), page_table, cw)
        q3, gt3 = q.reshape(bs, 1, B_WIDTH), gt.reshape(bs, 1, GATE_PAD)
        oc3, idx = _nsa_sample_select(q3, gt3, ckv, rel_bias, past_len)
        y_b = _nsa_sample_attend(idx, page_table, q3, gt3, oc3, kv_s.reshape(bs, KV_ROW, 1), kv_w.reshape(bs, 1, KV_ROW),
                                 tok_minor(state_win_kv[l]), tok_minor(cache_sel_kv[l]),
                                 rel_bias, past_len).reshape(bs, B_WIDTH)
        x1, h, top_e, top_w = _merge(xs, yn, bonus, gate, y_b, pg, ln_x_w[l], ln_x_b[l], w_br_a[l], w_br_b[l], w_out[l],
                                     norm_ffn[l], w_router[l], b_router[l], _row_tile(bs, 256))
        xs_next = _moe_residual(x1, h, top_e, top_w, w_up[l], b_up[l], w_down[l], b_down[l], g_fin, last)
        new['cmp_s'].append(kv6(kv_c, bs, 1))
        new['sel_s'].append(kv6(kv_s, bs, 1))
        new['win_s'].append(jnp.concatenate([state_win_kv[l], kv6(kv_w, bs, 1)], axis=1)[:, 1:])
        new['wkv_s'].append(wkv_s.astype(state_rwkv.dtype))
        new['shift_s'].append(s_a)
        xp, xs = xp_next, xs_next
    return (xp.reshape(bp, tp, D_MODEL), xs.reshape(bs, ts, D_MODEL),
            jnp.stack(new['cmp_p']), jnp.stack(new['sel_p']), jnp.stack(new['win_p']),
            jnp.stack(new['wkv_p']), jnp.stack(new['shift_p']),
            jnp.stack(new['cmp_s']), jnp.stack(new['sel_s']), jnp.stack(new['win_s']),
            jnp.stack(new['wkv_s']), jnp.stack(new['shift_s']))
```

```python
import functools
import math

import jax
import jax.numpy as jnp
import numpy as np
from jax import lax
from jax.experimental import pallas as pl
from jax.experimental.pallas import tpu as pltpu

F32 = jnp.float32
BF16 = jnp.bfloat16
I32 = jnp.int32
HIGHEST = lax.Precision.HIGHEST

D_MODEL = 1024
HEAD_DIM = 64
A_HEADS = 8
A_WIDTH = A_HEADS * HEAD_DIM
A_DECAY_RANK = 64
A_ICLR_RANK = 64
A_GATE_RANK = 128
A_GN_EPS = 64e-5
A_PROJ = 3 * A_WIDTH + A_DECAY_RANK + A_ICLR_RANK + A_GATE_RANK
B_HEADS = 8
B_KV_HEADS = 2
B_GROUP = B_HEADS // B_KV_HEADS
B_WIDTH = B_HEADS * HEAD_DIM
KV_WIDTH = B_KV_HEADS * HEAD_DIM
KV_ROW = 2 * KV_WIDTH
CMP_BLOCK = 32
CMP_HIDDEN = 128
SEL_BLOCK = 64
SEL_RATIO = SEL_BLOCK // CMP_BLOCK
TOP_N = 16
WINDOW = 512
Q_BLOCK = 128
FORCE_SCORE = 1e4
N_BUCKETS = 32
MAX_EXACT = N_BUCKETS // 2
MAX_DISTANCE = 1024
N_EXPERTS = 32
TOP_K = 4
D_FF = 1024
SWIGLU_ALPHA = 1.702
SWIGLU_LIMIT = 7.0
NORM_EPS = 1e-6
NEG_INF = -1e30
SCALE = HEAD_DIM ** -0.5
GATE_PAD = 128
GATE_PROJ = 2 * D_MODEL
MOE_ROWS = 256
LANES = 128
VMEM_LIMIT = 56 * 1024 * 1024


def _cparams(sem):
    return pltpu.CompilerParams(dimension_semantics=sem, vmem_limit_bytes=VMEM_LIMIT)


def _full(shape):
    n = len(shape)
    return pl.BlockSpec(shape, lambda *_: (0,) * n)


def _dot(a, b):
    return jnp.dot(a, b, preferred_element_type=F32)


def _dot_nt(a, b):
    return lax.dot_general(a, b, (((1,), (1,)), ((), ())), preferred_element_type=F32)


def _bf(x):
    return x.astype(BF16)


PROJ_SPLITS = (B_WIDTH, KV_ROW, KV_ROW, KV_ROW, GATE_PAD, GATE_PROJ)
CH_MAJOR = (np.arange(A_HEADS)[None, :] * HEAD_DIM + np.arange(HEAD_DIM)[:, None]).reshape(-1)
A_PERM = np.concatenate([CH_MAJOR, A_WIDTH + CH_MAJOR, 2 * A_WIDTH + CH_MAJOR, np.arange(3 * A_WIDTH, A_PROJ)])


def _rms_bf16(x_ref, g_ref):
    x = x_ref[...]
    return _bf(x * lax.rsqrt(jnp.mean(x * x, axis=-1, keepdims=True) + NORM_EPS) * g_ref[...])


def _proj_kernel(x_ref, g_ref, wa_ref, w_ref, oa_ref, *o_refs):
    h = _rms_bf16(x_ref, g_ref)
    oa_ref[...] = _dot(h, wa_ref[...])
    c = 0
    for o_ref, n in zip(o_refs, PROJ_SPLITS):
        o_ref[...] = _dot(h, w_ref[:, c:c + n])
        c += n


def _project(x, g, wa, w, tm):
    rows = x.shape[0]
    splits = (A_PROJ,) + PROJ_SPLITS
    return pl.pallas_call(
        _proj_kernel,
        grid=(rows // tm,),
        in_specs=[pl.BlockSpec((tm, D_MODEL), lambda i: (i, 0)), _full((1, D_MODEL)), _full(wa.shape), _full(w.shape)],
        out_specs=[pl.BlockSpec((tm, n), lambda i: (i, 0)) for n in splits],
        out_shape=[jax.ShapeDtypeStruct((rows, n), F32) for n in splits],
        compiler_params=_cparams(("parallel",)),
        name="norm_proj",
    )(x, g, wa, w)


PROJ_T_SPLITS = (A_PROJ, B_WIDTH, KV_ROW, KV_ROW, KV_ROW, GATE_PAD)
PROJ_N_SPLITS = (KV_ROW, KV_ROW, KV_ROW, GATE_PROJ)


def _proj_t_kernel(x_ref, g_ref, wt_ref, wn_ref, *o_refs):
    h = _rms_bf16(x_ref, g_ref)
    t = _dot_nt(wt_ref[...], h)
    c = 0
    for o_ref, n in zip(o_refs[:len(PROJ_T_SPLITS)], PROJ_T_SPLITS):
        o_ref[0] = t[c:c + n]
        c += n
    c = 0
    for o_ref, n in zip(o_refs[len(PROJ_T_SPLITS):], PROJ_N_SPLITS):
        o_ref[...] = _dot(h, wn_ref[:, c:c + n])
        c += n


def _project_t(x, g, wt, wn, tm, seq):
    rows = x.shape[0]
    per = seq // tm
    return pl.pallas_call(
        _proj_t_kernel,
        grid=(rows // tm,),
        in_specs=[pl.BlockSpec((tm, D_MODEL), lambda i: (i, 0)), _full((1, D_MODEL)), _full(wt.shape), _full(wn.shape)],
        out_specs=[pl.BlockSpec((1, n, tm), lambda i: (i // per, 0, i % per)) for n in PROJ_T_SPLITS]
                  + [pl.BlockSpec((tm, n), lambda i: (i, 0)) for n in PROJ_N_SPLITS],
        out_shape=[jax.ShapeDtypeStruct((rows // seq, n, seq), F32) for n in PROJ_T_SPLITS]
                  + [jax.ShapeDtypeStruct((rows, n), F32) for n in PROJ_N_SPLITS],
        compiler_params=_cparams(("parallel",)),
        name="norm_proj_t",
    )(x, g, wt, wn)


def _pack_w_in(w_in):
    a, rest = w_in[:, :A_PROJ], w_in[:, A_PROJ:]
    q, kv, gt, pg = (rest[:, :B_WIDTH], rest[:, B_WIDTH:B_WIDTH + 3 * KV_ROW],
                     rest[:, B_WIDTH + 3 * KV_ROW:B_WIDTH + 3 * KV_ROW + 3 * B_HEADS],
                     rest[:, B_WIDTH + 3 * KV_ROW + 3 * B_HEADS:])
    gt = jnp.pad(gt, ((0, 0), (0, GATE_PAD - 3 * B_HEADS)))
    w_t = jnp.concatenate([a[:, A_PERM], q, kv, gt], axis=1).T
    return _bf(a), _bf(jnp.concatenate([q, kv, gt, pg], axis=1)), _bf(w_t), _bf(jnp.concatenate([kv, pg], axis=1))


def _softplus(z):
    return jnp.maximum(z, 0.0) + jnp.log1p(jnp.exp(-jnp.abs(z)))


def _rwkv_prep_kernel(p_ref, prev_ref, mu_ref, w0_ref, wdu_ref, a0_ref, wiu_ref, wgu_ref, kk_ref, ka_ref, rk_ref,
                      ones_ref, r_o, w_o, k_o, v_o, kk_o, kb_o, g_o, bo_o):
    p = p_ref[...]
    ps = p + mu_ref[...] * (prev_ref[...] - p)
    r = ps[:, 0:A_WIDTH]
    k = ps[:, A_WIDTH:2 * A_WIDTH]
    v = ps[:, 2 * A_WIDTH:3 * A_WIDTH]
    c = 3 * A_WIDTH
    xw = ps[:, c:c + A_DECAY_RANK]
    xa = ps[:, c + A_DECAY_RANK:c + A_DECAY_RANK + A_ICLR_RANK]
    xg = ps[:, c + A_DECAY_RANK + A_ICLR_RANK:]
    w_log = -_softplus(-(w0_ref[...] + _dot(_bf(jnp.tanh(xw)), wdu_ref[...]))) - 0.5
    decay = jnp.exp(-jnp.exp(w_log))
    a = jax.nn.sigmoid(a0_ref[...] + _dot(_bf(xa), wiu_ref[...]))
    gate = _dot(_bf(jax.nn.sigmoid(xg)), wgu_ref[...])
    ones = ones_ref[...]
    kk = k * kk_ref[...]
    ss = jnp.dot(kk * kk, ones, precision=HIGHEST, preferred_element_type=F32)
    kk = kk / jnp.maximum(jnp.sqrt(ss), 1e-12)
    k2 = k * (1.0 + (a - 1.0) * ka_ref[...])
    rk = jnp.dot(r * k2 * rk_ref[...], ones, precision=HIGHEST, preferred_element_type=F32)
    r_o[...] = r
    w_o[...] = decay
    k_o[...] = k2
    v_o[...] = v
    kk_o[...] = kk
    kb_o[...] = kk * a
    g_o[...] = gate
    bo_o[...] = rk * v


def _rwkv_prep(p, prev, mu, w0, wdu, a0, wiu, wgu, k_k, k_a, r_k, tm):
    rows = p.shape[0]
    head = np.arange(A_WIDTH) // HEAD_DIM
    ones = jnp.asarray(head[:, None] == head[None, :], F32)
    row = lambda z: z.reshape(1, -1).astype(F32)
    spec_in = pl.BlockSpec((tm, A_PROJ), lambda i: (i, 0))
    spec_o = pl.BlockSpec((tm, A_WIDTH), lambda i: (i, 0))
    return pl.pallas_call(
        _rwkv_prep_kernel,
        grid=(rows // tm,),
        in_specs=[spec_in, spec_in, _full((1, A_PROJ)), _full((1, A_WIDTH)), _full((A_DECAY_RANK, A_WIDTH)),
                  _full((1, A_WIDTH)), _full((A_ICLR_RANK, A_WIDTH)), _full((A_GATE_RANK, A_WIDTH)),
                  _full((1, A_WIDTH)), _full((1, A_WIDTH)), _full((1, A_WIDTH)), _full((A_WIDTH, A_WIDTH))],
        out_specs=[spec_o] * 8,
        out_shape=[jax.ShapeDtypeStruct((rows, A_WIDTH), F32)] * 8,
        compiler_params=_cparams(("parallel",)),
        name="rwkv_prep",
    )(p, prev, row(mu), row(w0), _bf(wdu), row(a0), _bf(wiu), _bf(wgu), row(k_k), row(k_a), row(r_k), ones)


def _rwkv_prep_t_kernel(p_ref, mu_ref, w0_ref, wdu_ref, a0_ref, wiu_ref, wgu_ref, kk_ref, ka_ref, rk_ref,
                        r_o, w_o, k_o, v_o, kk_o, kb_o, g_o, bo_o, carry):
    tt = p_ref.shape[2]

    @pl.when(pl.program_id(1) == 0)
    def _():
        carry[...] = jnp.zeros_like(carry)

    p = p_ref[0]
    lane = lax.broadcasted_iota(I32, (1, tt), 1)
    prev = jnp.where(lane == 0, carry[:, 0:1], pltpu.roll(p, 1, 1))
    carry[:, 0:1] = p[:, tt - 1:tt]
    ps = p + mu_ref[...] * (prev - p)
    r = ps[0:A_WIDTH]
    k = ps[A_WIDTH:2 * A_WIDTH]
    v = ps[2 * A_WIDTH:3 * A_WIDTH]
    c = 3 * A_WIDTH
    xw = ps[c:c + A_DECAY_RANK]
    xa = ps[c + A_DECAY_RANK:c + A_DECAY_RANK + A_ICLR_RANK]
    xg = ps[c + A_DECAY_RANK + A_ICLR_RANK:]

    def head_sum(x):
        s = jnp.sum(x.reshape(HEAD_DIM, A_HEADS, tt), axis=0)
        return jnp.broadcast_to(s[None], (HEAD_DIM, A_HEADS, tt)).reshape(A_WIDTH, tt)

    w_log = -_softplus(-(w0_ref[...] + _dot(wdu_ref[...], _bf(jnp.tanh(xw))))) - 0.5
    decay = jnp.exp(-jnp.exp(w_log))
    a = jax.nn.sigmoid(a0_ref[...] + _dot(wiu_ref[...], _bf(xa)))
    gate = _dot(wgu_ref[...], _bf(jax.nn.sigmoid(xg)))
    kk = k * kk_ref[...]
    kk = kk / jnp.maximum(jnp.sqrt(head_sum(kk * kk)), 1e-12)
    k2 = k * (1.0 + (a - 1.0) * ka_ref[...])
    rk = head_sum(r * k2 * rk_ref[...])
    for o_ref, val in ((r_o, r), (w_o, decay), (k_o, k2), (kk_o, kk), (kb_o, kk * a)):
        o_ref[...] = val.reshape(N_JP, 2, 1, A_HEADS, tt)
    v_o[...] = v.reshape(HEAD_DIM, 1, A_HEADS, tt)
    g_o[0] = gate
    bo_o[0] = rk * v


def _rwkv_prep_t(p_t, mu, w0, wdu, a0, wiu, wgu, k_k, k_a, r_k, tt):
    b, _, t = p_t.shape
    col = lambda z, perm: z.reshape(-1)[perm].reshape(-1, 1).astype(F32)
    up = lambda w: _bf(w[:, CH_MAJOR].T)
    kshape = jax.ShapeDtypeStruct((N_JP, 2, b, A_HEADS, t), F32)
    kspec = pl.BlockSpec((N_JP, 2, 1, A_HEADS, tt), lambda bi, i: (0, 0, bi, 0, i))
    vspec = pl.BlockSpec((HEAD_DIM, 1, A_HEADS, tt), lambda bi, i: (0, bi, 0, i))
    nspec = pl.BlockSpec((1, A_WIDTH, tt), lambda bi, i: (bi, 0, i))
    nshape = jax.ShapeDtypeStruct((b, A_WIDTH, t), F32)
    return pl.pallas_call(
        _rwkv_prep_t_kernel,
        grid=(b, t // tt),
        in_specs=[pl.BlockSpec((1, A_PROJ, tt), lambda bi, i: (bi, 0, i)), _full((A_PROJ, 1)), _full((A_WIDTH, 1)),
                  _full((A_WIDTH, A_DECAY_RANK)), _full((A_WIDTH, 1)), _full((A_WIDTH, A_ICLR_RANK)),
                  _full((A_WIDTH, A_GATE_RANK)), _full((A_WIDTH, 1)), _full((A_WIDTH, 1)), _full((A_WIDTH, 1))],
        out_specs=[kspec, kspec, kspec, vspec, kspec, kspec, nspec, nspec],
        out_shape=[kshape, kshape, kshape, jax.ShapeDtypeStruct((HEAD_DIM, b, A_HEADS, t), F32), kshape, kshape,
                   nshape, nshape],
        scratch_shapes=[pltpu.VMEM((A_PROJ, LANES), F32)],
        compiler_params=_cparams(("parallel", "arbitrary")),
        name="rwkv_prep_t",
    )(p_t, col(mu, A_PERM), col(w0, CH_MAJOR), up(wdu), col(a0, CH_MAJOR), up(wiu), up(wgu),
      col(k_k, CH_MAJOR), col(k_a, CH_MAJOR), col(r_k, CH_MAJOR))


def _key_tiles_kernel(*refs):
    n = len(refs) // 2
    for z_ref, o_ref in zip(refs[:n], refs[n:]):
        for jp in range(N_JP):
            o_ref[0, jp] = z_ref[jp].T


def _key_tiles(zs, tt):
    t = zs[0].shape[2]
    return pl.pallas_call(
        _key_tiles_kernel,
        grid=(t // tt,),
        in_specs=[pl.BlockSpec((N_JP, LANES, tt), lambda i: (0, 0, i))] * len(zs),
        out_specs=[pl.BlockSpec((1, N_JP, tt, LANES), lambda i: (0, 0, i, 0))] * len(zs),
        out_shape=[jax.ShapeDtypeStruct((1, N_JP, t, LANES), F32)] * len(zs),
        compiler_params=_cparams(("parallel",)),
        name="wkv_key_tiles",
    )(*zs)


def _val_tiles_kernel(z_ref, o_ref, *, tt):
    for i in range(HEAD_DIM):
        x = z_ref[i]
        o_ref[pl.ds(i, tt, stride=HEAD_DIM), :] = jnp.concatenate([x, x], axis=0).T


def _val_tiles(z, tt):
    t = z.shape[2]
    return pl.pallas_call(
        functools.partial(_val_tiles_kernel, tt=tt),
        grid=(t // tt,),
        in_specs=[pl.BlockSpec((HEAD_DIM, BH_LANES, tt), lambda i: (0, 0, i))],
        out_specs=pl.BlockSpec((tt * HEAD_DIM, LANES), lambda i: (i, 0)),
        out_shape=jax.ShapeDtypeStruct((t * HEAD_DIM, LANES), F32),
        compiler_params=_cparams(("parallel",)),
        name="wkv_val_tiles",
    )(z)


def _val_untile_kernel(y_ref, o_ref, *, tt):
    for i in range(HEAD_DIM):
        o_ref[i] = y_ref[pl.ds(i, tt, stride=HEAD_DIM), :].T[0:BH_LANES]


def _val_untile(y, tt):
    t = y.shape[0] // HEAD_DIM
    return pl.pallas_call(
        functools.partial(_val_untile_kernel, tt=tt),
        grid=(t // tt,),
        in_specs=[pl.BlockSpec((tt * HEAD_DIM, LANES), lambda i: (i, 0))],
        out_specs=pl.BlockSpec((HEAD_DIM, BH_LANES, tt), lambda i: (0, 0, i)),
        out_shape=jax.ShapeDtypeStruct((HEAD_DIM, BH_LANES, t), F32),
        compiler_params=_cparams(("parallel",)),
        name="wkv_val_untile",
    )(y)


N_IO = HEAD_DIM // 8
N_JP = HEAD_DIM // 2
BH_LANES = 64


def _wkv_kernel(r_ref, w_ref, k_ref, v_ref, a_ref, b_ref, s0_ref, y_ref, sfin_ref, s_scr, *, tc):
    t_blk = pl.program_id(1)

    @pl.when(t_blk == 0)
    def _():
        s_scr[...] = s0_ref[0]

    def bc(ref, t, jp):
        return jnp.broadcast_to(ref[0, jp, pl.ds(t, 1), :], (8, LANES))

    def fold(x):
        return x + pltpu.roll(x, BH_LANES, 1)

    def step(t, carry):
        acc = [jnp.zeros((8, LANES), F32) for _ in range(N_IO)]
        for jp in range(N_JP):
            a_ = bc(a_ref, t, jp)
            for io in range(N_IO):
                acc[io] = acc[io] + s_scr[io, jp] * a_
        sa = [-fold(acc[io]) for io in range(N_IO)]
        vv = [v_ref[0, t, io * 8:(io + 1) * 8, :] for io in range(N_IO)]
        yacc = [jnp.zeros((8, LANES), F32) for _ in range(N_IO)]
        for jp in range(N_JP):
            w_ = bc(w_ref, t, jp)
            b_ = bc(b_ref, t, jp)
            k_ = bc(k_ref, t, jp)
            r_ = bc(r_ref, t, jp)
            for io in range(N_IO):
                s = s_scr[io, jp] * w_ + sa[io] * b_ + vv[io] * k_
                s_scr[io, jp] = s
                yacc[io] = yacc[io] + s * r_
        y = [fold(yacc[io]) for io in range(N_IO)]
        tot = y[0]
        for io in range(1, N_IO):
            tot = tot + y[io]
        mu = jnp.sum(tot, axis=0, keepdims=True) * (1.0 / HEAD_DIM)
        d = [y[io] - mu for io in range(N_IO)]
        sq = d[0] * d[0]
        for io in range(1, N_IO):
            sq = sq + d[io] * d[io]
        var = jnp.sum(sq, axis=0, keepdims=True) * (1.0 / HEAD_DIM)
        inv = lax.rsqrt(var + A_GN_EPS)
        for io in range(N_IO):
            y_ref[0, t, io * 8:(io + 1) * 8, :] = d[io] * inv
        return carry

    lax.fori_loop(0, tc, step, 0)

    @pl.when(t_blk == pl.num_programs(1) - 1)
    def _():
        sfin_ref[0] = s_scr[...]


def _wkv_scan(r, w, k, v, a, b, s0, tc):
    nb, t = v.shape[:2]
    kspec = pl.BlockSpec((1, N_JP, tc, LANES), lambda n, i: (n, 0, i, 0))
    vspec = pl.BlockSpec((1, tc, HEAD_DIM, LANES), lambda n, i: (n, i, 0, 0))
    sspec = pl.BlockSpec((1, N_IO, N_JP, 8, LANES), lambda n, i: (n, 0, 0, 0, 0))
    return pl.pallas_call(
        functools.partial(_wkv_kernel, tc=tc),
        grid=(nb, t // tc),
        in_specs=[kspec, kspec, kspec, vspec, kspec, kspec, sspec],
        out_specs=[vspec, sspec],
        out_shape=[jax.ShapeDtypeStruct((nb, t, HEAD_DIM, LANES), F32),
                   jax.ShapeDtypeStruct((nb, N_IO, N_JP, 8, LANES), F32)],
        scratch_shapes=[pltpu.VMEM((N_IO, N_JP, 8, LANES), F32)],
        compiler_params=_cparams(("parallel", "arbitrary")),
        name="wkv_scan",
    )(r, w, k, v, a, b, s0)


def _to_key_tiles(x, b, t):
    nb = b // 8
    x = x.reshape(nb, 8, t, A_HEADS, N_JP, 2).transpose(0, 4, 2, 5, 1, 3)
    return x.reshape(nb, N_JP, t, LANES)


def _to_val_tiles(x, b, t):
    nb = b // 8
    x = x.reshape(nb, 8, t, A_HEADS, HEAD_DIM).transpose(0, 2, 4, 1, 3).reshape(nb, t, HEAD_DIM, BH_LANES)
    return jnp.concatenate([x, x], axis=-1)


def _from_val_tiles(y, b, t):
    nb = b // 8
    y = y[..., :BH_LANES].reshape(nb, t, HEAD_DIM, 8, A_HEADS).transpose(0, 3, 1, 4, 2)
    return y.reshape(b * t, A_WIDTH)


def _state_to_tiles(s, b):
    nb = b // 8
    s = s.reshape(nb, 8, A_HEADS, N_IO, 8, N_JP, 2).transpose(0, 3, 5, 4, 6, 1, 2)
    return s.reshape(nb, N_IO, N_JP, 8, LANES)


def _state_from_tiles(s, b):
    nb = b // 8
    s = s.reshape(nb, N_IO, N_JP, 8, 2, 8, A_HEADS).transpose(0, 5, 6, 1, 3, 2, 4)
    return s.reshape(b, A_HEADS, HEAD_DIM, HEAD_DIM)


def _compress_paged_kernel(pt_ref, pe_ref, w1_ref, w2_ref, cache_ref, o_ref, buf, xk, xv, sem, *, n_pages, page_len):
    i = pl.program_id(0)
    slot = i % 2

    def copy(s_idx, sl, p):
        return pltpu.make_async_copy(cache_ref.at[pt_ref[s_idx * n_pages + p]], buf.at[sl, p], sem.at[sl, p])

    def fetch(s_idx, sl):
        for p in range(n_pages):
            copy(s_idx, sl, p).start()

    @pl.when(i == 0)
    def _():
        fetch(0, 0)

    @pl.when(i + 1 < pl.num_programs(0))
    def _():
        fetch(i + 1, 1 - slot)

    for p in range(n_pages):
        copy(i, slot, p).wait()
    for p in range(n_pages):
        t = buf[slot, p].T
        for m in range(page_len // CMP_BLOCK):
            row = (p * (page_len // CMP_BLOCK) + m) * CMP_PITCH
            xk[row:row + CMP_BLOCK, :] = t[m * CMP_BLOCK:(m + 1) * CMP_BLOCK, :KV_WIDTH]
            xv[row:row + CMP_BLOCK, :] = t[m * CMP_BLOCK:(m + 1) * CMP_BLOCK, KV_WIDTH:]
    _compress_kernel(xk, xv, pe_ref, w1_ref, w2_ref, o_ref, nblk=n_pages * page_len // CMP_BLOCK, pitch=CMP_PITCH)


CMP_PITCH = CMP_BLOCK + 1


def _compress_kernel(xk_ref, xv_ref, pe_ref, w1_ref, w2_ref, o_ref, *, nblk, pitch=CMP_BLOCK):
    for c, x_ref in enumerate((xk_ref, xv_ref)):
        acc = jnp.zeros((nblk, B_KV_HEADS * CMP_HIDDEN), F32)
        for tau in range(CMP_BLOCK):
            x = x_ref[pl.ds(tau, nblk, stride=pitch), :] + pe_ref[c, tau:tau + 1, :]
            acc = acc + _dot(_bf(x), w1_ref[c, tau])
        o_ref[:, c * KV_WIDTH:(c + 1) * KV_WIDTH] = _dot(_bf(jax.nn.gelu(acc)), w2_ref[c])


def _compress_weights(pe_k, pe_v, w_k1, w_k2, w_v1, w_v2):
    pe = jnp.stack([jnp.concatenate([pe_k, pe_k], axis=1), jnp.concatenate([pe_v, pe_v], axis=1)]).astype(F32)
    eye = jnp.eye(B_KV_HEADS, dtype=F32)
    w1 = jnp.stack([w_k1, w_v1]).reshape(2, CMP_BLOCK, HEAD_DIM, CMP_HIDDEN)
    w1 = jnp.einsum('ctdh,ge->ctgdeh', w1, eye).reshape(2, CMP_BLOCK, KV_WIDTH, B_KV_HEADS * CMP_HIDDEN)
    w2 = jnp.einsum('chd,ge->cghed', jnp.stack([w_k2, w_v2]), eye).reshape(2, B_KV_HEADS * CMP_HIDDEN, KV_WIDTH)
    return pe, _bf(w1), _bf(w2)


COMPRESS_W_SPECS = ((2, CMP_BLOCK, KV_WIDTH), (2, CMP_BLOCK, KV_WIDTH, B_KV_HEADS * CMP_HIDDEN),
                    (2, B_KV_HEADS * CMP_HIDDEN, KV_WIDTH))


def _compress_paged(cache_t, page_table, cw):
    s, n_pages = page_table.shape
    page = cache_t.shape[2]
    past = n_pages * page
    nblk = past // CMP_BLOCK
    grid_spec = pltpu.PrefetchScalarGridSpec(
        num_scalar_prefetch=1,
        grid=(s,),
        in_specs=[pl.BlockSpec(shp, lambda i, pt, n=len(shp): (0,) * n) for shp in COMPRESS_W_SPECS]
                 + [pl.BlockSpec(memory_space=pl.ANY)],
        out_specs=pl.BlockSpec((nblk, KV_ROW), lambda i, pt: (i, 0)),
        scratch_shapes=[pltpu.VMEM((2, n_pages, KV_ROW, page), F32), pltpu.VMEM((nblk * CMP_PITCH, KV_WIDTH), F32),
                        pltpu.VMEM((nblk * CMP_PITCH, KV_WIDTH), F32), pltpu.SemaphoreType.DMA((2, n_pages))],
    )
    return pl.pallas_call(
        functools.partial(_compress_paged_kernel, n_pages=n_pages, page_len=page),
        grid_spec=grid_spec,
        out_shape=jax.ShapeDtypeStruct((s * nblk, KV_ROW), F32),
        compiler_params=_cparams(("arbitrary",)),
        name="nsa_compress_paged",
    )(page_table.reshape(-1), *_compress_weights(*cw), cache_t)


def _compress(x, pe_k, pe_v, w_k1, w_k2, w_v1, w_v2, rows):
    n = x.shape[0]
    nblk = rows // CMP_BLOCK
    pe, w1, w2 = _compress_weights(pe_k, pe_v, w_k1, w_k2, w_v1, w_v2)
    return pl.pallas_call(
        functools.partial(_compress_kernel, nblk=nblk),
        grid=(n // rows,),
        in_specs=[pl.BlockSpec((rows, KV_WIDTH), lambda i: (i, 0)), pl.BlockSpec((rows, KV_WIDTH), lambda i: (i, 1)),
                  ] + [_full(shp) for shp in COMPRESS_W_SPECS],
        out_specs=pl.BlockSpec((nblk, KV_ROW), lambda i: (i, 0)),
        out_shape=jax.ShapeDtypeStruct((n // CMP_BLOCK, KV_ROW), F32),
        compiler_params=_cparams(("parallel",)),
        name="nsa_compress",
    )(x, x, pe, w1, w2)


def _rel_bucket(dist):
    n = jnp.maximum(dist, 0)
    log_ratio = jnp.log(jnp.maximum(n, 1).astype(F32) / MAX_EXACT) / math.log(MAX_DISTANCE / MAX_EXACT)
    large = jnp.minimum(MAX_EXACT + (log_ratio * (N_BUCKETS - MAX_EXACT)).astype(I32), N_BUCKETS - 1)
    return jnp.where(n < MAX_EXACT, n, large)


def _bias_of(rel_bias, dist):
    onehot = (_rel_bucket(dist)[..., None] == jnp.arange(N_BUCKETS)).astype(F32)
    return jnp.einsum('...b,bh->h...', onehot, rel_bias.astype(F32), precision=HIGHEST)


KEY_TILE = 512


def _nsa_prompt_kernel(q_ref, gate_ref, ck_ref, ckt_ref, ksel_ref, vsel_ref, kwin_ref, vwin_ref, bct_ref, tz_ref,
                       o_ref, imp_scr, mask_scr, *, nc, ns, nq, top_n, wtiles):
    i = pl.program_id(1)
    qb = Q_BLOCK
    gates = jax.nn.sigmoid(gate_ref[0])
    pos_l = i * qb + lax.broadcasted_iota(I32, (1, qb), 1)
    okT = pos_l >= lax.broadcasted_iota(I32, (nc, qb), 0) * CMP_BLOCK + (CMP_BLOCK - 1)
    blk = lax.broadcasted_iota(I32, (ns, qb), 0)
    cur = pos_l // SEL_BLOCK
    forced = (blk == 0) | (blk == cur) | (blk == cur - 1)
    k_s = lax.broadcasted_iota(I32, (qb, qb), 0)
    q_l = lax.broadcasted_iota(I32, (qb, qb), 1)
    n_e = lax.broadcasted_iota(I32, (qb, ns), 1)
    k_e = lax.broadcasted_iota(I32, (qb, ns), 0) // SEL_BLOCK
    wk = (wtiles + 1) * qb
    wb = jnp.maximum(i - wtiles, 0)
    d_w = pos_l - (wb * qb + lax.broadcasted_iota(I32, (wk, qb), 0))
    madd_w = jnp.where((d_w >= 0) & (d_w < WINDOW), 0.0, NEG_INF)
    n_tiles = (i * qb + qb + KEY_TILE - 1) // KEY_TILE
    per = KEY_TILE // qb

    def bias_tiles(h, first_blk, count):
        return jnp.concatenate([tz_ref[h, jnp.clip(i - (first_blk + c), 0, nq - 1)] for c in range(count)], axis=0)

    for g in range(B_KV_HEADS):
        ksl = slice(g * HEAD_DIM, (g + 1) * HEAD_DIM)
        vsl = slice(KV_WIDTH + g * HEAD_DIM, KV_WIDTH + (g + 1) * HEAD_DIM)
        qg = [_bf(q_ref[0, (g * B_GROUP + r) * HEAD_DIM:(g * B_GROUP + r + 1) * HEAD_DIM, :] * SCALE)
              for r in range(B_GROUP)]
        kc = _bf(ck_ref[:, ksl])
        vct = _bf(ckt_ref[0, vsl, :])
        o_c = []
        impT = jnp.zeros((nc, qb), F32)
        for r in range(B_GROUP):
            sT = jnp.where(okT, _dot(kc, qg[r]) + bct_ref[g * B_GROUP + r], NEG_INF)
            eT = jnp.exp(sT - jnp.max(sT, axis=0, keepdims=True))
            pT = eT / jnp.sum(eT, axis=0, keepdims=True) * okT.astype(F32)
            impT = impT + pT
            o_c.append(_dot(vct, _bf(pT)))
        imp_scr[...] = impT
        imp2 = imp_scr[pl.ds(0, ns, stride=SEL_RATIO), :] + imp_scr[pl.ds(1, ns, stride=SEL_RATIO), :]
        score = jnp.where(blk <= cur, imp2 + FORCE_SCORE * forced.astype(F32), NEG_INF)
        rank = jnp.zeros((ns, qb), I32)
        for m in range(ns):
            row = score[m:m + 1, :]
            rank = rank + ((row > score) | ((row == score) & (m < blk))).astype(I32)
        selT = _bf((rank < top_n) & (score > NEG_INF / 2))
        mask_scr[...] = jnp.full(mask_scr.shape, NEG_INF, F32)

        def fill(j, carry):
            m = _dot(_bf(n_e == 2 * j + k_e), selT) > 0.5
            m = m & ((j < i) | (q_l >= k_s))
            mask_scr[pl.ds(pl.multiple_of(j * qb, qb), qb), :] = jnp.where(m, 0.0, NEG_INF)
            return carry

        lax.fori_loop(0, i + 1, fill, 0)

        def body(jt, carry):
            ms, ls, accs = carry
            k0 = pl.multiple_of(jt * KEY_TILE, KEY_TILE)
            kt = _bf(ksel_ref[pl.ds(k0, KEY_TILE), ksl])
            vt = _bf(vsel_ref[0, vsl, pl.ds(k0, KEY_TILE)])
            madd = mask_scr[pl.ds(k0, KEY_TILE), :]
            ms2, ls2, accs2 = [], [], []
            for r in range(B_GROUP):
                s = _dot(kt, qg[r]) + bias_tiles(g * B_GROUP + r, jt * per, per) + madd
                m_new = jnp.maximum(ms[r], jnp.max(s, axis=0, keepdims=True))
                alpha = jnp.exp(ms[r] - m_new)
                p = jnp.exp(s - m_new)
                ls2.append(alpha * ls[r] + jnp.sum(p, axis=0, keepdims=True))
                accs2.append(alpha * accs[r] + _dot(vt, _bf(p)))
                ms2.append(m_new)
            return tuple(ms2), tuple(ls2), tuple(accs2)

        init = (tuple(jnp.full((1, qb), NEG_INF, F32) for _ in range(B_GROUP)),
                tuple(jnp.zeros((1, qb), F32) for _ in range(B_GROUP)),
                tuple(jnp.zeros((HEAD_DIM, qb), F32) for _ in range(B_GROUP)))
        _, ls, accs = lax.fori_loop(0, n_tiles, body, init)
        w0 = pl.multiple_of(wb * qb, qb)
        ktw = _bf(kwin_ref[pl.ds(w0, wk), ksl])
        vtw = _bf(vwin_ref[0, vsl, pl.ds(w0, wk)])
        for r in range(B_GROUP):
            h = g * B_GROUP + r
            s = _dot(ktw, qg[r]) + bias_tiles(h, wb, wtiles + 1) + madd_w
            p = jnp.exp(s - jnp.max(s, axis=0, keepdims=True))
            o_w = _dot(vtw, _bf(p)) / jnp.sum(p, axis=0, keepdims=True)
            o_s = accs[r] / ls[r]
            o_ref[0, h * HEAD_DIM:(h + 1) * HEAD_DIM, :] = (
                gates[3 * h:3 * h + 1] * o_c[r] + gates[3 * h + 1:3 * h + 2] * o_s + gates[3 * h + 2:3 * h + 3] * o_w)


def _nsa_prompt(q_t, gate_t, ckv, kv_sel, kvt_sel, kv_win, kvt_win, rel_bias):
    b, _, t = q_t.shape
    nq = t // Q_BLOCK
    nc = t // CMP_BLOCK
    ns = -(-t // SEL_BLOCK)
    wtiles = WINDOW // Q_BLOCK
    assert t % KEY_TILE == 0 and nc == SEL_RATIO * ns and WINDOW % Q_BLOCK == 0 and (wtiles + 1) * Q_BLOCK <= t
    cmp_end = jnp.arange(nc) * CMP_BLOCK + (CMP_BLOCK - 1)
    bct = _bias_of(rel_bias, jnp.arange(t)[None, :] - cmp_end[:, None])
    dz = (jnp.arange(nq)[:, None, None] * Q_BLOCK + jnp.arange(Q_BLOCK)[None, None, :]
          - jnp.arange(Q_BLOCK)[None, :, None])
    tz = _bias_of(rel_bias, dz)
    ckt = jnp.swapaxes(ckv.reshape(b, nc, KV_ROW), 1, 2)
    chan = lambda w: pl.BlockSpec((1, w, Q_BLOCK), lambda bi, i: (bi, 0, i))
    seq_rows = pl.BlockSpec((t, KV_ROW), lambda bi, i: (bi, 0))
    seq_chan = pl.BlockSpec((1, KV_ROW, t), lambda bi, i: (bi, 0, 0))
    return pl.pallas_call(
        functools.partial(_nsa_prompt_kernel, nc=nc, ns=ns, nq=nq, top_n=min(TOP_N, ns), wtiles=wtiles),
        grid=(b, nq),
        in_specs=[chan(B_WIDTH), chan(GATE_PAD),
                  pl.BlockSpec((nc, KV_ROW), lambda bi, i: (bi, 0)),
                  pl.BlockSpec((1, KV_ROW, nc), lambda bi, i: (bi, 0, 0)),
                  seq_rows, seq_chan, seq_rows, seq_chan,
                  pl.BlockSpec((B_HEADS, nc, Q_BLOCK), lambda bi, i: (0, 0, i)),
                  _full((B_HEADS, nq, Q_BLOCK, Q_BLOCK))],
        out_specs=chan(B_WIDTH),
        out_shape=jax.ShapeDtypeStruct((b, B_WIDTH, t), F32),
        scratch_shapes=[pltpu.VMEM((nc, Q_BLOCK), F32), pltpu.VMEM((t, Q_BLOCK), F32)],
        compiler_params=_cparams(("parallel", "arbitrary")),
        name="nsa_prompt",
    )(q_t, gate_t, ckv, ckt, kv_sel, kvt_sel, kv_win, kvt_win, bct, tz)


def _group_q(q_ref, g):
    rows = [q_ref[0, :, (g * B_GROUP + r) * HEAD_DIM:(g * B_GROUP + r + 1) * HEAD_DIM] for r in range(B_GROUP)]
    return _bf(jnp.concatenate(rows, axis=0) * SCALE)


def _nsa_sample_select_kernel(q_ref, gate_ref, ck_ref, bc_ref, pair_ref, tri_ref, oc_ref, idx_ref,
                              *, nsp, cur, top_n):
    gates = jax.nn.sigmoid(gate_ref[0])
    blk = lax.broadcasted_iota(I32, (1, nsp), 1)
    forced = (blk == 0) | (blk == cur) | (blk == cur - 1)
    mi = lax.broadcasted_iota(I32, (nsp, nsp), 0)
    ni = lax.broadcasted_iota(I32, (nsp, nsp), 1)
    kk = lax.broadcasted_iota(I32, (TOP_N, nsp), 0).astype(F32)
    nf = lax.broadcasted_iota(I32, (TOP_N, nsp), 1).astype(F32)
    outs = []
    for g in range(B_KV_HEADS):
        q4 = _group_q(q_ref, g)
        kc = _bf(ck_ref[:, g * HEAD_DIM:(g + 1) * HEAD_DIM])
        vc = _bf(ck_ref[:, KV_WIDTH + g * HEAD_DIM:KV_WIDTH + (g + 1) * HEAD_DIM])
        s = _dot_nt(q4, kc) + bc_ref[g * B_GROUP:(g + 1) * B_GROUP, :]
        e = jnp.exp(s - jnp.max(s, axis=-1, keepdims=True))
        p = e / jnp.sum(e, axis=-1, keepdims=True)
        o_c = _dot(_bf(p), vc)
        imp = ((p[0:1] + p[1:2]) + p[2:3]) + p[3:4]
        imp2 = jnp.dot(imp, pair_ref[...], precision=HIGHEST, preferred_element_type=F32)
        score = jnp.where(blk <= cur, imp2 + FORCE_SCORE * forced.astype(F32), NEG_INF)
        m1 = jnp.broadcast_to(score, (nsp, nsp))
        m2 = m1.T
        gt = (m2 > m1) | ((m2 == m1) & (mi < ni))
        rank = jnp.sum(gt.astype(F32), axis=0, keepdims=True)
        sel = (rank < top_n) & (score > NEG_INF / 2)
        before = _dot(_bf(sel), tri_ref[...])
        hit = jnp.broadcast_to(sel, (TOP_N, nsp)) & (jnp.broadcast_to(before, (TOP_N, nsp)) == kk)
        idx = jnp.sum(jnp.where(hit, nf, 0.0), axis=1, keepdims=True)
        cnt = jnp.sum(hit.astype(F32), axis=1, keepdims=True)
        idx_ref[0, g] = jnp.where(cnt > 0.5, idx, -1.0).astype(I32)
        for r in range(B_GROUP):
            h = g * B_GROUP + r
            outs.append(gates[:, 3 * h:3 * h + 1] * o_c[r:r + 1])
    oc_ref[0] = jnp.concatenate(outs, axis=1)


def _nsa_sample_select(q3, gate3, ckv, rel_bias, past_len):
    s = q3.shape[0]
    nc = past_len // CMP_BLOCK
    ns = -(-(past_len + 1) // SEL_BLOCK)
    nsp = -(-ns // LANES) * LANES
    cur = past_len // SEL_BLOCK
    bc = _bias_of(rel_bias, past_len - (jnp.arange(nc) * CMP_BLOCK + (CMP_BLOCK - 1)))
    pair = jnp.asarray(np.arange(nc)[:, None] // SEL_RATIO == np.arange(nsp)[None, :], F32)
    tri = jnp.asarray(np.arange(nsp)[:, None] < np.arange(nsp)[None, :], BF16)
    return pl.pallas_call(
        functools.partial(_nsa_sample_select_kernel, nsp=nsp, cur=cur, top_n=min(TOP_N, ns)),
        grid=(s,),
        in_specs=[pl.BlockSpec((1, 1, B_WIDTH), lambda i: (i, 0, 0)), pl.BlockSpec((1, 1, GATE_PAD), lambda i: (i, 0, 0)),
                  pl.BlockSpec((nc, KV_ROW), lambda i: (i, 0)), _full((B_HEADS, nc)), _full((nc, nsp)), _full((nsp, nsp))],
        out_specs=[pl.BlockSpec((1, 1, B_WIDTH), lambda i: (i, 0, 0)),
                   pl.BlockSpec((1, B_KV_HEADS, TOP_N, 1), lambda i: (i, 0, 0, 0))],
        out_shape=[jax.ShapeDtypeStruct((s, 1, B_WIDTH), F32), jax.ShapeDtypeStruct((s, B_KV_HEADS, TOP_N, 1), I32)],
        compiler_params=_cparams(("parallel",)),
        name="nsa_sample_select",
    )(q3, gate3, ckv, bc, pair, tri)


def _nsa_sample_attend_kernel(idx_ref, pt_ref, q_ref, gate_ref, oc_ref, ksel_ref, kwin_ref, win_ref, fpg_ref, bw_ref,
                              cache_ref, o_ref, buf, sem, *, cur, n_pages, past_len, page_len):
    i = pl.program_id(0)
    n_s = pl.num_programs(0)
    slot = i % 2
    halves = PAGE_BLOCKS
    n_slot = B_KV_HEADS * TOP_N

    def block_of(s_idx, j):
        return idx_ref[s_idx * n_slot + j]

    def copy(s_idx, sl, j):
        n = jnp.clip(block_of(s_idx, j), 0, cur - 1)
        page = pt_ref[s_idx * n_pages + n // halves]
        return pltpu.make_async_copy(cache_ref.at[page], buf.at[sl, j], sem.at[sl])

    def cached(s_idx, j):
        n = block_of(s_idx, j)
        return (n >= 0) & (n < cur)

    def fetch(s_idx, sl):
        for j in range(n_slot):
            @pl.when(cached(s_idx, j))
            def _():
                copy(s_idx, sl, j).start()

    @pl.when(i == 0)
    def _():
        fetch(0, 0)

    @pl.when(i + 1 < n_s)
    def _():
        fetch(i + 1, 1 - slot)

    for j in range(n_slot):
        @pl.when(cached(i, j))
        def _():
            copy(i, slot, j).wait()

        @pl.when(jnp.logical_not(cached(i, j)))
        def _():
            buf[slot, j] = jnp.zeros((KV_ROW, page_len), F32)
            buf[slot, j, :, 0:1] = ksel_ref[0]

    gates = jax.nn.sigmoid(gate_ref[0])
    t_l = lax.broadcasted_iota(I32, (1, page_len), 1)
    n_buf = win_ref.shape[2]
    j_w = lax.broadcasted_iota(I32, (1, n_buf), 1)
    d_w = n_buf - j_w
    ok_w = (d_w >= 0) & (d_w < WINDOW) & (past_len - d_w >= 0)
    f0 = fpg_ref[n_pages]
    outs = []
    for g in range(B_KV_HEADS):
        q4 = _group_q(q_ref, g)
        ksl = slice(g * HEAD_DIM, (g + 1) * HEAD_DIM)
        vsl = slice(KV_WIDTH + g * HEAD_DIM, KV_WIDTH + (g + 1) * HEAD_DIM)
        pieces = []
        for k in range(TOP_N):
            j = g * TOP_N + k
            n = block_of(i, j)
            pg = jnp.clip(n, 0, cur) // halves
            fb = fpg_ref[pg][g * B_GROUP:(g + 1) * B_GROUP, :]
            s_k = _dot(q4, _bf(buf[slot, j, ksl, :])) + fb
            ok = (n >= 0) & (t_l // SEL_BLOCK == n % halves) & (pg * page_len + t_l <= past_len)
            pieces.append(jnp.where(ok, s_k, NEG_INF))
        s = jnp.concatenate(pieces, axis=1)
        e = jnp.exp(s - jnp.max(s, axis=-1, keepdims=True))
        p = _bf(e / jnp.sum(e, axis=-1, keepdims=True))
        o_s = jnp.zeros((B_GROUP, HEAD_DIM), F32)
        for k in range(TOP_N):
            o_s = o_s + _dot_nt(p[:, k * page_len:(k + 1) * page_len], _bf(buf[slot, g * TOP_N + k, vsl, :]))
        kw = _bf(win_ref[0, ksl, :])
        vw = _bf(win_ref[0, vsl, :])
        s_w = jnp.where(ok_w, _dot(q4, kw) + bw_ref[g * B_GROUP:(g + 1) * B_GROUP, :], NEG_INF)
        k_new = _bf(kwin_ref[0, :, ksl])
        v_new = _bf(kwin_ref[0, :, vsl])
        s_n = (jnp.sum(q4.astype(F32) * k_new.astype(F32), axis=-1, keepdims=True)
               + f0[g * B_GROUP:(g + 1) * B_GROUP, 0:1])
        m = jnp.maximum(jnp.max(s_w, axis=-1, keepdims=True), s_n)
        e_w = jnp.exp(s_w - m)
        e_n = jnp.exp(s_n - m)
        den = jnp.sum(e_w, axis=-1, keepdims=True) + e_n
        o_w = _dot_nt(_bf(e_w / den), vw) + _bf(e_n / den).astype(F32) * v_new.astype(F32)
        for r in range(B_GROUP):
            h = g * B_GROUP + r
            outs.append(gates[:, 3 * h + 1:3 * h + 2] * o_s[r:r + 1] + gates[:, 3 * h + 2:3 * h + 3] * o_w[r:r + 1])
    o_ref[0] = oc_ref[0] + jnp.concatenate(outs, axis=1)


PAGE_BLOCKS = 2


def _nsa_sample_attend(idx, page_table, q3, gate3, oc3, ksel_t, kwin3, win_t, cache_t, rel_bias, past_len):
    s = q3.shape[0]
    n_pages = page_table.shape[1]
    cur = past_len // SEL_BLOCK
    n_buf = win_t.shape[2]
    page = cache_t.shape[2]
    assert page == PAGE_BLOCKS * SEL_BLOCK and past_len == n_pages * page
    keypos = jnp.arange(n_pages + 1)[:, None] * page + jnp.arange(page)[None, :]
    fpg = jnp.moveaxis(_bias_of(rel_bias, past_len - keypos), 0, 1)
    bw = _bias_of(rel_bias, n_buf - jnp.arange(n_buf))
    row3 = lambda w: pl.BlockSpec((1, 1, w), lambda i, *_: (i, 0, 0))
    grid_spec = pltpu.PrefetchScalarGridSpec(
        num_scalar_prefetch=2,
        grid=(s,),
        in_specs=[row3(B_WIDTH), row3(GATE_PAD), row3(B_WIDTH),
                  pl.BlockSpec((1, KV_ROW, 1), lambda i, *_: (i, 0, 0)), row3(KV_ROW),
                  pl.BlockSpec((1, KV_ROW, n_buf), lambda i, *_: (i, 0, 0)),
                  pl.BlockSpec((n_pages + 1, B_HEADS, page), lambda i, *_: (0, 0, 0)),
                  pl.BlockSpec((B_HEADS, n_buf), lambda i, *_: (0, 0)),
                  pl.BlockSpec(memory_space=pl.ANY)],
        out_specs=row3(B_WIDTH),
        scratch_shapes=[pltpu.VMEM((2, B_KV_HEADS * TOP_N, KV_ROW, page), F32), pltpu.SemaphoreType.DMA((2,))],
    )
    return pl.pallas_call(
        functools.partial(_nsa_sample_attend_kernel, cur=cur, n_pages=n_pages, past_len=past_len, page_len=page),
        grid_spec=grid_spec,
        out_shape=jax.ShapeDtypeStruct((s, 1, B_WIDTH), F32),
        compiler_params=_cparams(("arbitrary",)),
        name="nsa_sample_attend",
    )(idx.reshape(-1), page_table.reshape(-1), q3, gate3, oc3, ksel_t, kwin3, win_t, fpg, bw, cache_t)


def _merge_kernel(x_ref, yn_ref, bo_ref, ga_ref, yb_ref, pg_ref, lnw_ref, lnb_ref, wa_ref, wb_ref, wo_ref, nf_ref,
                  wr_ref, br_ref, x1_ref, h_ref, te_ref, tw_ref, *, a_transposed):
    if a_transposed:
        tm = x_ref.shape[0]
        yn = yn_ref[...].reshape(A_WIDTH, tm)
        y_a = ((yn * lnw_ref[...] + lnb_ref[...] + bo_ref[0]) * ga_ref[0]).T
        y_b = yb_ref[0].T
    else:
        y_a = (yn_ref[...] * lnw_ref[...] + lnb_ref[...] + bo_ref[...]) * ga_ref[...]
        y_b = yb_ref[...]
    g_a = jax.nn.sigmoid(pg_ref[:, :D_MODEL])
    g_b = jax.nn.sigmoid(pg_ref[:, D_MODEL:])
    m = g_a * _dot(_bf(y_a), wa_ref[...]) + g_b * _dot(_bf(y_b), wb_ref[...])
    x1 = x_ref[...] + _dot(_bf(m), wo_ref[...])
    x1_ref[...] = x1
    h = x1 * lax.rsqrt(jnp.mean(x1 * x1, axis=-1, keepdims=True) + NORM_EPS) * nf_ref[...]
    _store_rows(h_ref, h)
    logits = jnp.dot(h, wr_ref[...], precision=HIGHEST, preferred_element_type=F32) + br_ref[...]
    lane = lax.broadcasted_iota(I32, logits.shape, 1).astype(F32)
    col = lax.broadcasted_iota(I32, (logits.shape[0], TOP_K), 1)
    vals, idxs = [], []
    for _ in range(TOP_K):
        m_k = jnp.max(logits, axis=-1, keepdims=True)
        i_k = jnp.min(jnp.where(logits == m_k, lane, float(N_EXPERTS)), axis=-1, keepdims=True)
        vals.append(m_k)
        idxs.append(i_k)
        logits = jnp.where(lane == i_k, -jnp.inf, logits)
    e = [jnp.exp(v - vals[0]) for v in vals]
    den = ((e[0] + e[1]) + e[2]) + e[3]
    te = jnp.zeros(col.shape, F32)
    tw = jnp.zeros(col.shape, F32)
    for k in range(TOP_K):
        te = jnp.where(col == k, idxs[k], te)
        tw = jnp.where(col == k, e[k] / den, tw)
    te_ref[...] = te.astype(I32)
    tw_ref[...] = tw


def _merge(x, yn, bonus, gate, yb, pg, ln_w, ln_b, w_a, w_b, w_o, norm_ffn, w_router, b_router, tm, seq=None):
    rows = x.shape[0]
    row = lambda w: pl.BlockSpec((tm, w), lambda i: (i, 0))
    vec = lambda z: z.reshape(1, -1).astype(F32)
    if seq is None:
        a_specs = [row(A_WIDTH)] * 3 + [row(B_WIDTH)] + [_full((1, A_WIDTH))] * 2
        lnw, lnb = vec(ln_w), vec(ln_b)
    else:
        per = seq // tm
        nspec = pl.BlockSpec((1, A_WIDTH, tm), lambda i: (i // per, 0, i % per))
        a_specs = [pl.BlockSpec((HEAD_DIM, 1, A_HEADS, tm), lambda i: (0, i // per, 0, i % per)), nspec, nspec, nspec,
                   _full((A_WIDTH, 1)), _full((A_WIDTH, 1))]
        lnw, lnb = (z[CH_MAJOR].reshape(-1, 1).astype(F32) for z in (ln_w, ln_b))
        w_a = w_a[CH_MAJOR]
    return pl.pallas_call(
        functools.partial(_merge_kernel, a_transposed=seq is not None),
        grid=(rows // tm,),
        in_specs=[row(D_MODEL)] + a_specs[:4] + [row(GATE_PROJ)] + a_specs[4:]
                 + [_full((A_WIDTH, D_MODEL)), _full((B_WIDTH, D_MODEL)),
                  _full((D_MODEL, D_MODEL)), _full((1, D_MODEL)), _full((D_MODEL, N_EXPERTS)), _full((1, N_EXPERTS))],
        out_specs=[row(D_MODEL), pl.BlockSpec((tm * ROW_TILE, LANES), lambda i: (i, 0)), row(TOP_K), row(TOP_K)],
        out_shape=[jax.ShapeDtypeStruct((rows, D_MODEL), F32), jax.ShapeDtypeStruct((rows * ROW_TILE, LANES), F32),
                   jax.ShapeDtypeStruct((rows, TOP_K), I32), jax.ShapeDtypeStruct((rows, TOP_K), F32)],
        compiler_params=_cparams(("parallel",)),
        name="merge_router",
    )(x, yn, bonus, gate, yb, pg, lnw, lnb, _bf(w_a), _bf(w_b), _bf(w_o), vec(norm_ffn),
      w_router.astype(F32), vec(b_router))


ROW_TILE = D_MODEL // LANES


def _load_rows(ref, n, first=0, every=1):
    return jnp.concatenate([ref[pl.ds(first * ROW_TILE + s, n, stride=every * ROW_TILE), :] for s in range(ROW_TILE)],
                           axis=1)


def _store_rows(ref, x):
    for s in range(ROW_TILE):
        ref[pl.ds(s, x.shape[0], stride=ROW_TILE), :] = x[:, s * LANES:(s + 1) * LANES]


def _dispatch_kernel(dest_ref, h_ref, xs_in_ref, xs_ref, sem, *, tm):
    del xs_in_ref
    for t in range(tm):
        for k in range(TOP_K):
            pltpu.make_async_copy(h_ref.at[t], xs_ref.at[dest_ref[0, 0, t * TOP_K + k]], sem).start()
    for t in range(tm * TOP_K):
        pltpu.make_async_copy(h_ref.at[0], xs_ref.at[0], sem).wait()


def _dispatch(dest, h_tiles, slots, tm):
    n = h_tiles.shape[0]
    return pl.pallas_call(
        functools.partial(_dispatch_kernel, tm=tm),
        grid=(n // tm,),
        in_specs=[pl.BlockSpec((1, 1, tm * TOP_K), lambda i: (i, 0, 0), memory_space=pltpu.SMEM),
                  pl.BlockSpec((tm, ROW_TILE, LANES), lambda i: (i, 0, 0)), pl.BlockSpec(memory_space=pl.ANY)],
        out_specs=pl.BlockSpec(memory_space=pl.ANY),
        out_shape=jax.ShapeDtypeStruct((slots, ROW_TILE, LANES), F32),
        scratch_shapes=[pltpu.SemaphoreType.DMA(())],
        input_output_aliases={2: 0},
        compiler_params=_cparams(("arbitrary",)),
        name="moe_dispatch",
    )(dest.reshape(n // tm, 1, tm * TOP_K), h_tiles, jnp.zeros((slots, ROW_TILE, LANES), F32))


def _moe_kernel(ce_ref, nu_ref, x_ref, wu_ref, bu_ref, wd_ref, bd_ref, o_ref, wu_bf, wd_bf):
    c = pl.program_id(0)
    e = ce_ref[c]
    prev = ce_ref[jnp.maximum(c - 1, 0)]

    @pl.when((c == 0) | (e != prev))
    def _():
        wu_bf[...] = _bf(wu_ref[0])
        wd_bf[...] = _bf(wd_ref[0])

    @pl.when(c < nu_ref[0])
    def _():
        u = _dot(_bf(_load_rows(x_ref, MOE_ROWS)), wu_bf[...]) + bu_ref[0]
        glu = jnp.minimum(u[:, :D_FF], SWIGLU_LIMIT)
        lin = jnp.clip(u[:, D_FF:], -SWIGLU_LIMIT, SWIGLU_LIMIT)
        act = glu * jax.nn.sigmoid(SWIGLU_ALPHA * glu) * (lin + 1.0)
        _store_rows(o_ref, _dot(_bf(act), wd_bf[...]) + bd_ref[0])

    @pl.when(c >= nu_ref[0])
    def _():
        o_ref[...] = jnp.zeros_like(o_ref)


def _moe_experts(chunk_e, n_used, xs, w_up, b_up, w_down, b_down):
    n_chunks = xs.shape[0] // (MOE_ROWS * ROW_TILE)
    rows = pl.BlockSpec((MOE_ROWS * ROW_TILE, LANES), lambda c, ce, nu: (c, 0))
    grid_spec = pltpu.PrefetchScalarGridSpec(
        num_scalar_prefetch=2,
        grid=(n_chunks,),
        in_specs=[rows,
                  pl.BlockSpec((1, D_MODEL, 2 * D_FF), lambda c, ce, nu: (ce[c], 0, 0)),
                  pl.BlockSpec((1, 1, 2 * D_FF), lambda c, ce, nu: (ce[c], 0, 0)),
                  pl.BlockSpec((1, D_FF, D_MODEL), lambda c, ce, nu: (ce[c], 0, 0)),
                  pl.BlockSpec((1, 1, D_MODEL), lambda c, ce, nu: (ce[c], 0, 0))],
        out_specs=rows,
        scratch_shapes=[pltpu.VMEM((D_MODEL, 2 * D_FF), BF16), pltpu.VMEM((D_FF, D_MODEL), BF16)],
    )
    return pl.pallas_call(
        _moe_kernel,
        grid_spec=grid_spec,
        out_shape=jax.ShapeDtypeStruct(xs.shape, F32),
        compiler_params=_cparams(("arbitrary",)),
        name="moe_experts",
    )(chunk_e, n_used, xs, w_up, b_up.reshape(N_EXPERTS, 1, -1), w_down, b_down.reshape(N_EXPERTS, 1, -1))


def _combine_kernel(dcur_ref, dnext_ref, x_ref, w_ref, g_ref, ys_ref, o_ref, buf, sem, *, tm, normalize):
    i = pl.program_id(0)
    slot = i % 2
    n_a = tm * TOP_K

    def copy(d_ref, sl, a):
        return pltpu.make_async_copy(ys_ref.at[d_ref[0, 0, a]], buf.at[sl, pl.ds(a * ROW_TILE, ROW_TILE)], sem.at[sl])

    def fetch(d_ref, sl):
        for a in range(n_a):
            copy(d_ref, sl, a).start()

    @pl.when(i == 0)
    def _():
        fetch(dcur_ref, 0)

    @pl.when(i + 1 < pl.num_programs(0))
    def _():
        fetch(dnext_ref, 1 - slot)

    for a in range(n_a):
        copy(dcur_ref, slot, a).wait()
    x = x_ref[...]
    for k in range(TOP_K):
        x = x + w_ref[:, k:k + 1] * _load_rows(buf.at[slot], tm, first=k, every=TOP_K)
    if normalize:
        x = x * lax.rsqrt(jnp.mean(x * x, axis=-1, keepdims=True) + NORM_EPS) * g_ref[...]
    o_ref[...] = x


def _combine(dest, x1, top_w, ys_tiles, g, tm, normalize):
    n = x1.shape[0]
    steps = n // tm
    d2 = dest.reshape(steps, 1, tm * TOP_K)
    row = lambda w: pl.BlockSpec((tm, w), lambda i: (i, 0))
    return pl.pallas_call(
        functools.partial(_combine_kernel, tm=tm, normalize=normalize),
        grid=(steps,),
        in_specs=[pl.BlockSpec((1, 1, tm * TOP_K), lambda i: (i, 0, 0), memory_space=pltpu.SMEM),
                  pl.BlockSpec((1, 1, tm * TOP_K), lambda i: (jnp.minimum(i + 1, steps - 1), 0, 0),
                               memory_space=pltpu.SMEM),
                  row(D_MODEL), row(TOP_K), _full((1, D_MODEL)), pl.BlockSpec(memory_space=pl.ANY)],
        out_specs=row(D_MODEL),
        out_shape=jax.ShapeDtypeStruct((n, D_MODEL), F32),
        scratch_shapes=[pltpu.VMEM((2, tm * TOP_K * ROW_TILE, LANES), F32), pltpu.SemaphoreType.DMA((2,))],
        compiler_params=_cparams(("arbitrary",)),
        name="moe_combine",
    )(d2, d2, x1, top_w, g.reshape(1, -1).astype(F32), ys_tiles)


def _moe_residual(x1, h_tiles, top_e, top_w, w_up, b_up, w_down, b_down, g, normalize):
    n = x1.shape[0]
    n_assign = n * TOP_K
    onehot = (top_e.reshape(-1, 1) == jnp.arange(N_EXPERTS, dtype=I32)[None, :]).astype(I32)
    csum = jnp.cumsum(onehot, axis=0)
    counts = csum[-1]
    padded = (counts + MOE_ROWS - 1) // MOE_ROWS * MOE_ROWS
    pad_end = jnp.cumsum(padded)
    dest = jnp.sum(onehot * (csum - 1 + (pad_end - padded)[None, :]), axis=1).astype(I32)
    n_chunks = -(-(n_assign + N_EXPERTS * (MOE_ROWS - 1)) // MOE_ROWS)
    slots = n_chunks * MOE_ROWS
    chunk_e = jnp.minimum(jnp.searchsorted(pad_end, jnp.arange(n_chunks) * MOE_ROWS, side='right'),
                          N_EXPERTS - 1).astype(I32)
    n_used = (pad_end[-1] // MOE_ROWS).astype(I32).reshape(1)
    tm = _row_tile(n, 128)
    xs = _dispatch(dest, h_tiles.reshape(n, ROW_TILE, LANES), slots, tm)
    ys = _moe_experts(chunk_e, n_used, xs.reshape(slots * ROW_TILE, LANES), w_up, b_up, w_down, b_down)
    return _combine(dest, x1, top_w, ys.reshape(slots, ROW_TILE, LANES), g, tm, normalize)


def _row_tile(rows, cap):
    tm = cap
    while rows % tm:
        tm //= 2
    return tm


def _rwkv_group(p_a, prev, s0_tiles, b, t, rw, tc):
    mu, w0, wdu, a0, wiu, wgu, k_k, k_a, r_k = rw
    tm = _row_tile(b * t, 256)
    r, w, k2, v, kk, kb, gate, bonus = _rwkv_prep(p_a, prev, mu, w0, wdu, a0, wiu, wgu, k_k, k_a, r_k, tm)
    kt = lambda z: _to_key_tiles(z, b, t)
    yn, s_fin = _wkv_scan(kt(r), kt(w), kt(k2), _to_val_tiles(v, b, t), kt(kk), kt(kb), s0_tiles, tc)
    return _from_val_tiles(yn, b, t), gate, bonus, _state_from_tiles(s_fin, b)


def kernel(x_prompt, x_sample, cache_cmp_kv, cache_sel_kv, state_win_kv, state_rwkv, state_rwkv_shift, page_table,
           norm_attn, w_in, mu_shift, w0, w_decay_up, a0, w_iclr_up, w_gate_up, k_k, k_a, r_k, ln_x_w, ln_x_b,
           pe_cmp_k, pe_cmp_v, w_cmp_k1, w_cmp_k2, w_cmp_v1, w_cmp_v2, rel_bias, w_br_a, w_br_b, w_out,
           norm_ffn, w_router, b_router, w_up, b_up, w_down, b_down, norm_final):
    bp, tp, _ = x_prompt.shape
    bs, ts, _ = x_sample.shape
    depth = w_in.shape[0]
    past_len = page_table.shape[1] * cache_cmp_kv.shape[2]
    n_buf = state_win_kv.shape[2]
    assert ts == 1 and bp * A_HEADS == BH_LANES and bs % 8 == 0 and tp % LANES == 0
    xp = x_prompt.reshape(bp * tp, D_MODEL)
    xs = x_sample.reshape(bs, D_MODEL)
    new = {name: [] for name in ('cmp_p', 'sel_p', 'win_p', 'wkv_p', 'shift_p', 'cmp_s', 'sel_s', 'win_s', 'wkv_s', 'shift_s')}
    kv6 = lambda z, b, t: z.reshape(b, t, 2, B_KV_HEADS, HEAD_DIM)
    for l in range(depth):
        rw = (mu_shift[l], w0[l], w_decay_up[l], a0[l], w_iclr_up[l], w_gate_up[l], k_k[l], k_a[l], r_k[l])
        cw = (pe_cmp_k[l], pe_cmp_v[l], w_cmp_k1[l], w_cmp_k2[l], w_cmp_v1[l], w_cmp_v2[l])
        wa_nat, w_rest, w_t, w_n = _pack_w_in(w_in[l])
        g_attn = norm_attn[l].reshape(1, -1).astype(F32)
        last = l == depth - 1
        g_fin = norm_final if last else jnp.ones((D_MODEL,), F32)

        tq = _row_tile(tp, 256)
        pa_t, q_t, kvt_c, kvt_s, kvt_w, gt_t, kv_c, kv_s, kv_w, pg = _project_t(xp, g_attn, w_t, w_n, tq, tp)
        r, w, k2, v, kk, kb, gate, bonus = _rwkv_prep_t(pa_t, *rw, tq)
        r, w, k2, kk, kb = _key_tiles([z.reshape(N_JP, LANES, tp) for z in (r, w, k2, kk, kb)], LANES)
        v = _val_tiles(v.reshape(HEAD_DIM, BH_LANES, tp), LANES).reshape(1, tp, HEAD_DIM, LANES)
        s0 = jnp.zeros((1, N_IO, N_JP, 8, LANES), F32)
        yn, s_fin = _wkv_scan(r, w, k2, v, kk, kb, s0, _row_tile(tp, 32))
        yn = _val_untile(yn.reshape(tp * HEAD_DIM, LANES), LANES).reshape(HEAD_DIM, bp, A_HEADS, tp)
        ckv = _compress(kv_c, *cw, rows=_row_tile(tp, 2048))
        y_b = _nsa_prompt(q_t, gt_t, ckv, kv_s, kvt_s, kv_w, kvt_w, rel_bias)
        x1, h, top_e, top_w = _merge(xp, yn, bonus, gate, y_b, pg, ln_x_w[l], ln_x_b[l], w_br_a[l], w_br_b[l], w_out[l],
                                     norm_ffn[l], w_router[l], b_router[l], tq, seq=tp)
        xp_next = _moe_residual(x1, h, top_e, top_w, w_up[l], b_up[l], w_down[l], b_down[l], g_fin, last)
        n_win = min(WINDOW, tp)
        kv6t = lambda z: jnp.moveaxis(z.reshape(bp, 2, B_KV_HEADS, HEAD_DIM, tp), -1, 1)
        new['cmp_p'].append(kv6t(kvt_c))
        new['sel_p'].append(kv6t(kvt_s))
        new['win_p'].append(kv6t(kvt_w)[:, tp - n_win:])
        new['wkv_p'].append(_state_from_tiles(s_fin, bp))
        new['shift_p'].append(jnp.zeros((bp, A_PROJ), F32).at[:, A_PERM].set(pa_t[:, :, tp - 1]))

        s_a, q, kv_c, kv_s, kv_w, gt, pg = _project(xs, g_attn, wa_nat, w_rest, _row_tile(bs, 256))
        yn, gate, bonus, wkv_s = _rwkv_group(s_a, state_rwkv_shift[l], _state_to_tiles(state_rwkv[l].astype(F32), bs),
                                             bs, 1, rw, 1)
        tok_minor = lambda z: jnp.moveaxis(z, 1, -1).reshape(z.shape[0], KV_ROW, z.shape[1])
        ckv = _compress_paged(tok_minor(cache_cmp_kv[l]), page_table, cw)
        q3, gt3 = q.reshape(bs, 1, B_WIDTH), gt.reshape(bs, 1, GATE_PAD)
        oc3, idx = _nsa_sample_select(q3, gt3, ckv, rel_bias, past_len)
        y_b = _nsa_sample_attend(idx, page_table, q3, gt3, oc3, kv_s.reshape(bs, KV_ROW, 1), kv_w.reshape(bs, 1, KV_ROW),
                                 tok_minor(state_win_kv[l]), tok_minor(cache_sel_kv[l]),
                                 rel_bias, past_len).reshape(bs, B_WIDTH)
        x1, h, top_e, top_w = _merge(xs, yn, bonus, gate, y_b, pg, ln_x_w[l], ln_x_b[l], w_br_a[l], w_br_b[l], w_out[l],
                                     norm_ffn[l], w_router[l], b_router[l], _row_tile(bs, 256))
        xs_next = _moe_residual(x1, h, top_e, top_w, w_up[l], b_up[l], w_down[l], b_down[l], g_fin, last)
        new['cmp_s'].append(kv6(kv_c, bs, 1))
        new['sel_s'].append(kv6(kv_s, bs, 1))
        new['win_s'].append(jnp.concatenate([state_win_kv[l], kv6(kv_w, bs, 1)], axis=1)[:, 1:])
        new['wkv_s'].append(wkv_s.astype(state_rwkv.dtype))
        new['shift_s'].append(s_a)
        xp, xs = xp_next, xs_next
    return (xp.reshape(bp, tp, D_MODEL), xs.reshape(bs, ts, D_MODEL),
            jnp.stack(new['cmp_p']), jnp.stack(new['sel_p']), jnp.stack(new['win_p']),
            jnp.stack(new['wkv_p']), jnp.stack(new['shift_p']),
            jnp.stack(new['cmp_s']), jnp.stack(new['sel_s']), jnp.stack(new['win_s']),
            jnp.stack(new['wkv_s']), jnp.stack(new['shift_s']))
```

```python
import functools
import math

import jax
import jax.numpy as jnp
import numpy as np
from jax import lax
from jax.experimental import pallas as pl
from jax.experimental.pallas import tpu as pltpu

F32 = jnp.float32
BF16 = jnp.bfloat16
I32 = jnp.int32
HIGHEST = lax.Precision.HIGHEST

D_MODEL = 1024
HEAD_DIM = 64
A_HEADS = 8
A_WIDTH = A_HEADS * HEAD_DIM
A_DECAY_RANK = 64
A_ICLR_RANK = 64
A_GATE_RANK = 128
A_GN_EPS = 64e-5
A_PROJ = 3 * A_WIDTH + A_DECAY_RANK + A_ICLR_RANK + A_GATE_RANK
B_HEADS = 8
B_KV_HEADS = 2
B_GROUP = B_HEADS // B_KV_HEADS
B_WIDTH = B_HEADS * HEAD_DIM
KV_WIDTH = B_KV_HEADS * HEAD_DIM
KV_ROW = 2 * KV_WIDTH
CMP_BLOCK = 32
CMP_HIDDEN = 128
SEL_BLOCK = 64
SEL_RATIO = SEL_BLOCK // CMP_BLOCK
TOP_N = 16
WINDOW = 512
Q_BLOCK = 128
FORCE_SCORE = 1e4
N_BUCKETS = 32
MAX_EXACT = N_BUCKETS // 2
MAX_DISTANCE = 1024
N_EXPERTS = 32
TOP_K = 4
D_FF = 1024
SWIGLU_ALPHA = 1.702
SWIGLU_LIMIT = 7.0
NORM_EPS = 1e-6
NEG_INF = -1e30
SCALE = HEAD_DIM ** -0.5
GATE_PAD = 128
GATE_PROJ = 2 * D_MODEL
MOE_ROWS = 256
LANES = 128
VMEM_LIMIT = 56 * 1024 * 1024


def _cparams(sem):
    return pltpu.CompilerParams(dimension_semantics=sem, vmem_limit_bytes=VMEM_LIMIT)


def _full(shape):
    n = len(shape)
    return pl.BlockSpec(shape, lambda *_: (0,) * n)


def _dot(a, b):
    return jnp.dot(a, b, preferred_element_type=F32)


def _dot_nt(a, b):
    return lax.dot_general(a, b, (((1,), (1,)), ((), ())), preferred_element_type=F32)


def _bf(x):
    return x.astype(BF16)


PROJ_SPLITS = (B_WIDTH, KV_ROW, KV_ROW, KV_ROW, GATE_PAD, GATE_PROJ)
CH_MAJOR = (np.arange(A_HEADS)[None, :] * HEAD_DIM + np.arange(HEAD_DIM)[:, None]).reshape(-1)
A_PERM = np.concatenate([CH_MAJOR, A_WIDTH + CH_MAJOR, 2 * A_WIDTH + CH_MAJOR, np.arange(3 * A_WIDTH, A_PROJ)])


def _rms_bf16(x_ref, g_ref):
    x = x_ref[...]
    return _bf(x * lax.rsqrt(jnp.mean(x * x, axis=-1, keepdims=True) + NORM_EPS) * g_ref[...])


def _proj_kernel(x_ref, g_ref, wa_ref, w_ref, oa_ref, *o_refs):
    h = _rms_bf16(x_ref, g_ref)
    oa_ref[...] = _dot(h, wa_ref[...])
    c = 0
    for o_ref, n in zip(o_refs, PROJ_SPLITS):
        o_ref[...] = _dot(h, w_ref[:, c:c + n])
        c += n


def _project(x, g, wa, w, tm):
    rows = x.shape[0]
    splits = (A_PROJ,) + PROJ_SPLITS
    return pl.pallas_call(
        _proj_kernel,
        grid=(rows // tm,),
        in_specs=[pl.BlockSpec((tm, D_MODEL), lambda i: (i, 0)), _full((1, D_MODEL)), _full(wa.shape), _full(w.shape)],
        out_specs=[pl.BlockSpec((tm, n), lambda i: (i, 0)) for n in splits],
        out_shape=[jax.ShapeDtypeStruct((rows, n), F32) for n in splits],
        compiler_params=_cparams(("parallel",)),
        name="norm_proj",
    )(x, g, wa, w)


PROJ_T_SPLITS = (A_PROJ, B_WIDTH, KV_ROW, KV_ROW, KV_ROW, GATE_PAD)
PROJ_N_SPLITS = (KV_ROW, KV_ROW, KV_ROW, GATE_PROJ)


def _proj_t_kernel(x_ref, g_ref, wt_ref, wn_ref, *o_refs):
    h = _rms_bf16(x_ref, g_ref)
    t = _dot_nt(wt_ref[...], h)
    c = 0
    for o_ref, n in zip(o_refs[:len(PROJ_T_SPLITS)], PROJ_T_SPLITS):
        o_ref[0] = t[c:c + n]
        c += n
    c = 0
    for o_ref, n in zip(o_refs[len(PROJ_T_SPLITS):], PROJ_N_SPLITS):
        o_ref[...] = _dot(h, wn_ref[:, c:c + n])
        c += n


def _project_t(x, g, wt, wn, tm, seq):
    rows = x.shape[0]
    per = seq // tm
    return pl.pallas_call(
        _proj_t_kernel,
        grid=(rows // tm,),
        in_specs=[pl.BlockSpec((tm, D_MODEL), lambda i: (i, 0)), _full((1, D_MODEL)), _full(wt.shape), _full(wn.shape)],
        out_specs=[pl.BlockSpec((1, n, tm), lambda i: (i // per, 0, i % per)) for n in PROJ_T_SPLITS]
                  + [pl.BlockSpec((tm, n), lambda i: (i, 0)) for n in PROJ_N_SPLITS],
        out_shape=[jax.ShapeDtypeStruct((rows // seq, n, seq), F32) for n in PROJ_T_SPLITS]
                  + [jax.ShapeDtypeStruct((rows, n), F32) for n in PROJ_N_SPLITS],
        compiler_params=_cparams(("parallel",)),
        name="norm_proj_t",
    )(x, g, wt, wn)


def _pack_w_in(w_in):
    a, rest = w_in[:, :A_PROJ], w_in[:, A_PROJ:]
    q, kv, gt, pg = (rest[:, :B_WIDTH], rest[:, B_WIDTH:B_WIDTH + 3 * KV_ROW],
                     rest[:, B_WIDTH + 3 * KV_ROW:B_WIDTH + 3 * KV_ROW + 3 * B_HEADS],
                     rest[:, B_WIDTH + 3 * KV_ROW + 3 * B_HEADS:])
    gt = jnp.pad(gt, ((0, 0), (0, GATE_PAD - 3 * B_HEADS)))
    w_t = jnp.concatenate([a[:, A_PERM], q, kv, gt], axis=1).T
    return _bf(a), _bf(jnp.concatenate([q, kv, gt, pg], axis=1)), _bf(w_t), _bf(jnp.concatenate([kv, pg], axis=1))


def _softplus(z):
    return jnp.maximum(z, 0.0) + jnp.log1p(jnp.exp(-jnp.abs(z)))


def _rwkv_prep_kernel(p_ref, prev_ref, mu_ref, w0_ref, wdu_ref, a0_ref, wiu_ref, wgu_ref, kk_ref, ka_ref, rk_ref,
                      ones_ref, r_o, w_o, k_o, v_o, kk_o, kb_o, g_o, bo_o):
    p = p_ref[...]
    ps = p + mu_ref[...] * (prev_ref[...] - p)
    r = ps[:, 0:A_WIDTH]
    k = ps[:, A_WIDTH:2 * A_WIDTH]
    v = ps[:, 2 * A_WIDTH:3 * A_WIDTH]
    c = 3 * A_WIDTH
    xw = ps[:, c:c + A_DECAY_RANK]
    xa = ps[:, c + A_DECAY_RANK:c + A_DECAY_RANK + A_ICLR_RANK]
    xg = ps[:, c + A_DECAY_RANK + A_ICLR_RANK:]
    w_log = -_softplus(-(w0_ref[...] + _dot(_bf(jnp.tanh(xw)), wdu_ref[...]))) - 0.5
    decay = jnp.exp(-jnp.exp(w_log))
    a = jax.nn.sigmoid(a0_ref[...] + _dot(_bf(xa), wiu_ref[...]))
    gate = _dot(_bf(jax.nn.sigmoid(xg)), wgu_ref[...])
    ones = ones_ref[...]
    kk = k * kk_ref[...]
    ss = jnp.dot(kk * kk, ones, precision=HIGHEST, preferred_element_type=F32)
    kk = kk / jnp.maximum(jnp.sqrt(ss), 1e-12)
    k2 = k * (1.0 + (a - 1.0) * ka_ref[...])
    rk = jnp.dot(r * k2 * rk_ref[...], ones, precision=HIGHEST, preferred_element_type=F32)
    r_o[...] = r
    w_o[...] = decay
    k_o[...] = k2
    v_o[...] = v
    kk_o[...] = kk
    kb_o[...] = kk * a
    g_o[...] = gate
    bo_o[...] = rk * v


def _rwkv_prep(p, prev, mu, w0, wdu, a0, wiu, wgu, k_k, k_a, r_k, tm):
    rows = p.shape[0]
    head = np.arange(A_WIDTH) // HEAD_DIM
    ones = jnp.asarray(head[:, None] == head[None, :], F32)
    row = lambda z: z.reshape(1, -1).astype(F32)
    spec_in = pl.BlockSpec((tm, A_PROJ), lambda i: (i, 0))
    spec_o = pl.BlockSpec((tm, A_WIDTH), lambda i: (i, 0))
    return pl.pallas_call(
        _rwkv_prep_kernel,
        grid=(rows // tm,),
        in_specs=[spec_in, spec_in, _full((1, A_PROJ)), _full((1, A_WIDTH)), _full((A_DECAY_RANK, A_WIDTH)),
                  _full((1, A_WIDTH)), _full((A_ICLR_RANK, A_WIDTH)), _full((A_GATE_RANK, A_WIDTH)),
                  _full((1, A_WIDTH)), _full((1, A_WIDTH)), _full((1, A_WIDTH)), _full((A_WIDTH, A_WIDTH))],
        out_specs=[spec_o] * 8,
        out_shape=[jax.ShapeDtypeStruct((rows, A_WIDTH), F32)] * 8,
        compiler_params=_cparams(("parallel",)),
        name="rwkv_prep",
    )(p, prev, row(mu), row(w0), _bf(wdu), row(a0), _bf(wiu), _bf(wgu), row(k_k), row(k_a), row(r_k), ones)


def _rwkv_prep_t_kernel(p_ref, mu_ref, w0_ref, wdu_ref, a0_ref, wiu_ref, wgu_ref, kk_ref, ka_ref, rk_ref,
                        r_o, w_o, k_o, v_o, kk_o, kb_o, g_o, bo_o, carry):
    tt = p_ref.shape[2]

    @pl.when(pl.program_id(1) == 0)
    def _():
        carry[...] = jnp.zeros_like(carry)

    p = p_ref[0]
    lane = lax.broadcasted_iota(I32, (1, tt), 1)
    prev = jnp.where(lane == 0, carry[:, 0:1], pltpu.roll(p, 1, 1))
    carry[:, 0:1] = p[:, tt - 1:tt]
    ps = p + mu_ref[...] * (prev - p)
    r = ps[0:A_WIDTH]
    k = ps[A_WIDTH:2 * A_WIDTH]
    v = ps[2 * A_WIDTH:3 * A_WIDTH]
    c = 3 * A_WIDTH
    xw = ps[c:c + A_DECAY_RANK]
    xa = ps[c + A_DECAY_RANK:c + A_DECAY_RANK + A_ICLR_RANK]
    xg = ps[c + A_DECAY_RANK + A_ICLR_RANK:]

    def head_sum(x):
        s = jnp.sum(x.reshape(HEAD_DIM, A_HEADS, tt), axis=0)
        return jnp.broadcast_to(s[None], (HEAD_DIM, A_HEADS, tt)).reshape(A_WIDTH, tt)

    w_log = -_softplus(-(w0_ref[...] + _dot(wdu_ref[...], _bf(jnp.tanh(xw))))) - 0.5
    decay = jnp.exp(-jnp.exp(w_log))
    a = jax.nn.sigmoid(a0_ref[...] + _dot(wiu_ref[...], _bf(xa)))
    gate = _dot(wgu_ref[...], _bf(jax.nn.sigmoid(xg)))
    kk = k * kk_ref[...]
    kk = kk / jnp.maximum(jnp.sqrt(head_sum(kk * kk)), 1e-12)
    k2 = k * (1.0 + (a - 1.0) * ka_ref[...])
    rk = head_sum(r * k2 * rk_ref[...])
    for o_ref, val in ((r_o, r), (w_o, decay), (k_o, k2), (kk_o, kk), (kb_o, kk * a)):
        o_ref[...] = val.reshape(N_JP, 2, 1, A_HEADS, tt)
    v_o[...] = v.reshape(HEAD_DIM, 1, A_HEADS, tt)
    g_o[0] = gate
    bo_o[0] = rk * v


def _rwkv_prep_t(p_t, mu, w0, wdu, a0, wiu, wgu, k_k, k_a, r_k, tt):
    b, _, t = p_t.shape
    col = lambda z, perm: z.reshape(-1)[perm].reshape(-1, 1).astype(F32)
    up = lambda w: _bf(w[:, CH_MAJOR].T)
    kshape = jax.ShapeDtypeStruct((N_JP, 2, b, A_HEADS, t), F32)
    kspec = pl.BlockSpec((N_JP, 2, 1, A_HEADS, tt), lambda bi, i: (0, 0, bi, 0, i))
    vspec = pl.BlockSpec((HEAD_DIM, 1, A_HEADS, tt), lambda bi, i: (0, bi, 0, i))
    nspec = pl.BlockSpec((1, A_WIDTH, tt), lambda bi, i: (bi, 0, i))
    nshape = jax.ShapeDtypeStruct((b, A_WIDTH, t), F32)
    return pl.pallas_call(
        _rwkv_prep_t_kernel,
        grid=(b, t // tt),
        in_specs=[pl.BlockSpec((1, A_PROJ, tt), lambda bi, i: (bi, 0, i)), _full((A_PROJ, 1)), _full((A_WIDTH, 1)),
                  _full((A_WIDTH, A_DECAY_RANK)), _full((A_WIDTH, 1)), _full((A_WIDTH, A_ICLR_RANK)),
                  _full((A_WIDTH, A_GATE_RANK)), _full((A_WIDTH, 1)), _full((A_WIDTH, 1)), _full((A_WIDTH, 1))],
        out_specs=[kspec, kspec, kspec, vspec, kspec, kspec, nspec, nspec],
        out_shape=[kshape, kshape, kshape, jax.ShapeDtypeStruct((HEAD_DIM, b, A_HEADS, t), F32), kshape, kshape,
                   nshape, nshape],
        scratch_shapes=[pltpu.VMEM((A_PROJ, LANES), F32)],
        compiler_params=_cparams(("parallel", "arbitrary")),
        name="rwkv_prep_t",
    )(p_t, col(mu, A_PERM), col(w0, CH_MAJOR), up(wdu), col(a0, CH_MAJOR), up(wiu), up(wgu),
      col(k_k, CH_MAJOR), col(k_a, CH_MAJOR), col(r_k, CH_MAJOR))


def _key_tiles_kernel(*refs):
    n = len(refs) // 2
    for z_ref, o_ref in zip(refs[:n], refs[n:]):
        for jp in range(N_JP):
            o_ref[0, jp] = z_ref[jp].T


def _key_tiles(zs, tt):
    t = zs[0].shape[2]
    return pl.pallas_call(
        _key_tiles_kernel,
        grid=(t // tt,),
        in_specs=[pl.BlockSpec((N_JP, LANES, tt), lambda i: (0, 0, i))] * len(zs),
        out_specs=[pl.BlockSpec((1, N_JP, tt, LANES), lambda i: (0, 0, i, 0))] * len(zs),
        out_shape=[jax.ShapeDtypeStruct((1, N_JP, t, LANES), F32)] * len(zs),
        compiler_params=_cparams(("parallel",)),
        name="wkv_key_tiles",
    )(*zs)


def _val_tiles_kernel(z_ref, o_ref, *, tt):
    for i in range(HEAD_DIM):
        x = z_ref[i]
        o_ref[pl.ds(i, tt, stride=HEAD_DIM), :] = jnp.concatenate([x, x], axis=0).T


def _val_tiles(z, tt):
    t = z.shape[2]
    return pl.pallas_call(
        functools.partial(_val_tiles_kernel, tt=tt),
        grid=(t // tt,),
        in_specs=[pl.BlockSpec((HEAD_DIM, BH_LANES, tt), lambda i: (0, 0, i))],
        out_specs=pl.BlockSpec((tt * HEAD_DIM, LANES), lambda i: (i, 0)),
        out_shape=jax.ShapeDtypeStruct((t * HEAD_DIM, LANES), F32),
        compiler_params=_cparams(("parallel",)),
        name="wkv_val_tiles",
    )(z)


def _val_untile_kernel(y_ref, o_ref, *, tt):
    for i in range(HEAD_DIM):
        o_ref[i] = y_ref[pl.ds(i, tt, stride=HEAD_DIM), :].T[0:BH_LANES]


def _val_untile(y, tt):
    t = y.shape[0] // HEAD_DIM
    return pl.pallas_call(
        functools.partial(_val_untile_kernel, tt=tt),
        grid=(t // tt,),
        in_specs=[pl.BlockSpec((tt * HEAD_DIM, LANES), lambda i: (i, 0))],
        out_specs=pl.BlockSpec((HEAD_DIM, BH_LANES, tt), lambda i: (0, 0, i)),
        out_shape=jax.ShapeDtypeStruct((HEAD_DIM, BH_LANES, t), F32),
        compiler_params=_cparams(("parallel",)),
        name="wkv_val_untile",
    )(y)


N_IO = HEAD_DIM // 8
N_JP = HEAD_DIM // 2
BH_LANES = 64


def _wkv_kernel(r_ref, w_ref, k_ref, v_ref, a_ref, b_ref, s0_ref, y_ref, sfin_ref, s_scr, *, tc):
    t_blk = pl.program_id(1)

    @pl.when(t_blk == 0)
    def _():
        s_scr[...] = s0_ref[0]

    def bc(ref, t, jp):
        return jnp.broadcast_to(ref[0, jp, pl.ds(t, 1), :], (8, LANES))

    def fold(x):
        return x + pltpu.roll(x, BH_LANES, 1)

    def state_times_a(t):
        acc = [jnp.zeros((8, LANES), F32) for _ in range(N_IO)]
        for jp in range(N_JP):
            a_ = bc(a_ref, t, jp)
            for io in range(N_IO):
                acc[io] = acc[io] + s_scr[io, jp] * a_
        return tuple(-fold(acc[io]) for io in range(N_IO))

    def step(t, sa):
        t_next = jnp.minimum(t + 1, tc - 1)
        vv = [v_ref[0, t, io * 8:(io + 1) * 8, :] for io in range(N_IO)]
        acc = [jnp.zeros((8, LANES), F32) for _ in range(N_IO)]
        yacc = [jnp.zeros((8, LANES), F32) for _ in range(N_IO)]
        for jp in range(N_JP):
            w_, b_, k_, r_, a_ = (bc(ref, tt, jp) for ref, tt in
                                  ((w_ref, t), (b_ref, t), (k_ref, t), (r_ref, t), (a_ref, t_next)))
            for io in range(N_IO):
                s = s_scr[io, jp] * w_ + sa[io] * b_ + vv[io] * k_
                s_scr[io, jp] = s
                yacc[io] = yacc[io] + s * r_
                acc[io] = acc[io] + s * a_
        sa_next = [-fold(acc[io]) for io in range(N_IO)]
        y = [fold(yacc[io]) for io in range(N_IO)]
        tot = y[0]
        for io in range(1, N_IO):
            tot = tot + y[io]
        mu = jnp.sum(tot, axis=0, keepdims=True) * (1.0 / HEAD_DIM)
        d = [y[io] - mu for io in range(N_IO)]
        sq = d[0] * d[0]
        for io in range(1, N_IO):
            sq = sq + d[io] * d[io]
        var = jnp.sum(sq, axis=0, keepdims=True) * (1.0 / HEAD_DIM)
        inv = lax.rsqrt(var + A_GN_EPS)
        for io in range(N_IO):
            y_ref[0, t, io * 8:(io + 1) * 8, :] = d[io] * inv
        return tuple(sa_next)

    lax.fori_loop(0, tc, step, state_times_a(0))

    @pl.when(t_blk == pl.num_programs(1) - 1)
    def _():
        sfin_ref[0] = s_scr[...]


def _wkv_scan(r, w, k, v, a, b, s0, tc):
    nb, t = v.shape[:2]
    kspec = pl.BlockSpec((1, N_JP, tc, LANES), lambda n, i: (n, 0, i, 0))
    vspec = pl.BlockSpec((1, tc, HEAD_DIM, LANES), lambda n, i: (n, i, 0, 0))
    sspec = pl.BlockSpec((1, N_IO, N_JP, 8, LANES), lambda n, i: (n, 0, 0, 0, 0))
    return pl.pallas_call(
        functools.partial(_wkv_kernel, tc=tc),
        grid=(nb, t // tc),
        in_specs=[kspec, kspec, kspec, vspec, kspec, kspec, sspec],
        out_specs=[vspec, sspec],
        out_shape=[jax.ShapeDtypeStruct((nb, t, HEAD_DIM, LANES), F32),
                   jax.ShapeDtypeStruct((nb, N_IO, N_JP, 8, LANES), F32)],
        scratch_shapes=[pltpu.VMEM((N_IO, N_JP, 8, LANES), F32)],
        compiler_params=_cparams(("parallel", "arbitrary")),
        name="wkv_scan",
    )(r, w, k, v, a, b, s0)


def _wkv_step_kernel(r_ref, w_ref, k_ref, v_ref, a_ref, b_ref, s_ref, y_ref, so_ref):
    rt, wt, kt, vt, at, bt = (ref[...].T for ref in (r_ref, w_ref, k_ref, v_ref, a_ref, b_ref))
    outs = []
    for hh in range(2):
        hs = slice(hh * HEAD_DIM, (hh + 1) * HEAD_DIM)
        r_, w_, k_, a_, b_ = rt[hs], wt[hs], kt[hs], at[hs], bt[hs]
        ys = []
        for i in range(HEAD_DIM):
            s = s_ref[hh, i]
            sa = -jnp.sum(s * a_, axis=0, keepdims=True)
            s = s * w_ + sa * b_ + vt[hh * HEAD_DIM + i:hh * HEAD_DIM + i + 1] * k_
            so_ref[hh, i] = s
            ys.append(jnp.sum(s * r_, axis=0, keepdims=True))
        y = jnp.concatenate(ys, axis=0)
        d = y - jnp.mean(y, axis=0, keepdims=True)
        outs.append(d * lax.rsqrt(jnp.mean(d * d, axis=0, keepdims=True) + A_GN_EPS))
    y_ref[...] = jnp.concatenate(outs, axis=0).T


def _wkv_step(r, w, k, v, a, b, state):
    s = r.shape[0]
    vec = pl.BlockSpec((s, 2 * HEAD_DIM), lambda h: (0, h))
    st = pl.BlockSpec((2, HEAD_DIM, HEAD_DIM, s), lambda h: (h, 0, 0, 0))
    return pl.pallas_call(
        _wkv_step_kernel,
        grid=(A_HEADS // 2,),
        in_specs=[vec] * 6 + [st],
        out_specs=[vec, st],
        out_shape=[jax.ShapeDtypeStruct((s, A_WIDTH), F32), jax.ShapeDtypeStruct(state.shape, F32)],
        compiler_params=_cparams(("parallel",)),
        name="wkv_step",
    )(r, w, k, v, a, b, state)


def _state_from_tiles(s, b):
    nb = b // 8
    s = s.reshape(nb, N_IO, N_JP, 8, 2, 8, A_HEADS).transpose(0, 5, 6, 1, 3, 2, 4)
    return s.reshape(b, A_HEADS, HEAD_DIM, HEAD_DIM)


def _compress_paged_kernel(pt_ref, pe_ref, w1_ref, w2_ref, cache_ref, o_ref, buf, xk, xv, sem, *, n_pages, page_len):
    i = pl.program_id(0)
    slot = i % 2

    def copy(s_idx, sl, p):
        return pltpu.make_async_copy(cache_ref.at[pt_ref[s_idx * n_pages + p]], buf.at[sl, p], sem.at[sl, p])

    def fetch(s_idx, sl):
        for p in range(n_pages):
            copy(s_idx, sl, p).start()

    @pl.when(i == 0)
    def _():
        fetch(0, 0)

    @pl.when(i + 1 < pl.num_programs(0))
    def _():
        fetch(i + 1, 1 - slot)

    for p in range(n_pages):
        copy(i, slot, p).wait()
    for p in range(n_pages):
        t = buf[slot, p].T
        for m in range(page_len // CMP_BLOCK):
            row = (p * (page_len // CMP_BLOCK) + m) * CMP_PITCH
            xk[row:row + CMP_BLOCK, :] = t[m * CMP_BLOCK:(m + 1) * CMP_BLOCK, :KV_WIDTH]
            xv[row:row + CMP_BLOCK, :] = t[m * CMP_BLOCK:(m + 1) * CMP_BLOCK, KV_WIDTH:]
    _compress_kernel(xk, xv, pe_ref, w1_ref, w2_ref, o_ref, nblk=n_pages * page_len // CMP_BLOCK, pitch=CMP_PITCH)


CMP_PITCH = CMP_BLOCK + 1


def _compress_kernel(xk_ref, xv_ref, pe_ref, w1_ref, w2_ref, o_ref, *, nblk, pitch=CMP_BLOCK):
    for c, x_ref in enumerate((xk_ref, xv_ref)):
        acc = jnp.zeros((nblk, B_KV_HEADS * CMP_HIDDEN), F32)
        for tau in range(CMP_BLOCK):
            x = x_ref[pl.ds(tau, nblk, stride=pitch), :] + pe_ref[c, tau:tau + 1, :]
            acc = acc + _dot(_bf(x), w1_ref[c, tau])
        o_ref[:, c * KV_WIDTH:(c + 1) * KV_WIDTH] = _dot(_bf(jax.nn.gelu(acc)), w2_ref[c])


def _compress_weights(pe_k, pe_v, w_k1, w_k2, w_v1, w_v2):
    pe = jnp.stack([jnp.concatenate([pe_k, pe_k], axis=1), jnp.concatenate([pe_v, pe_v], axis=1)]).astype(F32)
    eye = jnp.eye(B_KV_HEADS, dtype=F32)
    w1 = jnp.stack([w_k1, w_v1]).reshape(2, CMP_BLOCK, HEAD_DIM, CMP_HIDDEN)
    w1 = jnp.einsum('ctdh,ge->ctgdeh', w1, eye).reshape(2, CMP_BLOCK, KV_WIDTH, B_KV_HEADS * CMP_HIDDEN)
    w2 = jnp.einsum('chd,ge->cghed', jnp.stack([w_k2, w_v2]), eye).reshape(2, B_KV_HEADS * CMP_HIDDEN, KV_WIDTH)
    return pe, _bf(w1), _bf(w2)


COMPRESS_W_SPECS = ((2, CMP_BLOCK, KV_WIDTH), (2, CMP_BLOCK, KV_WIDTH, B_KV_HEADS * CMP_HIDDEN),
                    (2, B_KV_HEADS * CMP_HIDDEN, KV_WIDTH))


def _compress_paged(cache_t, page_table, cw):
    s, n_pages = page_table.shape
    page = cache_t.shape[2]
    past = n_pages * page
    nblk = past // CMP_BLOCK
    grid_spec = pltpu.PrefetchScalarGridSpec(
        num_scalar_prefetch=1,
        grid=(s,),
        in_specs=[pl.BlockSpec(shp, lambda i, pt, n=len(shp): (0,) * n) for shp in COMPRESS_W_SPECS]
                 + [pl.BlockSpec(memory_space=pl.ANY)],
        out_specs=pl.BlockSpec((nblk, KV_ROW), lambda i, pt: (i, 0)),
        scratch_shapes=[pltpu.VMEM((2, n_pages, KV_ROW, page), F32), pltpu.VMEM((nblk * CMP_PITCH, KV_WIDTH), F32),
                        pltpu.VMEM((nblk * CMP_PITCH, KV_WIDTH), F32), pltpu.SemaphoreType.DMA((2, n_pages))],
    )
    return pl.pallas_call(
        functools.partial(_compress_paged_kernel, n_pages=n_pages, page_len=page),
        grid_spec=grid_spec,
        out_shape=jax.ShapeDtypeStruct((s * nblk, KV_ROW), F32),
        compiler_params=_cparams(("arbitrary",)),
        name="nsa_compress_paged",
    )(page_table.reshape(-1), *_compress_weights(*cw), cache_t)


def _compress(x, pe_k, pe_v, w_k1, w_k2, w_v1, w_v2, rows):
    n = x.shape[0]
    nblk = rows // CMP_BLOCK
    pe, w1, w2 = _compress_weights(pe_k, pe_v, w_k1, w_k2, w_v1, w_v2)
    return pl.pallas_call(
        functools.partial(_compress_kernel, nblk=nblk),
        grid=(n // rows,),
        in_specs=[pl.BlockSpec((rows, KV_WIDTH), lambda i: (i, 0)), pl.BlockSpec((rows, KV_WIDTH), lambda i: (i, 1)),
                  ] + [_full(shp) for shp in COMPRESS_W_SPECS],
        out_specs=pl.BlockSpec((nblk, KV_ROW), lambda i: (i, 0)),
        out_shape=jax.ShapeDtypeStruct((n // CMP_BLOCK, KV_ROW), F32),
        compiler_params=_cparams(("parallel",)),
        name="nsa_compress",
    )(x, x, pe, w1, w2)


def _rel_bucket(dist):
    n = jnp.maximum(dist, 0)
    log_ratio = jnp.log(jnp.maximum(n, 1).astype(F32) / MAX_EXACT) / math.log(MAX_DISTANCE / MAX_EXACT)
    large = jnp.minimum(MAX_EXACT + (log_ratio * (N_BUCKETS - MAX_EXACT)).astype(I32), N_BUCKETS - 1)
    return jnp.where(n < MAX_EXACT, n, large)


def _bias_of(rel_bias, dist):
    onehot = (_rel_bucket(dist)[..., None] == jnp.arange(N_BUCKETS)).astype(F32)
    return jnp.einsum('...b,bh->h...', onehot, rel_bias.astype(F32), precision=HIGHEST)


KEY_TILE = 512


def _nsa_prompt_kernel(q_ref, gate_ref, ck_ref, ckt_ref, ksel_ref, vsel_ref, kwin_ref, vwin_ref, bct_ref, tz_ref,
                       o_ref, imp_scr, mask_scr, *, nc, ns, nq, top_n, wtiles):
    i = pl.program_id(1)
    qb = Q_BLOCK
    gates = jax.nn.sigmoid(gate_ref[0])
    pos_l = i * qb + lax.broadcasted_iota(I32, (1, qb), 1)
    okT = pos_l >= lax.broadcasted_iota(I32, (nc, qb), 0) * CMP_BLOCK + (CMP_BLOCK - 1)
    blk = lax.broadcasted_iota(I32, (ns, qb), 0)
    cur = pos_l // SEL_BLOCK
    forced = (blk == 0) | (blk == cur) | (blk == cur - 1)
    k_s = lax.broadcasted_iota(I32, (qb, qb), 0)
    q_l = lax.broadcasted_iota(I32, (qb, qb), 1)
    n_e = lax.broadcasted_iota(I32, (qb, ns), 1)
    k_e = lax.broadcasted_iota(I32, (qb, ns), 0) // SEL_BLOCK
    wk = (wtiles + 1) * qb
    wb = jnp.maximum(i - wtiles, 0)
    d_w = pos_l - (wb * qb + lax.broadcasted_iota(I32, (wk, qb), 0))
    madd_w = jnp.where((d_w >= 0) & (d_w < WINDOW), 0.0, NEG_INF)
    n_tiles = (i * qb + qb + KEY_TILE - 1) // KEY_TILE
    per = KEY_TILE // qb

    def bias_tiles(h, first_blk, count):
        return jnp.concatenate([tz_ref[h, jnp.clip(i - (first_blk + c), 0, nq - 1)] for c in range(count)], axis=0)

    for g in range(B_KV_HEADS):
        ksl = slice(g * HEAD_DIM, (g + 1) * HEAD_DIM)
        vsl = slice(KV_WIDTH + g * HEAD_DIM, KV_WIDTH + (g + 1) * HEAD_DIM)
        qg = [_bf(q_ref[0, (g * B_GROUP + r) * HEAD_DIM:(g * B_GROUP + r + 1) * HEAD_DIM, :] * SCALE)
              for r in range(B_GROUP)]
        kc = _bf(ck_ref[:, ksl])
        vct = _bf(ckt_ref[0, vsl, :])
        o_c = []
        impT = jnp.zeros((nc, qb), F32)
        for r in range(B_GROUP):
            sT = jnp.where(okT, _dot(kc, qg[r]) + bct_ref[g * B_GROUP + r], NEG_INF)
            eT = jnp.exp(sT - jnp.max(sT, axis=0, keepdims=True))
            pT = eT / jnp.sum(eT, axis=0, keepdims=True) * okT.astype(F32)
            impT = impT + pT
            o_c.append(_dot(vct, _bf(pT)))
        imp_scr[...] = impT
        imp2 = imp_scr[pl.ds(0, ns, stride=SEL_RATIO), :] + imp_scr[pl.ds(1, ns, stride=SEL_RATIO), :]
        score = jnp.where(blk <= cur, imp2 + FORCE_SCORE * forced.astype(F32), NEG_INF)
        rank = jnp.zeros((ns, qb), I32)
        for m in range(ns):
            row = score[m:m + 1, :]
            rank = rank + ((row > score) | ((row == score) & (m < blk))).astype(I32)
        selT = _bf((rank < top_n) & (score > NEG_INF / 2))
        mask_scr[...] = jnp.full(mask_scr.shape, NEG_INF, F32)

        def fill(j, carry):
            m = _dot(_bf(n_e == 2 * j + k_e), selT) > 0.5
            m = m & ((j < i) | (q_l >= k_s))
            mask_scr[pl.ds(pl.multiple_of(j * qb, qb), qb), :] = jnp.where(m, 0.0, NEG_INF)
            return carry

        lax.fori_loop(0, i + 1, fill, 0)

        def body(jt, carry):
            ms, ls, accs = carry
            k0 = pl.multiple_of(jt * KEY_TILE, KEY_TILE)
            kt = _bf(ksel_ref[pl.ds(k0, KEY_TILE), ksl])
            vt = _bf(vsel_ref[0, vsl, pl.ds(k0, KEY_TILE)])
            madd = mask_scr[pl.ds(k0, KEY_TILE), :]
            ms2, ls2, accs2 = [], [], []
            for r in range(B_GROUP):
                s = _dot(kt, qg[r]) + bias_tiles(g * B_GROUP + r, jt * per, per) + madd
                m_new = jnp.maximum(ms[r], jnp.max(s, axis=0, keepdims=True))
                alpha = jnp.exp(ms[r] - m_new)
                p = jnp.exp(s - m_new)
                ls2.append(alpha * ls[r] + jnp.sum(p, axis=0, keepdims=True))
                accs2.append(alpha * accs[r] + _dot(vt, _bf(p)))
                ms2.append(m_new)
            return tuple(ms2), tuple(ls2), tuple(accs2)

        init = (tuple(jnp.full((1, qb), NEG_INF, F32) for _ in range(B_GROUP)),
                tuple(jnp.zeros((1, qb), F32) for _ in range(B_GROUP)),
                tuple(jnp.zeros((HEAD_DIM, qb), F32) for _ in range(B_GROUP)))
        _, ls, accs = lax.fori_loop(0, n_tiles, body, init)
        w0 = pl.multiple_of(wb * qb, qb)
        ktw = _bf(kwin_ref[pl.ds(w0, wk), ksl])
        vtw = _bf(vwin_ref[0, vsl, pl.ds(w0, wk)])
        for r in range(B_GROUP):
            h = g * B_GROUP + r
            s = _dot(ktw, qg[r]) + bias_tiles(h, wb, wtiles + 1) + madd_w
            p = jnp.exp(s - jnp.max(s, axis=0, keepdims=True))
            o_w = _dot(vtw, _bf(p)) / jnp.sum(p, axis=0, keepdims=True)
            o_s = accs[r] / ls[r]
            o_ref[0, h * HEAD_DIM:(h + 1) * HEAD_DIM, :] = (
                gates[3 * h:3 * h + 1] * o_c[r] + gates[3 * h + 1:3 * h + 2] * o_s + gates[3 * h + 2:3 * h + 3] * o_w)


def _nsa_prompt(q_t, gate_t, ckv, kv_sel, kvt_sel, kv_win, kvt_win, rel_bias):
    b, _, t = q_t.shape
    nq = t // Q_BLOCK
    nc = t // CMP_BLOCK
    ns = -(-t // SEL_BLOCK)
    wtiles = WINDOW // Q_BLOCK
    assert t % KEY_TILE == 0 and nc == SEL_RATIO * ns and WINDOW % Q_BLOCK == 0 and (wtiles + 1) * Q_BLOCK <= t
    cmp_end = jnp.arange(nc) * CMP_BLOCK + (CMP_BLOCK - 1)
    bct = _bias_of(rel_bias, jnp.arange(t)[None, :] - cmp_end[:, None])
    dz = (jnp.arange(nq)[:, None, None] * Q_BLOCK + jnp.arange(Q_BLOCK)[None, None, :]
          - jnp.arange(Q_BLOCK)[None, :, None])
    tz = _bias_of(rel_bias, dz)
    ckt = jnp.swapaxes(ckv.reshape(b, nc, KV_ROW), 1, 2)
    chan = lambda w: pl.BlockSpec((1, w, Q_BLOCK), lambda bi, i: (bi, 0, i))
    seq_rows = pl.BlockSpec((t, KV_ROW), lambda bi, i: (bi, 0))
    seq_chan = pl.BlockSpec((1, KV_ROW, t), lambda bi, i: (bi, 0, 0))
    return pl.pallas_call(
        functools.partial(_nsa_prompt_kernel, nc=nc, ns=ns, nq=nq, top_n=min(TOP_N, ns), wtiles=wtiles),
        grid=(b, nq),
        in_specs=[chan(B_WIDTH), chan(GATE_PAD),
                  pl.BlockSpec((nc, KV_ROW), lambda bi, i: (bi, 0)),
                  pl.BlockSpec((1, KV_ROW, nc), lambda bi, i: (bi, 0, 0)),
                  seq_rows, seq_chan, seq_rows, seq_chan,
                  pl.BlockSpec((B_HEADS, nc, Q_BLOCK), lambda bi, i: (0, 0, i)),
                  _full((B_HEADS, nq, Q_BLOCK, Q_BLOCK))],
        out_specs=chan(B_WIDTH),
        out_shape=jax.ShapeDtypeStruct((b, B_WIDTH, t), F32),
        scratch_shapes=[pltpu.VMEM((nc, Q_BLOCK), F32), pltpu.VMEM((t, Q_BLOCK), F32)],
        compiler_params=_cparams(("parallel", "arbitrary")),
        name="nsa_prompt",
    )(q_t, gate_t, ckv, ckt, kv_sel, kvt_sel, kv_win, kvt_win, bct, tz)


def _group_q(q_ref, g):
    rows = [q_ref[0, :, (g * B_GROUP + r) * HEAD_DIM:(g * B_GROUP + r + 1) * HEAD_DIM] for r in range(B_GROUP)]
    return _bf(jnp.concatenate(rows, axis=0) * SCALE)


def _nsa_sample_select_kernel(q_ref, gate_ref, ck_ref, bc_ref, pair_ref, tri_ref, oc_ref, idx_ref,
                              *, nsp, cur, top_n):
    gates = jax.nn.sigmoid(gate_ref[0])
    blk = lax.broadcasted_iota(I32, (1, nsp), 1)
    forced = (blk == 0) | (blk == cur) | (blk == cur - 1)
    mi = lax.broadcasted_iota(I32, (nsp, nsp), 0)
    ni = lax.broadcasted_iota(I32, (nsp, nsp), 1)
    kk = lax.broadcasted_iota(I32, (TOP_N, nsp), 0).astype(F32)
    nf = lax.broadcasted_iota(I32, (TOP_N, nsp), 1).astype(F32)
    outs = []
    for g in range(B_KV_HEADS):
        q4 = _group_q(q_ref, g)
        kc = _bf(ck_ref[:, g * HEAD_DIM:(g + 1) * HEAD_DIM])
        vc = _bf(ck_ref[:, KV_WIDTH + g * HEAD_DIM:KV_WIDTH + (g + 1) * HEAD_DIM])
        s = _dot_nt(q4, kc) + bc_ref[g * B_GROUP:(g + 1) * B_GROUP, :]
        e = jnp.exp(s - jnp.max(s, axis=-1, keepdims=True))
        p = e / jnp.sum(e, axis=-1, keepdims=True)
        o_c = _dot(_bf(p), vc)
        imp = ((p[0:1] + p[1:2]) + p[2:3]) + p[3:4]
        imp2 = jnp.dot(imp, pair_ref[...], precision=HIGHEST, preferred_element_type=F32)
        score = jnp.where(blk <= cur, imp2 + FORCE_SCORE * forced.astype(F32), NEG_INF)
        m1 = jnp.broadcast_to(score, (nsp, nsp))
        m2 = m1.T
        gt = (m2 > m1) | ((m2 == m1) & (mi < ni))
        rank = jnp.sum(gt.astype(F32), axis=0, keepdims=True)
        sel = (rank < top_n) & (score > NEG_INF / 2)
        before = _dot(_bf(sel), tri_ref[...])
        hit = jnp.broadcast_to(sel, (TOP_N, nsp)) & (jnp.broadcast_to(before, (TOP_N, nsp)) == kk)
        idx = jnp.sum(jnp.where(hit, nf, 0.0), axis=1, keepdims=True)
        cnt = jnp.sum(hit.astype(F32), axis=1, keepdims=True)
        idx_ref[0, g] = jnp.where(cnt > 0.5, idx, -1.0).astype(I32)
        for r in range(B_GROUP):
            h = g * B_GROUP + r
            outs.append(gates[:, 3 * h:3 * h + 1] * o_c[r:r + 1])
    oc_ref[0] = jnp.concatenate(outs, axis=1)


def _nsa_sample_select(q3, gate3, ckv, rel_bias, past_len):
    s = q3.shape[0]
    nc = past_len // CMP_BLOCK
    ns = -(-(past_len + 1) // SEL_BLOCK)
    nsp = -(-ns // LANES) * LANES
    cur = past_len // SEL_BLOCK
    bc = _bias_of(rel_bias, past_len - (jnp.arange(nc) * CMP_BLOCK + (CMP_BLOCK - 1)))
    pair = jnp.asarray(np.arange(nc)[:, None] // SEL_RATIO == np.arange(nsp)[None, :], F32)
    tri = jnp.asarray(np.arange(nsp)[:, None] < np.arange(nsp)[None, :], BF16)
    return pl.pallas_call(
        functools.partial(_nsa_sample_select_kernel, nsp=nsp, cur=cur, top_n=min(TOP_N, ns)),
        grid=(s,),
        in_specs=[pl.BlockSpec((1, 1, B_WIDTH), lambda i: (i, 0, 0)), pl.BlockSpec((1, 1, GATE_PAD), lambda i: (i, 0, 0)),
                  pl.BlockSpec((nc, KV_ROW), lambda i: (i, 0)), _full((B_HEADS, nc)), _full((nc, nsp)), _full((nsp, nsp))],
        out_specs=[pl.BlockSpec((1, 1, B_WIDTH), lambda i: (i, 0, 0)),
                   pl.BlockSpec((1, B_KV_HEADS, TOP_N, 1), lambda i: (i, 0, 0, 0))],
        out_shape=[jax.ShapeDtypeStruct((s, 1, B_WIDTH), F32), jax.ShapeDtypeStruct((s, B_KV_HEADS, TOP_N, 1), I32)],
        compiler_params=_cparams(("parallel",)),
        name="nsa_sample_select",
    )(q3, gate3, ckv, bc, pair, tri)


def _nsa_sample_attend_kernel(idx_ref, pt_ref, q_ref, gate_ref, oc_ref, ksel_ref, kwin_ref, win_ref, fpg_ref, bw_ref,
                              cache_ref, o_ref, buf, sem, *, cur, n_pages, past_len, page_len):
    i = pl.program_id(0)
    n_s = pl.num_programs(0)
    slot = i % 2
    halves = PAGE_BLOCKS
    n_slot = B_KV_HEADS * TOP_N

    def block_of(s_idx, j):
        return idx_ref[s_idx * n_slot + j]

    def copy(s_idx, sl, j):
        n = jnp.clip(block_of(s_idx, j), 0, cur - 1)
        page = pt_ref[s_idx * n_pages + n // halves]
        return pltpu.make_async_copy(cache_ref.at[page], buf.at[sl, j], sem.at[sl])

    def cached(s_idx, j):
        n = block_of(s_idx, j)
        return (n >= 0) & (n < cur)

    def fetch(s_idx, sl):
        for j in range(n_slot):
            @pl.when(cached(s_idx, j))
            def _():
                copy(s_idx, sl, j).start()

    @pl.when(i == 0)
    def _():
        fetch(0, 0)

    @pl.when(i + 1 < n_s)
    def _():
        fetch(i + 1, 1 - slot)

    for j in range(n_slot):
        @pl.when(cached(i, j))
        def _():
            copy(i, slot, j).wait()

        @pl.when(jnp.logical_not(cached(i, j)))
        def _():
            buf[slot, j] = jnp.zeros((KV_ROW, page_len), F32)
            buf[slot, j, :, 0:1] = ksel_ref[0]

    gates = jax.nn.sigmoid(gate_ref[0])
    t_l = lax.broadcasted_iota(I32, (1, page_len), 1)
    n_buf = win_ref.shape[2]
    j_w = lax.broadcasted_iota(I32, (1, n_buf), 1)
    d_w = n_buf - j_w
    ok_w = (d_w >= 0) & (d_w < WINDOW) & (past_len - d_w >= 0)
    f0 = fpg_ref[n_pages]
    outs = []
    for g in range(B_KV_HEADS):
        q4 = _group_q(q_ref, g)
        ksl = slice(g * HEAD_DIM, (g + 1) * HEAD_DIM)
        vsl = slice(KV_WIDTH + g * HEAD_DIM, KV_WIDTH + (g + 1) * HEAD_DIM)
        pieces = []
        for k in range(TOP_N):
            j = g * TOP_N + k
            n = block_of(i, j)
            pg = jnp.clip(n, 0, cur) // halves
            fb = fpg_ref[pg][g * B_GROUP:(g + 1) * B_GROUP, :]
            s_k = _dot(q4, _bf(buf[slot, j, ksl, :])) + fb
            ok = (n >= 0) & (t_l // SEL_BLOCK == n % halves) & (pg * page_len + t_l <= past_len)
            pieces.append(jnp.where(ok, s_k, NEG_INF))
        s = jnp.concatenate(pieces, axis=1)
        e = jnp.exp(s - jnp.max(s, axis=-1, keepdims=True))
        p = _bf(e / jnp.sum(e, axis=-1, keepdims=True))
        o_s = jnp.zeros((B_GROUP, HEAD_DIM), F32)
        for k in range(TOP_N):
            o_s = o_s + _dot_nt(p[:, k * page_len:(k + 1) * page_len], _bf(buf[slot, g * TOP_N + k, vsl, :]))
        kw = _bf(win_ref[0, ksl, :])
        vw = _bf(win_ref[0, vsl, :])
        s_w = jnp.where(ok_w, _dot(q4, kw) + bw_ref[g * B_GROUP:(g + 1) * B_GROUP, :], NEG_INF)
        k_new = _bf(kwin_ref[0, :, ksl])
        v_new = _bf(kwin_ref[0, :, vsl])
        s_n = (jnp.sum(q4.astype(F32) * k_new.astype(F32), axis=-1, keepdims=True)
               + f0[g * B_GROUP:(g + 1) * B_GROUP, 0:1])
        m = jnp.maximum(jnp.max(s_w, axis=-1, keepdims=True), s_n)
        e_w = jnp.exp(s_w - m)
        e_n = jnp.exp(s_n - m)
        den = jnp.sum(e_w, axis=-1, keepdims=True) + e_n
        o_w = _dot_nt(_bf(e_w / den), vw) + _bf(e_n / den).astype(F32) * v_new.astype(F32)
        for r in range(B_GROUP):
            h = g * B_GROUP + r
            outs.append(gates[:, 3 * h + 1:3 * h + 2] * o_s[r:r + 1] + gates[:, 3 * h + 2:3 * h + 3] * o_w[r:r + 1])
    o_ref[0] = oc_ref[0] + jnp.concatenate(outs, axis=1)


PAGE_BLOCKS = 2


def _nsa_sample_attend(idx, page_table, q3, gate3, oc3, ksel_t, kwin3, win_t, cache_t, rel_bias, past_len):
    s = q3.shape[0]
    n_pages = page_table.shape[1]
    cur = past_len // SEL_BLOCK
    n_buf = win_t.shape[2]
    page = cache_t.shape[2]
    assert page == PAGE_BLOCKS * SEL_BLOCK and past_len == n_pages * page
    keypos = jnp.arange(n_pages + 1)[:, None] * page + jnp.arange(page)[None, :]
    fpg = jnp.moveaxis(_bias_of(rel_bias, past_len - keypos), 0, 1)
    bw = _bias_of(rel_bias, n_buf - jnp.arange(n_buf))
    row3 = lambda w: pl.BlockSpec((1, 1, w), lambda i, *_: (i, 0, 0))
    grid_spec = pltpu.PrefetchScalarGridSpec(
        num_scalar_prefetch=2,
        grid=(s,),
        in_specs=[row3(B_WIDTH), row3(GATE_PAD), row3(B_WIDTH),
                  pl.BlockSpec((1, KV_ROW, 1), lambda i, *_: (i, 0, 0)), row3(KV_ROW),
                  pl.BlockSpec((1, KV_ROW, n_buf), lambda i, *_: (i, 0, 0)),
                  pl.BlockSpec((n_pages + 1, B_HEADS, page), lambda i, *_: (0, 0, 0)),
                  pl.BlockSpec((B_HEADS, n_buf), lambda i, *_: (0, 0)),
                  pl.BlockSpec(memory_space=pl.ANY)],
        out_specs=row3(B_WIDTH),
        scratch_shapes=[pltpu.VMEM((2, B_KV_HEADS * TOP_N, KV_ROW, page), F32), pltpu.SemaphoreType.DMA((2,))],
    )
    return pl.pallas_call(
        functools.partial(_nsa_sample_attend_kernel, cur=cur, n_pages=n_pages, past_len=past_len, page_len=page),
        grid_spec=grid_spec,
        out_shape=jax.ShapeDtypeStruct((s, 1, B_WIDTH), F32),
        compiler_params=_cparams(("arbitrary",)),
        name="nsa_sample_attend",
    )(idx.reshape(-1), page_table.reshape(-1), q3, gate3, oc3, ksel_t, kwin3, win_t, fpg, bw, cache_t)


def _merge_kernel(x_ref, yn_ref, bo_ref, ga_ref, yb_ref, pg_ref, lnw_ref, lnb_ref, wa_ref, wb_ref, wo_ref, nf_ref,
                  wr_ref, br_ref, x1_ref, h_ref, te_ref, tw_ref, *, a_transposed):
    if a_transposed:
        tm = x_ref.shape[0]
        yn = yn_ref[...].reshape(A_WIDTH, tm)
        y_a = ((yn * lnw_ref[...] + lnb_ref[...] + bo_ref[0]) * ga_ref[0]).T
        y_b = yb_ref[0].T
    else:
        y_a = (yn_ref[...] * lnw_ref[...] + lnb_ref[...] + bo_ref[...]) * ga_ref[...]
        y_b = yb_ref[...]
    g_a = jax.nn.sigmoid(pg_ref[:, :D_MODEL])
    g_b = jax.nn.sigmoid(pg_ref[:, D_MODEL:])
    m = g_a * _dot(_bf(y_a), wa_ref[...]) + g_b * _dot(_bf(y_b), wb_ref[...])
    x1 = x_ref[...] + _dot(_bf(m), wo_ref[...])
    x1_ref[...] = x1
    h = x1 * lax.rsqrt(jnp.mean(x1 * x1, axis=-1, keepdims=True) + NORM_EPS) * nf_ref[...]
    _store_rows(h_ref, h)
    logits = jnp.dot(h, wr_ref[...], precision=HIGHEST, preferred_element_type=F32) + br_ref[...]
    lane = lax.broadcasted_iota(I32, logits.shape, 1).astype(F32)
    col = lax.broadcasted_iota(I32, (logits.shape[0], TOP_K), 1)
    vals, idxs = [], []
    for _ in range(TOP_K):
        m_k = jnp.max(logits, axis=-1, keepdims=True)
        i_k = jnp.min(jnp.where(logits == m_k, lane, float(N_EXPERTS)), axis=-1, keepdims=True)
        vals.append(m_k)
        idxs.append(i_k)
        logits = jnp.where(lane == i_k, -jnp.inf, logits)
    e = [jnp.exp(v - vals[0]) for v in vals]
    den = ((e[0] + e[1]) + e[2]) + e[3]
    te = jnp.zeros(col.shape, F32)
    tw = jnp.zeros(col.shape, F32)
    for k in range(TOP_K):
        te = jnp.where(col == k, idxs[k], te)
        tw = jnp.where(col == k, e[k] / den, tw)
    te_ref[...] = te.astype(I32)
    tw_ref[...] = tw


def _merge(x, yn, bonus, gate, yb, pg, ln_w, ln_b, w_a, w_b, w_o, norm_ffn, w_router, b_router, tm, seq=None):
    rows = x.shape[0]
    row = lambda w: pl.BlockSpec((tm, w), lambda i: (i, 0))
    vec = lambda z: z.reshape(1, -1).astype(F32)
    if seq is None:
        a_specs = [row(A_WIDTH)] * 3 + [row(B_WIDTH)] + [_full((1, A_WIDTH))] * 2
        lnw, lnb = vec(ln_w), vec(ln_b)
    else:
        per = seq // tm
        nspec = pl.BlockSpec((1, A_WIDTH, tm), lambda i: (i // per, 0, i % per))
        a_specs = [pl.BlockSpec((HEAD_DIM, 1, A_HEADS, tm), lambda i: (0, i // per, 0, i % per)), nspec, nspec, nspec,
                   _full((A_WIDTH, 1)), _full((A_WIDTH, 1))]
        lnw, lnb = (z[CH_MAJOR].reshape(-1, 1).astype(F32) for z in (ln_w, ln_b))
        w_a = w_a[CH_MAJOR]
    return pl.pallas_call(
        functools.partial(_merge_kernel, a_transposed=seq is not None),
        grid=(rows // tm,),
        in_specs=[row(D_MODEL)] + a_specs[:4] + [row(GATE_PROJ)] + a_specs[4:]
                 + [_full((A_WIDTH, D_MODEL)), _full((B_WIDTH, D_MODEL)),
                  _full((D_MODEL, D_MODEL)), _full((1, D_MODEL)), _full((D_MODEL, N_EXPERTS)), _full((1, N_EXPERTS))],
        out_specs=[row(D_MODEL), pl.BlockSpec((tm * ROW_TILE, LANES), lambda i: (i, 0)), row(TOP_K), row(TOP_K)],
        out_shape=[jax.ShapeDtypeStruct((rows, D_MODEL), F32), jax.ShapeDtypeStruct((rows * ROW_TILE, LANES), F32),
                   jax.ShapeDtypeStruct((rows, TOP_K), I32), jax.ShapeDtypeStruct((rows, TOP_K), F32)],
        compiler_params=_cparams(("parallel",)),
        name="merge_router",
    )(x, yn, bonus, gate, yb, pg, lnw, lnb, _bf(w_a), _bf(w_b), _bf(w_o), vec(norm_ffn),
      w_router.astype(F32), vec(b_router))


ROW_TILE = D_MODEL // LANES


def _load_rows(ref, n, first=0, every=1):
    return jnp.concatenate([ref[pl.ds(first * ROW_TILE + s, n, stride=every * ROW_TILE), :] for s in range(ROW_TILE)],
                           axis=1)


def _store_rows(ref, x):
    for s in range(ROW_TILE):
        ref[pl.ds(s, x.shape[0], stride=ROW_TILE), :] = x[:, s * LANES:(s + 1) * LANES]


def _dispatch_kernel(dest_ref, h_ref, xs_in_ref, xs_ref, sem, *, tm):
    del xs_in_ref
    for t in range(tm):
        for k in range(TOP_K):
            pltpu.make_async_copy(h_ref.at[t], xs_ref.at[dest_ref[0, 0, t * TOP_K + k]], sem).start()
    for t in range(tm * TOP_K):
        pltpu.make_async_copy(h_ref.at[0], xs_ref.at[0], sem).wait()


def _dispatch(dest, h_tiles, slots, tm):
    n = h_tiles.shape[0]
    return pl.pallas_call(
        functools.partial(_dispatch_kernel, tm=tm),
        grid=(n // tm,),
        in_specs=[pl.BlockSpec((1, 1, tm * TOP_K), lambda i: (i, 0, 0), memory_space=pltpu.SMEM),
                  pl.BlockSpec((tm, ROW_TILE, LANES), lambda i: (i, 0, 0)), pl.BlockSpec(memory_space=pl.ANY)],
        out_specs=pl.BlockSpec(memory_space=pl.ANY),
        out_shape=jax.ShapeDtypeStruct((slots, ROW_TILE, LANES), F32),
        scratch_shapes=[pltpu.SemaphoreType.DMA(())],
        input_output_aliases={2: 0},
        compiler_params=_cparams(("arbitrary",)),
        name="moe_dispatch",
    )(dest.reshape(n // tm, 1, tm * TOP_K), h_tiles, jnp.zeros((slots, ROW_TILE, LANES), F32))


def _moe_kernel(ce_ref, nu_ref, x_ref, wu_ref, bu_ref, wd_ref, bd_ref, o_ref, wu_bf, wd_bf):
    c = pl.program_id(0)
    e = ce_ref[c]
    prev = ce_ref[jnp.maximum(c - 1, 0)]

    @pl.when((c == 0) | (e != prev))
    def _():
        wu_bf[...] = _bf(wu_ref[0])
        wd_bf[...] = _bf(wd_ref[0])

    @pl.when(c < nu_ref[0])
    def _():
        u = _dot(_bf(_load_rows(x_ref, MOE_ROWS)), wu_bf[...]) + bu_ref[0]
        glu = jnp.minimum(u[:, :D_FF], SWIGLU_LIMIT)
        lin = jnp.clip(u[:, D_FF:], -SWIGLU_LIMIT, SWIGLU_LIMIT)
        act = glu * jax.nn.sigmoid(SWIGLU_ALPHA * glu) * (lin + 1.0)
        _store_rows(o_ref, _dot(_bf(act), wd_bf[...]) + bd_ref[0])

    @pl.when(c >= nu_ref[0])
    def _():
        o_ref[...] = jnp.zeros_like(o_ref)


def _moe_experts(chunk_e, n_used, xs, w_up, b_up, w_down, b_down):
    n_chunks = xs.shape[0] // (MOE_ROWS * ROW_TILE)
    rows = pl.BlockSpec((MOE_ROWS * ROW_TILE, LANES), lambda c, ce, nu: (c, 0))
    grid_spec = pltpu.PrefetchScalarGridSpec(
        num_scalar_prefetch=2,
        grid=(n_chunks,),
        in_specs=[rows,
                  pl.BlockSpec((1, D_MODEL, 2 * D_FF), lambda c, ce, nu: (ce[c], 0, 0)),
                  pl.BlockSpec((1, 1, 2 * D_FF), lambda c, ce, nu: (ce[c], 0, 0)),
                  pl.BlockSpec((1, D_FF, D_MODEL), lambda c, ce, nu: (ce[c], 0, 0)),
                  pl.BlockSpec((1, 1, D_MODEL), lambda c, ce, nu: (ce[c], 0, 0))],
        out_specs=rows,
        scratch_shapes=[pltpu.VMEM((D_MODEL, 2 * D_FF), BF16), pltpu.VMEM((D_FF, D_MODEL), BF16)],
    )
    return pl.pallas_call(
        _moe_kernel,
        grid_spec=grid_spec,
        out_shape=jax.ShapeDtypeStruct(xs.shape, F32),
        compiler_params=_cparams(("arbitrary",)),
        name="moe_experts",
    )(chunk_e, n_used, xs, w_up, b_up.reshape(N_EXPERTS, 1, -1), w_down, b_down.reshape(N_EXPERTS, 1, -1))


def _combine_kernel(dcur_ref, dnext_ref, x_ref, w_ref, g_ref, ys_ref, o_ref, buf, sem, *, tm, normalize):
    i = pl.program_id(0)
    slot = i % 2
    n_a = tm * TOP_K

    def copy(d_ref, sl, a):
        return pltpu.make_async_copy(ys_ref.at[d_ref[0, 0, a]], buf.at[sl, pl.ds(a * ROW_TILE, ROW_TILE)], sem.at[sl])

    def fetch(d_ref, sl):
        for a in range(n_a):
            copy(d_ref, sl, a).start()

    @pl.when(i == 0)
    def _():
        fetch(dcur_ref, 0)

    @pl.when(i + 1 < pl.num_programs(0))
    def _():
        fetch(dnext_ref, 1 - slot)

    for a in range(n_a):
        copy(dcur_ref, slot, a).wait()
    x = x_ref[...]
    for k in range(TOP_K):
        x = x + w_ref[:, k:k + 1] * _load_rows(buf.at[slot], tm, first=k, every=TOP_K)
    if normalize:
        x = x * lax.rsqrt(jnp.mean(x * x, axis=-1, keepdims=True) + NORM_EPS) * g_ref[...]
    o_ref[...] = x


def _combine(dest, x1, top_w, ys_tiles, g, tm, normalize):
    n = x1.shape[0]
    steps = n // tm
    d2 = dest.reshape(steps, 1, tm * TOP_K)
    row = lambda w: pl.BlockSpec((tm, w), lambda i: (i, 0))
    return pl.pallas_call(
        functools.partial(_combine_kernel, tm=tm, normalize=normalize),
        grid=(steps,),
        in_specs=[pl.BlockSpec((1, 1, tm * TOP_K), lambda i: (i, 0, 0), memory_space=pltpu.SMEM),
                  pl.BlockSpec((1, 1, tm * TOP_K), lambda i: (jnp.minimum(i + 1, steps - 1), 0, 0),
                               memory_space=pltpu.SMEM),
                  row(D_MODEL), row(TOP_K), _full((1, D_MODEL)), pl.BlockSpec(memory_space=pl.ANY)],
        out_specs=row(D_MODEL),
        out_shape=jax.ShapeDtypeStruct((n, D_MODEL), F32),
        scratch_shapes=[pltpu.VMEM((2, tm * TOP_K * ROW_TILE, LANES), F32), pltpu.SemaphoreType.DMA((2,))],
        compiler_params=_cparams(("arbitrary",)),
        name="moe_combine",
    )(d2, d2, x1, top_w, g.reshape(1, -1).astype(F32), ys_tiles)


def _moe_residual(x1, h_tiles, top_e, top_w, w_up, b_up, w_down, b_down, g, normalize):
    n = x1.shape[0]
    n_assign = n * TOP_K
    onehot = (top_e.reshape(-1, 1) == jnp.arange(N_EXPERTS, dtype=I32)[None, :]).astype(I32)
    csum = jnp.cumsum(onehot, axis=0)
    counts = csum[-1]
    padded = (counts + MOE_ROWS - 1) // MOE_ROWS * MOE_ROWS
    pad_end = jnp.cumsum(padded)
    dest = jnp.sum(onehot * (csum - 1 + (pad_end - padded)[None, :]), axis=1).astype(I32)
    n_chunks = -(-(n_assign + N_EXPERTS * (MOE_ROWS - 1)) // MOE_ROWS)
    slots = n_chunks * MOE_ROWS
    chunk_e = jnp.sum(pad_end[None, :] <= (jnp.arange(n_chunks) * MOE_ROWS)[:, None], axis=1)
    chunk_e = jnp.minimum(chunk_e, N_EXPERTS - 1).astype(I32)
    n_used = (pad_end[-1] // MOE_ROWS).astype(I32).reshape(1)
    tm = _row_tile(n, 128)
    xs = _dispatch(dest, h_tiles.reshape(n, ROW_TILE, LANES), slots, tm)
    ys = _moe_experts(chunk_e, n_used, xs.reshape(slots * ROW_TILE, LANES), w_up, b_up, w_down, b_down)
    return _combine(dest, x1, top_w, ys.reshape(slots, ROW_TILE, LANES), g, tm, normalize)


def _row_tile(rows, cap):
    tm = cap
    while rows % tm:
        tm //= 2
    return tm


def kernel(x_prompt, x_sample, cache_cmp_kv, cache_sel_kv, state_win_kv, state_rwkv, state_rwkv_shift, page_table,
           norm_attn, w_in, mu_shift, w0, w_decay_up, a0, w_iclr_up, w_gate_up, k_k, k_a, r_k, ln_x_w, ln_x_b,
           pe_cmp_k, pe_cmp_v, w_cmp_k1, w_cmp_k2, w_cmp_v1, w_cmp_v2, rel_bias, w_br_a, w_br_b, w_out,
           norm_ffn, w_router, b_router, w_up, b_up, w_down, b_down, norm_final):
    bp, tp, _ = x_prompt.shape
    bs, ts, _ = x_sample.shape
    depth = w_in.shape[0]
    past_len = page_table.shape[1] * cache_cmp_kv.shape[2]
    n_buf = state_win_kv.shape[2]
    assert ts == 1 and bp * A_HEADS == BH_LANES and bs % 8 == 0 and tp % LANES == 0
    xp = x_prompt.reshape(bp * tp, D_MODEL)
    xs = x_sample.reshape(bs, D_MODEL)
    new = {name: [] for name in ('cmp_p', 'sel_p', 'win_p', 'wkv_p', 'shift_p', 'cmp_s', 'sel_s', 'win_s', 'wkv_s', 'shift_s')}
    kv6 = lambda z, b, t: z.reshape(b, t, 2, B_KV_HEADS, HEAD_DIM)
    for l in range(depth):
        rw = (mu_shift[l], w0[l], w_decay_up[l], a0[l], w_iclr_up[l], w_gate_up[l], k_k[l], k_a[l], r_k[l])
        cw = (pe_cmp_k[l], pe_cmp_v[l], w_cmp_k1[l], w_cmp_k2[l], w_cmp_v1[l], w_cmp_v2[l])
        wa_nat, w_rest, w_t, w_n = _pack_w_in(w_in[l])
        g_attn = norm_attn[l].reshape(1, -1).astype(F32)
        last = l == depth - 1
        g_fin = norm_final if last else jnp.ones((D_MODEL,), F32)

        tq = _row_tile(tp, 256)
        pa_t, q_t, kvt_c, kvt_s, kvt_w, gt_t, kv_c, kv_s, kv_w, pg = _project_t(xp, g_attn, w_t, w_n, tq, tp)
        r, w, k2, v, kk, kb, gate, bonus = _rwkv_prep_t(pa_t, *rw, tq)
        r, w, k2, kk, kb = _key_tiles([z.reshape(N_JP, LANES, tp) for z in (r, w, k2, kk, kb)], LANES)
        v = _val_tiles(v.reshape(HEAD_DIM, BH_LANES, tp), LANES).reshape(1, tp, HEAD_DIM, LANES)
        s0 = jnp.zeros((1, N_IO, N_JP, 8, LANES), F32)
        yn, s_fin = _wkv_scan(r, w, k2, v, kk, kb, s0, _row_tile(tp, 64))
        yn = _val_untile(yn.reshape(tp * HEAD_DIM, LANES), LANES).reshape(HEAD_DIM, bp, A_HEADS, tp)
        ckv = _compress(kv_c, *cw, rows=_row_tile(tp, 2048))
        y_b = _nsa_prompt(q_t, gt_t, ckv, kv_s, kvt_s, kv_w, kvt_w, rel_bias)
        x1, h, top_e, top_w = _merge(xp, yn, bonus, gate, y_b, pg, ln_x_w[l], ln_x_b[l], w_br_a[l], w_br_b[l], w_out[l],
                                     norm_ffn[l], w_router[l], b_router[l], tq, seq=tp)
        xp_next = _moe_residual(x1, h, top_e, top_w, w_up[l], b_up[l], w_down[l], b_down[l], g_fin, last)
        n_win = min(WINDOW, tp)
        kv6t = lambda z: jnp.moveaxis(z.reshape(bp, 2, B_KV_HEADS, HEAD_DIM, tp), -1, 1)
        new['cmp_p'].append(kv6t(kvt_c))
        new['sel_p'].append(kv6t(kvt_s))
        new['win_p'].append(kv6t(kvt_w)[:, tp - n_win:])
        new['wkv_p'].append(_state_from_tiles(s_fin, bp))
        new['shift_p'].append(jnp.zeros((bp, A_PROJ), F32).at[:, A_PERM].set(pa_t[:, :, tp - 1]))

        s_a, q, kv_c, kv_s, kv_w, gt, pg = _project(xs, g_attn, wa_nat, w_rest, _row_tile(bs, 256))
        r, w, k2, v, kk, kb, gate, bonus = _rwkv_prep(s_a, state_rwkv_shift[l], *rw, _row_tile(bs, 256))
        yn, wkv_s = _wkv_step(r, w, k2, v, kk, kb, jnp.transpose(state_rwkv[l].astype(F32), (1, 2, 3, 0)))
        wkv_s = jnp.transpose(wkv_s, (3, 0, 1, 2))
        tok_minor = lambda z: jnp.moveaxis(z, 1, -1).reshape(z.shape[0], KV_ROW, z.shape[1])
        ckv = _compress_paged(tok_minor(cache_cmp_kv[l]), page_table, cw)
        q3, gt3 = q.reshape(bs, 1, B_WIDTH), gt.reshape(bs, 1, GATE_PAD)
        oc3, idx = _nsa_sample_select(q3, gt3, ckv, rel_bias, past_len)
        y_b = _nsa_sample_attend(idx, page_table, q3, gt3, oc3, kv_s.reshape(bs, KV_ROW, 1), kv_w.reshape(bs, 1, KV_ROW),
                                 tok_minor(state_win_kv[l]), tok_minor(cache_sel_kv[l]),
                                 rel_bias, past_len).reshape(bs, B_WIDTH)
        x1, h, top_e, top_w = _merge(xs, yn, bonus, gate, y_b, pg, ln_x_w[l], ln_x_b[l], w_br_a[l], w_br_b[l], w_out[l],
                                     norm_ffn[l], w_router[l], b_router[l], _row_tile(bs, 256))
        xs_next = _moe_residual(x1, h, top_e, top_w, w_up[l], b_up[l], w_down[l], b_down[l], g_fin, last)
        new['cmp_s'].append(kv6(kv_c, bs, 1))
        new['sel_s'].append(kv6(kv_s, bs, 1))
        new['win_s'].append(jnp.concatenate([state_win_kv[l], kv6(kv_w, bs, 1)], axis=1)[:, 1:])
        new['wkv_s'].append(wkv_s.astype(state_rwkv.dtype))
        new['shift_s'].append(s_a)
        xp, xs = xp_next, xs_next
    return (xp.reshape(bp, tp, D_MODEL), xs.reshape(bs, ts, D_MODEL),
            jnp.stack(new['cmp_p']), jnp.stack(new['sel_p']), jnp.stack(new['win_p']),
            jnp.stack(new['wkv_p']), jnp.stack(new['shift_p']),
            jnp.stack(new['cmp_s']), jnp.stack(new['sel_s']), jnp.stack(new['win_s']),
            jnp.stack(new['wkv_s']), jnp.stack(new['shift_s']))
```

```python
import functools
import math

import jax
import jax.numpy as jnp
import numpy as np
from jax import lax
from jax.experimental import pallas as pl
from jax.experimental.pallas import tpu as pltpu

F32 = jnp.float32
BF16 = jnp.bfloat16
I32 = jnp.int32
HIGHEST = lax.Precision.HIGHEST

D_MODEL = 1024
HEAD_DIM = 64
A_HEADS = 8
A_WIDTH = A_HEADS * HEAD_DIM
A_DECAY_RANK = 64
A_ICLR_RANK = 64
A_GATE_RANK = 128
A_GN_EPS = 64e-5
A_PROJ = 3 * A_WIDTH + A_DECAY_RANK + A_ICLR_RANK + A_GATE_RANK
B_HEADS = 8
B_KV_HEADS = 2
B_GROUP = B_HEADS // B_KV_HEADS
B_WIDTH = B_HEADS * HEAD_DIM
KV_WIDTH = B_KV_HEADS * HEAD_DIM
KV_ROW = 2 * KV_WIDTH
CMP_BLOCK = 32
CMP_HIDDEN = 128
SEL_BLOCK = 64
SEL_RATIO = SEL_BLOCK // CMP_BLOCK
TOP_N = 16
WINDOW = 512
Q_BLOCK = 128
FORCE_SCORE = 1e4
N_BUCKETS = 32
MAX_EXACT = N_BUCKETS // 2
MAX_DISTANCE = 1024
N_EXPERTS = 32
TOP_K = 4
D_FF = 1024
SWIGLU_ALPHA = 1.702
SWIGLU_LIMIT = 7.0
NORM_EPS = 1e-6
NEG_INF = -1e30
SCALE = HEAD_DIM ** -0.5
GATE_PAD = 128
GATE_PROJ = 2 * D_MODEL
MOE_ROWS = 256
LANES = 128
VMEM_LIMIT = 56 * 1024 * 1024


def _cparams(sem):
    return pltpu.CompilerParams(dimension_semantics=sem, vmem_limit_bytes=VMEM_LIMIT)


def _full(shape):
    n = len(shape)
    return pl.BlockSpec(shape, lambda *_: (0,) * n)


def _dot(a, b):
    return jnp.dot(a, b, preferred_element_type=F32)


def _dot_nt(a, b):
    return lax.dot_general(a, b, (((1,), (1,)), ((), ())), preferred_element_type=F32)


def _bf(x):
    return x.astype(BF16)


PROJ_SPLITS = (B_WIDTH, KV_ROW, KV_ROW, KV_ROW, GATE_PAD, GATE_PROJ)
CH_MAJOR = (np.arange(A_HEADS)[None, :] * HEAD_DIM + np.arange(HEAD_DIM)[:, None]).reshape(-1)
A_PERM = np.concatenate([CH_MAJOR, A_WIDTH + CH_MAJOR, 2 * A_WIDTH + CH_MAJOR, np.arange(3 * A_WIDTH, A_PROJ)])


def _rms_bf16(x_ref, g_ref):
    x = x_ref[...]
    return _bf(x * lax.rsqrt(jnp.mean(x * x, axis=-1, keepdims=True) + NORM_EPS) * g_ref[...])


def _proj_kernel(x_ref, g_ref, wa_ref, w_ref, oa_ref, *o_refs):
    h = _rms_bf16(x_ref, g_ref)
    oa_ref[...] = _dot(h, wa_ref[...])
    c = 0
    for o_ref, n in zip(o_refs, PROJ_SPLITS):
        o_ref[...] = _dot(h, w_ref[:, c:c + n])
        c += n


def _project(x, g, wa, w, tm):
    rows = x.shape[0]
    splits = (A_PROJ,) + PROJ_SPLITS
    return pl.pallas_call(
        _proj_kernel,
        grid=(rows // tm,),
        in_specs=[pl.BlockSpec((tm, D_MODEL), lambda i: (i, 0)), _full((1, D_MODEL)), _full(wa.shape), _full(w.shape)],
        out_specs=[pl.BlockSpec((tm, n), lambda i: (i, 0)) for n in splits],
        out_shape=[jax.ShapeDtypeStruct((rows, n), F32) for n in splits],
        compiler_params=_cparams(("parallel",)),
        name="norm_proj",
    )(x, g, wa, w)


PROJ_T_SPLITS = (A_PROJ, B_WIDTH, KV_ROW, KV_ROW, KV_ROW, GATE_PAD)
PROJ_N_SPLITS = (KV_ROW, KV_ROW, KV_ROW, GATE_PROJ)


def _proj_t_kernel(x_ref, g_ref, wt_ref, wn_ref, *o_refs):
    h = _rms_bf16(x_ref, g_ref)
    t = _dot_nt(wt_ref[...], h)
    c = 0
    for o_ref, n in zip(o_refs[:len(PROJ_T_SPLITS)], PROJ_T_SPLITS):
        o_ref[0] = t[c:c + n]
        c += n
    c = 0
    for o_ref, n in zip(o_refs[len(PROJ_T_SPLITS):], PROJ_N_SPLITS):
        o_ref[...] = _dot(h, wn_ref[:, c:c + n])
        c += n


def _project_t(x, g, wt, wn, tm, seq):
    rows = x.shape[0]
    per = seq // tm
    return pl.pallas_call(
        _proj_t_kernel,
        grid=(rows // tm,),
        in_specs=[pl.BlockSpec((tm, D_MODEL), lambda i: (i, 0)), _full((1, D_MODEL)), _full(wt.shape), _full(wn.shape)],
        out_specs=[pl.BlockSpec((1, n, tm), lambda i: (i // per, 0, i % per)) for n in PROJ_T_SPLITS]
                  + [pl.BlockSpec((tm, n), lambda i: (i, 0)) for n in PROJ_N_SPLITS],
        out_shape=[jax.ShapeDtypeStruct((rows // seq, n, seq), F32) for n in PROJ_T_SPLITS]
                  + [jax.ShapeDtypeStruct((rows, n), F32) for n in PROJ_N_SPLITS],
        compiler_params=_cparams(("parallel",)),
        name="norm_proj_t",
    )(x, g, wt, wn)


def _pack_w_in(w_in):
    a, rest = w_in[:, :A_PROJ], w_in[:, A_PROJ:]
    q, kv, gt, pg = (rest[:, :B_WIDTH], rest[:, B_WIDTH:B_WIDTH + 3 * KV_ROW],
                     rest[:, B_WIDTH + 3 * KV_ROW:B_WIDTH + 3 * KV_ROW + 3 * B_HEADS],
                     rest[:, B_WIDTH + 3 * KV_ROW + 3 * B_HEADS:])
    gt = jnp.pad(gt, ((0, 0), (0, GATE_PAD - 3 * B_HEADS)))
    w_t = jnp.concatenate([a[:, A_PERM], q, kv, gt], axis=1).T
    return _bf(a), _bf(jnp.concatenate([q, kv, gt, pg], axis=1)), _bf(w_t), _bf(jnp.concatenate([kv, pg], axis=1))


def _softplus(z):
    return jnp.maximum(z, 0.0) + jnp.log1p(jnp.exp(-jnp.abs(z)))


def _rwkv_prep_kernel(p_ref, prev_ref, mu_ref, w0_ref, wdu_ref, a0_ref, wiu_ref, wgu_ref, kk_ref, ka_ref, rk_ref,
                      ones_ref, r_o, w_o, k_o, v_o, kk_o, kb_o, g_o, bo_o):
    p = p_ref[...]
    ps = p + mu_ref[...] * (prev_ref[...] - p)
    r = ps[:, 0:A_WIDTH]
    k = ps[:, A_WIDTH:2 * A_WIDTH]
    v = ps[:, 2 * A_WIDTH:3 * A_WIDTH]
    c = 3 * A_WIDTH
    xw = ps[:, c:c + A_DECAY_RANK]
    xa = ps[:, c + A_DECAY_RANK:c + A_DECAY_RANK + A_ICLR_RANK]
    xg = ps[:, c + A_DECAY_RANK + A_ICLR_RANK:]
    w_log = -_softplus(-(w0_ref[...] + _dot(_bf(jnp.tanh(xw)), wdu_ref[...]))) - 0.5
    decay = jnp.exp(-jnp.exp(w_log))
    a = jax.nn.sigmoid(a0_ref[...] + _dot(_bf(xa), wiu_ref[...]))
    gate = _dot(_bf(jax.nn.sigmoid(xg)), wgu_ref[...])
    ones = ones_ref[...]
    kk = k * kk_ref[...]
    ss = jnp.dot(kk * kk, ones, precision=HIGHEST, preferred_element_type=F32)
    kk = kk / jnp.maximum(jnp.sqrt(ss), 1e-12)
    k2 = k * (1.0 + (a - 1.0) * ka_ref[...])
    rk = jnp.dot(r * k2 * rk_ref[...], ones, precision=HIGHEST, preferred_element_type=F32)
    r_o[...] = r
    w_o[...] = decay
    k_o[...] = k2
    v_o[...] = v
    kk_o[...] = kk
    kb_o[...] = kk * a
    g_o[...] = gate
    bo_o[...] = rk * v


def _rwkv_prep(p, prev, mu, w0, wdu, a0, wiu, wgu, k_k, k_a, r_k, tm):
    rows = p.shape[0]
    head = np.arange(A_WIDTH) // HEAD_DIM
    ones = jnp.asarray(head[:, None] == head[None, :], F32)
    row = lambda z: z.reshape(1, -1).astype(F32)
    spec_in = pl.BlockSpec((tm, A_PROJ), lambda i: (i, 0))
    spec_o = pl.BlockSpec((tm, A_WIDTH), lambda i: (i, 0))
    return pl.pallas_call(
        _rwkv_prep_kernel,
        grid=(rows // tm,),
        in_specs=[spec_in, spec_in, _full((1, A_PROJ)), _full((1, A_WIDTH)), _full((A_DECAY_RANK, A_WIDTH)),
                  _full((1, A_WIDTH)), _full((A_ICLR_RANK, A_WIDTH)), _full((A_GATE_RANK, A_WIDTH)),
                  _full((1, A_WIDTH)), _full((1, A_WIDTH)), _full((1, A_WIDTH)), _full((A_WIDTH, A_WIDTH))],
        out_specs=[spec_o] * 8,
        out_shape=[jax.ShapeDtypeStruct((rows, A_WIDTH), F32)] * 8,
        compiler_params=_cparams(("parallel",)),
        name="rwkv_prep",
    )(p, prev, row(mu), row(w0), _bf(wdu), row(a0), _bf(wiu), _bf(wgu), row(k_k), row(k_a), row(r_k), ones)


def _rwkv_prep_t_kernel(p_ref, mu_ref, w0_ref, wdu_ref, a0_ref, wiu_ref, wgu_ref, kk_ref, ka_ref, rk_ref,
                        r_o, w_o, k_o, v_o, kk_o, kb_o, g_o, bo_o, carry):
    tt = p_ref.shape[2]

    @pl.when(pl.program_id(1) == 0)
    def _():
        carry[...] = jnp.zeros_like(carry)

    p = p_ref[0]
    lane = lax.broadcasted_iota(I32, (1, tt), 1)
    prev = jnp.where(lane == 0, carry[:, 0:1], pltpu.roll(p, 1, 1))
    carry[:, 0:1] = p[:, tt - 1:tt]
    ps = p + mu_ref[...] * (prev - p)
    r = ps[0:A_WIDTH]
    k = ps[A_WIDTH:2 * A_WIDTH]
    v = ps[2 * A_WIDTH:3 * A_WIDTH]
    c = 3 * A_WIDTH
    xw = ps[c:c + A_DECAY_RANK]
    xa = ps[c + A_DECAY_RANK:c + A_DECAY_RANK + A_ICLR_RANK]
    xg = ps[c + A_DECAY_RANK + A_ICLR_RANK:]

    def head_sum(x):
        s = jnp.sum(x.reshape(HEAD_DIM, A_HEADS, tt), axis=0)
        return jnp.broadcast_to(s[None], (HEAD_DIM, A_HEADS, tt)).reshape(A_WIDTH, tt)

    w_log = -_softplus(-(w0_ref[...] + _dot(wdu_ref[...], _bf(jnp.tanh(xw))))) - 0.5
    decay = jnp.exp(-jnp.exp(w_log))
    a = jax.nn.sigmoid(a0_ref[...] + _dot(wiu_ref[...], _bf(xa)))
    gate = _dot(wgu_ref[...], _bf(jax.nn.sigmoid(xg)))
    kk = k * kk_ref[...]
    kk = kk / jnp.maximum(jnp.sqrt(head_sum(kk * kk)), 1e-12)
    k2 = k * (1.0 + (a - 1.0) * ka_ref[...])
    rk = head_sum(r * k2 * rk_ref[...])
    for o_ref, val in ((r_o, r), (w_o, decay), (k_o, k2), (kk_o, kk), (kb_o, kk * a)):
        o_ref[...] = val.reshape(N_JP, 2, 1, A_HEADS, tt)
    v_o[...] = v.reshape(HEAD_DIM, 1, A_HEADS, tt)
    g_o[0] = gate
    bo_o[0] = rk * v


def _rwkv_prep_t(p_t, mu, w0, wdu, a0, wiu, wgu, k_k, k_a, r_k, tt):
    b, _, t = p_t.shape
    col = lambda z, perm: z.reshape(-1)[perm].reshape(-1, 1).astype(F32)
    up = lambda w: _bf(w[:, CH_MAJOR].T)
    kshape = jax.ShapeDtypeStruct((N_JP, 2, b, A_HEADS, t), F32)
    kspec = pl.BlockSpec((N_JP, 2, 1, A_HEADS, tt), lambda bi, i: (0, 0, bi, 0, i))
    vspec = pl.BlockSpec((HEAD_DIM, 1, A_HEADS, tt), lambda bi, i: (0, bi, 0, i))
    nspec = pl.BlockSpec((1, A_WIDTH, tt), lambda bi, i: (bi, 0, i))
    nshape = jax.ShapeDtypeStruct((b, A_WIDTH, t), F32)
    return pl.pallas_call(
        _rwkv_prep_t_kernel,
        grid=(b, t // tt),
        in_specs=[pl.BlockSpec((1, A_PROJ, tt), lambda bi, i: (bi, 0, i)), _full((A_PROJ, 1)), _full((A_WIDTH, 1)),
                  _full((A_WIDTH, A_DECAY_RANK)), _full((A_WIDTH, 1)), _full((A_WIDTH, A_ICLR_RANK)),
                  _full((A_WIDTH, A_GATE_RANK)), _full((A_WIDTH, 1)), _full((A_WIDTH, 1)), _full((A_WIDTH, 1))],
        out_specs=[kspec, kspec, kspec, vspec, kspec, kspec, nspec, nspec],
        out_shape=[kshape, kshape, kshape, jax.ShapeDtypeStruct((HEAD_DIM, b, A_HEADS, t), F32), kshape, kshape,
                   nshape, nshape],
        scratch_shapes=[pltpu.VMEM((A_PROJ, LANES), F32)],
        compiler_params=_cparams(("parallel", "arbitrary")),
        name="rwkv_prep_t",
    )(p_t, col(mu, A_PERM), col(w0, CH_MAJOR), up(wdu), col(a0, CH_MAJOR), up(wiu), up(wgu),
      col(k_k, CH_MAJOR), col(k_a, CH_MAJOR), col(r_k, CH_MAJOR))


def _key_tiles_kernel(*refs):
    n = len(refs) // 2
    for z_ref, o_ref in zip(refs[:n], refs[n:]):
        for jp in range(N_JP):
            o_ref[0, jp] = z_ref[jp].T


def _key_tiles(zs, tt):
    t = zs[0].shape[2]
    return pl.pallas_call(
        _key_tiles_kernel,
        grid=(t // tt,),
        in_specs=[pl.BlockSpec((N_JP, LANES, tt), lambda i: (0, 0, i))] * len(zs),
        out_specs=[pl.BlockSpec((1, N_JP, tt, LANES), lambda i: (0, 0, i, 0))] * len(zs),
        out_shape=[jax.ShapeDtypeStruct((1, N_JP, t, LANES), F32)] * len(zs),
        compiler_params=_cparams(("parallel",)),
        name="wkv_key_tiles",
    )(*zs)


def _val_tiles_kernel(z_ref, o_ref, *, tt):
    for i in range(HEAD_DIM):
        x = z_ref[i]
        o_ref[pl.ds(i, tt, stride=HEAD_DIM), :] = jnp.concatenate([x, x], axis=0).T


def _val_tiles(z, tt):
    t = z.shape[2]
    return pl.pallas_call(
        functools.partial(_val_tiles_kernel, tt=tt),
        grid=(t // tt,),
        in_specs=[pl.BlockSpec((HEAD_DIM, BH_LANES, tt), lambda i: (0, 0, i))],
        out_specs=pl.BlockSpec((tt * HEAD_DIM, LANES), lambda i: (i, 0)),
        out_shape=jax.ShapeDtypeStruct((t * HEAD_DIM, LANES), F32),
        compiler_params=_cparams(("parallel",)),
        name="wkv_val_tiles",
    )(z)


def _val_untile_kernel(y_ref, o_ref, *, tt):
    for i in range(HEAD_DIM):
        o_ref[i] = y_ref[pl.ds(i, tt, stride=HEAD_DIM), :].T[0:BH_LANES]


def _val_untile(y, tt):
    t = y.shape[0] // HEAD_DIM
    return pl.pallas_call(
        functools.partial(_val_untile_kernel, tt=tt),
        grid=(t // tt,),
        in_specs=[pl.BlockSpec((tt * HEAD_DIM, LANES), lambda i: (i, 0))],
        out_specs=pl.BlockSpec((HEAD_DIM, BH_LANES, tt), lambda i: (0, 0, i)),
        out_shape=jax.ShapeDtypeStruct((HEAD_DIM, BH_LANES, t), F32),
        compiler_params=_cparams(("parallel",)),
        name="wkv_val_untile",
    )(y)


N_IO = HEAD_DIM // 8
N_JP = HEAD_DIM // 2
BH_LANES = 64


def _wkv_kernel(r_ref, w_ref, k_ref, v_ref, a_ref, b_ref, s0_ref, y_ref, sfin_ref, s_scr, *, tc):
    t_blk = pl.program_id(1)

    @pl.when(t_blk == 0)
    def _():
        s_scr[...] = s0_ref[0]

    def bc(ref, t, jp):
        return jnp.broadcast_to(ref[0, jp, pl.ds(t, 1), :], (8, LANES))

    def fold(x):
        return x + pltpu.roll(x, BH_LANES, 1)

    def state_times_a(t):
        acc = [jnp.zeros((8, LANES), F32) for _ in range(N_IO)]
        for jp in range(N_JP):
            a_ = bc(a_ref, t, jp)
            for io in range(N_IO):
                acc[io] = acc[io] + s_scr[io, jp] * a_
        return tuple(-fold(acc[io]) for io in range(N_IO))

    def step(t, sa):
        t_next = jnp.minimum(t + 1, tc - 1)
        vv = [v_ref[0, t, io * 8:(io + 1) * 8, :] for io in range(N_IO)]
        acc = [jnp.zeros((8, LANES), F32) for _ in range(N_IO)]
        yacc = [jnp.zeros((8, LANES), F32) for _ in range(N_IO)]
        for jp in range(N_JP):
            w_, b_, k_, r_, a_ = (bc(ref, tt, jp) for ref, tt in
                                  ((w_ref, t), (b_ref, t), (k_ref, t), (r_ref, t), (a_ref, t_next)))
            for io in range(N_IO):
                s = s_scr[io, jp] * w_ + sa[io] * b_ + vv[io] * k_
                s_scr[io, jp] = s
                yacc[io] = yacc[io] + s * r_
                acc[io] = acc[io] + s * a_
        sa_next = [-fold(acc[io]) for io in range(N_IO)]
        y = [fold(yacc[io]) for io in range(N_IO)]
        tot = y[0]
        for io in range(1, N_IO):
            tot = tot + y[io]
        mu = jnp.sum(tot, axis=0, keepdims=True) * (1.0 / HEAD_DIM)
        d = [y[io] - mu for io in range(N_IO)]
        sq = d[0] * d[0]
        for io in range(1, N_IO):
            sq = sq + d[io] * d[io]
        var = jnp.sum(sq, axis=0, keepdims=True) * (1.0 / HEAD_DIM)
        inv = lax.rsqrt(var + A_GN_EPS)
        for io in range(N_IO):
            y_ref[0, t, io * 8:(io + 1) * 8, :] = d[io] * inv
        return tuple(sa_next)

    lax.fori_loop(0, tc, step, state_times_a(0))

    @pl.when(t_blk == pl.num_programs(1) - 1)
    def _():
        sfin_ref[0] = s_scr[...]


def _wkv_scan(r, w, k, v, a, b, s0, tc):
    nb, t = v.shape[:2]
    kspec = pl.BlockSpec((1, N_JP, tc, LANES), lambda n, i: (n, 0, i, 0))
    vspec = pl.BlockSpec((1, tc, HEAD_DIM, LANES), lambda n, i: (n, i, 0, 0))
    sspec = pl.BlockSpec((1, N_IO, N_JP, 8, LANES), lambda n, i: (n, 0, 0, 0, 0))
    return pl.pallas_call(
        functools.partial(_wkv_kernel, tc=tc),
        grid=(nb, t // tc),
        in_specs=[kspec, kspec, kspec, vspec, kspec, kspec, sspec],
        out_specs=[vspec, sspec],
        out_shape=[jax.ShapeDtypeStruct((nb, t, HEAD_DIM, LANES), F32),
                   jax.ShapeDtypeStruct((nb, N_IO, N_JP, 8, LANES), F32)],
        scratch_shapes=[pltpu.VMEM((N_IO, N_JP, 8, LANES), F32)],
        compiler_params=_cparams(("parallel", "arbitrary")),
        name="wkv_scan",
    )(r, w, k, v, a, b, s0)


def _wkv_step_kernel(r_ref, w_ref, k_ref, v_ref, a_ref, b_ref, s_ref, y_ref, so_ref):
    rt, wt, kt, vt, at, bt = (ref[...].T for ref in (r_ref, w_ref, k_ref, v_ref, a_ref, b_ref))
    outs = []
    for hh in range(2):
        hs = slice(hh * HEAD_DIM, (hh + 1) * HEAD_DIM)
        r_, w_, k_, a_, b_ = rt[hs], wt[hs], kt[hs], at[hs], bt[hs]
        ys = []
        for i in range(HEAD_DIM):
            s = s_ref[hh, i]
            sa = -jnp.sum(s * a_, axis=0, keepdims=True)
            s = s * w_ + sa * b_ + vt[hh * HEAD_DIM + i:hh * HEAD_DIM + i + 1] * k_
            so_ref[hh, i] = s
            ys.append(jnp.sum(s * r_, axis=0, keepdims=True))
        y = jnp.concatenate(ys, axis=0)
        d = y - jnp.mean(y, axis=0, keepdims=True)
        outs.append(d * lax.rsqrt(jnp.mean(d * d, axis=0, keepdims=True) + A_GN_EPS))
    y_ref[...] = jnp.concatenate(outs, axis=0).T


def _wkv_step(r, w, k, v, a, b, state):
    s = r.shape[0]
    vec = pl.BlockSpec((s, 2 * HEAD_DIM), lambda h: (0, h))
    st = pl.BlockSpec((2, HEAD_DIM, HEAD_DIM, s), lambda h: (h, 0, 0, 0))
    return pl.pallas_call(
        _wkv_step_kernel,
        grid=(A_HEADS // 2,),
        in_specs=[vec] * 6 + [st],
        out_specs=[vec, st],
        out_shape=[jax.ShapeDtypeStruct((s, A_WIDTH), F32), jax.ShapeDtypeStruct(state.shape, F32)],
        compiler_params=_cparams(("parallel",)),
        name="wkv_step",
    )(r, w, k, v, a, b, state)


def _state_from_tiles(s, b):
    nb = b // 8
    s = s.reshape(nb, N_IO, N_JP, 8, 2, 8, A_HEADS).transpose(0, 5, 6, 1, 3, 2, 4)
    return s.reshape(b, A_HEADS, HEAD_DIM, HEAD_DIM)


def _compress_paged_kernel(pt_ref, pe_ref, w1_ref, w2_ref, cache_ref, o_ref, buf, xk, xv, sem, *, n_pages, page_len):
    i = pl.program_id(0)
    slot = i % 2

    def copy(s_idx, sl, p):
        return pltpu.make_async_copy(cache_ref.at[pt_ref[s_idx * n_pages + p]], buf.at[sl, p], sem.at[sl, p])

    def fetch(s_idx, sl):
        for p in range(n_pages):
            copy(s_idx, sl, p).start()

    @pl.when(i == 0)
    def _():
        fetch(0, 0)

    @pl.when(i + 1 < pl.num_programs(0))
    def _():
        fetch(i + 1, 1 - slot)

    for p in range(n_pages):
        copy(i, slot, p).wait()
    for p in range(n_pages):
        t = buf[slot, p].T
        for m in range(page_len // CMP_BLOCK):
            row = (p * (page_len // CMP_BLOCK) + m) * CMP_PITCH
            xk[row:row + CMP_BLOCK, :] = t[m * CMP_BLOCK:(m + 1) * CMP_BLOCK, :KV_WIDTH]
            xv[row:row + CMP_BLOCK, :] = t[m * CMP_BLOCK:(m + 1) * CMP_BLOCK, KV_WIDTH:]
    _compress_kernel(xk, xv, pe_ref, w1_ref, w2_ref, o_ref, nblk=n_pages * page_len // CMP_BLOCK, pitch=CMP_PITCH)


CMP_PITCH = CMP_BLOCK + 1


def _compress_kernel(xk_ref, xv_ref, pe_ref, w1_ref, w2_ref, o_ref, *, nblk, pitch=CMP_BLOCK):
    for c, x_ref in enumerate((xk_ref, xv_ref)):
        acc = jnp.zeros((nblk, B_KV_HEADS * CMP_HIDDEN), F32)
        for tau in range(CMP_BLOCK):
            x = x_ref[pl.ds(tau, nblk, stride=pitch), :] + pe_ref[c, tau:tau + 1, :]
            acc = acc + _dot(_bf(x), w1_ref[c, tau])
        o_ref[:, c * KV_WIDTH:(c + 1) * KV_WIDTH] = _dot(_bf(jax.nn.gelu(acc)), w2_ref[c])


def _compress_weights(pe_k, pe_v, w_k1, w_k2, w_v1, w_v2):
    pe = jnp.stack([jnp.concatenate([pe_k, pe_k], axis=1), jnp.concatenate([pe_v, pe_v], axis=1)]).astype(F32)
    eye = jnp.eye(B_KV_HEADS, dtype=F32)
    w1 = jnp.stack([w_k1, w_v1]).reshape(2, CMP_BLOCK, HEAD_DIM, CMP_HIDDEN)
    w1 = jnp.einsum('ctdh,ge->ctgdeh', w1, eye).reshape(2, CMP_BLOCK, KV_WIDTH, B_KV_HEADS * CMP_HIDDEN)
    w2 = jnp.einsum('chd,ge->cghed', jnp.stack([w_k2, w_v2]), eye).reshape(2, B_KV_HEADS * CMP_HIDDEN, KV_WIDTH)
    return pe, _bf(w1), _bf(w2)


COMPRESS_W_SPECS = ((2, CMP_BLOCK, KV_WIDTH), (2, CMP_BLOCK, KV_WIDTH, B_KV_HEADS * CMP_HIDDEN),
                    (2, B_KV_HEADS * CMP_HIDDEN, KV_WIDTH))


def _compress_paged(cache_t, page_table, cw):
    s, n_pages = page_table.shape
    page = cache_t.shape[2]
    past = n_pages * page
    nblk = past // CMP_BLOCK
    grid_spec = pltpu.PrefetchScalarGridSpec(
        num_scalar_prefetch=1,
        grid=(s,),
        in_specs=[pl.BlockSpec(shp, lambda i, pt, n=len(shp): (0,) * n) for shp in COMPRESS_W_SPECS]
                 + [pl.BlockSpec(memory_space=pl.ANY)],
        out_specs=pl.BlockSpec((nblk, KV_ROW), lambda i, pt: (i, 0)),
        scratch_shapes=[pltpu.VMEM((2, n_pages, KV_ROW, page), F32), pltpu.VMEM((nblk * CMP_PITCH, KV_WIDTH), F32),
                        pltpu.VMEM((nblk * CMP_PITCH, KV_WIDTH), F32), pltpu.SemaphoreType.DMA((2, n_pages))],
    )
    return pl.pallas_call(
        functools.partial(_compress_paged_kernel, n_pages=n_pages, page_len=page),
        grid_spec=grid_spec,
        out_shape=jax.ShapeDtypeStruct((s * nblk, KV_ROW), F32),
        compiler_params=_cparams(("arbitrary",)),
        name="nsa_compress_paged",
    )(page_table.reshape(-1), *_compress_weights(*cw), cache_t)


def _compress(x, pe_k, pe_v, w_k1, w_k2, w_v1, w_v2, rows):
    n = x.shape[0]
    nblk = rows // CMP_BLOCK
    pe, w1, w2 = _compress_weights(pe_k, pe_v, w_k1, w_k2, w_v1, w_v2)
    return pl.pallas_call(
        functools.partial(_compress_kernel, nblk=nblk),
        grid=(n // rows,),
        in_specs=[pl.BlockSpec((rows, KV_WIDTH), lambda i: (i, 0)), pl.BlockSpec((rows, KV_WIDTH), lambda i: (i, 1)),
                  ] + [_full(shp) for shp in COMPRESS_W_SPECS],
        out_specs=pl.BlockSpec((nblk, KV_ROW), lambda i: (i, 0)),
        out_shape=jax.ShapeDtypeStruct((n // CMP_BLOCK, KV_ROW), F32),
        compiler_params=_cparams(("parallel",)),
        name="nsa_compress",
    )(x, x, pe, w1, w2)


def _rel_bucket(dist):
    n = jnp.maximum(dist, 0)
    log_ratio = jnp.log(jnp.maximum(n, 1).astype(F32) / MAX_EXACT) / math.log(MAX_DISTANCE / MAX_EXACT)
    large = jnp.minimum(MAX_EXACT + (log_ratio * (N_BUCKETS - MAX_EXACT)).astype(I32), N_BUCKETS - 1)
    return jnp.where(n < MAX_EXACT, n, large)


def _bias_of(rel_bias, dist):
    onehot = (_rel_bucket(dist)[..., None] == jnp.arange(N_BUCKETS)).astype(F32)
    return jnp.einsum('...b,bh->h...', onehot, rel_bias.astype(F32), precision=HIGHEST)


KEY_TILE = 512


def _nsa_prompt_kernel(q_ref, gate_ref, ck_ref, ckt_ref, ksel_ref, vsel_ref, kwin_ref, vwin_ref, bct_ref, tz_ref,
                       o_ref, imp_scr, *, nc, ns, nq, top_n, wtiles):
    i = pl.program_id(1)
    qb = Q_BLOCK
    gates = jax.nn.sigmoid(gate_ref[0])
    pos_l = i * qb + lax.broadcasted_iota(I32, (1, qb), 1)
    okT = pos_l >= lax.broadcasted_iota(I32, (nc, qb), 0) * CMP_BLOCK + (CMP_BLOCK - 1)
    blk = lax.broadcasted_iota(I32, (ns, qb), 0)
    cur = pos_l // SEL_BLOCK
    forced = (blk == 0) | (blk == cur) | (blk == cur - 1)
    k_t = lax.broadcasted_iota(I32, (KEY_TILE, qb), 0)
    n_e = lax.broadcasted_iota(I32, (KEY_TILE, ns), 1)
    k_e = lax.broadcasted_iota(I32, (KEY_TILE, ns), 0)
    wk = (wtiles + 1) * qb
    wb = jnp.maximum(i - wtiles, 0)
    d_w = pos_l - (wb * qb + lax.broadcasted_iota(I32, (wk, qb), 0))
    madd_w = jnp.where((d_w >= 0) & (d_w < WINDOW), 0.0, NEG_INF)
    n_tiles = (i * qb + qb + KEY_TILE - 1) // KEY_TILE
    per = KEY_TILE // qb

    def bias_tiles(h, first_blk, count):
        return jnp.concatenate([tz_ref[h, jnp.clip(i - (first_blk + c), 0, nq - 1)] for c in range(count)], axis=0)

    ksl = [slice(g * HEAD_DIM, (g + 1) * HEAD_DIM) for g in range(B_KV_HEADS)]
    vsl = [slice(KV_WIDTH + g * HEAD_DIM, KV_WIDTH + (g + 1) * HEAD_DIM) for g in range(B_KV_HEADS)]
    qh = [_bf(q_ref[0, h * HEAD_DIM:(h + 1) * HEAD_DIM, :] * SCALE) for h in range(B_HEADS)]
    o_c, sel_t = [], []
    for g in range(B_KV_HEADS):
        kc = _bf(ck_ref[:, ksl[g]])
        vct = _bf(ckt_ref[0, vsl[g], :])
        impT = jnp.zeros((nc, qb), F32)
        for h in range(g * B_GROUP, (g + 1) * B_GROUP):
            sT = jnp.where(okT, _dot(kc, qh[h]) + bct_ref[h], NEG_INF)
            eT = jnp.exp(sT - jnp.max(sT, axis=0, keepdims=True))
            pT = eT / jnp.sum(eT, axis=0, keepdims=True) * okT.astype(F32)
            impT = impT + pT
            o_c.append(_dot(vct, _bf(pT)))
        imp_scr[...] = impT
        imp2 = imp_scr[pl.ds(0, ns, stride=SEL_RATIO), :] + imp_scr[pl.ds(1, ns, stride=SEL_RATIO), :]
        score = jnp.where(blk <= cur, imp2 + FORCE_SCORE * forced.astype(F32), NEG_INF)
        rank = jnp.zeros((ns, qb), I32)
        for m in range(ns):
            row = score[m:m + 1, :]
            rank = rank + ((row > score) | ((row == score) & (m < blk))).astype(I32)
        sel_t.append(_bf((rank < top_n) & (score > NEG_INF / 2)))

    def body(jt, carry):
        ms, ls, accs = carry
        k0 = pl.multiple_of(jt * KEY_TILE, KEY_TILE)
        causal = k0 + k_t <= pos_l
        in_blk = _bf(n_e == (k0 + k_e) // SEL_BLOCK)
        ms2, ls2, accs2 = [], [], []
        for g in range(B_KV_HEADS):
            kt = _bf(ksel_ref[pl.ds(k0, KEY_TILE), ksl[g]])
            vt = _bf(vsel_ref[0, vsl[g], pl.ds(k0, KEY_TILE)])
            madd = jnp.where((_dot(in_blk, sel_t[g]) > 0.5) & causal, 0.0, NEG_INF)
            for h in range(g * B_GROUP, (g + 1) * B_GROUP):
                s = _dot(kt, qh[h]) + bias_tiles(h, jt * per, per) + madd
                m_new = jnp.maximum(ms[h], jnp.max(s, axis=0, keepdims=True))
                alpha = jnp.exp(ms[h] - m_new)
                p = jnp.exp(s - m_new)
                ls2.append(alpha * ls[h] + jnp.sum(p, axis=0, keepdims=True))
                accs2.append(alpha * accs[h] + _dot(vt, _bf(p)))
                ms2.append(m_new)
        return tuple(ms2), tuple(ls2), tuple(accs2)

    init = (tuple(jnp.full((1, qb), NEG_INF, F32) for _ in range(B_HEADS)),
            tuple(jnp.zeros((1, qb), F32) for _ in range(B_HEADS)),
            tuple(jnp.zeros((HEAD_DIM, qb), F32) for _ in range(B_HEADS)))
    _, ls, accs = lax.fori_loop(0, n_tiles, body, init)
    w0 = pl.multiple_of(wb * qb, qb)
    for g in range(B_KV_HEADS):
        ktw = _bf(kwin_ref[pl.ds(w0, wk), ksl[g]])
        vtw = _bf(vwin_ref[0, vsl[g], pl.ds(w0, wk)])
        for h in range(g * B_GROUP, (g + 1) * B_GROUP):
            s = _dot(ktw, qh[h]) + bias_tiles(h, wb, wtiles + 1) + madd_w
            p = jnp.exp(s - jnp.max(s, axis=0, keepdims=True))
            o_w = _dot(vtw, _bf(p)) / jnp.sum(p, axis=0, keepdims=True)
            o_s = accs[h] / ls[h]
            o_ref[0, h * HEAD_DIM:(h + 1) * HEAD_DIM, :] = (
                gates[3 * h:3 * h + 1] * o_c[h] + gates[3 * h + 1:3 * h + 2] * o_s + gates[3 * h + 2:3 * h + 3] * o_w)


def _nsa_prompt(q_t, gate_t, ckv, kv_sel, kvt_sel, kv_win, kvt_win, rel_bias):
    b, _, t = q_t.shape
    nq = t // Q_BLOCK
    nc = t // CMP_BLOCK
    ns = -(-t // SEL_BLOCK)
    wtiles = WINDOW // Q_BLOCK
    assert t % KEY_TILE == 0 and nc == SEL_RATIO * ns and WINDOW % Q_BLOCK == 0 and (wtiles + 1) * Q_BLOCK <= t
    cmp_end = jnp.arange(nc) * CMP_BLOCK + (CMP_BLOCK - 1)
    bct = _bias_of(rel_bias, jnp.arange(t)[None, :] - cmp_end[:, None])
    dz = (jnp.arange(nq)[:, None, None] * Q_BLOCK + jnp.arange(Q_BLOCK)[None, None, :]
          - jnp.arange(Q_BLOCK)[None, :, None])
    tz = _bias_of(rel_bias, dz)
    ckt = jnp.swapaxes(ckv.reshape(b, nc, KV_ROW), 1, 2)
    chan = lambda w: pl.BlockSpec((1, w, Q_BLOCK), lambda bi, i: (bi, 0, i))
    seq_rows = pl.BlockSpec((t, KV_ROW), lambda bi, i: (bi, 0))
    seq_chan = pl.BlockSpec((1, KV_ROW, t), lambda bi, i: (bi, 0, 0))
    return pl.pallas_call(
        functools.partial(_nsa_prompt_kernel, nc=nc, ns=ns, nq=nq, top_n=min(TOP_N, ns), wtiles=wtiles),
        grid=(b, nq),
        in_specs=[chan(B_WIDTH), chan(GATE_PAD),
                  pl.BlockSpec((nc, KV_ROW), lambda bi, i: (bi, 0)),
                  pl.BlockSpec((1, KV_ROW, nc), lambda bi, i: (bi, 0, 0)),
                  seq_rows, seq_chan, seq_rows, seq_chan,
                  pl.BlockSpec((B_HEADS, nc, Q_BLOCK), lambda bi, i: (0, 0, i)),
                  _full((B_HEADS, nq, Q_BLOCK, Q_BLOCK))],
        out_specs=chan(B_WIDTH),
        out_shape=jax.ShapeDtypeStruct((b, B_WIDTH, t), F32),
        scratch_shapes=[pltpu.VMEM((nc, Q_BLOCK), F32)],
        compiler_params=_cparams(("parallel", "arbitrary")),
        name="nsa_prompt",
    )(q_t, gate_t, ckv, ckt, kv_sel, kvt_sel, kv_win, kvt_win, bct, tz)


def _group_q(q_ref, g):
    rows = [q_ref[0, :, (g * B_GROUP + r) * HEAD_DIM:(g * B_GROUP + r + 1) * HEAD_DIM] for r in range(B_GROUP)]
    return _bf(jnp.concatenate(rows, axis=0) * SCALE)


def _nsa_sample_select_kernel(q_ref, gate_ref, ck_ref, bc_ref, pair_ref, tri_ref, oc_ref, idx_ref,
                              *, nsp, cur, top_n):
    gates = jax.nn.sigmoid(gate_ref[0])
    blk = lax.broadcasted_iota(I32, (1, nsp), 1)
    forced = (blk == 0) | (blk == cur) | (blk == cur - 1)
    mi = lax.broadcasted_iota(I32, (nsp, nsp), 0)
    ni = lax.broadcasted_iota(I32, (nsp, nsp), 1)
    kk = lax.broadcasted_iota(I32, (TOP_N, nsp), 0).astype(F32)
    nf = lax.broadcasted_iota(I32, (TOP_N, nsp), 1).astype(F32)
    outs = []
    for g in range(B_KV_HEADS):
        q4 = _group_q(q_ref, g)
        kc = _bf(ck_ref[:, g * HEAD_DIM:(g + 1) * HEAD_DIM])
        vc = _bf(ck_ref[:, KV_WIDTH + g * HEAD_DIM:KV_WIDTH + (g + 1) * HEAD_DIM])
        s = _dot_nt(q4, kc) + bc_ref[g * B_GROUP:(g + 1) * B_GROUP, :]
        e = jnp.exp(s - jnp.max(s, axis=-1, keepdims=True))
        p = e / jnp.sum(e, axis=-1, keepdims=True)
        o_c = _dot(_bf(p), vc)
        imp = ((p[0:1] + p[1:2]) + p[2:3]) + p[3:4]
        imp2 = jnp.dot(imp, pair_ref[...], precision=HIGHEST, preferred_element_type=F32)
        score = jnp.where(blk <= cur, imp2 + FORCE_SCORE * forced.astype(F32), NEG_INF)
        m1 = jnp.broadcast_to(score, (nsp, nsp))
        m2 = m1.T
        gt = (m2 > m1) | ((m2 == m1) & (mi < ni))
        rank = jnp.sum(gt.astype(F32), axis=0, keepdims=True)
        sel = (rank < top_n) & (score > NEG_INF / 2)
        before = _dot(_bf(sel), tri_ref[...])
        hit = jnp.broadcast_to(sel, (TOP_N, nsp)) & (jnp.broadcast_to(before, (TOP_N, nsp)) == kk)
        idx = jnp.sum(jnp.where(hit, nf, 0.0), axis=1, keepdims=True)
        cnt = jnp.sum(hit.astype(F32), axis=1, keepdims=True)
        idx_ref[0, g] = jnp.where(cnt > 0.5, idx, -1.0).astype(I32)
        for r in range(B_GROUP):
            h = g * B_GROUP + r
            outs.append(gates[:, 3 * h:3 * h + 1] * o_c[r:r + 1])
    oc_ref[0] = jnp.concatenate(outs, axis=1)


def _nsa_sample_select(q3, gate3, ckv, rel_bias, past_len):
    s = q3.shape[0]
    nc = past_len // CMP_BLOCK
    ns = -(-(past_len + 1) // SEL_BLOCK)
    nsp = -(-ns // LANES) * LANES
    cur = past_len // SEL_BLOCK
    bc = _bias_of(rel_bias, past_len - (jnp.arange(nc) * CMP_BLOCK + (CMP_BLOCK - 1)))
    pair = jnp.asarray(np.arange(nc)[:, None] // SEL_RATIO == np.arange(nsp)[None, :], F32)
    tri = jnp.asarray(np.arange(nsp)[:, None] < np.arange(nsp)[None, :], BF16)
    return pl.pallas_call(
        functools.partial(_nsa_sample_select_kernel, nsp=nsp, cur=cur, top_n=min(TOP_N, ns)),
        grid=(s,),
        in_specs=[pl.BlockSpec((1, 1, B_WIDTH), lambda i: (i, 0, 0)), pl.BlockSpec((1, 1, GATE_PAD), lambda i: (i, 0, 0)),
                  pl.BlockSpec((nc, KV_ROW), lambda i: (i, 0)), _full((B_HEADS, nc)), _full((nc, nsp)), _full((nsp, nsp))],
        out_specs=[pl.BlockSpec((1, 1, B_WIDTH), lambda i: (i, 0, 0)),
                   pl.BlockSpec((1, B_KV_HEADS, TOP_N, 1), lambda i: (i, 0, 0, 0))],
        out_shape=[jax.ShapeDtypeStruct((s, 1, B_WIDTH), F32), jax.ShapeDtypeStruct((s, B_KV_HEADS, TOP_N, 1), I32)],
        compiler_params=_cparams(("parallel",)),
        name="nsa_sample_select",
    )(q3, gate3, ckv, bc, pair, tri)


def _nsa_sample_attend_kernel(idx_ref, pt_ref, q_ref, gate_ref, oc_ref, ksel_ref, kwin_ref, win_ref, fpg_ref, bw_ref,
                              cache_ref, o_ref, buf, sem, *, cur, n_pages, past_len, page_len):
    i = pl.program_id(0)
    n_s = pl.num_programs(0)
    slot = i % 2
    halves = PAGE_BLOCKS
    n_slot = B_KV_HEADS * TOP_N

    def block_of(s_idx, j):
        return idx_ref[s_idx * n_slot + j]

    def copy(s_idx, sl, j):
        n = jnp.clip(block_of(s_idx, j), 0, cur - 1)
        page = pt_ref[s_idx * n_pages + n // halves]
        return pltpu.make_async_copy(cache_ref.at[page], buf.at[sl, j], sem.at[sl])

    def cached(s_idx, j):
        n = block_of(s_idx, j)
        return (n >= 0) & (n < cur)

    def fetch(s_idx, sl):
        for j in range(n_slot):
            @pl.when(cached(s_idx, j))
            def _():
                copy(s_idx, sl, j).start()

    @pl.when(i == 0)
    def _():
        fetch(0, 0)

    @pl.when(i + 1 < n_s)
    def _():
        fetch(i + 1, 1 - slot)

    for j in range(n_slot):
        @pl.when(cached(i, j))
        def _():
            copy(i, slot, j).wait()

        @pl.when(jnp.logical_not(cached(i, j)))
        def _():
            buf[slot, j] = jnp.zeros((KV_ROW, page_len), F32)
            buf[slot, j, :, 0:1] = ksel_ref[0]

    gates = jax.nn.sigmoid(gate_ref[0])
    t_l = lax.broadcasted_iota(I32, (1, page_len), 1)
    n_buf = win_ref.shape[2]
    j_w = lax.broadcasted_iota(I32, (1, n_buf), 1)
    d_w = n_buf - j_w
    ok_w = (d_w >= 0) & (d_w < WINDOW) & (past_len - d_w >= 0)
    f0 = fpg_ref[n_pages]
    outs = []
    for g in range(B_KV_HEADS):
        q4 = _group_q(q_ref, g)
        ksl = slice(g * HEAD_DIM, (g + 1) * HEAD_DIM)
        vsl = slice(KV_WIDTH + g * HEAD_DIM, KV_WIDTH + (g + 1) * HEAD_DIM)
        pieces = []
        for k in range(TOP_N):
            j = g * TOP_N + k
            n = block_of(i, j)
            pg = jnp.clip(n, 0, cur) // halves
            fb = fpg_ref[pg][g * B_GROUP:(g + 1) * B_GROUP, :]
            s_k = _dot(q4, _bf(buf[slot, j, ksl, :])) + fb
            ok = (n >= 0) & (t_l // SEL_BLOCK == n % halves) & (pg * page_len + t_l <= past_len)
            pieces.append(jnp.where(ok, s_k, NEG_INF))
        s = jnp.concatenate(pieces, axis=1)
        e = jnp.exp(s - jnp.max(s, axis=-1, keepdims=True))
        p = _bf(e / jnp.sum(e, axis=-1, keepdims=True))
        o_s = jnp.zeros((B_GROUP, HEAD_DIM), F32)
        for k in range(TOP_N):
            o_s = o_s + _dot_nt(p[:, k * page_len:(k + 1) * page_len], _bf(buf[slot, g * TOP_N + k, vsl, :]))
        kw = _bf(win_ref[0, ksl, :])
        vw = _bf(win_ref[0, vsl, :])
        s_w = jnp.where(ok_w, _dot(q4, kw) + bw_ref[g * B_GROUP:(g + 1) * B_GROUP, :], NEG_INF)
        k_new = _bf(kwin_ref[0, :, ksl])
        v_new = _bf(kwin_ref[0, :, vsl])
        s_n = (jnp.sum(q4.astype(F32) * k_new.astype(F32), axis=-1, keepdims=True)
               + f0[g * B_GROUP:(g + 1) * B_GROUP, 0:1])
        m = jnp.maximum(jnp.max(s_w, axis=-1, keepdims=True), s_n)
        e_w = jnp.exp(s_w - m)
        e_n = jnp.exp(s_n - m)
        den = jnp.sum(e_w, axis=-1, keepdims=True) + e_n
        o_w = _dot_nt(_bf(e_w / den), vw) + _bf(e_n / den).astype(F32) * v_new.astype(F32)
        for r in range(B_GROUP):
            h = g * B_GROUP + r
            outs.append(gates[:, 3 * h + 1:3 * h + 2] * o_s[r:r + 1] + gates[:, 3 * h + 2:3 * h + 3] * o_w[r:r + 1])
    o_ref[0] = oc_ref[0] + jnp.concatenate(outs, axis=1)


PAGE_BLOCKS = 2


def _nsa_sample_attend(idx, page_table, q3, gate3, oc3, ksel_t, kwin3, win_t, cache_t, rel_bias, past_len):
    s = q3.shape[0]
    n_pages = page_table.shape[1]
    cur = past_len // SEL_BLOCK
    n_buf = win_t.shape[2]
    page = cache_t.shape[2]
    assert page == PAGE_BLOCKS * SEL_BLOCK and past_len == n_pages * page
    keypos = jnp.arange(n_pages + 1)[:, None] * page + jnp.arange(page)[None, :]
    fpg = jnp.moveaxis(_bias_of(rel_bias, past_len - keypos), 0, 1)
    bw = _bias_of(rel_bias, n_buf - jnp.arange(n_buf))
    row3 = lambda w: pl.BlockSpec((1, 1, w), lambda i, *_: (i, 0, 0))
    grid_spec = pltpu.PrefetchScalarGridSpec(
        num_scalar_prefetch=2,
        grid=(s,),
        in_specs=[row3(B_WIDTH), row3(GATE_PAD), row3(B_WIDTH),
                  pl.BlockSpec((1, KV_ROW, 1), lambda i, *_: (i, 0, 0)), row3(KV_ROW),
                  pl.BlockSpec((1, KV_ROW, n_buf), lambda i, *_: (i, 0, 0)),
                  pl.BlockSpec((n_pages + 1, B_HEADS, page), lambda i, *_: (0, 0, 0)),
                  pl.BlockSpec((B_HEADS, n_buf), lambda i, *_: (0, 0)),
                  pl.BlockSpec(memory_space=pl.ANY)],
        out_specs=row3(B_WIDTH),
        scratch_shapes=[pltpu.VMEM((2, B_KV_HEADS * TOP_N, KV_ROW, page), F32), pltpu.SemaphoreType.DMA((2,))],
    )
    return pl.pallas_call(
        functools.partial(_nsa_sample_attend_kernel, cur=cur, n_pages=n_pages, past_len=past_len, page_len=page),
        grid_spec=grid_spec,
        out_shape=jax.ShapeDtypeStruct((s, 1, B_WIDTH), F32),
        compiler_params=_cparams(("arbitrary",)),
        name="nsa_sample_attend",
    )(idx.reshape(-1), page_table.reshape(-1), q3, gate3, oc3, ksel_t, kwin3, win_t, fpg, bw, cache_t)


def _merge_kernel(x_ref, yn_ref, bo_ref, ga_ref, yb_ref, pg_ref, lnw_ref, lnb_ref, wa_ref, wb_ref, wo_ref, nf_ref,
                  wr_ref, br_ref, x1_ref, h_ref, te_ref, tw_ref, *, a_transposed):
    if a_transposed:
        tm = x_ref.shape[0]
        yn = yn_ref[...].reshape(A_WIDTH, tm)
        y_a = ((yn * lnw_ref[...] + lnb_ref[...] + bo_ref[0]) * ga_ref[0]).T
        y_b = yb_ref[0].T
    else:
        y_a = (yn_ref[...] * lnw_ref[...] + lnb_ref[...] + bo_ref[...]) * ga_ref[...]
        y_b = yb_ref[...]
    g_a = jax.nn.sigmoid(pg_ref[:, :D_MODEL])
    g_b = jax.nn.sigmoid(pg_ref[:, D_MODEL:])
    m = g_a * _dot(_bf(y_a), wa_ref[...]) + g_b * _dot(_bf(y_b), wb_ref[...])
    x1 = x_ref[...] + _dot(_bf(m), wo_ref[...])
    x1_ref[...] = x1
    h = x1 * lax.rsqrt(jnp.mean(x1 * x1, axis=-1, keepdims=True) + NORM_EPS) * nf_ref[...]
    _store_rows(h_ref, h)
    logits = jnp.dot(h, wr_ref[...], precision=HIGHEST, preferred_element_type=F32) + br_ref[...]
    lane = lax.broadcasted_iota(I32, logits.shape, 1).astype(F32)
    col = lax.broadcasted_iota(I32, (logits.shape[0], TOP_K), 1)
    vals, idxs = [], []
    for _ in range(TOP_K):
        m_k = jnp.max(logits, axis=-1, keepdims=True)
        i_k = jnp.min(jnp.where(logits == m_k, lane, float(N_EXPERTS)), axis=-1, keepdims=True)
        vals.append(m_k)
        idxs.append(i_k)
        logits = jnp.where(lane == i_k, -jnp.inf, logits)
    e = [jnp.exp(v - vals[0]) for v in vals]
    den = ((e[0] + e[1]) + e[2]) + e[3]
    te = jnp.zeros(col.shape, F32)
    tw = jnp.zeros(col.shape, F32)
    for k in range(TOP_K):
        te = jnp.where(col == k, idxs[k], te)
        tw = jnp.where(col == k, e[k] / den, tw)
    te_ref[...] = te.astype(I32)
    tw_ref[...] = tw


def _merge(x, yn, bonus, gate, yb, pg, ln_w, ln_b, w_a, w_b, w_o, norm_ffn, w_router, b_router, tm, seq=None):
    rows = x.shape[0]
    row = lambda w: pl.BlockSpec((tm, w), lambda i: (i, 0))
    vec = lambda z: z.reshape(1, -1).astype(F32)
    if seq is None:
        a_specs = [row(A_WIDTH)] * 3 + [row(B_WIDTH)] + [_full((1, A_WIDTH))] * 2
        lnw, lnb = vec(ln_w), vec(ln_b)
    else:
        per = seq // tm
        nspec = pl.BlockSpec((1, A_WIDTH, tm), lambda i: (i // per, 0, i % per))
        a_specs = [pl.BlockSpec((HEAD_DIM, 1, A_HEADS, tm), lambda i: (0, i // per, 0, i % per)), nspec, nspec, nspec,
                   _full((A_WIDTH, 1)), _full((A_WIDTH, 1))]
        lnw, lnb = (z[CH_MAJOR].reshape(-1, 1).astype(F32) for z in (ln_w, ln_b))
        w_a = w_a[CH_MAJOR]
    return pl.pallas_call(
        functools.partial(_merge_kernel, a_transposed=seq is not None),
        grid=(rows // tm,),
        in_specs=[row(D_MODEL)] + a_specs[:4] + [row(GATE_PROJ)] + a_specs[4:]
                 + [_full((A_WIDTH, D_MODEL)), _full((B_WIDTH, D_MODEL)),
                  _full((D_MODEL, D_MODEL)), _full((1, D_MODEL)), _full((D_MODEL, N_EXPERTS)), _full((1, N_EXPERTS))],
        out_specs=[row(D_MODEL), pl.BlockSpec((tm * ROW_TILE, LANES), lambda i: (i, 0)), row(TOP_K), row(TOP_K)],
        out_shape=[jax.ShapeDtypeStruct((rows, D_MODEL), F32), jax.ShapeDtypeStruct((rows * ROW_TILE, LANES), F32),
                   jax.ShapeDtypeStruct((rows, TOP_K), I32), jax.ShapeDtypeStruct((rows, TOP_K), F32)],
        compiler_params=_cparams(("parallel",)),
        name="merge_router",
    )(x, yn, bonus, gate, yb, pg, lnw, lnb, _bf(w_a), _bf(w_b), _bf(w_o), vec(norm_ffn),
      w_router.astype(F32), vec(b_router))


ROW_TILE = D_MODEL // LANES


def _load_rows(ref, n, first=0, every=1):
    return jnp.concatenate([ref[pl.ds(first * ROW_TILE + s, n, stride=every * ROW_TILE), :] for s in range(ROW_TILE)],
                           axis=1)


def _store_rows(ref, x):
    for s in range(ROW_TILE):
        ref[pl.ds(s, x.shape[0], stride=ROW_TILE), :] = x[:, s * LANES:(s + 1) * LANES]


def _dispatch_kernel(dest_ref, h_ref, xs_in_ref, xs_ref, sem, *, tm):
    del xs_in_ref
    for t in range(tm):
        for k in range(TOP_K):
            pltpu.make_async_copy(h_ref.at[t], xs_ref.at[dest_ref[0, 0, t * TOP_K + k]], sem).start()
    for t in range(tm * TOP_K):
        pltpu.make_async_copy(h_ref.at[0], xs_ref.at[0], sem).wait()


def _dispatch(dest, h_tiles, slots, tm):
    n = h_tiles.shape[0]
    return pl.pallas_call(
        functools.partial(_dispatch_kernel, tm=tm),
        grid=(n // tm,),
        in_specs=[pl.BlockSpec((1, 1, tm * TOP_K), lambda i: (i, 0, 0), memory_space=pltpu.SMEM),
                  pl.BlockSpec((tm, ROW_TILE, LANES), lambda i: (i, 0, 0)), pl.BlockSpec(memory_space=pl.ANY)],
        out_specs=pl.BlockSpec(memory_space=pl.ANY),
        out_shape=jax.ShapeDtypeStruct((slots, ROW_TILE, LANES), F32),
        scratch_shapes=[pltpu.SemaphoreType.DMA(())],
        input_output_aliases={2: 0},
        compiler_params=_cparams(("arbitrary",)),
        name="moe_dispatch",
    )(dest.reshape(n // tm, 1, tm * TOP_K), h_tiles, jnp.zeros((slots, ROW_TILE, LANES), F32))


def _moe_kernel(ce_ref, nu_ref, x_ref, wu_ref, bu_ref, wd_ref, bd_ref, o_ref, wu_bf, wd_bf):
    c = pl.program_id(0)
    e = ce_ref[c]
    prev = ce_ref[jnp.maximum(c - 1, 0)]

    @pl.when((c == 0) | (e != prev))
    def _():
        wu_bf[...] = _bf(wu_ref[0])
        wd_bf[...] = _bf(wd_ref[0])

    @pl.when(c < nu_ref[0])
    def _():
        u = _dot(_bf(_load_rows(x_ref, MOE_ROWS)), wu_bf[...]) + bu_ref[0]
        glu = jnp.minimum(u[:, :D_FF], SWIGLU_LIMIT)
        lin = jnp.clip(u[:, D_FF:], -SWIGLU_LIMIT, SWIGLU_LIMIT)
        act = glu * jax.nn.sigmoid(SWIGLU_ALPHA * glu) * (lin + 1.0)
        _store_rows(o_ref, _dot(_bf(act), wd_bf[...]) + bd_ref[0])

    @pl.when(c >= nu_ref[0])
    def _():
        o_ref[...] = jnp.zeros_like(o_ref)


def _moe_experts(chunk_e, n_used, xs, w_up, b_up, w_down, b_down):
    n_chunks = xs.shape[0] // (MOE_ROWS * ROW_TILE)
    rows = pl.BlockSpec((MOE_ROWS * ROW_TILE, LANES), lambda c, ce, nu: (c, 0))
    grid_spec = pltpu.PrefetchScalarGridSpec(
        num_scalar_prefetch=2,
        grid=(n_chunks,),
        in_specs=[rows,
                  pl.BlockSpec((1, D_MODEL, 2 * D_FF), lambda c, ce, nu: (ce[c], 0, 0)),
                  pl.BlockSpec((1, 1, 2 * D_FF), lambda c, ce, nu: (ce[c], 0, 0)),
                  pl.BlockSpec((1, D_FF, D_MODEL), lambda c, ce, nu: (ce[c], 0, 0)),
                  pl.BlockSpec((1, 1, D_MODEL), lambda c, ce, nu: (ce[c], 0, 0))],
        out_specs=rows,
        scratch_shapes=[pltpu.VMEM((D_MODEL, 2 * D_FF), BF16), pltpu.VMEM((D_FF, D_MODEL), BF16)],
    )
    return pl.pallas_call(
        _moe_kernel,
        grid_spec=grid_spec,
        out_shape=jax.ShapeDtypeStruct(xs.shape, F32),
        compiler_params=_cparams(("arbitrary",)),
        name="moe_experts",
    )(chunk_e, n_used, xs, w_up, b_up.reshape(N_EXPERTS, 1, -1), w_down, b_down.reshape(N_EXPERTS, 1, -1))


def _combine_kernel(dcur_ref, dnext_ref, x_ref, w_ref, g_ref, ys_ref, o_ref, buf, sem, *, tm, normalize):
    i = pl.program_id(0)
    slot = i % 2
    n_a = tm * TOP_K

    def copy(d_ref, sl, a):
        return pltpu.make_async_copy(ys_ref.at[d_ref[0, 0, a]], buf.at[sl, pl.ds(a * ROW_TILE, ROW_TILE)], sem.at[sl])

    def fetch(d_ref, sl):
        for a in range(n_a):
            copy(d_ref, sl, a).start()

    @pl.when(i == 0)
    def _():
        fetch(dcur_ref, 0)

    @pl.when(i + 1 < pl.num_programs(0))
    def _():
        fetch(dnext_ref, 1 - slot)

    for a in range(n_a):
        copy(dcur_ref, slot, a).wait()
    x = x_ref[...]
    for k in range(TOP_K):
        x = x + w_ref[:, k:k + 1] * _load_rows(buf.at[slot], tm, first=k, every=TOP_K)
    if normalize:
        x = x * lax.rsqrt(jnp.mean(x * x, axis=-1, keepdims=True) + NORM_EPS) * g_ref[...]
    o_ref[...] = x


def _combine(dest, x1, top_w, ys_tiles, g, tm, normalize):
    n = x1.shape[0]
    steps = n // tm
    d2 = dest.reshape(steps, 1, tm * TOP_K)
    row = lambda w: pl.BlockSpec((tm, w), lambda i: (i, 0))
    return pl.pallas_call(
        functools.partial(_combine_kernel, tm=tm, normalize=normalize),
        grid=(steps,),
        in_specs=[pl.BlockSpec((1, 1, tm * TOP_K), lambda i: (i, 0, 0), memory_space=pltpu.SMEM),
                  pl.BlockSpec((1, 1, tm * TOP_K), lambda i: (jnp.minimum(i + 1, steps - 1), 0, 0),
                               memory_space=pltpu.SMEM),
                  row(D_MODEL), row(TOP_K), _full((1, D_MODEL)), pl.BlockSpec(memory_space=pl.ANY)],
        out_specs=row(D_MODEL),
        out_shape=jax.ShapeDtypeStruct((n, D_MODEL), F32),
        scratch_shapes=[pltpu.VMEM((2, tm * TOP_K * ROW_TILE, LANES), F32), pltpu.SemaphoreType.DMA((2,))],
        compiler_params=_cparams(("arbitrary",)),
        name="moe_combine",
    )(d2, d2, x1, top_w, g.reshape(1, -1).astype(F32), ys_tiles)


def _moe_residual(x1, h_tiles, top_e, top_w, w_up, b_up, w_down, b_down, g, normalize):
    n = x1.shape[0]
    n_assign = n * TOP_K
    onehot = (top_e.reshape(-1, 1) == jnp.arange(N_EXPERTS, dtype=I32)[None, :]).astype(I32)
    csum = jnp.cumsum(onehot, axis=0)
    counts = csum[-1]
    padded = (counts + MOE_ROWS - 1) // MOE_ROWS * MOE_ROWS
    pad_end = jnp.cumsum(padded)
    dest = jnp.sum(onehot * (csum - 1 + (pad_end - padded)[None, :]), axis=1).astype(I32)
    n_chunks = -(-(n_assign + N_EXPERTS * (MOE_ROWS - 1)) // MOE_ROWS)
    slots = n_chunks * MOE_ROWS
    chunk_e = jnp.sum(pad_end[None, :] <= (jnp.arange(n_chunks) * MOE_ROWS)[:, None], axis=1)
    chunk_e = jnp.minimum(chunk_e, N_EXPERTS - 1).astype(I32)
    n_used = (pad_end[-1] // MOE_ROWS).astype(I32).reshape(1)
    tm = _row_tile(n, 256)
    xs = _dispatch(dest, h_tiles.reshape(n, ROW_TILE, LANES), slots, tm)
    ys = _moe_experts(chunk_e, n_used, xs.reshape(slots * ROW_TILE, LANES), w_up, b_up, w_down, b_down)
    return _combine(dest, x1, top_w, ys.reshape(slots, ROW_TILE, LANES), g, tm, normalize)


def _row_tile(rows, cap):
    tm = cap
    while rows % tm:
        tm //= 2
    return tm


def kernel(x_prompt, x_sample, cache_cmp_kv, cache_sel_kv, state_win_kv, state_rwkv, state_rwkv_shift, page_table,
           norm_attn, w_in, mu_shift, w0, w_decay_up, a0, w_iclr_up, w_gate_up, k_k, k_a, r_k, ln_x_w, ln_x_b,
           pe_cmp_k, pe_cmp_v, w_cmp_k1, w_cmp_k2, w_cmp_v1, w_cmp_v2, rel_bias, w_br_a, w_br_b, w_out,
           norm_ffn, w_router, b_router, w_up, b_up, w_down, b_down, norm_final):
    bp, tp, _ = x_prompt.shape
    bs, ts, _ = x_sample.shape
    depth = w_in.shape[0]
    past_len = page_table.shape[1] * cache_cmp_kv.shape[2]
    n_buf = state_win_kv.shape[2]
    assert ts == 1 and bp * A_HEADS == BH_LANES and bs % 8 == 0 and tp % LANES == 0
    xp = x_prompt.reshape(bp * tp, D_MODEL)
    xs = x_sample.reshape(bs, D_MODEL)
    new = {name: [] for name in ('cmp_p', 'sel_p', 'win_p', 'wkv_p', 'shift_p', 'cmp_s', 'sel_s', 'win_s', 'wkv_s', 'shift_s')}
    kv6 = lambda z, b, t: z.reshape(b, t, 2, B_KV_HEADS, HEAD_DIM)
    for l in range(depth):
        rw = (mu_shift[l], w0[l], w_decay_up[l], a0[l], w_iclr_up[l], w_gate_up[l], k_k[l], k_a[l], r_k[l])
        cw = (pe_cmp_k[l], pe_cmp_v[l], w_cmp_k1[l], w_cmp_k2[l], w_cmp_v1[l], w_cmp_v2[l])
        wa_nat, w_rest, w_t, w_n = _pack_w_in(w_in[l])
        g_attn = norm_attn[l].reshape(1, -1).astype(F32)
        last = l == depth - 1
        g_fin = norm_final if last else jnp.ones((D_MODEL,), F32)

        tq = _row_tile(tp, 256)
        pa_t, q_t, kvt_c, kvt_s, kvt_w, gt_t, kv_c, kv_s, kv_w, pg = _project_t(xp, g_attn, w_t, w_n, tq, tp)
        r, w, k2, v, kk, kb, gate, bonus = _rwkv_prep_t(pa_t, *rw, tq)
        r, w, k2, kk, kb = _key_tiles([z.reshape(N_JP, LANES, tp) for z in (r, w, k2, kk, kb)], LANES)
        v = _val_tiles(v.reshape(HEAD_DIM, BH_LANES, tp), LANES).reshape(1, tp, HEAD_DIM, LANES)
        s0 = jnp.zeros((1, N_IO, N_JP, 8, LANES), F32)
        yn, s_fin = _wkv_scan(r, w, k2, v, kk, kb, s0, _row_tile(tp, 64))
        yn = _val_untile(yn.reshape(tp * HEAD_DIM, LANES), LANES).reshape(HEAD_DIM, bp, A_HEADS, tp)
        ckv = _compress(kv_c, *cw, rows=_row_tile(tp, 2048))
        y_b = _nsa_prompt(q_t, gt_t, ckv, kv_s, kvt_s, kv_w, kvt_w, rel_bias)
        x1, h, top_e, top_w = _merge(xp, yn, bonus, gate, y_b, pg, ln_x_w[l], ln_x_b[l], w_br_a[l], w_br_b[l], w_out[l],
                                     norm_ffn[l], w_router[l], b_router[l], tq, seq=tp)
        xp_next = _moe_residual(x1, h, top_e, top_w, w_up[l], b_up[l], w_down[l], b_down[l], g_fin, last)
        n_win = min(WINDOW, tp)
        kv6t = lambda z: jnp.moveaxis(z.reshape(bp, 2, B_KV_HEADS, HEAD_DIM, tp), -1, 1)
        new['cmp_p'].append(kv6t(kvt_c))
        new['sel_p'].append(kv6t(kvt_s))
        new['win_p'].append(kv6t(kvt_w)[:, tp - n_win:])
        new['wkv_p'].append(_state_from_tiles(s_fin, bp))
        new['shift_p'].append(jnp.zeros((bp, A_PROJ), F32).at[:, A_PERM].set(pa_t[:, :, tp - 1]))

        s_a, q, kv_c, kv_s, kv_w, gt, pg = _project(xs, g_attn, wa_nat, w_rest, _row_tile(bs, 256))
        r, w, k2, v, kk, kb, gate, bonus = _rwkv_prep(s_a, state_rwkv_shift[l], *rw, _row_tile(bs, 256))
        yn, wkv_s = _wkv_step(r, w, k2, v, kk, kb, jnp.transpose(state_rwkv[l].astype(F32), (1, 2, 3, 0)))
        wkv_s = jnp.transpose(wkv_s, (3, 0, 1, 2))
        tok_minor = lambda z: jnp.moveaxis(z, 1, -1).reshape(z.shape[0], KV_ROW, z.shape[1])
        ckv = _compress_paged(tok_minor(cache_cmp_kv[l]), page_table, cw)
        q3, gt3 = q.reshape(bs, 1, B_WIDTH), gt.reshape(bs, 1, GATE_PAD)
        oc3, idx = _nsa_sample_select(q3, gt3, ckv, rel_bias, past_len)
        y_b = _nsa_sample_attend(idx, page_table, q3, gt3, oc3, kv_s.reshape(bs, KV_ROW, 1), kv_w.reshape(bs, 1, KV_ROW),
                                 tok_minor(state_win_kv[l]), tok_minor(cache_sel_kv[l]),
                                 rel_bias, past_len).reshape(bs, B_WIDTH)
        x1, h, top_e, top_w = _merge(xs, yn, bonus, gate, y_b, pg, ln_x_w[l], ln_x_b[l], w_br_a[l], w_br_b[l], w_out[l],
                                     norm_ffn[l], w_router[l], b_router[l], _row_tile(bs, 256))
        xs_next = _moe_residual(x1, h, top_e, top_w, w_up[l], b_up[l], w_down[l], b_down[l], g_fin, last)
        new['cmp_s'].append(kv6(kv_c, bs, 1))
        new['sel_s'].append(kv6(kv_s, bs, 1))
        new['win_s'].append(jnp.concatenate([state_win_kv[l], kv6(kv_w, bs, 1)], axis=1)[:, 1:])
        new['wkv_s'].append(wkv_s.astype(state_rwkv.dtype))
        new['shift_s'].append(s_a)
        xp, xs = xp_next, xs_next
    return (xp.reshape(bp, tp, D_MODEL), xs.reshape(bs, ts, D_MODEL),
            jnp.stack(new['cmp_p']), jnp.stack(new['sel_p']), jnp.stack(new['win_p']),
            jnp.stack(new['wkv_p']), jnp.stack(new['shift_p']),
            jnp.stack(new['cmp_s']), jnp.stack(new['sel_s']), jnp.stack(new['win_s']),
            jnp.stack(new['wkv_s']), jnp.stack(new['shift_s']))
```

```python
import functools
import math

import jax
import jax.numpy as jnp
import numpy as np
from jax import lax
from jax.experimental import pallas as pl
from jax.experimental.pallas import tpu as pltpu

F32 = jnp.float32
BF16 = jnp.bfloat16
I32 = jnp.int32
HIGHEST = lax.Precision.HIGHEST

D_MODEL = 1024
HEAD_DIM = 64
A_HEADS = 8
A_WIDTH = A_HEADS * HEAD_DIM
A_DECAY_RANK = 64
A_ICLR_RANK = 64
A_GATE_RANK = 128
A_GN_EPS = 64e-5
A_PROJ = 3 * A_WIDTH + A_DECAY_RANK + A_ICLR_RANK + A_GATE_RANK
B_HEADS = 8
B_KV_HEADS = 2
B_GROUP = B_HEADS // B_KV_HEADS
B_WIDTH = B_HEADS * HEAD_DIM
KV_WIDTH = B_KV_HEADS * HEAD_DIM
KV_ROW = 2 * KV_WIDTH
CMP_BLOCK = 32
CMP_HIDDEN = 128
SEL_BLOCK = 64
SEL_RATIO = SEL_BLOCK // CMP_BLOCK
TOP_N = 16
WINDOW = 512
Q_BLOCK = 128
FORCE_SCORE = 1e4
N_BUCKETS = 32
MAX_EXACT = N_BUCKETS // 2
MAX_DISTANCE = 1024
N_EXPERTS = 32
TOP_K = 4
D_FF = 1024
SWIGLU_ALPHA = 1.702
SWIGLU_LIMIT = 7.0
NORM_EPS = 1e-6
NEG_INF = -1e30
SCALE = HEAD_DIM ** -0.5
GATE_PAD = 128
GATE_PROJ = 2 * D_MODEL
MOE_ROWS = 256
LANES = 128
VMEM_LIMIT = 56 * 1024 * 1024


def _cparams(sem):
    return pltpu.CompilerParams(dimension_semantics=sem, vmem_limit_bytes=VMEM_LIMIT)


def _full(shape):
    n = len(shape)
    return pl.BlockSpec(shape, lambda *_: (0,) * n)


def _dot(a, b):
    return jnp.dot(a, b, preferred_element_type=F32)


def _dot_nt(a, b):
    return lax.dot_general(a, b, (((1,), (1,)), ((), ())), preferred_element_type=F32)


def _bf(x):
    return x.astype(BF16)


PROJ_SPLITS = (B_WIDTH, KV_ROW, KV_ROW, KV_ROW, GATE_PAD, GATE_PROJ)
CH_MAJOR = (np.arange(A_HEADS)[None, :] * HEAD_DIM + np.arange(HEAD_DIM)[:, None]).reshape(-1)
A_PERM = np.concatenate([CH_MAJOR, A_WIDTH + CH_MAJOR, 2 * A_WIDTH + CH_MAJOR, np.arange(3 * A_WIDTH, A_PROJ)])


def _rms_bf16(x_ref, g_ref):
    x = x_ref[...]
    return _bf(x * lax.rsqrt(jnp.mean(x * x, axis=-1, keepdims=True) + NORM_EPS) * g_ref[...])


def _proj_kernel(x_ref, g_ref, wa_ref, w_ref, oa_ref, *o_refs):
    h = _rms_bf16(x_ref, g_ref)
    oa_ref[...] = _dot(h, wa_ref[...])
    c = 0
    for o_ref, n in zip(o_refs, PROJ_SPLITS):
        o_ref[...] = _dot(h, w_ref[:, c:c + n])
        c += n


def _project(x, g, wa, w, tm):
    rows = x.shape[0]
    splits = (A_PROJ,) + PROJ_SPLITS
    return pl.pallas_call(
        _proj_kernel,
        grid=(rows // tm,),
        in_specs=[pl.BlockSpec((tm, D_MODEL), lambda i: (i, 0)), _full((1, D_MODEL)), _full(wa.shape), _full(w.shape)],
        out_specs=[pl.BlockSpec((tm, n), lambda i: (i, 0)) for n in splits],
        out_shape=[jax.ShapeDtypeStruct((rows, n), F32) for n in splits],
        compiler_params=_cparams(("parallel",)),
        name="norm_proj",
    )(x, g, wa, w)


PROJ_T_SPLITS = (A_PROJ, B_WIDTH, KV_ROW, KV_ROW, KV_ROW, GATE_PAD)
PROJ_N_SPLITS = (KV_ROW, KV_ROW, KV_ROW, GATE_PROJ)


def _proj_t_kernel(x_ref, g_ref, wt_ref, wn_ref, *o_refs):
    h = _rms_bf16(x_ref, g_ref)
    t = _dot_nt(wt_ref[...], h)
    c = 0
    for o_ref, n in zip(o_refs[:len(PROJ_T_SPLITS)], PROJ_T_SPLITS):
        o_ref[0] = t[c:c + n]
        c += n
    c = 0
    for o_ref, n in zip(o_refs[len(PROJ_T_SPLITS):], PROJ_N_SPLITS):
        o_ref[...] = _dot(h, wn_ref[:, c:c + n])
        c += n


def _project_t(x, g, wt, wn, tm, seq):
    rows = x.shape[0]
    per = seq // tm
    return pl.pallas_call(
        _proj_t_kernel,
        grid=(rows // tm,),
        in_specs=[pl.BlockSpec((tm, D_MODEL), lambda i: (i, 0)), _full((1, D_MODEL)), _full(wt.shape), _full(wn.shape)],
        out_specs=[pl.BlockSpec((1, n, tm), lambda i: (i // per, 0, i % per)) for n in PROJ_T_SPLITS]
                  + [pl.BlockSpec((tm, n), lambda i: (i, 0)) for n in PROJ_N_SPLITS],
        out_shape=[jax.ShapeDtypeStruct((rows // seq, n, seq), F32) for n in PROJ_T_SPLITS]
                  + [jax.ShapeDtypeStruct((rows, n), F32) for n in PROJ_N_SPLITS],
        compiler_params=_cparams(("parallel",)),
        name="norm_proj_t",
    )(x, g, wt, wn)


def _pack_w_in(w_in):
    a, rest = w_in[:, :A_PROJ], w_in[:, A_PROJ:]
    q, kv, gt, pg = (rest[:, :B_WIDTH], rest[:, B_WIDTH:B_WIDTH + 3 * KV_ROW],
                     rest[:, B_WIDTH + 3 * KV_ROW:B_WIDTH + 3 * KV_ROW + 3 * B_HEADS],
                     rest[:, B_WIDTH + 3 * KV_ROW + 3 * B_HEADS:])
    gt = jnp.pad(gt, ((0, 0), (0, GATE_PAD - 3 * B_HEADS)))
    w_t = jnp.concatenate([a[:, A_PERM], q, kv, gt], axis=1).T
    return _bf(a), _bf(jnp.concatenate([q, kv, gt, pg], axis=1)), _bf(w_t), _bf(jnp.concatenate([kv, pg], axis=1))


def _softplus(z):
    return jnp.maximum(z, 0.0) + jnp.log1p(jnp.exp(-jnp.abs(z)))


def _rwkv_prep_kernel(p_ref, prev_ref, mu_ref, w0_ref, wdu_ref, a0_ref, wiu_ref, wgu_ref, kk_ref, ka_ref, rk_ref,
                      ones_ref, r_o, w_o, k_o, v_o, kk_o, kb_o, g_o, bo_o):
    p = p_ref[...]
    ps = p + mu_ref[...] * (prev_ref[...] - p)
    r = ps[:, 0:A_WIDTH]
    k = ps[:, A_WIDTH:2 * A_WIDTH]
    v = ps[:, 2 * A_WIDTH:3 * A_WIDTH]
    c = 3 * A_WIDTH
    xw = ps[:, c:c + A_DECAY_RANK]
    xa = ps[:, c + A_DECAY_RANK:c + A_DECAY_RANK + A_ICLR_RANK]
    xg = ps[:, c + A_DECAY_RANK + A_ICLR_RANK:]
    w_log = -_softplus(-(w0_ref[...] + _dot(_bf(jnp.tanh(xw)), wdu_ref[...]))) - 0.5
    decay = jnp.exp(-jnp.exp(w_log))
    a = jax.nn.sigmoid(a0_ref[...] + _dot(_bf(xa), wiu_ref[...]))
    gate = _dot(_bf(jax.nn.sigmoid(xg)), wgu_ref[...])
    ones = ones_ref[...]
    kk = k * kk_ref[...]
    ss = jnp.dot(kk * kk, ones, precision=HIGHEST, preferred_element_type=F32)
    kk = kk / jnp.maximum(jnp.sqrt(ss), 1e-12)
    k2 = k * (1.0 + (a - 1.0) * ka_ref[...])
    rk = jnp.dot(r * k2 * rk_ref[...], ones, precision=HIGHEST, preferred_element_type=F32)
    r_o[...] = r
    w_o[...] = decay
    k_o[...] = k2
    v_o[...] = v
    kk_o[...] = kk
    kb_o[...] = kk * a
    g_o[...] = gate
    bo_o[...] = rk * v


def _rwkv_prep(p, prev, mu, w0, wdu, a0, wiu, wgu, k_k, k_a, r_k, tm):
    rows = p.shape[0]
    head = np.arange(A_WIDTH) // HEAD_DIM
    ones = jnp.asarray(head[:, None] == head[None, :], F32)
    row = lambda z: z.reshape(1, -1).astype(F32)
    spec_in = pl.BlockSpec((tm, A_PROJ), lambda i: (i, 0))
    spec_o = pl.BlockSpec((tm, A_WIDTH), lambda i: (i, 0))
    return pl.pallas_call(
        _rwkv_prep_kernel,
        grid=(rows // tm,),
        in_specs=[spec_in, spec_in, _full((1, A_PROJ)), _full((1, A_WIDTH)), _full((A_DECAY_RANK, A_WIDTH)),
                  _full((1, A_WIDTH)), _full((A_ICLR_RANK, A_WIDTH)), _full((A_GATE_RANK, A_WIDTH)),
                  _full((1, A_WIDTH)), _full((1, A_WIDTH)), _full((1, A_WIDTH)), _full((A_WIDTH, A_WIDTH))],
        out_specs=[spec_o] * 8,
        out_shape=[jax.ShapeDtypeStruct((rows, A_WIDTH), F32)] * 8,
        compiler_params=_cparams(("parallel",)),
        name="rwkv_prep",
    )(p, prev, row(mu), row(w0), _bf(wdu), row(a0), _bf(wiu), _bf(wgu), row(k_k), row(k_a), row(r_k), ones)


def _rwkv_prep_t_kernel(p_ref, mu_ref, w0_ref, wdu_ref, a0_ref, wiu_ref, wgu_ref, kk_ref, ka_ref, rk_ref,
                        r_o, w_o, k_o, v_o, kk_o, kb_o, g_o, bo_o, carry):
    tt = p_ref.shape[2]

    @pl.when(pl.program_id(1) == 0)
    def _():
        carry[...] = jnp.zeros_like(carry)

    p = p_ref[0]
    lane = lax.broadcasted_iota(I32, (1, tt), 1)
    prev = jnp.where(lane == 0, carry[:, 0:1], pltpu.roll(p, 1, 1))
    carry[:, 0:1] = p[:, tt - 1:tt]
    ps = p + mu_ref[...] * (prev - p)
    r = ps[0:A_WIDTH]
    k = ps[A_WIDTH:2 * A_WIDTH]
    v = ps[2 * A_WIDTH:3 * A_WIDTH]
    c = 3 * A_WIDTH
    xw = ps[c:c + A_DECAY_RANK]
    xa = ps[c + A_DECAY_RANK:c + A_DECAY_RANK + A_ICLR_RANK]
    xg = ps[c + A_DECAY_RANK + A_ICLR_RANK:]

    def head_sum(x):
        s = jnp.sum(x.reshape(HEAD_DIM, A_HEADS, tt), axis=0)
        return jnp.broadcast_to(s[None], (HEAD_DIM, A_HEADS, tt)).reshape(A_WIDTH, tt)

    w_log = -_softplus(-(w0_ref[...] + _dot(wdu_ref[...], _bf(jnp.tanh(xw))))) - 0.5
    decay = jnp.exp(-jnp.exp(w_log))
    a = jax.nn.sigmoid(a0_ref[...] + _dot(wiu_ref[...], _bf(xa)))
    gate = _dot(wgu_ref[...], _bf(jax.nn.sigmoid(xg)))
    kk = k * kk_ref[...]
    kk = kk / jnp.maximum(jnp.sqrt(head_sum(kk * kk)), 1e-12)
    k2 = k * (1.0 + (a - 1.0) * ka_ref[...])
    rk = head_sum(r * k2 * rk_ref[...])
    for o_ref, val in ((r_o, r), (w_o, decay), (k_o, k2), (kk_o, kk), (kb_o, kk * a)):
        o_ref[...] = val.reshape(N_JP, 2, 1, A_HEADS, tt)
    v_o[...] = v.reshape(HEAD_DIM, 1, A_HEADS, tt)
    g_o[0] = gate
    bo_o[0] = rk * v


def _rwkv_prep_t(p_t, mu, w0, wdu, a0, wiu, wgu, k_k, k_a, r_k, tt):
    b, _, t = p_t.shape
    col = lambda z, perm: z.reshape(-1)[perm].reshape(-1, 1).astype(F32)
    up = lambda w: _bf(w[:, CH_MAJOR].T)
    kshape = jax.ShapeDtypeStruct((N_JP, 2, b, A_HEADS, t), F32)
    kspec = pl.BlockSpec((N_JP, 2, 1, A_HEADS, tt), lambda bi, i: (0, 0, bi, 0, i))
    vspec = pl.BlockSpec((HEAD_DIM, 1, A_HEADS, tt), lambda bi, i: (0, bi, 0, i))
    nspec = pl.BlockSpec((1, A_WIDTH, tt), lambda bi, i: (bi, 0, i))
    nshape = jax.ShapeDtypeStruct((b, A_WIDTH, t), F32)
    return pl.pallas_call(
        _rwkv_prep_t_kernel,
        grid=(b, t // tt),
        in_specs=[pl.BlockSpec((1, A_PROJ, tt), lambda bi, i: (bi, 0, i)), _full((A_PROJ, 1)), _full((A_WIDTH, 1)),
                  _full((A_WIDTH, A_DECAY_RANK)), _full((A_WIDTH, 1)), _full((A_WIDTH, A_ICLR_RANK)),
                  _full((A_WIDTH, A_GATE_RANK)), _full((A_WIDTH, 1)), _full((A_WIDTH, 1)), _full((A_WIDTH, 1))],
        out_specs=[kspec, kspec, kspec, vspec, kspec, kspec, nspec, nspec],
        out_shape=[kshape, kshape, kshape, jax.ShapeDtypeStruct((HEAD_DIM, b, A_HEADS, t), F32), kshape, kshape,
                   nshape, nshape],
        scratch_shapes=[pltpu.VMEM((A_PROJ, LANES), F32)],
        compiler_params=_cparams(("parallel", "arbitrary")),
        name="rwkv_prep_t",
    )(p_t, col(mu, A_PERM), col(w0, CH_MAJOR), up(wdu), col(a0, CH_MAJOR), up(wiu), up(wgu),
      col(k_k, CH_MAJOR), col(k_a, CH_MAJOR), col(r_k, CH_MAJOR))


def _key_tiles_kernel(*refs):
    n = len(refs) // 2
    for z_ref, o_ref in zip(refs[:n], refs[n:]):
        for jp in range(N_JP):
            o_ref[0, jp] = z_ref[jp].T


def _key_tiles(zs, tt):
    t = zs[0].shape[2]
    return pl.pallas_call(
        _key_tiles_kernel,
        grid=(t // tt,),
        in_specs=[pl.BlockSpec((N_JP, LANES, tt), lambda i: (0, 0, i))] * len(zs),
        out_specs=[pl.BlockSpec((1, N_JP, tt, LANES), lambda i: (0, 0, i, 0))] * len(zs),
        out_shape=[jax.ShapeDtypeStruct((1, N_JP, t, LANES), F32)] * len(zs),
        compiler_params=_cparams(("parallel",)),
        name="wkv_key_tiles",
    )(*zs)


def _val_tiles_kernel(z_ref, o_ref, *, tt):
    for i in range(HEAD_DIM):
        x = z_ref[i]
        o_ref[pl.ds(i, tt, stride=HEAD_DIM), :] = jnp.concatenate([x, x], axis=0).T


def _val_tiles(z, tt):
    t = z.shape[2]
    return pl.pallas_call(
        functools.partial(_val_tiles_kernel, tt=tt),
        grid=(t // tt,),
        in_specs=[pl.BlockSpec((HEAD_DIM, BH_LANES, tt), lambda i: (0, 0, i))],
        out_specs=pl.BlockSpec((tt * HEAD_DIM, LANES), lambda i: (i, 0)),
        out_shape=jax.ShapeDtypeStruct((t * HEAD_DIM, LANES), F32),
        compiler_params=_cparams(("parallel",)),
        name="wkv_val_tiles",
    )(z)


def _val_untile_kernel(y_ref, o_ref, *, tt):
    for i in range(HEAD_DIM):
        o_ref[i] = y_ref[pl.ds(i, tt, stride=HEAD_DIM), :].T[0:BH_LANES]


def _val_untile(y, tt):
    t = y.shape[0] // HEAD_DIM
    return pl.pallas_call(
        functools.partial(_val_untile_kernel, tt=tt),
        grid=(t // tt,),
        in_specs=[pl.BlockSpec((tt * HEAD_DIM, LANES), lambda i: (i, 0))],
        out_specs=pl.BlockSpec((HEAD_DIM, BH_LANES, tt), lambda i: (0, 0, i)),
        out_shape=jax.ShapeDtypeStruct((HEAD_DIM, BH_LANES, t), F32),
        compiler_params=_cparams(("parallel",)),
        name="wkv_val_untile",
    )(y)


N_IO = HEAD_DIM // 8
N_JP = HEAD_DIM // 2
BH_LANES = 64


def _wkv_kernel(r_ref, w_ref, k_ref, v_ref, a_ref, b_ref, s0_ref, y_ref, sfin_ref, s_scr, *, tc):
    t_blk = pl.program_id(1)

    @pl.when(t_blk == 0)
    def _():
        s_scr[...] = s0_ref[0]

    def bc(ref, t, jp):
        return jnp.broadcast_to(ref[0, jp, pl.ds(t, 1), :], (8, LANES))

    def fold(x):
        return x + pltpu.roll(x, BH_LANES, 1)

    def state_times_a(t):
        acc = [jnp.zeros((8, LANES), F32) for _ in range(N_IO)]
        for jp in range(N_JP):
            a_ = bc(a_ref, t, jp)
            for io in range(N_IO):
                acc[io] = acc[io] + s_scr[io, jp] * a_
        return tuple(-fold(acc[io]) for io in range(N_IO))

    def step(t, sa):
        t_next = jnp.minimum(t + 1, tc - 1)
        vv = [v_ref[0, t, io * 8:(io + 1) * 8, :] for io in range(N_IO)]
        acc = [jnp.zeros((8, LANES), F32) for _ in range(N_IO)]
        yacc = [jnp.zeros((8, LANES), F32) for _ in range(N_IO)]
        for half in range(2):
            for jp in range(N_JP):
                w_, b_, k_, r_, a_ = (bc(ref, tt, jp) for ref, tt in
                                      ((w_ref, t), (b_ref, t), (k_ref, t), (r_ref, t), (a_ref, t_next)))
                for io in range(half * N_IO // 2, (half + 1) * N_IO // 2):
                    s = s_scr[io, jp] * w_ + sa[io] * b_ + vv[io] * k_
                    s_scr[io, jp] = s
                    yacc[io] = yacc[io] + s * r_
                    acc[io] = acc[io] + s * a_
        sa_next = [-fold(acc[io]) for io in range(N_IO)]
        y = [fold(yacc[io]) for io in range(N_IO)]
        tot = y[0]
        for io in range(1, N_IO):
            tot = tot + y[io]
        mu = jnp.sum(tot, axis=0, keepdims=True) * (1.0 / HEAD_DIM)
        d = [y[io] - mu for io in range(N_IO)]
        sq = d[0] * d[0]
        for io in range(1, N_IO):
            sq = sq + d[io] * d[io]
        var = jnp.sum(sq, axis=0, keepdims=True) * (1.0 / HEAD_DIM)
        inv = lax.rsqrt(var + A_GN_EPS)
        for io in range(N_IO):
            y_ref[0, t, io * 8:(io + 1) * 8, :] = d[io] * inv
        return tuple(sa_next)

    lax.fori_loop(0, tc, step, state_times_a(0))

    @pl.when(t_blk == pl.num_programs(1) - 1)
    def _():
        sfin_ref[0] = s_scr[...]


def _wkv_scan(r, w, k, v, a, b, s0, tc):
    nb, t = v.shape[:2]
    kspec = pl.BlockSpec((1, N_JP, tc, LANES), lambda n, i: (n, 0, i, 0))
    vspec = pl.BlockSpec((1, tc, HEAD_DIM, LANES), lambda n, i: (n, i, 0, 0))
    sspec = pl.BlockSpec((1, N_IO, N_JP, 8, LANES), lambda n, i: (n, 0, 0, 0, 0))
    return pl.pallas_call(
        functools.partial(_wkv_kernel, tc=tc),
        grid=(nb, t // tc),
        in_specs=[kspec, kspec, kspec, vspec, kspec, kspec, sspec],
        out_specs=[vspec, sspec],
        out_shape=[jax.ShapeDtypeStruct((nb, t, HEAD_DIM, LANES), F32),
                   jax.ShapeDtypeStruct((nb, N_IO, N_JP, 8, LANES), F32)],
        scratch_shapes=[pltpu.VMEM((N_IO, N_JP, 8, LANES), F32)],
        compiler_params=_cparams(("parallel", "arbitrary")),
        name="wkv_scan",
    )(r, w, k, v, a, b, s0)


def _wkv_step_kernel(r_ref, w_ref, k_ref, v_ref, a_ref, b_ref, s_ref, y_ref, so_ref):
    rt, wt, kt, vt, at, bt = (ref[...].T for ref in (r_ref, w_ref, k_ref, v_ref, a_ref, b_ref))
    outs = []
    for hh in range(2):
        hs = slice(hh * HEAD_DIM, (hh + 1) * HEAD_DIM)
        r_, w_, k_, a_, b_ = rt[hs], wt[hs], kt[hs], at[hs], bt[hs]
        ys = []
        for i in range(HEAD_DIM):
            s = s_ref[hh, i]
            sa = -jnp.sum(s * a_, axis=0, keepdims=True)
            s = s * w_ + sa * b_ + vt[hh * HEAD_DIM + i:hh * HEAD_DIM + i + 1] * k_
            so_ref[hh, i] = s
            ys.append(jnp.sum(s * r_, axis=0, keepdims=True))
        y = jnp.concatenate(ys, axis=0)
        d = y - jnp.mean(y, axis=0, keepdims=True)
        outs.append(d * lax.rsqrt(jnp.mean(d * d, axis=0, keepdims=True) + A_GN_EPS))
    y_ref[...] = jnp.concatenate(outs, axis=0).T


def _wkv_step(r, w, k, v, a, b, state):
    s = r.shape[0]
    vec = pl.BlockSpec((s, 2 * HEAD_DIM), lambda h: (0, h))
    st = pl.BlockSpec((2, HEAD_DIM, HEAD_DIM, s), lambda h: (h, 0, 0, 0))
    return pl.pallas_call(
        _wkv_step_kernel,
        grid=(A_HEADS // 2,),
        in_specs=[vec] * 6 + [st],
        out_specs=[vec, st],
        out_shape=[jax.ShapeDtypeStruct((s, A_WIDTH), F32), jax.ShapeDtypeStruct(state.shape, F32)],
        compiler_params=_cparams(("parallel",)),
        name="wkv_step",
    )(r, w, k, v, a, b, state)


def _state_from_tiles(s, b):
    nb = b // 8
    s = s.reshape(nb, N_IO, N_JP, 8, 2, 8, A_HEADS).transpose(0, 5, 6, 1, 3, 2, 4)
    return s.reshape(b, A_HEADS, HEAD_DIM, HEAD_DIM)


def _compress_paged_kernel(pt_ref, pe_ref, w1_ref, w2_ref, cache_ref, o_ref, buf, xk, xv, sem, *, n_pages, page_len):
    i = pl.program_id(0)
    slot = i % 2

    def copy(s_idx, sl, p):
        return pltpu.make_async_copy(cache_ref.at[pt_ref[s_idx * n_pages + p]], buf.at[sl, p], sem.at[sl, p])

    def fetch(s_idx, sl):
        for p in range(n_pages):
            copy(s_idx, sl, p).start()

    @pl.when(i == 0)
    def _():
        fetch(0, 0)

    @pl.when(i + 1 < pl.num_programs(0))
    def _():
        fetch(i + 1, 1 - slot)

    for p in range(n_pages):
        copy(i, slot, p).wait()
    for p in range(n_pages):
        t = buf[slot, p].T
        for m in range(page_len // CMP_BLOCK):
            row = (p * (page_len // CMP_BLOCK) + m) * CMP_PITCH
            xk[row:row + CMP_BLOCK, :] = t[m * CMP_BLOCK:(m + 1) * CMP_BLOCK, :KV_WIDTH]
            xv[row:row + CMP_BLOCK, :] = t[m * CMP_BLOCK:(m + 1) * CMP_BLOCK, KV_WIDTH:]
    _compress_kernel(xk, xv, pe_ref, w1_ref, w2_ref, o_ref, nblk=n_pages * page_len // CMP_BLOCK, pitch=CMP_PITCH)


CMP_PITCH = CMP_BLOCK + 1


def _compress_kernel(xk_ref, xv_ref, pe_ref, w1_ref, w2_ref, o_ref, *, nblk, pitch=CMP_BLOCK):
    for c, x_ref in enumerate((xk_ref, xv_ref)):
        acc = jnp.zeros((nblk, B_KV_HEADS * CMP_HIDDEN), F32)
        for tau in range(0, CMP_BLOCK, 2):
            x = jnp.concatenate([x_ref[pl.ds(tau + u, nblk, stride=pitch), :] + pe_ref[c, tau + u:tau + u + 1, :]
                                 for u in range(2)], axis=1)
            acc = acc + _dot(_bf(x), w1_ref[c, tau // 2])
        o_ref[:, c * KV_WIDTH:(c + 1) * KV_WIDTH] = _dot(_bf(jax.nn.gelu(acc)), w2_ref[c])


def _compress_weights(pe_k, pe_v, w_k1, w_k2, w_v1, w_v2):
    pe = jnp.stack([jnp.concatenate([pe_k, pe_k], axis=1), jnp.concatenate([pe_v, pe_v], axis=1)]).astype(F32)
    eye = jnp.eye(B_KV_HEADS, dtype=F32)
    w1 = jnp.stack([w_k1, w_v1]).reshape(2, CMP_BLOCK, HEAD_DIM, CMP_HIDDEN)
    w1 = jnp.einsum('ctdh,ge->ctgdeh', w1, eye).reshape(2, CMP_BLOCK // 2, 2 * KV_WIDTH, B_KV_HEADS * CMP_HIDDEN)
    w2 = jnp.einsum('chd,ge->cghed', jnp.stack([w_k2, w_v2]), eye).reshape(2, B_KV_HEADS * CMP_HIDDEN, KV_WIDTH)
    return pe, _bf(w1), _bf(w2)


COMPRESS_W_SPECS = ((2, CMP_BLOCK, KV_WIDTH), (2, CMP_BLOCK // 2, 2 * KV_WIDTH, B_KV_HEADS * CMP_HIDDEN),
                    (2, B_KV_HEADS * CMP_HIDDEN, KV_WIDTH))


def _compress_paged(cache_t, page_table, cw):
    s, n_pages = page_table.shape
    page = cache_t.shape[2]
    past = n_pages * page
    nblk = past // CMP_BLOCK
    grid_spec = pltpu.PrefetchScalarGridSpec(
        num_scalar_prefetch=1,
        grid=(s,),
        in_specs=[pl.BlockSpec(shp, lambda i, pt, n=len(shp): (0,) * n) for shp in COMPRESS_W_SPECS]
                 + [pl.BlockSpec(memory_space=pl.ANY)],
        out_specs=pl.BlockSpec((nblk, KV_ROW), lambda i, pt: (i, 0)),
        scratch_shapes=[pltpu.VMEM((2, n_pages, KV_ROW, page), F32), pltpu.VMEM((nblk * CMP_PITCH, KV_WIDTH), F32),
                        pltpu.VMEM((nblk * CMP_PITCH, KV_WIDTH), F32), pltpu.SemaphoreType.DMA((2, n_pages))],
    )
    return pl.pallas_call(
        functools.partial(_compress_paged_kernel, n_pages=n_pages, page_len=page),
        grid_spec=grid_spec,
        out_shape=jax.ShapeDtypeStruct((s * nblk, KV_ROW), F32),
        compiler_params=_cparams(("arbitrary",)),
        name="nsa_compress_paged",
    )(page_table.reshape(-1), *_compress_weights(*cw), cache_t)


def _compress(x, pe_k, pe_v, w_k1, w_k2, w_v1, w_v2, rows):
    n = x.shape[0]
    nblk = rows // CMP_BLOCK
    pe, w1, w2 = _compress_weights(pe_k, pe_v, w_k1, w_k2, w_v1, w_v2)
    return pl.pallas_call(
        functools.partial(_compress_kernel, nblk=nblk),
        grid=(n // rows,),
        in_specs=[pl.BlockSpec((rows, KV_WIDTH), lambda i: (i, 0)), pl.BlockSpec((rows, KV_WIDTH), lambda i: (i, 1)),
                  ] + [_full(shp) for shp in COMPRESS_W_SPECS],
        out_specs=pl.BlockSpec((nblk, KV_ROW), lambda i: (i, 0)),
        out_shape=jax.ShapeDtypeStruct((n // CMP_BLOCK, KV_ROW), F32),
        compiler_params=_cparams(("parallel",)),
        name="nsa_compress",
    )(x, x, pe, w1, w2)


def _rel_bucket(dist):
    n = jnp.maximum(dist, 0)
    log_ratio = jnp.log(jnp.maximum(n, 1).astype(F32) / MAX_EXACT) / math.log(MAX_DISTANCE / MAX_EXACT)
    large = jnp.minimum(MAX_EXACT + (log_ratio * (N_BUCKETS - MAX_EXACT)).astype(I32), N_BUCKETS - 1)
    return jnp.where(n < MAX_EXACT, n, large)


def _bias_of(rel_bias, dist):
    onehot = (_rel_bucket(dist)[..., None] == jnp.arange(N_BUCKETS)).astype(F32)
    return jnp.einsum('...b,bh->h...', onehot, rel_bias.astype(F32), precision=HIGHEST)


KEY_TILE = 512


def _nsa_prompt_kernel(q_ref, gate_ref, ck_ref, ckt_ref, ksel_ref, vsel_ref, kwin_ref, vwin_ref, bct_ref, tz_ref,
                       o_ref, imp_scr, *, nc, ns, nq, top_n, wtiles):
    i = pl.program_id(1)
    qb = Q_BLOCK
    gates = jax.nn.sigmoid(gate_ref[0])
    pos_l = i * qb + lax.broadcasted_iota(I32, (1, qb), 1)
    okT = pos_l >= lax.broadcasted_iota(I32, (nc, qb), 0) * CMP_BLOCK + (CMP_BLOCK - 1)
    blk = lax.broadcasted_iota(I32, (ns, qb), 0)
    cur = pos_l // SEL_BLOCK
    forced = (blk == 0) | (blk == cur) | (blk == cur - 1)
    k_t = lax.broadcasted_iota(I32, (KEY_TILE, qb), 0)
    n_e = lax.broadcasted_iota(I32, (KEY_TILE, ns), 1)
    k_e = lax.broadcasted_iota(I32, (KEY_TILE, ns), 0)
    wk = (wtiles + 1) * qb
    wb = jnp.maximum(i - wtiles, 0)
    d_w = pos_l - (wb * qb + lax.broadcasted_iota(I32, (wk, qb), 0))
    madd_w = jnp.where((d_w >= 0) & (d_w < WINDOW), 0.0, NEG_INF)
    n_tiles = (i * qb + qb + KEY_TILE - 1) // KEY_TILE
    per = KEY_TILE // qb

    def bias_tiles(h, first_blk, count):
        return jnp.concatenate([tz_ref[h, jnp.clip(i - (first_blk + c), 0, nq - 1)] for c in range(count)], axis=0)

    ksl = [slice(g * HEAD_DIM, (g + 1) * HEAD_DIM) for g in range(B_KV_HEADS)]
    vsl = [slice(KV_WIDTH + g * HEAD_DIM, KV_WIDTH + (g + 1) * HEAD_DIM) for g in range(B_KV_HEADS)]
    qh = [_bf(q_ref[0, h * HEAD_DIM:(h + 1) * HEAD_DIM, :] * SCALE) for h in range(B_HEADS)]
    o_c, sel_t = [], []
    for g in range(B_KV_HEADS):
        kc = _bf(ck_ref[:, ksl[g]])
        vct = _bf(ckt_ref[0, vsl[g], :])
        impT = jnp.zeros((nc, qb), F32)
        for h in range(g * B_GROUP, (g + 1) * B_GROUP):
            sT = jnp.where(okT, _dot(kc, qh[h]) + bct_ref[h], NEG_INF)
            eT = jnp.exp(sT - jnp.max(sT, axis=0, keepdims=True))
            pT = eT / jnp.sum(eT, axis=0, keepdims=True) * okT.astype(F32)
            impT = impT + pT
            o_c.append(_dot(vct, _bf(pT)))
        imp_scr[...] = impT
        imp2 = imp_scr[pl.ds(0, ns, stride=SEL_RATIO), :] + imp_scr[pl.ds(1, ns, stride=SEL_RATIO), :]
        score = jnp.where(blk <= cur, imp2 + FORCE_SCORE * forced.astype(F32), NEG_INF)
        rank = jnp.zeros((ns, qb), I32)
        for m in range(ns):
            row = score[m:m + 1, :]
            rank = rank + ((row > score) | ((row == score) & (m < blk))).astype(I32)
        sel_t.append(_bf((rank < top_n) & (score > NEG_INF / 2)))

    def body(jt, carry):
        ms, ls, accs = carry
        k0 = pl.multiple_of(jt * KEY_TILE, KEY_TILE)
        causal = k0 + k_t <= pos_l
        in_blk = _bf(n_e == (k0 + k_e) // SEL_BLOCK)
        ms2, ls2, accs2 = [], [], []
        for g in range(B_KV_HEADS):
            kt = _bf(ksel_ref[pl.ds(k0, KEY_TILE), ksl[g]])
            vt = _bf(vsel_ref[0, vsl[g], pl.ds(k0, KEY_TILE)])
            madd = jnp.where((_dot(in_blk, sel_t[g]) > 0.5) & causal, 0.0, NEG_INF)
            for h in range(g * B_GROUP, (g + 1) * B_GROUP):
                s = _dot(kt, qh[h]) + bias_tiles(h, jt * per, per) + madd
                m_new = jnp.maximum(ms[h], jnp.max(s, axis=0, keepdims=True))
                alpha = jnp.exp(ms[h] - m_new)
                p = jnp.exp(s - m_new)
                ls2.append(alpha * ls[h] + jnp.sum(p, axis=0, keepdims=True))
                accs2.append(alpha * accs[h] + _dot(vt, _bf(p)))
                ms2.append(m_new)
        return tuple(ms2), tuple(ls2), tuple(accs2)

    init = (tuple(jnp.full((1, qb), NEG_INF, F32) for _ in range(B_HEADS)),
            tuple(jnp.zeros((1, qb), F32) for _ in range(B_HEADS)),
            tuple(jnp.zeros((HEAD_DIM, qb), F32) for _ in range(B_HEADS)))
    _, ls, accs = lax.fori_loop(0, n_tiles, body, init)
    w0 = pl.multiple_of(wb * qb, qb)
    for g in range(B_KV_HEADS):
        ktw = _bf(kwin_ref[pl.ds(w0, wk), ksl[g]])
        vtw = _bf(vwin_ref[0, vsl[g], pl.ds(w0, wk)])
        for h in range(g * B_GROUP, (g + 1) * B_GROUP):
            s = _dot(ktw, qh[h]) + bias_tiles(h, wb, wtiles + 1) + madd_w
            p = jnp.exp(s - jnp.max(s, axis=0, keepdims=True))
            o_w = _dot(vtw, _bf(p)) / jnp.sum(p, axis=0, keepdims=True)
            o_s = accs[h] / ls[h]
            o_ref[0, h * HEAD_DIM:(h + 1) * HEAD_DIM, :] = (
                gates[3 * h:3 * h + 1] * o_c[h] + gates[3 * h + 1:3 * h + 2] * o_s + gates[3 * h + 2:3 * h + 3] * o_w)


def _nsa_prompt(q_t, gate_t, ckv, kv_sel, kvt_sel, kv_win, kvt_win, rel_bias):
    b, _, t = q_t.shape
    nq = t // Q_BLOCK
    nc = t // CMP_BLOCK
    ns = -(-t // SEL_BLOCK)
    wtiles = WINDOW // Q_BLOCK
    assert t % KEY_TILE == 0 and nc == SEL_RATIO * ns and WINDOW % Q_BLOCK == 0 and (wtiles + 1) * Q_BLOCK <= t
    cmp_end = jnp.arange(nc) * CMP_BLOCK + (CMP_BLOCK - 1)
    bct = _bias_of(rel_bias, jnp.arange(t)[None, :] - cmp_end[:, None])
    dz = (jnp.arange(nq)[:, None, None] * Q_BLOCK + jnp.arange(Q_BLOCK)[None, None, :]
          - jnp.arange(Q_BLOCK)[None, :, None])
    tz = _bias_of(rel_bias, dz)
    ckt = jnp.swapaxes(ckv.reshape(b, nc, KV_ROW), 1, 2)
    chan = lambda w: pl.BlockSpec((1, w, Q_BLOCK), lambda bi, i: (bi, 0, i))
    seq_rows = pl.BlockSpec((t, KV_ROW), lambda bi, i: (bi, 0))
    seq_chan = pl.BlockSpec((1, KV_ROW, t), lambda bi, i: (bi, 0, 0))
    return pl.pallas_call(
        functools.partial(_nsa_prompt_kernel, nc=nc, ns=ns, nq=nq, top_n=min(TOP_N, ns), wtiles=wtiles),
        grid=(b, nq),
        in_specs=[chan(B_WIDTH), chan(GATE_PAD),
                  pl.BlockSpec((nc, KV_ROW), lambda bi, i: (bi, 0)),
                  pl.BlockSpec((1, KV_ROW, nc), lambda bi, i: (bi, 0, 0)),
                  seq_rows, seq_chan, seq_rows, seq_chan,
                  pl.BlockSpec((B_HEADS, nc, Q_BLOCK), lambda bi, i: (0, 0, i)),
                  _full((B_HEADS, nq, Q_BLOCK, Q_BLOCK))],
        out_specs=chan(B_WIDTH),
        out_shape=jax.ShapeDtypeStruct((b, B_WIDTH, t), F32),
        scratch_shapes=[pltpu.VMEM((nc, Q_BLOCK), F32)],
        compiler_params=_cparams(("parallel", "arbitrary")),
        name="nsa_prompt",
    )(q_t, gate_t, ckv, ckt, kv_sel, kvt_sel, kv_win, kvt_win, bct, tz)


def _group_q(q_ref, g):
    rows = [q_ref[0, :, (g * B_GROUP + r) * HEAD_DIM:(g * B_GROUP + r + 1) * HEAD_DIM] for r in range(B_GROUP)]
    return _bf(jnp.concatenate(rows, axis=0) * SCALE)


def _nsa_sample_select_kernel(q_ref, gate_ref, ck_ref, bc_ref, pair_ref, tri_ref, oc_ref, idx_ref,
                              *, nsp, cur, top_n):
    gates = jax.nn.sigmoid(gate_ref[0])
    blk = lax.broadcasted_iota(I32, (1, nsp), 1)
    forced = (blk == 0) | (blk == cur) | (blk == cur - 1)
    mi = lax.broadcasted_iota(I32, (nsp, nsp), 0)
    ni = lax.broadcasted_iota(I32, (nsp, nsp), 1)
    kk = lax.broadcasted_iota(I32, (TOP_N, nsp), 0).astype(F32)
    nf = lax.broadcasted_iota(I32, (TOP_N, nsp), 1).astype(F32)
    outs = []
    for g in range(B_KV_HEADS):
        q4 = _group_q(q_ref, g)
        kc = _bf(ck_ref[:, g * HEAD_DIM:(g + 1) * HEAD_DIM])
        vc = _bf(ck_ref[:, KV_WIDTH + g * HEAD_DIM:KV_WIDTH + (g + 1) * HEAD_DIM])
        s = _dot_nt(q4, kc) + bc_ref[g * B_GROUP:(g + 1) * B_GROUP, :]
        e = jnp.exp(s - jnp.max(s, axis=-1, keepdims=True))
        p = e / jnp.sum(e, axis=-1, keepdims=True)
        o_c = _dot(_bf(p), vc)
        imp = ((p[0:1] + p[1:2]) + p[2:3]) + p[3:4]
        imp2 = jnp.dot(imp, pair_ref[...], precision=HIGHEST, preferred_element_type=F32)
        score = jnp.where(blk <= cur, imp2 + FORCE_SCORE * forced.astype(F32), NEG_INF)
        m1 = jnp.broadcast_to(score, (nsp, nsp))
        m2 = m1.T
        gt = (m2 > m1) | ((m2 == m1) & (mi < ni))
        rank = jnp.sum(gt.astype(F32), axis=0, keepdims=True)
        sel = (rank < top_n) & (score > NEG_INF / 2)
        before = _dot(_bf(sel), tri_ref[...])
        hit = jnp.broadcast_to(sel, (TOP_N, nsp)) & (jnp.broadcast_to(before, (TOP_N, nsp)) == kk)
        idx = jnp.sum(jnp.where(hit, nf, 0.0), axis=1, keepdims=True)
        cnt = jnp.sum(hit.astype(F32), axis=1, keepdims=True)
        idx_ref[0, g] = jnp.where(cnt > 0.5, idx, -1.0).astype(I32)
        for r in range(B_GROUP):
            h = g * B_GROUP + r
            outs.append(gates[:, 3 * h:3 * h + 1] * o_c[r:r + 1])
    oc_ref[0] = jnp.concatenate(outs, axis=1)


def _nsa_sample_select(q3, gate3, ckv, rel_bias, past_len):
    s = q3.shape[0]
    nc = past_len // CMP_BLOCK
    ns = -(-(past_len + 1) // SEL_BLOCK)
    nsp = -(-ns // LANES) * LANES
    cur = past_len // SEL_BLOCK
    bc = _bias_of(rel_bias, past_len - (jnp.arange(nc) * CMP_BLOCK + (CMP_BLOCK - 1)))
    pair = jnp.asarray(np.arange(nc)[:, None] // SEL_RATIO == np.arange(nsp)[None, :], F32)
    tri = jnp.asarray(np.arange(nsp)[:, None] < np.arange(nsp)[None, :], BF16)
    return pl.pallas_call(
        functools.partial(_nsa_sample_select_kernel, nsp=nsp, cur=cur, top_n=min(TOP_N, ns)),
        grid=(s,),
        in_specs=[pl.BlockSpec((1, 1, B_WIDTH), lambda i: (i, 0, 0)), pl.BlockSpec((1, 1, GATE_PAD), lambda i: (i, 0, 0)),
                  pl.BlockSpec((nc, KV_ROW), lambda i: (i, 0)), _full((B_HEADS, nc)), _full((nc, nsp)), _full((nsp, nsp))],
        out_specs=[pl.BlockSpec((1, 1, B_WIDTH), lambda i: (i, 0, 0)),
                   pl.BlockSpec((1, B_KV_HEADS, TOP_N, 1), lambda i: (i, 0, 0, 0))],
        out_shape=[jax.ShapeDtypeStruct((s, 1, B_WIDTH), F32), jax.ShapeDtypeStruct((s, B_KV_HEADS, TOP_N, 1), I32)],
        compiler_params=_cparams(("parallel",)),
        name="nsa_sample_select",
    )(q3, gate3, ckv, bc, pair, tri)


def _nsa_sample_attend_kernel(idx_ref, pt_ref, q_ref, gate_ref, oc_ref, ksel_ref, kwin_ref, win_ref, fpg_ref, bw_ref,
                              cache_ref, o_ref, buf, sem, *, cur, n_pages, past_len, page_len):
    i = pl.program_id(0)
    n_s = pl.num_programs(0)
    slot = i % 2
    halves = PAGE_BLOCKS
    n_slot = B_KV_HEADS * TOP_N

    def block_of(s_idx, j):
        return idx_ref[s_idx * n_slot + j]

    def copy(s_idx, sl, j):
        n = jnp.clip(block_of(s_idx, j), 0, cur - 1)
        page = pt_ref[s_idx * n_pages + n // halves]
        return pltpu.make_async_copy(cache_ref.at[page], buf.at[sl, j], sem.at[sl])

    def cached(s_idx, j):
        n = block_of(s_idx, j)
        return (n >= 0) & (n < cur)

    def fetch(s_idx, sl):
        for j in range(n_slot):
            @pl.when(cached(s_idx, j))
            def _():
                copy(s_idx, sl, j).start()

    @pl.when(i == 0)
    def _():
        fetch(0, 0)

    @pl.when(i + 1 < n_s)
    def _():
        fetch(i + 1, 1 - slot)

    for j in range(n_slot):
        @pl.when(cached(i, j))
        def _():
            copy(i, slot, j).wait()

        @pl.when(jnp.logical_not(cached(i, j)))
        def _():
            buf[slot, j] = jnp.zeros((KV_ROW, page_len), F32)
            buf[slot, j, :, 0:1] = ksel_ref[0]

    gates = jax.nn.sigmoid(gate_ref[0])
    t_l = lax.broadcasted_iota(I32, (1, page_len), 1)
    n_buf = win_ref.shape[2]
    j_w = lax.broadcasted_iota(I32, (1, n_buf), 1)
    d_w = n_buf - j_w
    ok_w = (d_w >= 0) & (d_w < WINDOW) & (past_len - d_w >= 0)
    f0 = fpg_ref[n_pages]
    outs = []
    for g in range(B_KV_HEADS):
        q4 = _group_q(q_ref, g)
        ksl = slice(g * HEAD_DIM, (g + 1) * HEAD_DIM)
        vsl = slice(KV_WIDTH + g * HEAD_DIM, KV_WIDTH + (g + 1) * HEAD_DIM)
        pieces = []
        for k in range(TOP_N):
            j = g * TOP_N + k
            n = block_of(i, j)
            pg = jnp.clip(n, 0, cur) // halves
            fb = fpg_ref[pg][g * B_GROUP:(g + 1) * B_GROUP, :]
            s_k = _dot(q4, _bf(buf[slot, j, ksl, :])) + fb
            ok = (n >= 0) & (t_l // SEL_BLOCK == n % halves) & (pg * page_len + t_l <= past_len)
            pieces.append(jnp.where(ok, s_k, NEG_INF))
        s = jnp.concatenate(pieces, axis=1)
        e = jnp.exp(s - jnp.max(s, axis=-1, keepdims=True))
        p = _bf(e / jnp.sum(e, axis=-1, keepdims=True))
        o_s = jnp.zeros((B_GROUP, HEAD_DIM), F32)
        for k in range(TOP_N):
            o_s = o_s + _dot_nt(p[:, k * page_len:(k + 1) * page_len], _bf(buf[slot, g * TOP_N + k, vsl, :]))
        kw = _bf(win_ref[0, ksl, :])
        vw = _bf(win_ref[0, vsl, :])
        s_w = jnp.where(ok_w, _dot(q4, kw) + bw_ref[g * B_GROUP:(g + 1) * B_GROUP, :], NEG_INF)
        k_new = _bf(kwin_ref[0, :, ksl])
        v_new = _bf(kwin_ref[0, :, vsl])
        s_n = (jnp.sum(q4.astype(F32) * k_new.astype(F32), axis=-1, keepdims=True)
               + f0[g * B_GROUP:(g + 1) * B_GROUP, 0:1])
        m = jnp.maximum(jnp.max(s_w, axis=-1, keepdims=True), s_n)
        e_w = jnp.exp(s_w - m)
        e_n = jnp.exp(s_n - m)
        den = jnp.sum(e_w, axis=-1, keepdims=True) + e_n
        o_w = _dot_nt(_bf(e_w / den), vw) + _bf(e_n / den).astype(F32) * v_new.astype(F32)
        for r in range(B_GROUP):
            h = g * B_GROUP + r
            outs.append(gates[:, 3 * h + 1:3 * h + 2] * o_s[r:r + 1] + gates[:, 3 * h + 2:3 * h + 3] * o_w[r:r + 1])
    o_ref[0] = oc_ref[0] + jnp.concatenate(outs, axis=1)


PAGE_BLOCKS = 2


def _nsa_sample_attend(idx, page_table, q3, gate3, oc3, ksel_t, kwin3, win_t, cache_t, rel_bias, past_len):
    s = q3.shape[0]
    n_pages = page_table.shape[1]
    cur = past_len // SEL_BLOCK
    n_buf = win_t.shape[2]
    page = cache_t.shape[2]
    assert page == PAGE_BLOCKS * SEL_BLOCK and past_len == n_pages * page
    keypos = jnp.arange(n_pages + 1)[:, None] * page + jnp.arange(page)[None, :]
    fpg = jnp.moveaxis(_bias_of(rel_bias, past_len - keypos), 0, 1)
    bw = _bias_of(rel_bias, n_buf - jnp.arange(n_buf))
    row3 = lambda w: pl.BlockSpec((1, 1, w), lambda i, *_: (i, 0, 0))
    grid_spec = pltpu.PrefetchScalarGridSpec(
        num_scalar_prefetch=2,
        grid=(s,),
        in_specs=[row3(B_WIDTH), row3(GATE_PAD), row3(B_WIDTH),
                  pl.BlockSpec((1, KV_ROW, 1), lambda i, *_: (i, 0, 0)), row3(KV_ROW),
                  pl.BlockSpec((1, KV_ROW, n_buf), lambda i, *_: (i, 0, 0)),
                  pl.BlockSpec((n_pages + 1, B_HEADS, page), lambda i, *_: (0, 0, 0)),
                  pl.BlockSpec((B_HEADS, n_buf), lambda i, *_: (0, 0)),
                  pl.BlockSpec(memory_space=pl.ANY)],
        out_specs=row3(B_WIDTH),
        scratch_shapes=[pltpu.VMEM((2, B_KV_HEADS * TOP_N, KV_ROW, page), F32), pltpu.SemaphoreType.DMA((2,))],
    )
    return pl.pallas_call(
        functools.partial(_nsa_sample_attend_kernel, cur=cur, n_pages=n_pages, past_len=past_len, page_len=page),
        grid_spec=grid_spec,
        out_shape=jax.ShapeDtypeStruct((s, 1, B_WIDTH), F32),
        compiler_params=_cparams(("arbitrary",)),
        name="nsa_sample_attend",
    )(idx.reshape(-1), page_table.reshape(-1), q3, gate3, oc3, ksel_t, kwin3, win_t, fpg, bw, cache_t)


def _merge_kernel(x_ref, yn_ref, bo_ref, ga_ref, yb_ref, pg_ref, lnw_ref, lnb_ref, wa_ref, wb_ref, wo_ref, nf_ref,
                  wr_ref, br_ref, x1_ref, h_ref, te_ref, tw_ref, *, a_transposed):
    if a_transposed:
        tm = x_ref.shape[0]
        yn = yn_ref[...].reshape(A_WIDTH, tm)
        y_a = ((yn * lnw_ref[...] + lnb_ref[...] + bo_ref[0]) * ga_ref[0]).T
        y_b = yb_ref[0].T
    else:
        y_a = (yn_ref[...] * lnw_ref[...] + lnb_ref[...] + bo_ref[...]) * ga_ref[...]
        y_b = yb_ref[...]
    g_a = jax.nn.sigmoid(pg_ref[:, :D_MODEL])
    g_b = jax.nn.sigmoid(pg_ref[:, D_MODEL:])
    m = g_a * _dot(_bf(y_a), wa_ref[...]) + g_b * _dot(_bf(y_b), wb_ref[...])
    x1 = x_ref[...] + _dot(_bf(m), wo_ref[...])
    x1_ref[...] = x1
    h = x1 * lax.rsqrt(jnp.mean(x1 * x1, axis=-1, keepdims=True) + NORM_EPS) * nf_ref[...]
    _store_rows(h_ref, h)
    logits = _dot(_bf(h), wr_ref[...]) + br_ref[...]
    lane = lax.broadcasted_iota(I32, logits.shape, 1).astype(F32)
    col = lax.broadcasted_iota(I32, (logits.shape[0], TOP_K), 1)
    vals, idxs = [], []
    for _ in range(TOP_K):
        m_k = jnp.max(logits, axis=-1, keepdims=True)
        i_k = jnp.min(jnp.where(logits == m_k, lane, float(N_EXPERTS)), axis=-1, keepdims=True)
        vals.append(m_k)
        idxs.append(i_k)
        logits = jnp.where(lane == i_k, -jnp.inf, logits)
    e = [jnp.exp(v - vals[0]) for v in vals]
    den = ((e[0] + e[1]) + e[2]) + e[3]
    te = jnp.zeros(col.shape, F32)
    tw = jnp.zeros(col.shape, F32)
    for k in range(TOP_K):
        te = jnp.where(col == k, idxs[k], te)
        tw = jnp.where(col == k, e[k] / den, tw)
    te_ref[...] = te.astype(I32)
    tw_ref[...] = tw


def _merge(x, yn, bonus, gate, yb, pg, ln_w, ln_b, w_a, w_b, w_o, norm_ffn, w_router, b_router, tm, seq=None):
    rows = x.shape[0]
    row = lambda w: pl.BlockSpec((tm, w), lambda i: (i, 0))
    vec = lambda z: z.reshape(1, -1).astype(F32)
    if seq is None:
        a_specs = [row(A_WIDTH)] * 3 + [row(B_WIDTH)] + [_full((1, A_WIDTH))] * 2
        lnw, lnb = vec(ln_w), vec(ln_b)
    else:
        per = seq // tm
        nspec = pl.BlockSpec((1, A_WIDTH, tm), lambda i: (i // per, 0, i % per))
        a_specs = [pl.BlockSpec((HEAD_DIM, 1, A_HEADS, tm), lambda i: (0, i // per, 0, i % per)), nspec, nspec, nspec,
                   _full((A_WIDTH, 1)), _full((A_WIDTH, 1))]
        lnw, lnb = (z[CH_MAJOR].reshape(-1, 1).astype(F32) for z in (ln_w, ln_b))
        w_a = w_a[CH_MAJOR]
    return pl.pallas_call(
        functools.partial(_merge_kernel, a_transposed=seq is not None),
        grid=(rows // tm,),
        in_specs=[row(D_MODEL)] + a_specs[:4] + [row(GATE_PROJ)] + a_specs[4:]
                 + [_full((A_WIDTH, D_MODEL)), _full((B_WIDTH, D_MODEL)),
                  _full((D_MODEL, D_MODEL)), _full((1, D_MODEL)), _full((D_MODEL, N_EXPERTS)), _full((1, N_EXPERTS))],
        out_specs=[row(D_MODEL), pl.BlockSpec((tm * ROW_TILE, LANES), lambda i: (i, 0)), row(TOP_K), row(TOP_K)],
        out_shape=[jax.ShapeDtypeStruct((rows, D_MODEL), F32), jax.ShapeDtypeStruct((rows * ROW_TILE, LANES), F32),
                   jax.ShapeDtypeStruct((rows, TOP_K), I32), jax.ShapeDtypeStruct((rows, TOP_K), F32)],
        compiler_params=_cparams(("parallel",)),
        name="merge_router",
    )(x, yn, bonus, gate, yb, pg, lnw, lnb, _bf(w_a), _bf(w_b), _bf(w_o), vec(norm_ffn),
      _bf(w_router), vec(b_router))


ROW_TILE = D_MODEL // LANES


def _load_rows(ref, n, first=0, every=1):
    return jnp.concatenate([ref[pl.ds(first * ROW_TILE + s, n, stride=every * ROW_TILE), :] for s in range(ROW_TILE)],
                           axis=1)


def _store_rows(ref, x):
    for s in range(ROW_TILE):
        ref[pl.ds(s, x.shape[0], stride=ROW_TILE), :] = x[:, s * LANES:(s + 1) * LANES]


def _dispatch_kernel(dest_ref, h_ref, xs_in_ref, xs_ref, sem, *, tm):
    del xs_in_ref
    for t in range(tm):
        for k in range(TOP_K):
            pltpu.make_async_copy(h_ref.at[t], xs_ref.at[dest_ref[0, 0, t * TOP_K + k]], sem).start()
    for t in range(tm * TOP_K):
        pltpu.make_async_copy(h_ref.at[0], xs_ref.at[0], sem).wait()


def _dispatch(dest, h_tiles, slots, tm):
    n = h_tiles.shape[0]
    return pl.pallas_call(
        functools.partial(_dispatch_kernel, tm=tm),
        grid=(n // tm,),
        in_specs=[pl.BlockSpec((1, 1, tm * TOP_K), lambda i: (i, 0, 0), memory_space=pltpu.SMEM),
                  pl.BlockSpec((tm, ROW_TILE, LANES), lambda i: (i, 0, 0)), pl.BlockSpec(memory_space=pl.ANY)],
        out_specs=pl.BlockSpec(memory_space=pl.ANY),
        out_shape=jax.ShapeDtypeStruct((slots, ROW_TILE, LANES), F32),
        scratch_shapes=[pltpu.SemaphoreType.DMA(())],
        input_output_aliases={2: 0},
        compiler_params=_cparams(("arbitrary",)),
        name="moe_dispatch",
    )(dest.reshape(n // tm, 1, tm * TOP_K), h_tiles, jnp.zeros((slots, ROW_TILE, LANES), F32))


def _moe_kernel(ce_ref, nu_ref, x_ref, wu_ref, bu_ref, wd_ref, bd_ref, o_ref, wu_bf, wd_bf):
    c = pl.program_id(0)
    e = ce_ref[c]
    prev = ce_ref[jnp.maximum(c - 1, 0)]

    @pl.when((c == 0) | (e != prev))
    def _():
        wu_bf[...] = _bf(wu_ref[0])
        wd_bf[...] = _bf(wd_ref[0])

    @pl.when(c < nu_ref[0])
    def _():
        u = _dot(_bf(_load_rows(x_ref, MOE_ROWS)), wu_bf[...]) + bu_ref[0]
        glu = jnp.minimum(u[:, :D_FF], SWIGLU_LIMIT)
        lin = jnp.clip(u[:, D_FF:], -SWIGLU_LIMIT, SWIGLU_LIMIT)
        act = glu * jax.nn.sigmoid(SWIGLU_ALPHA * glu) * (lin + 1.0)
        _store_rows(o_ref, _dot(_bf(act), wd_bf[...]) + bd_ref[0])

    @pl.when(c >= nu_ref[0])
    def _():
        o_ref[...] = jnp.zeros_like(o_ref)


def _moe_experts(chunk_e, n_used, xs, w_up, b_up, w_down, b_down):
    n_chunks = xs.shape[0] // (MOE_ROWS * ROW_TILE)
    rows = pl.BlockSpec((MOE_ROWS * ROW_TILE, LANES), lambda c, ce, nu: (c, 0))
    grid_spec = pltpu.PrefetchScalarGridSpec(
        num_scalar_prefetch=2,
        grid=(n_chunks,),
        in_specs=[rows,
                  pl.BlockSpec((1, D_MODEL, 2 * D_FF), lambda c, ce, nu: (ce[c], 0, 0)),
                  pl.BlockSpec((1, 1, 2 * D_FF), lambda c, ce, nu: (ce[c], 0, 0)),
                  pl.BlockSpec((1, D_FF, D_MODEL), lambda c, ce, nu: (ce[c], 0, 0)),
                  pl.BlockSpec((1, 1, D_MODEL), lambda c, ce, nu: (ce[c], 0, 0))],
        out_specs=rows,
        scratch_shapes=[pltpu.VMEM((D_MODEL, 2 * D_FF), BF16), pltpu.VMEM((D_FF, D_MODEL), BF16)],
    )
    return pl.pallas_call(
        _moe_kernel,
        grid_spec=grid_spec,
        out_shape=jax.ShapeDtypeStruct(xs.shape, F32),
        compiler_params=_cparams(("arbitrary",)),
        name="moe_experts",
    )(chunk_e, n_used, xs, w_up, b_up.reshape(N_EXPERTS, 1, -1), w_down, b_down.reshape(N_EXPERTS, 1, -1))


def _combine_kernel(dcur_ref, dnext_ref, x_ref, w_ref, g_ref, ys_ref, o_ref, buf, sem, *, tm, normalize):
    i = pl.program_id(0)
    slot = i % 2
    n_a = tm * TOP_K

    def copy(d_ref, sl, a):
        return pltpu.make_async_copy(ys_ref.at[d_ref[0, 0, a]], buf.at[sl, pl.ds(a * ROW_TILE, ROW_TILE)], sem.at[sl])

    def fetch(d_ref, sl):
        for a in range(n_a):
            copy(d_ref, sl, a).start()

    @pl.when(i == 0)
    def _():
        fetch(dcur_ref, 0)

    @pl.when(i + 1 < pl.num_programs(0))
    def _():
        fetch(dnext_ref, 1 - slot)

    for a in range(n_a):
        copy(dcur_ref, slot, a).wait()
    x = x_ref[...]
    for k in range(TOP_K):
        x = x + w_ref[:, k:k + 1] * _load_rows(buf.at[slot], tm, first=k, every=TOP_K)
    if normalize:
        x = x * lax.rsqrt(jnp.mean(x * x, axis=-1, keepdims=True) + NORM_EPS) * g_ref[...]
    o_ref[...] = x


def _combine(dest, x1, top_w, ys_tiles, g, tm, normalize):
    n = x1.shape[0]
    steps = n // tm
    d2 = dest.reshape(steps, 1, tm * TOP_K)
    row = lambda w: pl.BlockSpec((tm, w), lambda i: (i, 0))
    return pl.pallas_call(
        functools.partial(_combine_kernel, tm=tm, normalize=normalize),
        grid=(steps,),
        in_specs=[pl.BlockSpec((1, 1, tm * TOP_K), lambda i: (i, 0, 0), memory_space=pltpu.SMEM),
                  pl.BlockSpec((1, 1, tm * TOP_K), lambda i: (jnp.minimum(i + 1, steps - 1), 0, 0),
                               memory_space=pltpu.SMEM),
                  row(D_MODEL), row(TOP_K), _full((1, D_MODEL)), pl.BlockSpec(memory_space=pl.ANY)],
        out_specs=row(D_MODEL),
        out_shape=jax.ShapeDtypeStruct((n, D_MODEL), F32),
        scratch_shapes=[pltpu.VMEM((2, tm * TOP_K * ROW_TILE, LANES), F32), pltpu.SemaphoreType.DMA((2,))],
        compiler_params=_cparams(("arbitrary",)),
        name="moe_combine",
    )(d2, d2, x1, top_w, g.reshape(1, -1).astype(F32), ys_tiles)


def _moe_residual(x1, h_tiles, top_e, top_w, w_up, b_up, w_down, b_down, g, normalize):
    n = x1.shape[0]
    n_assign = n * TOP_K
    onehot = (top_e.reshape(-1, 1) == jnp.arange(N_EXPERTS, dtype=I32)[None, :]).astype(I32)
    csum = jnp.cumsum(onehot, axis=0)
    counts = csum[-1]
    padded = (counts + MOE_ROWS - 1) // MOE_ROWS * MOE_ROWS
    pad_end = jnp.cumsum(padded)
    dest = jnp.sum(onehot * (csum - 1 + (pad_end - padded)[None, :]), axis=1).astype(I32)
    n_chunks = -(-(n_assign + N_EXPERTS * (MOE_ROWS - 1)) // MOE_ROWS)
    slots = n_chunks * MOE_ROWS
    chunk_e = jnp.sum(pad_end[None, :] <= (jnp.arange(n_chunks) * MOE_ROWS)[:, None], axis=1)
    chunk_e = jnp.minimum(chunk_e, N_EXPERTS - 1).astype(I32)
    n_used = (pad_end[-1] // MOE_ROWS).astype(I32).reshape(1)
    tm = _row_tile(n, 256)
    xs = _dispatch(dest, h_tiles.reshape(n, ROW_TILE, LANES), slots, tm)
    ys = _moe_experts(chunk_e, n_used, xs.reshape(slots * ROW_TILE, LANES), w_up, b_up, w_down, b_down)
    return _combine(dest, x1, top_w, ys.reshape(slots, ROW_TILE, LANES), g, tm, normalize)


def _row_tile(rows, cap):
    tm = cap
    while rows % tm:
        tm //= 2
    return tm


def kernel(x_prompt, x_sample, cache_cmp_kv, cache_sel_kv, state_win_kv, state_rwkv, state_rwkv_shift, page_table,
           norm_attn, w_in, mu_shift, w0, w_decay_up, a0, w_iclr_up, w_gate_up, k_k, k_a, r_k, ln_x_w, ln_x_b,
           pe_cmp_k, pe_cmp_v, w_cmp_k1, w_cmp_k2, w_cmp_v1, w_cmp_v2, rel_bias, w_br_a, w_br_b, w_out,
           norm_ffn, w_router, b_router, w_up, b_up, w_down, b_down, norm_final):
    bp, tp, _ = x_prompt.shape
    bs, ts, _ = x_sample.shape
    depth = w_in.shape[0]
    past_len = page_table.shape[1] * cache_cmp_kv.shape[2]
    n_buf = state_win_kv.shape[2]
    assert ts == 1 and bp * A_HEADS == BH_LANES and bs % 8 == 0 and tp % LANES == 0
    xp = x_prompt.reshape(bp * tp, D_MODEL)
    xs = x_sample.reshape(bs, D_MODEL)
    new = {name: [] for name in ('cmp_p', 'sel_p', 'win_p', 'wkv_p', 'shift_p', 'cmp_s', 'sel_s', 'win_s', 'wkv_s', 'shift_s')}
    kv6 = lambda z, b, t: z.reshape(b, t, 2, B_KV_HEADS, HEAD_DIM)
    for l in range(depth):
        rw = (mu_shift[l], w0[l], w_decay_up[l], a0[l], w_iclr_up[l], w_gate_up[l], k_k[l], k_a[l], r_k[l])
        cw = (pe_cmp_k[l], pe_cmp_v[l], w_cmp_k1[l], w_cmp_k2[l], w_cmp_v1[l], w_cmp_v2[l])
        wa_nat, w_rest, w_t, w_n = _pack_w_in(w_in[l])
        g_attn = norm_attn[l].reshape(1, -1).astype(F32)
        last = l == depth - 1
        g_fin = norm_final if last else jnp.ones((D_MODEL,), F32)

        tq = _row_tile(tp, 256)
        pa_t, q_t, kvt_c, kvt_s, kvt_w, gt_t, kv_c, kv_s, kv_w, pg = _project_t(xp, g_attn, w_t, w_n, tq, tp)
        r, w, k2, v, kk, kb, gate, bonus = _rwkv_prep_t(pa_t, *rw, tq)
        r, w, k2, kk, kb = _key_tiles([z.reshape(N_JP, LANES, tp) for z in (r, w, k2, kk, kb)], LANES)
        v = _val_tiles(v.reshape(HEAD_DIM, BH_LANES, tp), LANES).reshape(1, tp, HEAD_DIM, LANES)
        s0 = jnp.zeros((1, N_IO, N_JP, 8, LANES), F32)
        yn, s_fin = _wkv_scan(r, w, k2, v, kk, kb, s0, _row_tile(tp, 64))
        yn = _val_untile(yn.reshape(tp * HEAD_DIM, LANES), LANES).reshape(HEAD_DIM, bp, A_HEADS, tp)
        ckv = _compress(kv_c, *cw, rows=_row_tile(tp, 2048))
        y_b = _nsa_prompt(q_t, gt_t, ckv, kv_s, kvt_s, kv_w, kvt_w, rel_bias)
        x1, h, top_e, top_w = _merge(xp, yn, bonus, gate, y_b, pg, ln_x_w[l], ln_x_b[l], w_br_a[l], w_br_b[l], w_out[l],
                                     norm_ffn[l], w_router[l], b_router[l], tq, seq=tp)
        xp_next = _moe_residual(x1, h, top_e, top_w, w_up[l], b_up[l], w_down[l], b_down[l], g_fin, last)
        n_win = min(WINDOW, tp)
        kv6t = lambda z: jnp.moveaxis(z.reshape(bp, 2, B_KV_HEADS, HEAD_DIM, tp), -1, 1)
        new['cmp_p'].append(kv6t(kvt_c))
        new['sel_p'].append(kv6t(kvt_s))
        new['win_p'].append(kv6t(kvt_w)[:, tp - n_win:])
        new['wkv_p'].append(_state_from_tiles(s_fin, bp))
        new['shift_p'].append(jnp.zeros((bp, A_PROJ), F32).at[:, A_PERM].set(pa_t[:, :, tp - 1]))

        s_a, q, kv_c, kv_s, kv_w, gt, pg = _project(xs, g_attn, wa_nat, w_rest, _row_tile(bs, 256))
        r, w, k2, v, kk, kb, gate, bonus = _rwkv_prep(s_a, state_rwkv_shift[l], *rw, _row_tile(bs, 256))
        yn, wkv_s = _wkv_step(r, w, k2, v, kk, kb, jnp.transpose(state_rwkv[l].astype(F32), (1, 2, 3, 0)))
        wkv_s = jnp.transpose(wkv_s, (3, 0, 1, 2))
        tok_minor = lambda z: jnp.moveaxis(z, 1, -1).reshape(z.shape[0], KV_ROW, z.shape[1])
        ckv = _compress_paged(tok_minor(cache_cmp_kv[l]), page_table, cw)
        q3, gt3 = q.reshape(bs, 1, B_WIDTH), gt.reshape(bs, 1, GATE_PAD)
        oc3, idx = _nsa_sample_select(q3, gt3, ckv, rel_bias, past_len)
        y_b = _nsa_sample_attend(idx, page_table, q3, gt3, oc3, kv_s.reshape(bs, KV_ROW, 1), kv_w.reshape(bs, 1, KV_ROW),
                                 tok_minor(state_win_kv[l]), tok_minor(cache_sel_kv[l]),
                                 rel_bias, past_len).reshape(bs, B_WIDTH)
        x1, h, top_e, top_w = _merge(xs, yn, bonus, gate, y_b, pg, ln_x_w[l], ln_x_b[l], w_br_a[l], w_br_b[l], w_out[l],
                                     norm_ffn[l], w_router[l], b_router[l], _row_tile(bs, 256))
        xs_next = _moe_residual(x1, h, top_e, top_w, w_up[l], b_up[l], w_down[l], b_down[l], g_fin, last)
        new['cmp_s'].append(kv6(kv_c, bs, 1))
        new['sel_s'].append(kv6(kv_s, bs, 1))
        new['win_s'].append(jnp.concatenate([state_win_kv[l], kv6(kv_w, bs, 1)], axis=1)[:, 1:])
        new['wkv_s'].append(wkv_s.astype(state_rwkv.dtype))
        new['shift_s'].append(s_a)
        xp, xs = xp_next, xs_next
    return (xp.reshape(bp, tp, D_MODEL), xs.reshape(bs, ts, D_MODEL),
            jnp.stack(new['cmp_p']), jnp.stack(new['sel_p']), jnp.stack(new['win_p']),
            jnp.stack(new['wkv_p']), jnp.stack(new['shift_p']),
            jnp.stack(new['cmp_s']), jnp.stack(new['sel_s']), jnp.stack(new['win_s']),
            jnp.stack(new['wkv_s']), jnp.stack(new['shift_s']))
```

```python
import functools
import math

import jax
import jax.numpy as jnp
import numpy as np
from jax import lax
from jax.experimental import pallas as pl
from jax.experimental.pallas import tpu as pltpu

F32 = jnp.float32
BF16 = jnp.bfloat16
I32 = jnp.int32
HIGHEST = lax.Precision.HIGHEST

D_MODEL = 1024
HEAD_DIM = 64
A_HEADS = 8
A_WIDTH = A_HEADS * HEAD_DIM
A_DECAY_RANK = 64
A_ICLR_RANK = 64
A_GATE_RANK = 128
A_GN_EPS = 64e-5
A_PROJ = 3 * A_WIDTH + A_DECAY_RANK + A_ICLR_RANK + A_GATE_RANK
B_HEADS = 8
B_KV_HEADS = 2
B_GROUP = B_HEADS // B_KV_HEADS
B_WIDTH = B_HEADS * HEAD_DIM
KV_WIDTH = B_KV_HEADS * HEAD_DIM
KV_ROW = 2 * KV_WIDTH
CMP_BLOCK = 32
CMP_HIDDEN = 128
SEL_BLOCK = 64
SEL_RATIO = SEL_BLOCK // CMP_BLOCK
TOP_N = 16
WINDOW = 512
Q_BLOCK = 128
FORCE_SCORE = 1e4
N_BUCKETS = 32
MAX_EXACT = N_BUCKETS // 2
MAX_DISTANCE = 1024
N_EXPERTS = 32
TOP_K = 4
D_FF = 1024
SWIGLU_ALPHA = 1.702
SWIGLU_LIMIT = 7.0
NORM_EPS = 1e-6
NEG_INF = -1e30
SCALE = HEAD_DIM ** -0.5
GATE_PAD = 128
GATE_PROJ = 2 * D_MODEL
MOE_ROWS = 512
LANES = 128
VMEM_LIMIT = 56 * 1024 * 1024


def _cparams(sem):
    return pltpu.CompilerParams(dimension_semantics=sem, vmem_limit_bytes=VMEM_LIMIT)


def _full(shape):
    n = len(shape)
    return pl.BlockSpec(shape, lambda *_: (0,) * n)


def _dot(a, b):
    return jnp.dot(a, b, preferred_element_type=F32)


def _dot_nt(a, b):
    return lax.dot_general(a, b, (((1,), (1,)), ((), ())), preferred_element_type=F32)


def _bf(x):
    return x.astype(BF16)


PROJ_SPLITS = (B_WIDTH, KV_ROW, KV_ROW, KV_ROW, GATE_PAD, GATE_PROJ)
CH_MAJOR = (np.arange(A_HEADS)[None, :] * HEAD_DIM + np.arange(HEAD_DIM)[:, None]).reshape(-1)
A_PERM = np.concatenate([CH_MAJOR, A_WIDTH + CH_MAJOR, 2 * A_WIDTH + CH_MAJOR, np.arange(3 * A_WIDTH, A_PROJ)])


def _rms_bf16(x_ref, g_ref):
    x = x_ref[...]
    return _bf(x * lax.rsqrt(jnp.mean(x * x, axis=-1, keepdims=True) + NORM_EPS) * g_ref[...])


def _proj_kernel(x_ref, g_ref, wa_ref, w_ref, oa_ref, *o_refs):
    h = _rms_bf16(x_ref, g_ref)
    oa_ref[...] = _dot(h, wa_ref[...])
    c = 0
    for o_ref, n in zip(o_refs, PROJ_SPLITS):
        o_ref[...] = _dot(h, w_ref[:, c:c + n])
        c += n


def _project(x, g, wa, w, tm):
    rows = x.shape[0]
    splits = (A_PROJ,) + PROJ_SPLITS
    return pl.pallas_call(
        _proj_kernel,
        grid=(rows // tm,),
        in_specs=[pl.BlockSpec((tm, D_MODEL), lambda i: (i, 0)), _full((1, D_MODEL)), _full(wa.shape), _full(w.shape)],
        out_specs=[pl.BlockSpec((tm, n), lambda i: (i, 0)) for n in splits],
        out_shape=[jax.ShapeDtypeStruct((rows, n), F32) for n in splits],
        compiler_params=_cparams(("parallel",)),
        name="norm_proj",
    )(x, g, wa, w)


PROJ_T_SPLITS = (A_PROJ, B_WIDTH, KV_ROW, KV_ROW, KV_ROW, GATE_PAD)
PROJ_N_SPLITS = (KV_ROW, KV_ROW, KV_ROW, GATE_PROJ)


def _proj_t_kernel(x_ref, g_ref, wt_ref, wn_ref, *o_refs):
    h = _rms_bf16(x_ref, g_ref)
    t = _dot_nt(wt_ref[...], h)
    c = 0
    for o_ref, n in zip(o_refs[:len(PROJ_T_SPLITS)], PROJ_T_SPLITS):
        o_ref[0] = t[c:c + n]
        c += n
    c = 0
    for o_ref, n in zip(o_refs[len(PROJ_T_SPLITS):], PROJ_N_SPLITS):
        o_ref[...] = _dot(h, wn_ref[:, c:c + n])
        c += n


def _project_t(x, g, wt, wn, tm, seq):
    rows = x.shape[0]
    per = seq // tm
    return pl.pallas_call(
        _proj_t_kernel,
        grid=(rows // tm,),
        in_specs=[pl.BlockSpec((tm, D_MODEL), lambda i: (i, 0)), _full((1, D_MODEL)), _full(wt.shape), _full(wn.shape)],
        out_specs=[pl.BlockSpec((1, n, tm), lambda i: (i // per, 0, i % per)) for n in PROJ_T_SPLITS]
                  + [pl.BlockSpec((tm, n), lambda i: (i, 0)) for n in PROJ_N_SPLITS],
        out_shape=[jax.ShapeDtypeStruct((rows // seq, n, seq), F32) for n in PROJ_T_SPLITS]
                  + [jax.ShapeDtypeStruct((rows, n), F32) for n in PROJ_N_SPLITS],
        compiler_params=_cparams(("parallel",)),
        name="norm_proj_t",
    )(x, g, wt, wn)


def _pack_w_in(w_in):
    a, rest = w_in[:, :A_PROJ], w_in[:, A_PROJ:]
    q, kv, gt, pg = (rest[:, :B_WIDTH], rest[:, B_WIDTH:B_WIDTH + 3 * KV_ROW],
                     rest[:, B_WIDTH + 3 * KV_ROW:B_WIDTH + 3 * KV_ROW + 3 * B_HEADS],
                     rest[:, B_WIDTH + 3 * KV_ROW + 3 * B_HEADS:])
    gt = jnp.pad(gt, ((0, 0), (0, GATE_PAD - 3 * B_HEADS)))
    w_t = jnp.concatenate([a[:, A_PERM], q, kv, gt], axis=1).T
    return _bf(a), _bf(jnp.concatenate([q, kv, gt, pg], axis=1)), _bf(w_t), _bf(jnp.concatenate([kv, pg], axis=1))


def _softplus(z):
    return jnp.maximum(z, 0.0) + jnp.log1p(jnp.exp(-jnp.abs(z)))


def _rwkv_prep_kernel(p_ref, prev_ref, mu_ref, w0_ref, wdu_ref, a0_ref, wiu_ref, wgu_ref, kk_ref, ka_ref, rk_ref,
                      ones_ref, r_o, w_o, k_o, v_o, kk_o, kb_o, g_o, bo_o):
    p = p_ref[...]
    ps = p + mu_ref[...] * (prev_ref[...] - p)
    r = ps[:, 0:A_WIDTH]
    k = ps[:, A_WIDTH:2 * A_WIDTH]
    v = ps[:, 2 * A_WIDTH:3 * A_WIDTH]
    c = 3 * A_WIDTH
    xw = ps[:, c:c + A_DECAY_RANK]
    xa = ps[:, c + A_DECAY_RANK:c + A_DECAY_RANK + A_ICLR_RANK]
    xg = ps[:, c + A_DECAY_RANK + A_ICLR_RANK:]
    w_log = -_softplus(-(w0_ref[...] + _dot(_bf(jnp.tanh(xw)), wdu_ref[...]))) - 0.5
    decay = jnp.exp(-jnp.exp(w_log))
    a = jax.nn.sigmoid(a0_ref[...] + _dot(_bf(xa), wiu_ref[...]))
    gate = _dot(_bf(jax.nn.sigmoid(xg)), wgu_ref[...])
    ones = ones_ref[...]
    kk = k * kk_ref[...]
    ss = jnp.dot(kk * kk, ones, precision=HIGHEST, preferred_element_type=F32)
    kk = kk / jnp.maximum(jnp.sqrt(ss), 1e-12)
    k2 = k * (1.0 + (a - 1.0) * ka_ref[...])
    rk = jnp.dot(r * k2 * rk_ref[...], ones, precision=HIGHEST, preferred_element_type=F32)
    r_o[...] = r
    w_o[...] = decay
    k_o[...] = k2
    v_o[...] = v
    kk_o[...] = kk
    kb_o[...] = kk * a
    g_o[...] = gate
    bo_o[...] = rk * v


def _rwkv_prep(p, prev, mu, w0, wdu, a0, wiu, wgu, k_k, k_a, r_k, tm):
    rows = p.shape[0]
    head = np.arange(A_WIDTH) // HEAD_DIM
    ones = jnp.asarray(head[:, None] == head[None, :], F32)
    row = lambda z: z.reshape(1, -1).astype(F32)
    spec_in = pl.BlockSpec((tm, A_PROJ), lambda i: (i, 0))
    spec_o = pl.BlockSpec((tm, A_WIDTH), lambda i: (i, 0))
    return pl.pallas_call(
        _rwkv_prep_kernel,
        grid=(rows // tm,),
        in_specs=[spec_in, spec_in, _full((1, A_PROJ)), _full((1, A_WIDTH)), _full((A_DECAY_RANK, A_WIDTH)),
                  _full((1, A_WIDTH)), _full((A_ICLR_RANK, A_WIDTH)), _full((A_GATE_RANK, A_WIDTH)),
                  _full((1, A_WIDTH)), _full((1, A_WIDTH)), _full((1, A_WIDTH)), _full((A_WIDTH, A_WIDTH))],
        out_specs=[spec_o] * 8,
        out_shape=[jax.ShapeDtypeStruct((rows, A_WIDTH), F32)] * 8,
        compiler_params=_cparams(("parallel",)),
        name="rwkv_prep",
    )(p, prev, row(mu), row(w0), _bf(wdu), row(a0), _bf(wiu), _bf(wgu), row(k_k), row(k_a), row(r_k), ones)


def _rwkv_prep_t_kernel(p_ref, mu_ref, w0_ref, wdu_ref, a0_ref, wiu_ref, wgu_ref, kk_ref, ka_ref, rk_ref,
                        r_o, w_o, k_o, v_o, kk_o, kb_o, g_o, bo_o, carry):
    tt = p_ref.shape[2]

    @pl.when(pl.program_id(1) == 0)
    def _():
        carry[...] = jnp.zeros_like(carry)

    p = p_ref[0]
    lane = lax.broadcasted_iota(I32, (1, tt), 1)
    prev = jnp.where(lane == 0, carry[:, 0:1], pltpu.roll(p, 1, 1))
    carry[:, 0:1] = p[:, tt - 1:tt]
    ps = p + mu_ref[...] * (prev - p)
    r = ps[0:A_WIDTH]
    k = ps[A_WIDTH:2 * A_WIDTH]
    v = ps[2 * A_WIDTH:3 * A_WIDTH]
    c = 3 * A_WIDTH
    xw = ps[c:c + A_DECAY_RANK]
    xa = ps[c + A_DECAY_RANK:c + A_DECAY_RANK + A_ICLR_RANK]
    xg = ps[c + A_DECAY_RANK + A_ICLR_RANK:]

    def head_sum(x):
        s = jnp.sum(x.reshape(HEAD_DIM, A_HEADS, tt), axis=0)
        return jnp.broadcast_to(s[None], (HEAD_DIM, A_HEADS, tt)).reshape(A_WIDTH, tt)

    w_log = -_softplus(-(w0_ref[...] + _dot(wdu_ref[...], _bf(jnp.tanh(xw))))) - 0.5
    decay = jnp.exp(-jnp.exp(w_log))
    a = jax.nn.sigmoid(a0_ref[...] + _dot(wiu_ref[...], _bf(xa)))
    gate = _dot(wgu_ref[...], _bf(jax.nn.sigmoid(xg)))
    kk = k * kk_ref[...]
    kk = kk / jnp.maximum(jnp.sqrt(head_sum(kk * kk)), 1e-12)
    k2 = k * (1.0 + (a - 1.0) * ka_ref[...])
    rk = head_sum(r * k2 * rk_ref[...])
    for o_ref, val in ((r_o, r), (w_o, decay), (k_o, k2), (kk_o, kk), (kb_o, kk * a)):
        o_ref[...] = val.reshape(N_JP, 2, 1, A_HEADS, tt)
    v_o[...] = v.reshape(HEAD_DIM, 1, A_HEADS, tt)
    g_o[0] = gate
    bo_o[0] = rk * v


def _rwkv_prep_t(p_t, mu, w0, wdu, a0, wiu, wgu, k_k, k_a, r_k, tt):
    b, _, t = p_t.shape
    col = lambda z, perm: z.reshape(-1)[perm].reshape(-1, 1).astype(F32)
    up = lambda w: _bf(w[:, CH_MAJOR].T)
    kshape = jax.ShapeDtypeStruct((N_JP, 2, b, A_HEADS, t), F32)
    kspec = pl.BlockSpec((N_JP, 2, 1, A_HEADS, tt), lambda bi, i: (0, 0, bi, 0, i))
    vspec = pl.BlockSpec((HEAD_DIM, 1, A_HEADS, tt), lambda bi, i: (0, bi, 0, i))
    nspec = pl.BlockSpec((1, A_WIDTH, tt), lambda bi, i: (bi, 0, i))
    nshape = jax.ShapeDtypeStruct((b, A_WIDTH, t), F32)
    return pl.pallas_call(
        _rwkv_prep_t_kernel,
        grid=(b, t // tt),
        in_specs=[pl.BlockSpec((1, A_PROJ, tt), lambda bi, i: (bi, 0, i)), _full((A_PROJ, 1)), _full((A_WIDTH, 1)),
                  _full((A_WIDTH, A_DECAY_RANK)), _full((A_WIDTH, 1)), _full((A_WIDTH, A_ICLR_RANK)),
                  _full((A_WIDTH, A_GATE_RANK)), _full((A_WIDTH, 1)), _full((A_WIDTH, 1)), _full((A_WIDTH, 1))],
        out_specs=[kspec, kspec, kspec, vspec, kspec, kspec, nspec, nspec],
        out_shape=[kshape, kshape, kshape, jax.ShapeDtypeStruct((HEAD_DIM, b, A_HEADS, t), F32), kshape, kshape,
                   nshape, nshape],
        scratch_shapes=[pltpu.VMEM((A_PROJ, LANES), F32)],
        compiler_params=_cparams(("parallel", "arbitrary")),
        name="rwkv_prep_t",
    )(p_t, col(mu, A_PERM), col(w0, CH_MAJOR), up(wdu), col(a0, CH_MAJOR), up(wiu), up(wgu),
      col(k_k, CH_MAJOR), col(k_a, CH_MAJOR), col(r_k, CH_MAJOR))


def _key_tiles_kernel(*refs):
    n = len(refs) // 2
    for z_ref, o_ref in zip(refs[:n], refs[n:]):
        for jp in range(N_JP):
            o_ref[0, jp] = z_ref[jp].T


def _key_tiles(zs, tt):
    t = zs[0].shape[2]
    return pl.pallas_call(
        _key_tiles_kernel,
        grid=(t // tt,),
        in_specs=[pl.BlockSpec((N_JP, LANES, tt), lambda i: (0, 0, i))] * len(zs),
        out_specs=[pl.BlockSpec((1, N_JP, tt, LANES), lambda i: (0, 0, i, 0))] * len(zs),
        out_shape=[jax.ShapeDtypeStruct((1, N_JP, t, LANES), F32)] * len(zs),
        compiler_params=_cparams(("parallel",)),
        name="wkv_key_tiles",
    )(*zs)


def _val_tiles_kernel(z_ref, o_ref, *, tt):
    for i in range(HEAD_DIM):
        x = z_ref[i]
        o_ref[pl.ds(i, tt, stride=HEAD_DIM), :] = jnp.concatenate([x, x], axis=0).T


def _val_tiles(z, tt):
    t = z.shape[2]
    return pl.pallas_call(
        functools.partial(_val_tiles_kernel, tt=tt),
        grid=(t // tt,),
        in_specs=[pl.BlockSpec((HEAD_DIM, BH_LANES, tt), lambda i: (0, 0, i))],
        out_specs=pl.BlockSpec((tt * HEAD_DIM, LANES), lambda i: (i, 0)),
        out_shape=jax.ShapeDtypeStruct((t * HEAD_DIM, LANES), F32),
        compiler_params=_cparams(("parallel",)),
        name="wkv_val_tiles",
    )(z)


def _val_untile_kernel(y_ref, o_ref, *, tt):
    for i in range(HEAD_DIM):
        o_ref[i] = y_ref[pl.ds(i, tt, stride=HEAD_DIM), :].T[0:BH_LANES]


def _val_untile(y, tt):
    t = y.shape[0] // HEAD_DIM
    return pl.pallas_call(
        functools.partial(_val_untile_kernel, tt=tt),
        grid=(t // tt,),
        in_specs=[pl.BlockSpec((tt * HEAD_DIM, LANES), lambda i: (i, 0))],
        out_specs=pl.BlockSpec((HEAD_DIM, BH_LANES, tt), lambda i: (0, 0, i)),
        out_shape=jax.ShapeDtypeStruct((HEAD_DIM, BH_LANES, t), F32),
        compiler_params=_cparams(("parallel",)),
        name="wkv_val_untile",
    )(y)


N_IO = HEAD_DIM // 8
N_JP = HEAD_DIM // 2
BH_LANES = 64


def _wkv_kernel(r_ref, w_ref, k_ref, v_ref, a_ref, b_ref, s0_ref, y_ref, sfin_ref, s_scr, *, tc):
    t_blk = pl.program_id(1)

    @pl.when(t_blk == 0)
    def _():
        s_scr[...] = s0_ref[0]

    def bc(ref, t, jp):
        return jnp.broadcast_to(ref[0, jp, pl.ds(t, 1), :], (8, LANES))

    def fold(x):
        return x + pltpu.roll(x, BH_LANES, 1)

    def state_times_a(t):
        acc = [jnp.zeros((8, LANES), F32) for _ in range(N_IO)]
        for jp in range(N_JP):
            a_ = bc(a_ref, t, jp)
            for io in range(N_IO):
                acc[io] = acc[io] + s_scr[io, jp] * a_
        return tuple(-fold(acc[io]) for io in range(N_IO))

    def step(t, sa):
        t_next = jnp.minimum(t + 1, tc - 1)
        vv = [v_ref[0, t, io * 8:(io + 1) * 8, :] for io in range(N_IO)]
        acc = [jnp.zeros((8, LANES), F32) for _ in range(N_IO)]
        yacc = [jnp.zeros((8, LANES), F32) for _ in range(N_IO)]
        for half in range(2):
            for jp in range(N_JP):
                w_, b_, k_, r_, a_ = (bc(ref, tt, jp) for ref, tt in
                                      ((w_ref, t), (b_ref, t), (k_ref, t), (r_ref, t), (a_ref, t_next)))
                for io in range(half * N_IO // 2, (half + 1) * N_IO // 2):
                    s = s_scr[io, jp] * w_ + sa[io] * b_ + vv[io] * k_
                    s_scr[io, jp] = s
                    yacc[io] = yacc[io] + s * r_
                    acc[io] = acc[io] + s * a_
        sa_next = [-fold(acc[io]) for io in range(N_IO)]
        y = [fold(yacc[io]) for io in range(N_IO)]
        tot = y[0]
        for io in range(1, N_IO):
            tot = tot + y[io]
        mu = jnp.sum(tot, axis=0, keepdims=True) * (1.0 / HEAD_DIM)
        d = [y[io] - mu for io in range(N_IO)]
        sq = d[0] * d[0]
        for io in range(1, N_IO):
            sq = sq + d[io] * d[io]
        var = jnp.sum(sq, axis=0, keepdims=True) * (1.0 / HEAD_DIM)
        inv = lax.rsqrt(var + A_GN_EPS)
        for io in range(N_IO):
            y_ref[0, t, io * 8:(io + 1) * 8, :] = d[io] * inv
        return tuple(sa_next)

    lax.fori_loop(0, tc, step, state_times_a(0))

    @pl.when(t_blk == pl.num_programs(1) - 1)
    def _():
        sfin_ref[0] = s_scr[...]


def _wkv_scan(r, w, k, v, a, b, s0, tc):
    nb, t = v.shape[:2]
    kspec = pl.BlockSpec((1, N_JP, tc, LANES), lambda n, i: (n, 0, i, 0))
    vspec = pl.BlockSpec((1, tc, HEAD_DIM, LANES), lambda n, i: (n, i, 0, 0))
    sspec = pl.BlockSpec((1, N_IO, N_JP, 8, LANES), lambda n, i: (n, 0, 0, 0, 0))
    return pl.pallas_call(
        functools.partial(_wkv_kernel, tc=tc),
        grid=(nb, t // tc),
        in_specs=[kspec, kspec, kspec, vspec, kspec, kspec, sspec],
        out_specs=[vspec, sspec],
        out_shape=[jax.ShapeDtypeStruct((nb, t, HEAD_DIM, LANES), F32),
                   jax.ShapeDtypeStruct((nb, N_IO, N_JP, 8, LANES), F32)],
        scratch_shapes=[pltpu.VMEM((N_IO, N_JP, 8, LANES), F32)],
        compiler_params=_cparams(("parallel", "arbitrary")),
        name="wkv_scan",
    )(r, w, k, v, a, b, s0)


def _wkv_step_kernel(r_ref, w_ref, k_ref, v_ref, a_ref, b_ref, s_ref, y_ref, so_ref):
    rt, wt, kt, vt, at, bt = (ref[...].T for ref in (r_ref, w_ref, k_ref, v_ref, a_ref, b_ref))
    outs = []
    for hh in range(2):
        hs = slice(hh * HEAD_DIM, (hh + 1) * HEAD_DIM)
        r_, w_, k_, a_, b_ = rt[hs], wt[hs], kt[hs], at[hs], bt[hs]
        ys = []
        for i in range(HEAD_DIM):
            s = s_ref[hh, i]
            sa = -jnp.sum(s * a_, axis=0, keepdims=True)
            s = s * w_ + sa * b_ + vt[hh * HEAD_DIM + i:hh * HEAD_DIM + i + 1] * k_
            so_ref[hh, i] = s
            ys.append(jnp.sum(s * r_, axis=0, keepdims=True))
        y = jnp.concatenate(ys, axis=0)
        d = y - jnp.mean(y, axis=0, keepdims=True)
        outs.append(d * lax.rsqrt(jnp.mean(d * d, axis=0, keepdims=True) + A_GN_EPS))
    y_ref[...] = jnp.concatenate(outs, axis=0).T


def _wkv_step(r, w, k, v, a, b, state):
    s = r.shape[0]
    vec = pl.BlockSpec((s, 2 * HEAD_DIM), lambda h: (0, h))
    st = pl.BlockSpec((2, HEAD_DIM, HEAD_DIM, s), lambda h: (h, 0, 0, 0))
    return pl.pallas_call(
        _wkv_step_kernel,
        grid=(A_HEADS // 2,),
        in_specs=[vec] * 6 + [st],
        out_specs=[vec, st],
        out_shape=[jax.ShapeDtypeStruct((s, A_WIDTH), F32), jax.ShapeDtypeStruct(state.shape, F32)],
        compiler_params=_cparams(("parallel",)),
        name="wkv_step",
    )(r, w, k, v, a, b, state)


def _state_from_tiles(s, b):
    nb = b // 8
    s = s.reshape(nb, N_IO, N_JP, 8, 2, 8, A_HEADS).transpose(0, 5, 6, 1, 3, 2, 4)
    return s.reshape(b, A_HEADS, HEAD_DIM, HEAD_DIM)


def _compress_paged_kernel(pt_ref, pe_ref, w1_ref, w2_ref, cache_ref, o_ref, buf, xk, xv, sem, *, n_pages, page_len):
    i = pl.program_id(0)
    slot = i % 2

    def copy(s_idx, sl, p):
        return pltpu.make_async_copy(cache_ref.at[pt_ref[s_idx * n_pages + p]], buf.at[sl, p], sem.at[sl, p])

    def fetch(s_idx, sl):
        for p in range(n_pages):
            copy(s_idx, sl, p).start()

    @pl.when(i == 0)
    def _():
        fetch(0, 0)

    @pl.when(i + 1 < pl.num_programs(0))
    def _():
        fetch(i + 1, 1 - slot)

    for p in range(n_pages):
        copy(i, slot, p).wait()
    for p in range(n_pages):
        t = buf[slot, p].T
        for m in range(page_len // CMP_BLOCK):
            row = (p * (page_len // CMP_BLOCK) + m) * CMP_PITCH
            xk[row:row + CMP_BLOCK, :] = t[m * CMP_BLOCK:(m + 1) * CMP_BLOCK, :KV_WIDTH]
            xv[row:row + CMP_BLOCK, :] = t[m * CMP_BLOCK:(m + 1) * CMP_BLOCK, KV_WIDTH:]
    _compress_kernel(xk, xv, pe_ref, w1_ref, w2_ref, o_ref, nblk=n_pages * page_len // CMP_BLOCK, pitch=CMP_PITCH)


CMP_PITCH = CMP_BLOCK + 1


def _compress_kernel(xk_ref, xv_ref, pe_ref, w1_ref, w2_ref, o_ref, *, nblk, pitch=CMP_BLOCK):
    for c, x_ref in enumerate((xk_ref, xv_ref)):
        acc = jnp.zeros((nblk, B_KV_HEADS * CMP_HIDDEN), F32)
        for tau in range(0, CMP_BLOCK, 2):
            x = jnp.concatenate([x_ref[pl.ds(tau + u, nblk, stride=pitch), :] + pe_ref[c, tau + u:tau + u + 1, :]
                                 for u in range(2)], axis=1)
            acc = acc + _dot(_bf(x), w1_ref[c, tau // 2])
        o_ref[:, c * KV_WIDTH:(c + 1) * KV_WIDTH] = _dot(_bf(jax.nn.gelu(acc)), w2_ref[c])


def _compress_weights(pe_k, pe_v, w_k1, w_k2, w_v1, w_v2):
    pe = jnp.stack([jnp.concatenate([pe_k, pe_k], axis=1), jnp.concatenate([pe_v, pe_v], axis=1)]).astype(F32)
    eye = jnp.eye(B_KV_HEADS, dtype=F32)
    w1 = jnp.stack([w_k1, w_v1]).reshape(2, CMP_BLOCK, HEAD_DIM, CMP_HIDDEN)
    w1 = jnp.einsum('ctdh,ge->ctgdeh', w1, eye).reshape(2, CMP_BLOCK // 2, 2 * KV_WIDTH, B_KV_HEADS * CMP_HIDDEN)
    w2 = jnp.einsum('chd,ge->cghed', jnp.stack([w_k2, w_v2]), eye).reshape(2, B_KV_HEADS * CMP_HIDDEN, KV_WIDTH)
    return pe, _bf(w1), _bf(w2)


COMPRESS_W_SPECS = ((2, CMP_BLOCK, KV_WIDTH), (2, CMP_BLOCK // 2, 2 * KV_WIDTH, B_KV_HEADS * CMP_HIDDEN),
                    (2, B_KV_HEADS * CMP_HIDDEN, KV_WIDTH))


def _compress_paged(cache_t, page_table, cw):
    s, n_pages = page_table.shape
    page = cache_t.shape[2]
    past = n_pages * page
    nblk = past // CMP_BLOCK
    grid_spec = pltpu.PrefetchScalarGridSpec(
        num_scalar_prefetch=1,
        grid=(s,),
        in_specs=[pl.BlockSpec(shp, lambda i, pt, n=len(shp): (0,) * n) for shp in COMPRESS_W_SPECS]
                 + [pl.BlockSpec(memory_space=pl.ANY)],
        out_specs=pl.BlockSpec((nblk, KV_ROW), lambda i, pt: (i, 0)),
        scratch_shapes=[pltpu.VMEM((2, n_pages, KV_ROW, page), F32), pltpu.VMEM((nblk * CMP_PITCH, KV_WIDTH), F32),
                        pltpu.VMEM((nblk * CMP_PITCH, KV_WIDTH), F32), pltpu.SemaphoreType.DMA((2, n_pages))],
    )
    return pl.pallas_call(
        functools.partial(_compress_paged_kernel, n_pages=n_pages, page_len=page),
        grid_spec=grid_spec,
        out_shape=jax.ShapeDtypeStruct((s * nblk, KV_ROW), F32),
        compiler_params=_cparams(("arbitrary",)),
        name="nsa_compress_paged",
    )(page_table.reshape(-1), *_compress_weights(*cw), cache_t)


def _compress(x, pe_k, pe_v, w_k1, w_k2, w_v1, w_v2, rows):
    n = x.shape[0]
    nblk = rows // CMP_BLOCK
    pe, w1, w2 = _compress_weights(pe_k, pe_v, w_k1, w_k2, w_v1, w_v2)
    return pl.pallas_call(
        functools.partial(_compress_kernel, nblk=nblk),
        grid=(n // rows,),
        in_specs=[pl.BlockSpec((rows, KV_WIDTH), lambda i: (i, 0)), pl.BlockSpec((rows, KV_WIDTH), lambda i: (i, 1)),
                  ] + [_full(shp) for shp in COMPRESS_W_SPECS],
        out_specs=pl.BlockSpec((nblk, KV_ROW), lambda i: (i, 0)),
        out_shape=jax.ShapeDtypeStruct((n // CMP_BLOCK, KV_ROW), F32),
        compiler_params=_cparams(("parallel",)),
        name="nsa_compress",
    )(x, x, pe, w1, w2)


def _rel_bucket(dist):
    n = jnp.maximum(dist, 0)
    log_ratio = jnp.log(jnp.maximum(n, 1).astype(F32) / MAX_EXACT) / math.log(MAX_DISTANCE / MAX_EXACT)
    large = jnp.minimum(MAX_EXACT + (log_ratio * (N_BUCKETS - MAX_EXACT)).astype(I32), N_BUCKETS - 1)
    return jnp.where(n < MAX_EXACT, n, large)


def _bias_of(rel_bias, dist):
    onehot = (_rel_bucket(dist)[..., None] == jnp.arange(N_BUCKETS)).astype(F32)
    return jnp.einsum('...b,bh->h...', onehot, rel_bias.astype(F32), precision=HIGHEST)


KEY_TILE = 512


def _nsa_prompt_kernel(q_ref, gate_ref, ck_ref, ckt_ref, ksel_ref, vsel_ref, kwin_ref, vwin_ref, bct_ref, tz_ref,
                       o_ref, imp_scr, *, nc, ns, nq, top_n, wtiles):
    i = pl.program_id(1)
    qb = Q_BLOCK
    gates = jax.nn.sigmoid(gate_ref[0])
    pos_l = i * qb + lax.broadcasted_iota(I32, (1, qb), 1)
    okT = pos_l >= lax.broadcasted_iota(I32, (nc, qb), 0) * CMP_BLOCK + (CMP_BLOCK - 1)
    blk = lax.broadcasted_iota(I32, (ns, qb), 0)
    cur = pos_l // SEL_BLOCK
    forced = (blk == 0) | (blk == cur) | (blk == cur - 1)
    k_t = lax.broadcasted_iota(I32, (KEY_TILE, qb), 0)
    n_e = lax.broadcasted_iota(I32, (KEY_TILE, ns), 1)
    k_e = lax.broadcasted_iota(I32, (KEY_TILE, ns), 0)
    wk = (wtiles + 1) * qb
    wb = jnp.maximum(i - wtiles, 0)
    d_w = pos_l - (wb * qb + lax.broadcasted_iota(I32, (wk, qb), 0))
    madd_w = jnp.where((d_w >= 0) & (d_w < WINDOW), 0.0, NEG_INF)
    n_tiles = (i * qb + qb + KEY_TILE - 1) // KEY_TILE
    per = KEY_TILE // qb

    def bias_tiles(h, first_blk, count):
        return jnp.concatenate([tz_ref[h, jnp.clip(i - (first_blk + c), 0, nq - 1)] for c in range(count)], axis=0)

    ksl = [slice(g * HEAD_DIM, (g + 1) * HEAD_DIM) for g in range(B_KV_HEADS)]
    vsl = [slice(KV_WIDTH + g * HEAD_DIM, KV_WIDTH + (g + 1) * HEAD_DIM) for g in range(B_KV_HEADS)]
    qh = [_bf(q_ref[0, h * HEAD_DIM:(h + 1) * HEAD_DIM, :] * SCALE) for h in range(B_HEADS)]
    o_c, sel_t = [], []
    for g in range(B_KV_HEADS):
        kc = _bf(ck_ref[:, ksl[g]])
        vct = _bf(ckt_ref[0, vsl[g], :])
        impT = jnp.zeros((nc, qb), F32)
        for h in range(g * B_GROUP, (g + 1) * B_GROUP):
            sT = jnp.where(okT, _dot(kc, qh[h]) + bct_ref[h], NEG_INF)
            eT = jnp.exp(sT - jnp.max(sT, axis=0, keepdims=True))
            pT = eT / jnp.sum(eT, axis=0, keepdims=True) * okT.astype(F32)
            impT = impT + pT
            o_c.append(_dot(vct, _bf(pT)))
        imp_scr[...] = impT
        imp2 = imp_scr[pl.ds(0, ns, stride=SEL_RATIO), :] + imp_scr[pl.ds(1, ns, stride=SEL_RATIO), :]
        score = jnp.where(blk <= cur, imp2 + FORCE_SCORE * forced.astype(F32), NEG_INF)
        rank = jnp.zeros((ns, qb), I32)
        for m in range(ns):
            row = score[m:m + 1, :]
            rank = rank + ((row > score) | ((row == score) & (m < blk))).astype(I32)
        sel_t.append(_bf((rank < top_n) & (score > NEG_INF / 2)))

    def body(jt, carry):
        ms, ls, accs = carry
        k0 = pl.multiple_of(jt * KEY_TILE, KEY_TILE)
        causal = k0 + k_t <= pos_l
        in_blk = _bf(n_e == (k0 + k_e) // SEL_BLOCK)
        ms2, ls2, accs2 = [], [], []
        for g in range(B_KV_HEADS):
            kt = _bf(ksel_ref[pl.ds(k0, KEY_TILE), ksl[g]])
            vt = _bf(vsel_ref[0, vsl[g], pl.ds(k0, KEY_TILE)])
            madd = jnp.where((_dot(in_blk, sel_t[g]) > 0.5) & causal, 0.0, NEG_INF)
            for h in range(g * B_GROUP, (g + 1) * B_GROUP):
                s = _dot(kt, qh[h]) + bias_tiles(h, jt * per, per) + madd
                m_new = jnp.maximum(ms[h], jnp.max(s, axis=0, keepdims=True))
                alpha = jnp.exp(ms[h] - m_new)
                p = jnp.exp(s - m_new)
                ls2.append(alpha * ls[h] + jnp.sum(p, axis=0, keepdims=True))
                accs2.append(alpha * accs[h] + _dot(vt, _bf(p)))
                ms2.append(m_new)
        return tuple(ms2), tuple(ls2), tuple(accs2)

    init = (tuple(jnp.full((1, qb), NEG_INF, F32) for _ in range(B_HEADS)),
            tuple(jnp.zeros((1, qb), F32) for _ in range(B_HEADS)),
            tuple(jnp.zeros((HEAD_DIM, qb), F32) for _ in range(B_HEADS)))
    _, ls, accs = lax.fori_loop(0, n_tiles, body, init)
    w0 = pl.multiple_of(wb * qb, qb)
    for g in range(B_KV_HEADS):
        ktw = _bf(kwin_ref[pl.ds(w0, wk), ksl[g]])
        vtw = _bf(vwin_ref[0, vsl[g], pl.ds(w0, wk)])
        for h in range(g * B_GROUP, (g + 1) * B_GROUP):
            s = _dot(ktw, qh[h]) + bias_tiles(h, wb, wtiles + 1) + madd_w
            p = jnp.exp(s - jnp.max(s, axis=0, keepdims=True))
            o_w = _dot(vtw, _bf(p)) / jnp.sum(p, axis=0, keepdims=True)
            o_s = accs[h] / ls[h]
            o_ref[0, h * HEAD_DIM:(h + 1) * HEAD_DIM, :] = (
                gates[3 * h:3 * h + 1] * o_c[h] + gates[3 * h + 1:3 * h + 2] * o_s + gates[3 * h + 2:3 * h + 3] * o_w)


def _nsa_prompt(q_t, gate_t, ckv, kv_sel, kvt_sel, kv_win, kvt_win, rel_bias):
    b, _, t = q_t.shape
    nq = t // Q_BLOCK
    nc = t // CMP_BLOCK
    ns = -(-t // SEL_BLOCK)
    wtiles = WINDOW // Q_BLOCK
    assert t % KEY_TILE == 0 and nc == SEL_RATIO * ns and WINDOW % Q_BLOCK == 0 and (wtiles + 1) * Q_BLOCK <= t
    cmp_end = jnp.arange(nc) * CMP_BLOCK + (CMP_BLOCK - 1)
    bct = _bias_of(rel_bias, jnp.arange(t)[None, :] - cmp_end[:, None])
    dz = (jnp.arange(nq)[:, None, None] * Q_BLOCK + jnp.arange(Q_BLOCK)[None, None, :]
          - jnp.arange(Q_BLOCK)[None, :, None])
    tz = _bias_of(rel_bias, dz)
    ckt = jnp.swapaxes(ckv.reshape(b, nc, KV_ROW), 1, 2)
    chan = lambda w: pl.BlockSpec((1, w, Q_BLOCK), lambda bi, i: (bi, 0, i))
    seq_rows = pl.BlockSpec((t, KV_ROW), lambda bi, i: (bi, 0))
    seq_chan = pl.BlockSpec((1, KV_ROW, t), lambda bi, i: (bi, 0, 0))
    return pl.pallas_call(
        functools.partial(_nsa_prompt_kernel, nc=nc, ns=ns, nq=nq, top_n=min(TOP_N, ns), wtiles=wtiles),
        grid=(b, nq),
        in_specs=[chan(B_WIDTH), chan(GATE_PAD),
                  pl.BlockSpec((nc, KV_ROW), lambda bi, i: (bi, 0)),
                  pl.BlockSpec((1, KV_ROW, nc), lambda bi, i: (bi, 0, 0)),
                  seq_rows, seq_chan, seq_rows, seq_chan,
                  pl.BlockSpec((B_HEADS, nc, Q_BLOCK), lambda bi, i: (0, 0, i)),
                  _full((B_HEADS, nq, Q_BLOCK, Q_BLOCK))],
        out_specs=chan(B_WIDTH),
        out_shape=jax.ShapeDtypeStruct((b, B_WIDTH, t), F32),
        scratch_shapes=[pltpu.VMEM((nc, Q_BLOCK), F32)],
        compiler_params=_cparams(("parallel", "arbitrary")),
        name="nsa_prompt",
    )(q_t, gate_t, ckv, ckt, kv_sel, kvt_sel, kv_win, kvt_win, bct, tz)


def _group_q(q_ref, g):
    rows = [q_ref[0, :, (g * B_GROUP + r) * HEAD_DIM:(g * B_GROUP + r + 1) * HEAD_DIM] for r in range(B_GROUP)]
    return _bf(jnp.concatenate(rows, axis=0) * SCALE)


def _nsa_sample_select_kernel(q_ref, gate_ref, ck_ref, bc_ref, pair_ref, tri_ref, oc_ref, idx_ref,
                              *, nsp, cur, top_n):
    gates = jax.nn.sigmoid(gate_ref[0])
    blk = lax.broadcasted_iota(I32, (1, nsp), 1)
    forced = (blk == 0) | (blk == cur) | (blk == cur - 1)
    mi = lax.broadcasted_iota(I32, (nsp, nsp), 0)
    ni = lax.broadcasted_iota(I32, (nsp, nsp), 1)
    kk = lax.broadcasted_iota(I32, (TOP_N, nsp), 0).astype(F32)
    nf = lax.broadcasted_iota(I32, (TOP_N, nsp), 1).astype(F32)
    outs = []
    for g in range(B_KV_HEADS):
        q4 = _group_q(q_ref, g)
        kc = _bf(ck_ref[:, g * HEAD_DIM:(g + 1) * HEAD_DIM])
        vc = _bf(ck_ref[:, KV_WIDTH + g * HEAD_DIM:KV_WIDTH + (g + 1) * HEAD_DIM])
        s = _dot_nt(q4, kc) + bc_ref[g * B_GROUP:(g + 1) * B_GROUP, :]
        e = jnp.exp(s - jnp.max(s, axis=-1, keepdims=True))
        p = e / jnp.sum(e, axis=-1, keepdims=True)
        o_c = _dot(_bf(p), vc)
        imp = ((p[0:1] + p[1:2]) + p[2:3]) + p[3:4]
        imp2 = jnp.dot(imp, pair_ref[...], precision=HIGHEST, preferred_element_type=F32)
        score = jnp.where(blk <= cur, imp2 + FORCE_SCORE * forced.astype(F32), NEG_INF)
        m1 = jnp.broadcast_to(score, (nsp, nsp))
        m2 = m1.T
        gt = (m2 > m1) | ((m2 == m1) & (mi < ni))
        rank = jnp.sum(gt.astype(F32), axis=0, keepdims=True)
        sel = (rank < top_n) & (score > NEG_INF / 2)
        before = _dot(_bf(sel), tri_ref[...])
        hit = jnp.broadcast_to(sel, (TOP_N, nsp)) & (jnp.broadcast_to(before, (TOP_N, nsp)) == kk)
        idx = jnp.sum(jnp.where(hit, nf, 0.0), axis=1, keepdims=True)
        cnt = jnp.sum(hit.astype(F32), axis=1, keepdims=True)
        idx_ref[0, g] = jnp.where(cnt > 0.5, idx, -1.0).astype(I32)
        for r in range(B_GROUP):
            h = g * B_GROUP + r
            outs.append(gates[:, 3 * h:3 * h + 1] * o_c[r:r + 1])
    oc_ref[0] = jnp.concatenate(outs, axis=1)


def _nsa_sample_select(q3, gate3, ckv, rel_bias, past_len):
    s = q3.shape[0]
    nc = past_len // CMP_BLOCK
    ns = -(-(past_len + 1) // SEL_BLOCK)
    nsp = -(-ns // LANES) * LANES
    cur = past_len // SEL_BLOCK
    bc = _bias_of(rel_bias, past_len - (jnp.arange(nc) * CMP_BLOCK + (CMP_BLOCK - 1)))
    pair = jnp.asarray(np.arange(nc)[:, None] // SEL_RATIO == np.arange(nsp)[None, :], F32)
    tri = jnp.asarray(np.arange(nsp)[:, None] < np.arange(nsp)[None, :], BF16)
    return pl.pallas_call(
        functools.partial(_nsa_sample_select_kernel, nsp=nsp, cur=cur, top_n=min(TOP_N, ns)),
        grid=(s,),
        in_specs=[pl.BlockSpec((1, 1, B_WIDTH), lambda i: (i, 0, 0)), pl.BlockSpec((1, 1, GATE_PAD), lambda i: (i, 0, 0)),
                  pl.BlockSpec((nc, KV_ROW), lambda i: (i, 0)), _full((B_HEADS, nc)), _full((nc, nsp)), _full((nsp, nsp))],
        out_specs=[pl.BlockSpec((1, 1, B_WIDTH), lambda i: (i, 0, 0)),
                   pl.BlockSpec((1, B_KV_HEADS, TOP_N, 1), lambda i: (i, 0, 0, 0))],
        out_shape=[jax.ShapeDtypeStruct((s, 1, B_WIDTH), F32), jax.ShapeDtypeStruct((s, B_KV_HEADS, TOP_N, 1), I32)],
        compiler_params=_cparams(("parallel",)),
        name="nsa_sample_select",
    )(q3, gate3, ckv, bc, pair, tri)


def _nsa_sample_attend_kernel(idx_ref, pt_ref, q_ref, gate_ref, oc_ref, ksel_ref, kwin_ref, win_ref, fpg_ref, bw_ref,
                              cache_ref, o_ref, buf, sem, *, cur, n_pages, past_len, page_len):
    i = pl.program_id(0)
    n_s = pl.num_programs(0)
    slot = i % 2
    halves = PAGE_BLOCKS
    n_slot = B_KV_HEADS * TOP_N

    def block_of(s_idx, j):
        return idx_ref[s_idx * n_slot + j]

    def copy(s_idx, sl, j):
        n = jnp.clip(block_of(s_idx, j), 0, cur - 1)
        page = pt_ref[s_idx * n_pages + n // halves]
        return pltpu.make_async_copy(cache_ref.at[page], buf.at[sl, j], sem.at[sl])

    def cached(s_idx, j):
        n = block_of(s_idx, j)
        return (n >= 0) & (n < cur)

    def fetch(s_idx, sl):
        for j in range(n_slot):
            @pl.when(cached(s_idx, j))
            def _():
                copy(s_idx, sl, j).start()

    @pl.when(i == 0)
    def _():
        fetch(0, 0)

    @pl.when(i + 1 < n_s)
    def _():
        fetch(i + 1, 1 - slot)

    for j in range(n_slot):
        @pl.when(cached(i, j))
        def _():
            copy(i, slot, j).wait()

        @pl.when(jnp.logical_not(cached(i, j)))
        def _():
            buf[slot, j] = jnp.zeros((KV_ROW, page_len), F32)
            buf[slot, j, :, 0:1] = ksel_ref[0]

    gates = jax.nn.sigmoid(gate_ref[0])
    t_l = lax.broadcasted_iota(I32, (1, page_len), 1)
    n_buf = win_ref.shape[2]
    j_w = lax.broadcasted_iota(I32, (1, n_buf), 1)
    d_w = n_buf - j_w
    ok_w = (d_w >= 0) & (d_w < WINDOW) & (past_len - d_w >= 0)
    f0 = fpg_ref[n_pages]
    outs = []
    for g in range(B_KV_HEADS):
        q4 = _group_q(q_ref, g)
        ksl = slice(g * HEAD_DIM, (g + 1) * HEAD_DIM)
        vsl = slice(KV_WIDTH + g * HEAD_DIM, KV_WIDTH + (g + 1) * HEAD_DIM)
        pieces = []
        for k in range(TOP_N):
            j = g * TOP_N + k
            n = block_of(i, j)
            pg = jnp.clip(n, 0, cur) // halves
            fb = fpg_ref[pg][g * B_GROUP:(g + 1) * B_GROUP, :]
            s_k = _dot(q4, _bf(buf[slot, j, ksl, :])) + fb
            ok = (n >= 0) & (t_l // SEL_BLOCK == n % halves) & (pg * page_len + t_l <= past_len)
            pieces.append(jnp.where(ok, s_k, NEG_INF))
        s = jnp.concatenate(pieces, axis=1)
        e = jnp.exp(s - jnp.max(s, axis=-1, keepdims=True))
        p = _bf(e / jnp.sum(e, axis=-1, keepdims=True))
        o_s = jnp.zeros((B_GROUP, HEAD_DIM), F32)
        for k in range(TOP_N):
            o_s = o_s + _dot_nt(p[:, k * page_len:(k + 1) * page_len], _bf(buf[slot, g * TOP_N + k, vsl, :]))
        kw = _bf(win_ref[0, ksl, :])
        vw = _bf(win_ref[0, vsl, :])
        s_w = jnp.where(ok_w, _dot(q4, kw) + bw_ref[g * B_GROUP:(g + 1) * B_GROUP, :], NEG_INF)
        k_new = _bf(kwin_ref[0, :, ksl])
        v_new = _bf(kwin_ref[0, :, vsl])
        s_n = (jnp.sum(q4.astype(F32) * k_new.astype(F32), axis=-1, keepdims=True)
               + f0[g * B_GROUP:(g + 1) * B_GROUP, 0:1])
        m = jnp.maximum(jnp.max(s_w, axis=-1, keepdims=True), s_n)
        e_w = jnp.exp(s_w - m)
        e_n = jnp.exp(s_n - m)
        den = jnp.sum(e_w, axis=-1, keepdims=True) + e_n
        o_w = _dot_nt(_bf(e_w / den), vw) + _bf(e_n / den).astype(F32) * v_new.astype(F32)
        for r in range(B_GROUP):
            h = g * B_GROUP + r
            outs.append(gates[:, 3 * h + 1:3 * h + 2] * o_s[r:r + 1] + gates[:, 3 * h + 2:3 * h + 3] * o_w[r:r + 1])
    o_ref[0] = oc_ref[0] + jnp.concatenate(outs, axis=1)


PAGE_BLOCKS = 2


def _nsa_sample_attend(idx, page_table, q3, gate3, oc3, ksel_t, kwin3, win_t, cache_t, rel_bias, past_len):
    s = q3.shape[0]
    n_pages = page_table.shape[1]
    cur = past_len // SEL_BLOCK
    n_buf = win_t.shape[2]
    page = cache_t.shape[2]
    assert page == PAGE_BLOCKS * SEL_BLOCK and past_len == n_pages * page
    keypos = jnp.arange(n_pages + 1)[:, None] * page + jnp.arange(page)[None, :]
    fpg = jnp.moveaxis(_bias_of(rel_bias, past_len - keypos), 0, 1)
    bw = _bias_of(rel_bias, n_buf - jnp.arange(n_buf))
    row3 = lambda w: pl.BlockSpec((1, 1, w), lambda i, *_: (i, 0, 0))
    grid_spec = pltpu.PrefetchScalarGridSpec(
        num_scalar_prefetch=2,
        grid=(s,),
        in_specs=[row3(B_WIDTH), row3(GATE_PAD), row3(B_WIDTH),
                  pl.BlockSpec((1, KV_ROW, 1), lambda i, *_: (i, 0, 0)), row3(KV_ROW),
                  pl.BlockSpec((1, KV_ROW, n_buf), lambda i, *_: (i, 0, 0)),
                  pl.BlockSpec((n_pages + 1, B_HEADS, page), lambda i, *_: (0, 0, 0)),
                  pl.BlockSpec((B_HEADS, n_buf), lambda i, *_: (0, 0)),
                  pl.BlockSpec(memory_space=pl.ANY)],
        out_specs=row3(B_WIDTH),
        scratch_shapes=[pltpu.VMEM((2, B_KV_HEADS * TOP_N, KV_ROW, page), F32), pltpu.SemaphoreType.DMA((2,))],
    )
    return pl.pallas_call(
        functools.partial(_nsa_sample_attend_kernel, cur=cur, n_pages=n_pages, past_len=past_len, page_len=page),
        grid_spec=grid_spec,
        out_shape=jax.ShapeDtypeStruct((s, 1, B_WIDTH), F32),
        compiler_params=_cparams(("arbitrary",)),
        name="nsa_sample_attend",
    )(idx.reshape(-1), page_table.reshape(-1), q3, gate3, oc3, ksel_t, kwin3, win_t, fpg, bw, cache_t)


def _merge_kernel(x_ref, yn_ref, bo_ref, ga_ref, yb_ref, pg_ref, lnw_ref, lnb_ref, wa_ref, wb_ref, wo_ref, nf_ref,
                  wr_ref, br_ref, x1_ref, h_ref, te_ref, tw_ref, *, a_transposed):
    if a_transposed:
        tm = x_ref.shape[0]
        yn = yn_ref[...].reshape(A_WIDTH, tm)
        y_a = ((yn * lnw_ref[...] + lnb_ref[...] + bo_ref[0]) * ga_ref[0]).T
        y_b = yb_ref[0].T
    else:
        y_a = (yn_ref[...] * lnw_ref[...] + lnb_ref[...] + bo_ref[...]) * ga_ref[...]
        y_b = yb_ref[...]
    g_a = jax.nn.sigmoid(pg_ref[:, :D_MODEL])
    g_b = jax.nn.sigmoid(pg_ref[:, D_MODEL:])
    m = g_a * _dot(_bf(y_a), wa_ref[...]) + g_b * _dot(_bf(y_b), wb_ref[...])
    x1 = x_ref[...] + _dot(_bf(m), wo_ref[...])
    x1_ref[...] = x1
    h = x1 * lax.rsqrt(jnp.mean(x1 * x1, axis=-1, keepdims=True) + NORM_EPS) * nf_ref[...]
    _store_rows(h_ref, h)
    logits = _dot(_bf(h), wr_ref[...]) + br_ref[...]
    lane = lax.broadcasted_iota(I32, logits.shape, 1).astype(F32)
    col = lax.broadcasted_iota(I32, (logits.shape[0], TOP_K), 1)
    vals, idxs = [], []
    for _ in range(TOP_K):
        m_k = jnp.max(logits, axis=-1, keepdims=True)
        i_k = jnp.min(jnp.where(logits == m_k, lane, float(N_EXPERTS)), axis=-1, keepdims=True)
        vals.append(m_k)
        idxs.append(i_k)
        logits = jnp.where(lane == i_k, -jnp.inf, logits)
    e = [jnp.exp(v - vals[0]) for v in vals]
    den = ((e[0] + e[1]) + e[2]) + e[3]
    te = jnp.zeros(col.shape, F32)
    tw = jnp.zeros(col.shape, F32)
    for k in range(TOP_K):
        te = jnp.where(col == k, idxs[k], te)
        tw = jnp.where(col == k, e[k] / den, tw)
    te_ref[...] = te.astype(I32)
    tw_ref[...] = tw


def _merge(x, yn, bonus, gate, yb, pg, ln_w, ln_b, w_a, w_b, w_o, norm_ffn, w_router, b_router, tm, seq=None):
    rows = x.shape[0]
    row = lambda w: pl.BlockSpec((tm, w), lambda i: (i, 0))
    vec = lambda z: z.reshape(1, -1).astype(F32)
    if seq is None:
        a_specs = [row(A_WIDTH)] * 3 + [row(B_WIDTH)] + [_full((1, A_WIDTH))] * 2
        lnw, lnb = vec(ln_w), vec(ln_b)
    else:
        per = seq // tm
        nspec = pl.BlockSpec((1, A_WIDTH, tm), lambda i: (i // per, 0, i % per))
        a_specs = [pl.BlockSpec((HEAD_DIM, 1, A_HEADS, tm), lambda i: (0, i // per, 0, i % per)), nspec, nspec, nspec,
                   _full((A_WIDTH, 1)), _full((A_WIDTH, 1))]
        lnw, lnb = (z[CH_MAJOR].reshape(-1, 1).astype(F32) for z in (ln_w, ln_b))
        w_a = w_a[CH_MAJOR]
    return pl.pallas_call(
        functools.partial(_merge_kernel, a_transposed=seq is not None),
        grid=(rows // tm,),
        in_specs=[row(D_MODEL)] + a_specs[:4] + [row(GATE_PROJ)] + a_specs[4:]
                 + [_full((A_WIDTH, D_MODEL)), _full((B_WIDTH, D_MODEL)),
                  _full((D_MODEL, D_MODEL)), _full((1, D_MODEL)), _full((D_MODEL, N_EXPERTS)), _full((1, N_EXPERTS))],
        out_specs=[row(D_MODEL), pl.BlockSpec((tm * ROW_TILE, LANES), lambda i: (i, 0)), row(TOP_K), row(TOP_K)],
        out_shape=[jax.ShapeDtypeStruct((rows, D_MODEL), F32), jax.ShapeDtypeStruct((rows * ROW_TILE, LANES), F32),
                   jax.ShapeDtypeStruct((rows, TOP_K), I32), jax.ShapeDtypeStruct((rows, TOP_K), F32)],
        compiler_params=_cparams(("parallel",)),
        name="merge_router",
    )(x, yn, bonus, gate, yb, pg, lnw, lnb, _bf(w_a), _bf(w_b), _bf(w_o), vec(norm_ffn),
      _bf(w_router), vec(b_router))


ROW_TILE = D_MODEL // LANES


def _load_rows(ref, n, first=0, every=1):
    return jnp.concatenate([ref[pl.ds(first * ROW_TILE + s, n, stride=every * ROW_TILE), :] for s in range(ROW_TILE)],
                           axis=1)


def _store_rows(ref, x):
    for s in range(ROW_TILE):
        ref[pl.ds(s, x.shape[0], stride=ROW_TILE), :] = x[:, s * LANES:(s + 1) * LANES]


def _dispatch_kernel(dest_ref, h_ref, xs_in_ref, xs_ref, sem, *, tm):
    del xs_in_ref
    for t in range(tm):
        for k in range(TOP_K):
            pltpu.make_async_copy(h_ref.at[t], xs_ref.at[dest_ref[0, 0, t * TOP_K + k]], sem).start()
    for t in range(tm * TOP_K):
        pltpu.make_async_copy(h_ref.at[0], xs_ref.at[0], sem).wait()


def _dispatch(dest, h_tiles, slots, tm):
    n = h_tiles.shape[0]
    return pl.pallas_call(
        functools.partial(_dispatch_kernel, tm=tm),
        grid=(n // tm,),
        in_specs=[pl.BlockSpec((1, 1, tm * TOP_K), lambda i: (i, 0, 0), memory_space=pltpu.SMEM),
                  pl.BlockSpec((tm, ROW_TILE, LANES), lambda i: (i, 0, 0)), pl.BlockSpec(memory_space=pl.ANY)],
        out_specs=pl.BlockSpec(memory_space=pl.ANY),
        out_shape=jax.ShapeDtypeStruct((slots, ROW_TILE, LANES), F32),
        scratch_shapes=[pltpu.SemaphoreType.DMA(())],
        input_output_aliases={2: 0},
        compiler_params=_cparams(("arbitrary",)),
        name="moe_dispatch",
    )(dest.reshape(n // tm, 1, tm * TOP_K), h_tiles, jnp.zeros((slots, ROW_TILE, LANES), F32))


def _moe_kernel(ce_ref, nu_ref, x_ref, wu_ref, bu_ref, wd_ref, bd_ref, o_ref, wu_bf, wd_bf):
    c = pl.program_id(0)
    e = ce_ref[c]
    prev = ce_ref[jnp.maximum(c - 1, 0)]

    @pl.when((c == 0) | (e != prev))
    def _():
        wu_bf[...] = _bf(wu_ref[0])
        wd_bf[...] = _bf(wd_ref[0])

    @pl.when(c < nu_ref[0])
    def _():
        u = _dot(_bf(_load_rows(x_ref, MOE_ROWS)), wu_bf[...]) + bu_ref[0]
        glu = jnp.minimum(u[:, :D_FF], SWIGLU_LIMIT)
        lin = jnp.clip(u[:, D_FF:], -SWIGLU_LIMIT, SWIGLU_LIMIT)
        act = glu * jax.nn.sigmoid(SWIGLU_ALPHA * glu) * (lin + 1.0)
        _store_rows(o_ref, _dot(_bf(act), wd_bf[...]) + bd_ref[0])

    @pl.when(c >= nu_ref[0])
    def _():
        o_ref[...] = jnp.zeros_like(o_ref)


def _moe_experts(chunk_e, n_used, xs, w_up, b_up, w_down, b_down):
    n_chunks = xs.shape[0] // (MOE_ROWS * ROW_TILE)
    rows = pl.BlockSpec((MOE_ROWS * ROW_TILE, LANES), lambda c, ce, nu: (c, 0))
    grid_spec = pltpu.PrefetchScalarGridSpec(
        num_scalar_prefetch=2,
        grid=(n_chunks,),
        in_specs=[rows,
                  pl.BlockSpec((1, D_MODEL, 2 * D_FF), lambda c, ce, nu: (ce[c], 0, 0)),
                  pl.BlockSpec((1, 1, 2 * D_FF), lambda c, ce, nu: (ce[c], 0, 0)),
                  pl.BlockSpec((1, D_FF, D_MODEL), lambda c, ce, nu: (ce[c], 0, 0)),
                  pl.BlockSpec((1, 1, D_MODEL), lambda c, ce, nu: (ce[c], 0, 0))],
        out_specs=rows,
        scratch_shapes=[pltpu.VMEM((D_MODEL, 2 * D_FF), BF16), pltpu.VMEM((D_FF, D_MODEL), BF16)],
    )
    return pl.pallas_call(
        _moe_kernel,
        grid_spec=grid_spec,
        out_shape=jax.ShapeDtypeStruct(xs.shape, F32),
        compiler_params=_cparams(("arbitrary",)),
        name="moe_experts",
    )(chunk_e, n_used, xs, w_up, b_up.reshape(N_EXPERTS, 1, -1), w_down, b_down.reshape(N_EXPERTS, 1, -1))


def _combine_kernel(dcur_ref, dnext_ref, x_ref, w_ref, g_ref, ys_ref, o_ref, buf, sem, *, tm, normalize):
    i = pl.program_id(0)
    slot = i % 2
    n_a = tm * TOP_K

    def copy(d_ref, sl, a):
        return pltpu.make_async_copy(ys_ref.at[d_ref[0, 0, a]], buf.at[sl, pl.ds(a * ROW_TILE, ROW_TILE)], sem.at[sl])

    def fetch(d_ref, sl):
        for a in range(n_a):
            copy(d_ref, sl, a).start()

    @pl.when(i == 0)
    def _():
        fetch(dcur_ref, 0)

    @pl.when(i + 1 < pl.num_programs(0))
    def _():
        fetch(dnext_ref, 1 - slot)

    for a in range(n_a):
        copy(dcur_ref, slot, a).wait()
    x = x_ref[...]
    for k in range(TOP_K):
        x = x + w_ref[:, k:k + 1] * _load_rows(buf.at[slot], tm, first=k, every=TOP_K)
    if normalize:
        x = x * lax.rsqrt(jnp.mean(x * x, axis=-1, keepdims=True) + NORM_EPS) * g_ref[...]
    o_ref[...] = x


def _combine(dest, x1, top_w, ys_tiles, g, tm, normalize):
    n = x1.shape[0]
    steps = n // tm
    d2 = dest.reshape(steps, 1, tm * TOP_K)
    row = lambda w: pl.BlockSpec((tm, w), lambda i: (i, 0))
    return pl.pallas_call(
        functools.partial(_combine_kernel, tm=tm, normalize=normalize),
        grid=(steps,),
        in_specs=[pl.BlockSpec((1, 1, tm * TOP_K), lambda i: (i, 0, 0), memory_space=pltpu.SMEM),
                  pl.BlockSpec((1, 1, tm * TOP_K), lambda i: (jnp.minimum(i + 1, steps - 1), 0, 0),
                               memory_space=pltpu.SMEM),
                  row(D_MODEL), row(TOP_K), _full((1, D_MODEL)), pl.BlockSpec(memory_space=pl.ANY)],
        out_specs=row(D_MODEL),
        out_shape=jax.ShapeDtypeStruct((n, D_MODEL), F32),
        scratch_shapes=[pltpu.VMEM((2, tm * TOP_K * ROW_TILE, LANES), F32), pltpu.SemaphoreType.DMA((2,))],
        compiler_params=_cparams(("arbitrary",)),
        name="moe_combine",
    )(d2, d2, x1, top_w, g.reshape(1, -1).astype(F32), ys_tiles)


def _moe_residual(x1, h_tiles, top_e, top_w, w_up, b_up, w_down, b_down, g, normalize):
    n = x1.shape[0]
    n_assign = n * TOP_K
    onehot = (top_e.reshape(-1, 1) == jnp.arange(N_EXPERTS, dtype=I32)[None, :]).astype(I32)
    csum = jnp.cumsum(onehot, axis=0)
    counts = csum[-1]
    padded = (counts + MOE_ROWS - 1) // MOE_ROWS * MOE_ROWS
    pad_end = jnp.cumsum(padded)
    dest = jnp.sum(onehot * (csum - 1 + (pad_end - padded)[None, :]), axis=1).astype(I32)
    n_chunks = -(-(n_assign + N_EXPERTS * (MOE_ROWS - 1)) // MOE_ROWS)
    slots = n_chunks * MOE_ROWS
    chunk_e = jnp.sum(pad_end[None, :] <= (jnp.arange(n_chunks) * MOE_ROWS)[:, None], axis=1)
    chunk_e = jnp.minimum(chunk_e, N_EXPERTS - 1).astype(I32)
    n_used = (pad_end[-1] // MOE_ROWS).astype(I32).reshape(1)
    tm = _row_tile(n, 256)
    xs = _dispatch(dest, h_tiles.reshape(n, ROW_TILE, LANES), slots, tm)
    ys = _moe_experts(chunk_e, n_used, xs.reshape(slots * ROW_TILE, LANES), w_up, b_up, w_down, b_down)
    return _combine(dest, x1, top_w, ys.reshape(slots, ROW_TILE, LANES), g, tm, normalize)


def _row_tile(rows, cap):
    tm = cap
    while rows % tm:
        tm //= 2
    return tm


def kernel(x_prompt, x_sample, cache_cmp_kv, cache_sel_kv, state_win_kv, state_rwkv, state_rwkv_shift, page_table,
           norm_attn, w_in, mu_shift, w0, w_decay_up, a0, w_iclr_up, w_gate_up, k_k, k_a, r_k, ln_x_w, ln_x_b,
           pe_cmp_k, pe_cmp_v, w_cmp_k1, w_cmp_k2, w_cmp_v1, w_cmp_v2, rel_bias, w_br_a, w_br_b, w_out,
           norm_ffn, w_router, b_router, w_up, b_up, w_down, b_down, norm_final):
    bp, tp, _ = x_prompt.shape
    bs, ts, _ = x_sample.shape
    depth = w_in.shape[0]
    past_len = page_table.shape[1] * cache_cmp_kv.shape[2]
    n_buf = state_win_kv.shape[2]
    assert ts == 1 and bp * A_HEADS == BH_LANES and bs % 8 == 0 and tp % LANES == 0
    xp = x_prompt.reshape(bp * tp, D_MODEL)
    xs = x_sample.reshape(bs, D_MODEL)
    new = {name: [] for name in ('cmp_p', 'sel_p', 'win_p', 'wkv_p', 'shift_p', 'cmp_s', 'sel_s', 'win_s', 'wkv_s', 'shift_s')}
    kv6 = lambda z, b, t: z.reshape(b, t, 2, B_KV_HEADS, HEAD_DIM)
    for l in range(depth):
        rw = (mu_shift[l], w0[l], w_decay_up[l], a0[l], w_iclr_up[l], w_gate_up[l], k_k[l], k_a[l], r_k[l])
        cw = (pe_cmp_k[l], pe_cmp_v[l], w_cmp_k1[l], w_cmp_k2[l], w_cmp_v1[l], w_cmp_v2[l])
        wa_nat, w_rest, w_t, w_n = _pack_w_in(w_in[l])
        g_attn = norm_attn[l].reshape(1, -1).astype(F32)
        last = l == depth - 1
        g_fin = norm_final if last else jnp.ones((D_MODEL,), F32)

        tq = _row_tile(tp, 256)
        pa_t, q_t, kvt_c, kvt_s, kvt_w, gt_t, kv_c, kv_s, kv_w, pg = _project_t(xp, g_attn, w_t, w_n, tq, tp)
        r, w, k2, v, kk, kb, gate, bonus = _rwkv_prep_t(pa_t, *rw, tq)
        r, w, k2, kk, kb = _key_tiles([z.reshape(N_JP, LANES, tp) for z in (r, w, k2, kk, kb)], LANES)
        v = _val_tiles(v.reshape(HEAD_DIM, BH_LANES, tp), LANES).reshape(1, tp, HEAD_DIM, LANES)
        s0 = jnp.zeros((1, N_IO, N_JP, 8, LANES), F32)
        yn, s_fin = _wkv_scan(r, w, k2, v, kk, kb, s0, _row_tile(tp, 64))
        yn = _val_untile(yn.reshape(tp * HEAD_DIM, LANES), LANES).reshape(HEAD_DIM, bp, A_HEADS, tp)
        ckv = _compress(kv_c, *cw, rows=_row_tile(tp, 2048))
        y_b = _nsa_prompt(q_t, gt_t, ckv, kv_s, kvt_s, kv_w, kvt_w, rel_bias)
        x1, h, top_e, top_w = _merge(xp, yn, bonus, gate, y_b, pg, ln_x_w[l], ln_x_b[l], w_br_a[l], w_br_b[l], w_out[l],
                                     norm_ffn[l], w_router[l], b_router[l], tq, seq=tp)
        xp_next = _moe_residual(x1, h, top_e, top_w, w_up[l], b_up[l], w_down[l], b_down[l], g_fin, last)
        n_win = min(WINDOW, tp)
        kv6t = lambda z: jnp.moveaxis(z.reshape(bp, 2, B_KV_HEADS, HEAD_DIM, tp), -1, 1)
        new['cmp_p'].append(kv6t(kvt_c))
        new['sel_p'].append(kv6t(kvt_s))
        new['win_p'].append(kv6t(kvt_w)[:, tp - n_win:])
        new['wkv_p'].append(_state_from_tiles(s_fin, bp))
        new['shift_p'].append(jnp.zeros((bp, A_PROJ), F32).at[:, A_PERM].set(pa_t[:, :, tp - 1]))

        s_a, q, kv_c, kv_s, kv_w, gt, pg = _project(xs, g_attn, wa_nat, w_rest, _row_tile(bs, 256))
        r, w, k2, v, kk, kb, gate, bonus = _rwkv_prep(s_a, state_rwkv_shift[l], *rw, _row_tile(bs, 256))
        yn, wkv_s = _wkv_step(r, w, k2, v, kk, kb, jnp.transpose(state_rwkv[l].astype(F32), (1, 2, 3, 0)))
        wkv_s = jnp.transpose(wkv_s, (3, 0, 1, 2))
        tok_minor = lambda z: jnp.moveaxis(z, 1, -1).reshape(z.shape[0], KV_ROW, z.shape[1])
        ckv = _compress_paged(tok_minor(cache_cmp_kv[l]), page_table, cw)
        q3, gt3 = q.reshape(bs, 1, B_WIDTH), gt.reshape(bs, 1, GATE_PAD)
        oc3, idx = _nsa_sample_select(q3, gt3, ckv, rel_bias, past_len)
        y_b = _nsa_sample_attend(idx, page_table, q3, gt3, oc3, kv_s.reshape(bs, KV_ROW, 1), kv_w.reshape(bs, 1, KV_ROW),
                                 tok_minor(state_win_kv[l]), tok_minor(cache_sel_kv[l]),
                                 rel_bias, past_len).reshape(bs, B_WIDTH)
        x1, h, top_e, top_w = _merge(xs, yn, bonus, gate, y_b, pg, ln_x_w[l], ln_x_b[l], w_br_a[l], w_br_b[l], w_out[l],
                                     norm_ffn[l], w_router[l], b_router[l], _row_tile(bs, 256))
        xs_next = _moe_residual(x1, h, top_e, top_w, w_up[l], b_up[l], w_down[l], b_down[l], g_fin, last)
        new['cmp_s'].append(kv6(kv_c, bs, 1))
        new['sel_s'].append(kv6(kv_s, bs, 1))
        new['win_s'].append(jnp.concatenate([state_win_kv[l], kv6(kv_w, bs, 1)], axis=1)[:, 1:])
        new['wkv_s'].append(wkv_s.astype(state_rwkv.dtype))
        new['shift_s'].append(s_a)
        xp, xs = xp_next, xs_next
    return (xp.reshape(bp, tp, D_MODEL), xs.reshape(bs, ts, D_MODEL),
            jnp.stack(new['cmp_p']), jnp.stack(new['sel_p']), jnp.stack(new['win_p']),
            jnp.stack(new['wkv_p']), jnp.stack(new['shift_p']),
            jnp.stack(new['cmp_s']), jnp.stack(new['sel_s']), jnp.stack(new['win_s']),
            jnp.stack(new['wkv_s']), jnp.stack(new['shift_s']))
```

```python
import functools
import math

import jax
import jax.numpy as jnp
import numpy as np
from jax import lax
from jax.experimental import pallas as pl
from jax.experimental.pallas import tpu as pltpu

F32 = jnp.float32
BF16 = jnp.bfloat16
I32 = jnp.int32
HIGHEST = lax.Precision.HIGHEST

D_MODEL = 1024
HEAD_DIM = 64
A_HEADS = 8
A_WIDTH = A_HEADS * HEAD_DIM
A_DECAY_RANK = 64
A_ICLR_RANK = 64
A_GATE_RANK = 128
A_GN_EPS = 64e-5
A_PROJ = 3 * A_WIDTH + A_DECAY_RANK + A_ICLR_RANK + A_GATE_RANK
B_HEADS = 8
B_KV_HEADS = 2
B_GROUP = B_HEADS // B_KV_HEADS
B_WIDTH = B_HEADS * HEAD_DIM
KV_WIDTH = B_KV_HEADS * HEAD_DIM
KV_ROW = 2 * KV_WIDTH
CMP_BLOCK = 32
CMP_HIDDEN = 128
SEL_BLOCK = 64
SEL_RATIO = SEL_BLOCK // CMP_BLOCK
TOP_N = 16
WINDOW = 512
Q_BLOCK = 128
FORCE_SCORE = 1e4
N_BUCKETS = 32
MAX_EXACT = N_BUCKETS // 2
MAX_DISTANCE = 1024
N_EXPERTS = 32
TOP_K = 4
D_FF = 1024
SWIGLU_ALPHA = 1.702
SWIGLU_LIMIT = 7.0
NORM_EPS = 1e-6
NEG_INF = -1e30
SCALE = HEAD_DIM ** -0.5
GATE_PAD = 128
GATE_PROJ = 2 * D_MODEL
MOE_ROWS_MIN, MOE_ROWS_MAX = 64, 512
LANES = 128
VMEM_LIMIT = 56 * 1024 * 1024


def _cparams(sem):
    return pltpu.CompilerParams(dimension_semantics=sem, vmem_limit_bytes=VMEM_LIMIT)


def _full(shape):
    n = len(shape)
    return pl.BlockSpec(shape, lambda *_: (0,) * n)


def _dot(a, b):
    return jnp.dot(a, b, preferred_element_type=F32)


def _dot_nt(a, b):
    return lax.dot_general(a, b, (((1,), (1,)), ((), ())), preferred_element_type=F32)


def _bf(x):
    return x.astype(BF16)


PROJ_SPLITS = (B_WIDTH, KV_ROW, KV_ROW, KV_ROW, GATE_PAD, GATE_PROJ)
CH_MAJOR = (np.arange(A_HEADS)[None, :] * HEAD_DIM + np.arange(HEAD_DIM)[:, None]).reshape(-1)
A_PERM = np.concatenate([CH_MAJOR, A_WIDTH + CH_MAJOR, 2 * A_WIDTH + CH_MAJOR, np.arange(3 * A_WIDTH, A_PROJ)])


def _rms_bf16(x_ref, g_ref):
    x = x_ref[...]
    return _bf(x * lax.rsqrt(jnp.mean(x * x, axis=-1, keepdims=True) + NORM_EPS) * g_ref[...])


def _proj_kernel(x_ref, g_ref, wa_ref, w_ref, oa_ref, *o_refs):
    h = _rms_bf16(x_ref, g_ref)
    oa_ref[...] = _dot(h, wa_ref[...])
    c = 0
    for o_ref, n in zip(o_refs, PROJ_SPLITS):
        o_ref[...] = _dot(h, w_ref[:, c:c + n])
        c += n


def _project(x, g, wa, w, tm):
    rows = x.shape[0]
    splits = (A_PROJ,) + PROJ_SPLITS
    return pl.pallas_call(
        _proj_kernel,
        grid=(rows // tm,),
        in_specs=[pl.BlockSpec((tm, D_MODEL), lambda i: (i, 0)), _full((1, D_MODEL)), _full(wa.shape), _full(w.shape)],
        out_specs=[pl.BlockSpec((tm, n), lambda i: (i, 0)) for n in splits],
        out_shape=[jax.ShapeDtypeStruct((rows, n), F32) for n in splits],
        compiler_params=_cparams(("parallel",)),
        name="norm_proj",
    )(x, g, wa, w)


PROJ_T_SPLITS = (A_PROJ, B_WIDTH, KV_ROW, KV_ROW, KV_ROW, GATE_PAD)
PROJ_N_SPLITS = (KV_ROW, KV_ROW, KV_ROW, GATE_PROJ)


def _proj_t_kernel(x_ref, g_ref, wt_ref, wn_ref, *o_refs):
    h = _rms_bf16(x_ref, g_ref)
    t = _dot_nt(wt_ref[...], h)
    c = 0
    for o_ref, n in zip(o_refs[:len(PROJ_T_SPLITS)], PROJ_T_SPLITS):
        o_ref[0] = t[c:c + n]
        c += n
    c = 0
    for o_ref, n in zip(o_refs[len(PROJ_T_SPLITS):], PROJ_N_SPLITS):
        o_ref[...] = _dot(h, wn_ref[:, c:c + n])
        c += n


def _project_t(x, g, wt, wn, tm, seq):
    rows = x.shape[0]
    per = seq // tm
    return pl.pallas_call(
        _proj_t_kernel,
        grid=(rows // tm,),
        in_specs=[pl.BlockSpec((tm, D_MODEL), lambda i: (i, 0)), _full((1, D_MODEL)), _full(wt.shape), _full(wn.shape)],
        out_specs=[pl.BlockSpec((1, n, tm), lambda i: (i // per, 0, i % per)) for n in PROJ_T_SPLITS]
                  + [pl.BlockSpec((tm, n), lambda i: (i, 0)) for n in PROJ_N_SPLITS],
        out_shape=[jax.ShapeDtypeStruct((rows // seq, n, seq), F32) for n in PROJ_T_SPLITS]
                  + [jax.ShapeDtypeStruct((rows, n), F32) for n in PROJ_N_SPLITS],
        compiler_params=_cparams(("parallel",)),
        name="norm_proj_t",
    )(x, g, wt, wn)


def _pack_w_in(w_in):
    a, rest = w_in[:, :A_PROJ], w_in[:, A_PROJ:]
    q, kv, gt, pg = (rest[:, :B_WIDTH], rest[:, B_WIDTH:B_WIDTH + 3 * KV_ROW],
                     rest[:, B_WIDTH + 3 * KV_ROW:B_WIDTH + 3 * KV_ROW + 3 * B_HEADS],
                     rest[:, B_WIDTH + 3 * KV_ROW + 3 * B_HEADS:])
    gt = jnp.pad(gt, ((0, 0), (0, GATE_PAD - 3 * B_HEADS)))
    w_t = jnp.concatenate([a[:, A_PERM], q, kv, gt], axis=1).T
    return _bf(a), _bf(jnp.concatenate([q, kv, gt, pg], axis=1)), _bf(w_t), _bf(jnp.concatenate([kv, pg], axis=1))


def _softplus(z):
    return jnp.maximum(z, 0.0) + jnp.log1p(jnp.exp(-jnp.abs(z)))


def _rwkv_prep_kernel(p_ref, prev_ref, mu_ref, w0_ref, wdu_ref, a0_ref, wiu_ref, wgu_ref, kk_ref, ka_ref, rk_ref,
                      ones_ref, r_o, w_o, k_o, v_o, kk_o, kb_o, g_o, bo_o):
    p = p_ref[...]
    ps = p + mu_ref[...] * (prev_ref[...] - p)
    r = ps[:, 0:A_WIDTH]
    k = ps[:, A_WIDTH:2 * A_WIDTH]
    v = ps[:, 2 * A_WIDTH:3 * A_WIDTH]
    c = 3 * A_WIDTH
    xw = ps[:, c:c + A_DECAY_RANK]
    xa = ps[:, c + A_DECAY_RANK:c + A_DECAY_RANK + A_ICLR_RANK]
    xg = ps[:, c + A_DECAY_RANK + A_ICLR_RANK:]
    w_log = -_softplus(-(w0_ref[...] + _dot(_bf(jnp.tanh(xw)), wdu_ref[...]))) - 0.5
    decay = jnp.exp(-jnp.exp(w_log))
    a = jax.nn.sigmoid(a0_ref[...] + _dot(_bf(xa), wiu_ref[...]))
    gate = _dot(_bf(jax.nn.sigmoid(xg)), wgu_ref[...])
    ones = ones_ref[...]
    kk = k * kk_ref[...]
    ss = jnp.dot(kk * kk, ones, precision=HIGHEST, preferred_element_type=F32)
    kk = kk / jnp.maximum(jnp.sqrt(ss), 1e-12)
    k2 = k * (1.0 + (a - 1.0) * ka_ref[...])
    rk = jnp.dot(r * k2 * rk_ref[...], ones, precision=HIGHEST, preferred_element_type=F32)
    r_o[...] = r
    w_o[...] = decay
    k_o[...] = k2
    v_o[...] = v
    kk_o[...] = kk
    kb_o[...] = kk * a
    g_o[...] = gate
    bo_o[...] = rk * v


def _rwkv_prep(p, prev, mu, w0, wdu, a0, wiu, wgu, k_k, k_a, r_k, tm):
    rows = p.shape[0]
    head = np.arange(A_WIDTH) // HEAD_DIM
    ones = jnp.asarray(head[:, None] == head[None, :], F32)
    row = lambda z: z.reshape(1, -1).astype(F32)
    spec_in = pl.BlockSpec((tm, A_PROJ), lambda i: (i, 0))
    spec_o = pl.BlockSpec((tm, A_WIDTH), lambda i: (i, 0))
    return pl.pallas_call(
        _rwkv_prep_kernel,
        grid=(rows // tm,),
        in_specs=[spec_in, spec_in, _full((1, A_PROJ)), _full((1, A_WIDTH)), _full((A_DECAY_RANK, A_WIDTH)),
                  _full((1, A_WIDTH)), _full((A_ICLR_RANK, A_WIDTH)), _full((A_GATE_RANK, A_WIDTH)),
                  _full((1, A_WIDTH)), _full((1, A_WIDTH)), _full((1, A_WIDTH)), _full((A_WIDTH, A_WIDTH))],
        out_specs=[spec_o] * 8,
        out_shape=[jax.ShapeDtypeStruct((rows, A_WIDTH), F32)] * 8,
        compiler_params=_cparams(("parallel",)),
        name="rwkv_prep",
    )(p, prev, row(mu), row(w0), _bf(wdu), row(a0), _bf(wiu), _bf(wgu), row(k_k), row(k_a), row(r_k), ones)


def _rwkv_prep_t_kernel(p_ref, mu_ref, w0_ref, wdu_ref, a0_ref, wiu_ref, wgu_ref, kk_ref, ka_ref, rk_ref,
                        r_o, w_o, k_o, v_o, kk_o, kb_o, g_o, bo_o, carry):
    tt = p_ref.shape[2]

    @pl.when(pl.program_id(1) == 0)
    def _():
        carry[...] = jnp.zeros_like(carry)

    p = p_ref[0]
    lane = lax.broadcasted_iota(I32, (1, tt), 1)
    prev = jnp.where(lane == 0, carry[:, 0:1], pltpu.roll(p, 1, 1))
    carry[:, 0:1] = p[:, tt - 1:tt]
    ps = p + mu_ref[...] * (prev - p)
    r = ps[0:A_WIDTH]
    k = ps[A_WIDTH:2 * A_WIDTH]
    v = ps[2 * A_WIDTH:3 * A_WIDTH]
    c = 3 * A_WIDTH
    xw = ps[c:c + A_DECAY_RANK]
    xa = ps[c + A_DECAY_RANK:c + A_DECAY_RANK + A_ICLR_RANK]
    xg = ps[c + A_DECAY_RANK + A_ICLR_RANK:]

    def head_sum(x):
        s = jnp.sum(x.reshape(HEAD_DIM, A_HEADS, tt), axis=0)
        return jnp.broadcast_to(s[None], (HEAD_DIM, A_HEADS, tt)).reshape(A_WIDTH, tt)

    w_log = -_softplus(-(w0_ref[...] + _dot(wdu_ref[...], _bf(jnp.tanh(xw))))) - 0.5
    decay = jnp.exp(-jnp.exp(w_log))
    a = jax.nn.sigmoid(a0_ref[...] + _dot(wiu_ref[...], _bf(xa)))
    gate = _dot(wgu_ref[...], _bf(jax.nn.sigmoid(xg)))
    kk = k * kk_ref[...]
    kk = kk / jnp.maximum(jnp.sqrt(head_sum(kk * kk)), 1e-12)
    k2 = k * (1.0 + (a - 1.0) * ka_ref[...])
    rk = head_sum(r * k2 * rk_ref[...])
    for o_ref, val in ((r_o, r), (w_o, decay), (k_o, k2), (kk_o, kk), (kb_o, kk * a)):
        o_ref[...] = val.reshape(N_JP, 2, 1, A_HEADS, tt)
    v_o[...] = v.reshape(HEAD_DIM, 1, A_HEADS, tt)
    g_o[0] = gate
    bo_o[0] = rk * v


def _rwkv_prep_t(p_t, mu, w0, wdu, a0, wiu, wgu, k_k, k_a, r_k, tt):
    b, _, t = p_t.shape
    col = lambda z, perm: z.reshape(-1)[perm].reshape(-1, 1).astype(F32)
    up = lambda w: _bf(w[:, CH_MAJOR].T)
    kshape = jax.ShapeDtypeStruct((N_JP, 2, b, A_HEADS, t), F32)
    kspec = pl.BlockSpec((N_JP, 2, 1, A_HEADS, tt), lambda bi, i: (0, 0, bi, 0, i))
    vspec = pl.BlockSpec((HEAD_DIM, 1, A_HEADS, tt), lambda bi, i: (0, bi, 0, i))
    nspec = pl.BlockSpec((1, A_WIDTH, tt), lambda bi, i: (bi, 0, i))
    nshape = jax.ShapeDtypeStruct((b, A_WIDTH, t), F32)
    return pl.pallas_call(
        _rwkv_prep_t_kernel,
        grid=(b, t // tt),
        in_specs=[pl.BlockSpec((1, A_PROJ, tt), lambda bi, i: (bi, 0, i)), _full((A_PROJ, 1)), _full((A_WIDTH, 1)),
                  _full((A_WIDTH, A_DECAY_RANK)), _full((A_WIDTH, 1)), _full((A_WIDTH, A_ICLR_RANK)),
                  _full((A_WIDTH, A_GATE_RANK)), _full((A_WIDTH, 1)), _full((A_WIDTH, 1)), _full((A_WIDTH, 1))],
        out_specs=[kspec, kspec, kspec, vspec, kspec, kspec, nspec, nspec],
        out_shape=[kshape, kshape, kshape, jax.ShapeDtypeStruct((HEAD_DIM, b, A_HEADS, t), F32), kshape, kshape,
                   nshape, nshape],
        scratch_shapes=[pltpu.VMEM((A_PROJ, LANES), F32)],
        compiler_params=_cparams(("parallel", "arbitrary")),
        name="rwkv_prep_t",
    )(p_t, col(mu, A_PERM), col(w0, CH_MAJOR), up(wdu), col(a0, CH_MAJOR), up(wiu), up(wgu),
      col(k_k, CH_MAJOR), col(k_a, CH_MAJOR), col(r_k, CH_MAJOR))


def _key_tiles_kernel(*refs):
    n = len(refs) // 2
    for z_ref, o_ref in zip(refs[:n], refs[n:]):
        for jp in range(N_JP):
            o_ref[0, jp] = z_ref[jp].T


def _key_tiles(zs, tt):
    t = zs[0].shape[2]
    return pl.pallas_call(
        _key_tiles_kernel,
        grid=(t // tt,),
        in_specs=[pl.BlockSpec((N_JP, LANES, tt), lambda i: (0, 0, i))] * len(zs),
        out_specs=[pl.BlockSpec((1, N_JP, tt, LANES), lambda i: (0, 0, i, 0))] * len(zs),
        out_shape=[jax.ShapeDtypeStruct((1, N_JP, t, LANES), F32)] * len(zs),
        compiler_params=_cparams(("parallel",)),
        name="wkv_key_tiles",
    )(*zs)


def _val_tiles_kernel(z_ref, o_ref, *, tt):
    for i in range(HEAD_DIM):
        x = z_ref[i]
        o_ref[pl.ds(i, tt, stride=HEAD_DIM), :] = jnp.concatenate([x, x], axis=0).T


def _val_tiles(z, tt):
    t = z.shape[2]
    return pl.pallas_call(
        functools.partial(_val_tiles_kernel, tt=tt),
        grid=(t // tt,),
        in_specs=[pl.BlockSpec((HEAD_DIM, BH_LANES, tt), lambda i: (0, 0, i))],
        out_specs=pl.BlockSpec((tt * HEAD_DIM, LANES), lambda i: (i, 0)),
        out_shape=jax.ShapeDtypeStruct((t * HEAD_DIM, LANES), F32),
        compiler_params=_cparams(("parallel",)),
        name="wkv_val_tiles",
    )(z)


def _val_untile_kernel(y_ref, o_ref, *, tt):
    for i in range(HEAD_DIM):
        o_ref[i] = y_ref[pl.ds(i, tt, stride=HEAD_DIM), :].T[0:BH_LANES]


def _val_untile(y, tt):
    t = y.shape[0] // HEAD_DIM
    return pl.pallas_call(
        functools.partial(_val_untile_kernel, tt=tt),
        grid=(t // tt,),
        in_specs=[pl.BlockSpec((tt * HEAD_DIM, LANES), lambda i: (i, 0))],
        out_specs=pl.BlockSpec((HEAD_DIM, BH_LANES, tt), lambda i: (0, 0, i)),
        out_shape=jax.ShapeDtypeStruct((HEAD_DIM, BH_LANES, t), F32),
        compiler_params=_cparams(("parallel",)),
        name="wkv_val_untile",
    )(y)


N_IO = HEAD_DIM // 8
N_JP = HEAD_DIM // 2
BH_LANES = 64


def _wkv_kernel(r_ref, w_ref, k_ref, v_ref, a_ref, b_ref, s0_ref, y_ref, sfin_ref, s_scr, *, tc):
    t_blk = pl.program_id(1)

    @pl.when(t_blk == 0)
    def _():
        s_scr[...] = s0_ref[0]

    def bc(ref, t, jp):
        return jnp.broadcast_to(ref[0, jp, pl.ds(t, 1), :], (8, LANES))

    def fold(x):
        return x + pltpu.roll(x, BH_LANES, 1)

    def state_times_a(t):
        acc = [jnp.zeros((8, LANES), F32) for _ in range(N_IO)]
        for jp in range(N_JP):
            a_ = bc(a_ref, t, jp)
            for io in range(N_IO):
                acc[io] = acc[io] + s_scr[io, jp] * a_
        return tuple(-fold(acc[io]) for io in range(N_IO))

    def step(t, sa):
        t_next = jnp.minimum(t + 1, tc - 1)
        vv = [v_ref[0, t, io * 8:(io + 1) * 8, :] for io in range(N_IO)]
        acc = [jnp.zeros((8, LANES), F32) for _ in range(N_IO)]
        yacc = [jnp.zeros((8, LANES), F32) for _ in range(N_IO)]
        for half in range(2):
            for jp in range(N_JP):
                w_, b_, k_, r_, a_ = (bc(ref, tt, jp) for ref, tt in
                                      ((w_ref, t), (b_ref, t), (k_ref, t), (r_ref, t), (a_ref, t_next)))
                for io in range(half * N_IO // 2, (half + 1) * N_IO // 2):
                    s = s_scr[io, jp] * w_ + sa[io] * b_ + vv[io] * k_
                    s_scr[io, jp] = s
                    yacc[io] = yacc[io] + s * r_
                    acc[io] = acc[io] + s * a_
        sa_next = [-fold(acc[io]) for io in range(N_IO)]
        y = [fold(yacc[io]) for io in range(N_IO)]
        tot = y[0]
        for io in range(1, N_IO):
            tot = tot + y[io]
        mu = jnp.sum(tot, axis=0, keepdims=True) * (1.0 / HEAD_DIM)
        d = [y[io] - mu for io in range(N_IO)]
        sq = d[0] * d[0]
        for io in range(1, N_IO):
            sq = sq + d[io] * d[io]
        var = jnp.sum(sq, axis=0, keepdims=True) * (1.0 / HEAD_DIM)
        inv = lax.rsqrt(var + A_GN_EPS)
        for io in range(N_IO):
            y_ref[0, t, io * 8:(io + 1) * 8, :] = d[io] * inv
        return tuple(sa_next)

    lax.fori_loop(0, tc, step, state_times_a(0))

    @pl.when(t_blk == pl.num_programs(1) - 1)
    def _():
        sfin_ref[0] = s_scr[...]


def _wkv_scan(r, w, k, v, a, b, s0, tc):
    nb, t = v.shape[:2]
    kspec = pl.BlockSpec((1, N_JP, tc, LANES), lambda n, i: (n, 0, i, 0))
    vspec = pl.BlockSpec((1, tc, HEAD_DIM, LANES), lambda n, i: (n, i, 0, 0))
    sspec = pl.BlockSpec((1, N_IO, N_JP, 8, LANES), lambda n, i: (n, 0, 0, 0, 0))
    return pl.pallas_call(
        functools.partial(_wkv_kernel, tc=tc),
        grid=(nb, t // tc),
        in_specs=[kspec, kspec, kspec, vspec, kspec, kspec, sspec],
        out_specs=[vspec, sspec],
        out_shape=[jax.ShapeDtypeStruct((nb, t, HEAD_DIM, LANES), F32),
                   jax.ShapeDtypeStruct((nb, N_IO, N_JP, 8, LANES), F32)],
        scratch_shapes=[pltpu.VMEM((N_IO, N_JP, 8, LANES), F32)],
        compiler_params=_cparams(("parallel", "arbitrary")),
        name="wkv_scan",
    )(r, w, k, v, a, b, s0)


def _wkv_step_kernel(r_ref, w_ref, k_ref, v_ref, a_ref, b_ref, s_ref, y_ref, so_ref):
    rt, wt, kt, vt, at, bt = (ref[...].T for ref in (r_ref, w_ref, k_ref, v_ref, a_ref, b_ref))
    outs = []
    for hh in range(2):
        hs = slice(hh * HEAD_DIM, (hh + 1) * HEAD_DIM)
        r_, w_, k_, a_, b_ = rt[hs], wt[hs], kt[hs], at[hs], bt[hs]
        ys = []
        for i in range(HEAD_DIM):
            s = s_ref[hh, i]
            sa = -jnp.sum(s * a_, axis=0, keepdims=True)
            s = s * w_ + sa * b_ + vt[hh * HEAD_DIM + i:hh * HEAD_DIM + i + 1] * k_
            so_ref[hh, i] = s
            ys.append(jnp.sum(s * r_, axis=0, keepdims=True))
        y = jnp.concatenate(ys, axis=0)
        d = y - jnp.mean(y, axis=0, keepdims=True)
        outs.append(d * lax.rsqrt(jnp.mean(d * d, axis=0, keepdims=True) + A_GN_EPS))
    y_ref[...] = jnp.concatenate(outs, axis=0).T


def _wkv_step(r, w, k, v, a, b, state):
    s = r.shape[0]
    vec = pl.BlockSpec((s, 2 * HEAD_DIM), lambda h: (0, h))
    st = pl.BlockSpec((2, HEAD_DIM, HEAD_DIM, s), lambda h: (h, 0, 0, 0))
    return pl.pallas_call(
        _wkv_step_kernel,
        grid=(A_HEADS // 2,),
        in_specs=[vec] * 6 + [st],
        out_specs=[vec, st],
        out_shape=[jax.ShapeDtypeStruct((s, A_WIDTH), F32), jax.ShapeDtypeStruct(state.shape, F32)],
        compiler_params=_cparams(("parallel",)),
        name="wkv_step",
    )(r, w, k, v, a, b, state)


def _state_from_tiles(s, b):
    nb = b // 8
    s = s.reshape(nb, N_IO, N_JP, 8, 2, 8, A_HEADS).transpose(0, 5, 6, 1, 3, 2, 4)
    return s.reshape(b, A_HEADS, HEAD_DIM, HEAD_DIM)


def _compress_paged_kernel(pt_ref, pe_ref, w1_ref, w2_ref, cache_ref, o_ref, buf, xk, xv, sem, *, n_pages, page_len):
    i = pl.program_id(0)
    slot = i % 2

    def copy(s_idx, sl, p):
        return pltpu.make_async_copy(cache_ref.at[pt_ref[s_idx * n_pages + p]], buf.at[sl, p], sem.at[sl, p])

    def fetch(s_idx, sl):
        for p in range(n_pages):
            copy(s_idx, sl, p).start()

    @pl.when(i == 0)
    def _():
        fetch(0, 0)

    @pl.when(i + 1 < pl.num_programs(0))
    def _():
        fetch(i + 1, 1 - slot)

    for p in range(n_pages):
        copy(i, slot, p).wait()
    for p in range(n_pages):
        t = buf[slot, p].T
        for m in range(page_len // CMP_BLOCK):
            row = (p * (page_len // CMP_BLOCK) + m) * CMP_PITCH
            xk[row:row + CMP_BLOCK, :] = t[m * CMP_BLOCK:(m + 1) * CMP_BLOCK, :KV_WIDTH]
            xv[row:row + CMP_BLOCK, :] = t[m * CMP_BLOCK:(m + 1) * CMP_BLOCK, KV_WIDTH:]
    _compress_kernel(xk, xv, pe_ref, w1_ref, w2_ref, o_ref, nblk=n_pages * page_len // CMP_BLOCK, pitch=CMP_PITCH)


CMP_PITCH = CMP_BLOCK + 1


def _compress_kernel(xk_ref, xv_ref, pe_ref, w1_ref, w2_ref, o_ref, *, nblk, pitch=CMP_BLOCK):
    for c, x_ref in enumerate((xk_ref, xv_ref)):
        acc = jnp.zeros((nblk, B_KV_HEADS * CMP_HIDDEN), F32)
        for tau in range(0, CMP_BLOCK, 2):
            x = jnp.concatenate([x_ref[pl.ds(tau + u, nblk, stride=pitch), :] + pe_ref[c, tau + u:tau + u + 1, :]
                                 for u in range(2)], axis=1)
            acc = acc + _dot(_bf(x), w1_ref[c, tau // 2])
        o_ref[:, c * KV_WIDTH:(c + 1) * KV_WIDTH] = _dot(_bf(jax.nn.gelu(acc)), w2_ref[c])


def _compress_weights(pe_k, pe_v, w_k1, w_k2, w_v1, w_v2):
    pe = jnp.stack([jnp.concatenate([pe_k, pe_k], axis=1), jnp.concatenate([pe_v, pe_v], axis=1)]).astype(F32)
    eye = jnp.eye(B_KV_HEADS, dtype=F32)
    w1 = jnp.stack([w_k1, w_v1]).reshape(2, CMP_BLOCK, HEAD_DIM, CMP_HIDDEN)
    w1 = jnp.einsum('ctdh,ge->ctgdeh', w1, eye).reshape(2, CMP_BLOCK // 2, 2 * KV_WIDTH, B_KV_HEADS * CMP_HIDDEN)
    w2 = jnp.einsum('chd,ge->cghed', jnp.stack([w_k2, w_v2]), eye).reshape(2, B_KV_HEADS * CMP_HIDDEN, KV_WIDTH)
    return pe, _bf(w1), _bf(w2)


COMPRESS_W_SPECS = ((2, CMP_BLOCK, KV_WIDTH), (2, CMP_BLOCK // 2, 2 * KV_WIDTH, B_KV_HEADS * CMP_HIDDEN),
                    (2, B_KV_HEADS * CMP_HIDDEN, KV_WIDTH))


def _compress_paged(cache_t, page_table, cw):
    s, n_pages = page_table.shape
    page = cache_t.shape[2]
    past = n_pages * page
    nblk = past // CMP_BLOCK
    grid_spec = pltpu.PrefetchScalarGridSpec(
        num_scalar_prefetch=1,
        grid=(s,),
        in_specs=[pl.BlockSpec(shp, lambda i, pt, n=len(shp): (0,) * n) for shp in COMPRESS_W_SPECS]
                 + [pl.BlockSpec(memory_space=pl.ANY)],
        out_specs=pl.BlockSpec((nblk, KV_ROW), lambda i, pt: (i, 0)),
        scratch_shapes=[pltpu.VMEM((2, n_pages, KV_ROW, page), F32), pltpu.VMEM((nblk * CMP_PITCH, KV_WIDTH), F32),
                        pltpu.VMEM((nblk * CMP_PITCH, KV_WIDTH), F32), pltpu.SemaphoreType.DMA((2, n_pages))],
    )
    return pl.pallas_call(
        functools.partial(_compress_paged_kernel, n_pages=n_pages, page_len=page),
        grid_spec=grid_spec,
        out_shape=jax.ShapeDtypeStruct((s * nblk, KV_ROW), F32),
        compiler_params=_cparams(("arbitrary",)),
        name="nsa_compress_paged",
    )(page_table.reshape(-1), *_compress_weights(*cw), cache_t)


def _compress(x, pe_k, pe_v, w_k1, w_k2, w_v1, w_v2, rows):
    n = x.shape[0]
    nblk = rows // CMP_BLOCK
    pe, w1, w2 = _compress_weights(pe_k, pe_v, w_k1, w_k2, w_v1, w_v2)
    return pl.pallas_call(
        functools.partial(_compress_kernel, nblk=nblk),
        grid=(n // rows,),
        in_specs=[pl.BlockSpec((rows, KV_WIDTH), lambda i: (i, 0)), pl.BlockSpec((rows, KV_WIDTH), lambda i: (i, 1)),
                  ] + [_full(shp) for shp in COMPRESS_W_SPECS],
        out_specs=pl.BlockSpec((nblk, KV_ROW), lambda i: (i, 0)),
        out_shape=jax.ShapeDtypeStruct((n // CMP_BLOCK, KV_ROW), F32),
        compiler_params=_cparams(("parallel",)),
        name="nsa_compress",
    )(x, x, pe, w1, w2)


def _rel_bucket(dist):
    n = jnp.maximum(dist, 0)
    log_ratio = jnp.log(jnp.maximum(n, 1).astype(F32) / MAX_EXACT) / math.log(MAX_DISTANCE / MAX_EXACT)
    large = jnp.minimum(MAX_EXACT + (log_ratio * (N_BUCKETS - MAX_EXACT)).astype(I32), N_BUCKETS - 1)
    return jnp.where(n < MAX_EXACT, n, large)


def _bias_of(rel_bias, dist):
    onehot = (_rel_bucket(dist)[..., None] == jnp.arange(N_BUCKETS)).astype(F32)
    return jnp.einsum('...b,bh->h...', onehot, rel_bias.astype(F32), precision=HIGHEST)


KEY_TILE = 512


def _nsa_prompt_kernel(q_ref, gate_ref, ck_ref, ckt_ref, ksel_ref, vsel_ref, kwin_ref, vwin_ref, bct_ref, tz_ref,
                       o_ref, imp_scr, *, nc, ns, nq, top_n, wtiles):
    i = pl.program_id(1)
    qb = Q_BLOCK
    gates = jax.nn.sigmoid(gate_ref[0])
    pos_l = i * qb + lax.broadcasted_iota(I32, (1, qb), 1)
    okT = pos_l >= lax.broadcasted_iota(I32, (nc, qb), 0) * CMP_BLOCK + (CMP_BLOCK - 1)
    blk = lax.broadcasted_iota(I32, (ns, qb), 0)
    cur = pos_l // SEL_BLOCK
    forced = (blk == 0) | (blk == cur) | (blk == cur - 1)
    k_t = lax.broadcasted_iota(I32, (KEY_TILE, qb), 0)
    n_e = lax.broadcasted_iota(I32, (KEY_TILE, ns), 1)
    k_e = lax.broadcasted_iota(I32, (KEY_TILE, ns), 0)
    wk = (wtiles + 1) * qb
    wb = jnp.maximum(i - wtiles, 0)
    d_w = pos_l - (wb * qb + lax.broadcasted_iota(I32, (wk, qb), 0))
    madd_w = jnp.where((d_w >= 0) & (d_w < WINDOW), 0.0, NEG_INF)
    n_tiles = (i * qb + qb + KEY_TILE - 1) // KEY_TILE
    per = KEY_TILE // qb

    def bias_tiles(h, first_blk, count):
        return jnp.concatenate([tz_ref[h, jnp.clip(i - (first_blk + c), 0, nq - 1)] for c in range(count)], axis=0)

    ksl = [slice(g * HEAD_DIM, (g + 1) * HEAD_DIM) for g in range(B_KV_HEADS)]
    vsl = [slice(KV_WIDTH + g * HEAD_DIM, KV_WIDTH + (g + 1) * HEAD_DIM) for g in range(B_KV_HEADS)]
    qh = [_bf(q_ref[0, h * HEAD_DIM:(h + 1) * HEAD_DIM, :] * SCALE) for h in range(B_HEADS)]
    o_c, sel_t = [], []
    for g in range(B_KV_HEADS):
        kc = _bf(ck_ref[:, ksl[g]])
        vct = _bf(ckt_ref[0, vsl[g], :])
        impT = jnp.zeros((nc, qb), F32)
        for h in range(g * B_GROUP, (g + 1) * B_GROUP):
            sT = jnp.where(okT, _dot(kc, qh[h]) + bct_ref[h], NEG_INF)
            eT = jnp.exp(sT - jnp.max(sT, axis=0, keepdims=True))
            pT = eT / jnp.sum(eT, axis=0, keepdims=True) * okT.astype(F32)
            impT = impT + pT
            o_c.append(_dot(vct, _bf(pT)))
        imp_scr[...] = impT
        imp2 = imp_scr[pl.ds(0, ns, stride=SEL_RATIO), :] + imp_scr[pl.ds(1, ns, stride=SEL_RATIO), :]
        score = jnp.where(blk <= cur, imp2 + FORCE_SCORE * forced.astype(F32), NEG_INF)
        rank = jnp.zeros((ns, qb), I32)
        for m in range(ns):
            row = score[m:m + 1, :]
            rank = rank + ((row > score) | ((row == score) & (m < blk))).astype(I32)
        sel_t.append(_bf((rank < top_n) & (score > NEG_INF / 2)))

    def body(jt, carry):
        ms, ls, accs = carry
        k0 = pl.multiple_of(jt * KEY_TILE, KEY_TILE)
        causal = k0 + k_t <= pos_l
        in_blk = _bf(n_e == (k0 + k_e) // SEL_BLOCK)
        ms2, ls2, accs2 = [], [], []
        for g in range(B_KV_HEADS):
            kt = _bf(ksel_ref[pl.ds(k0, KEY_TILE), ksl[g]])
            vt = _bf(vsel_ref[0, vsl[g], pl.ds(k0, KEY_TILE)])
            madd = jnp.where((_dot(in_blk, sel_t[g]) > 0.5) & causal, 0.0, NEG_INF)
            for h in range(g * B_GROUP, (g + 1) * B_GROUP):
                s = _dot(kt, qh[h]) + bias_tiles(h, jt * per, per) + madd
                m_new = jnp.maximum(ms[h], jnp.max(s, axis=0, keepdims=True))
                alpha = jnp.exp(ms[h] - m_new)
                p = jnp.exp(s - m_new)
                ls2.append(alpha * ls[h] + jnp.sum(p, axis=0, keepdims=True))
                accs2.append(alpha * accs[h] + _dot(vt, _bf(p)))
                ms2.append(m_new)
        return tuple(ms2), tuple(ls2), tuple(accs2)

    init = (tuple(jnp.full((1, qb), NEG_INF, F32) for _ in range(B_HEADS)),
            tuple(jnp.zeros((1, qb), F32) for _ in range(B_HEADS)),
            tuple(jnp.zeros((HEAD_DIM, qb), F32) for _ in range(B_HEADS)))
    _, ls, accs = lax.fori_loop(0, n_tiles, body, init)
    w0 = pl.multiple_of(wb * qb, qb)
    for g in range(B_KV_HEADS):
        ktw = _bf(kwin_ref[pl.ds(w0, wk), ksl[g]])
        vtw = _bf(vwin_ref[0, vsl[g], pl.ds(w0, wk)])
        for h in range(g * B_GROUP, (g + 1) * B_GROUP):
            s = _dot(ktw, qh[h]) + bias_tiles(h, wb, wtiles + 1) + madd_w
            p = jnp.exp(s - jnp.max(s, axis=0, keepdims=True))
            o_w = _dot(vtw, _bf(p)) / jnp.sum(p, axis=0, keepdims=True)
            o_s = accs[h] / ls[h]
            o_ref[0, h * HEAD_DIM:(h + 1) * HEAD_DIM, :] = (
                gates[3 * h:3 * h + 1] * o_c[h] + gates[3 * h + 1:3 * h + 2] * o_s + gates[3 * h + 2:3 * h + 3] * o_w)


def _nsa_prompt(q_t, gate_t, ckv, kv_sel, kvt_sel, kv_win, kvt_win, rel_bias):
    b, _, t = q_t.shape
    nq = t // Q_BLOCK
    nc = t // CMP_BLOCK
    ns = -(-t // SEL_BLOCK)
    wtiles = WINDOW // Q_BLOCK
    assert t % KEY_TILE == 0 and nc == SEL_RATIO * ns and WINDOW % Q_BLOCK == 0 and (wtiles + 1) * Q_BLOCK <= t
    cmp_end = jnp.arange(nc) * CMP_BLOCK + (CMP_BLOCK - 1)
    bct = _bias_of(rel_bias, jnp.arange(t)[None, :] - cmp_end[:, None])
    dz = (jnp.arange(nq)[:, None, None] * Q_BLOCK + jnp.arange(Q_BLOCK)[None, None, :]
          - jnp.arange(Q_BLOCK)[None, :, None])
    tz = _bias_of(rel_bias, dz)
    ckt = jnp.swapaxes(ckv.reshape(b, nc, KV_ROW), 1, 2)
    chan = lambda w: pl.BlockSpec((1, w, Q_BLOCK), lambda bi, i: (bi, 0, i))
    seq_rows = pl.BlockSpec((t, KV_ROW), lambda bi, i: (bi, 0))
    seq_chan = pl.BlockSpec((1, KV_ROW, t), lambda bi, i: (bi, 0, 0))
    return pl.pallas_call(
        functools.partial(_nsa_prompt_kernel, nc=nc, ns=ns, nq=nq, top_n=min(TOP_N, ns), wtiles=wtiles),
        grid=(b, nq),
        in_specs=[chan(B_WIDTH), chan(GATE_PAD),
                  pl.BlockSpec((nc, KV_ROW), lambda bi, i: (bi, 0)),
                  pl.BlockSpec((1, KV_ROW, nc), lambda bi, i: (bi, 0, 0)),
                  seq_rows, seq_chan, seq_rows, seq_chan,
                  pl.BlockSpec((B_HEADS, nc, Q_BLOCK), lambda bi, i: (0, 0, i)),
                  _full((B_HEADS, nq, Q_BLOCK, Q_BLOCK))],
        out_specs=chan(B_WIDTH),
        out_shape=jax.ShapeDtypeStruct((b, B_WIDTH, t), F32),
        scratch_shapes=[pltpu.VMEM((nc, Q_BLOCK), F32)],
        compiler_params=_cparams(("parallel", "arbitrary")),
        name="nsa_prompt",
    )(q_t, gate_t, ckv, ckt, kv_sel, kvt_sel, kv_win, kvt_win, bct, tz)


def _group_q(q_ref, g):
    rows = [q_ref[0, :, (g * B_GROUP + r) * HEAD_DIM:(g * B_GROUP + r + 1) * HEAD_DIM] for r in range(B_GROUP)]
    return _bf(jnp.concatenate(rows, axis=0) * SCALE)


def _nsa_sample_select_kernel(q_ref, gate_ref, ck_ref, bc_ref, pair_ref, tri_ref, oc_ref, idx_ref,
                              *, nsp, cur, top_n):
    gates = jax.nn.sigmoid(gate_ref[0])
    blk = lax.broadcasted_iota(I32, (1, nsp), 1)
    forced = (blk == 0) | (blk == cur) | (blk == cur - 1)
    mi = lax.broadcasted_iota(I32, (nsp, nsp), 0)
    ni = lax.broadcasted_iota(I32, (nsp, nsp), 1)
    kk = lax.broadcasted_iota(I32, (TOP_N, nsp), 0).astype(F32)
    nf = lax.broadcasted_iota(I32, (TOP_N, nsp), 1).astype(F32)
    outs = []
    for g in range(B_KV_HEADS):
        q4 = _group_q(q_ref, g)
        kc = _bf(ck_ref[:, g * HEAD_DIM:(g + 1) * HEAD_DIM])
        vc = _bf(ck_ref[:, KV_WIDTH + g * HEAD_DIM:KV_WIDTH + (g + 1) * HEAD_DIM])
        s = _dot_nt(q4, kc) + bc_ref[g * B_GROUP:(g + 1) * B_GROUP, :]
        e = jnp.exp(s - jnp.max(s, axis=-1, keepdims=True))
        p = e / jnp.sum(e, axis=-1, keepdims=True)
        o_c = _dot(_bf(p), vc)
        imp = ((p[0:1] + p[1:2]) + p[2:3]) + p[3:4]
        imp2 = jnp.dot(imp, pair_ref[...], precision=HIGHEST, preferred_element_type=F32)
        score = jnp.where(blk <= cur, imp2 + FORCE_SCORE * forced.astype(F32), NEG_INF)
        m1 = jnp.broadcast_to(score, (nsp, nsp))
        m2 = m1.T
        gt = (m2 > m1) | ((m2 == m1) & (mi < ni))
        rank = jnp.sum(gt.astype(F32), axis=0, keepdims=True)
        sel = (rank < top_n) & (score > NEG_INF / 2)
        before = _dot(_bf(sel), tri_ref[...])
        hit = jnp.broadcast_to(sel, (TOP_N, nsp)) & (jnp.broadcast_to(before, (TOP_N, nsp)) == kk)
        idx = jnp.sum(jnp.where(hit, nf, 0.0), axis=1, keepdims=True)
        cnt = jnp.sum(hit.astype(F32), axis=1, keepdims=True)
        idx_ref[0, g] = jnp.where(cnt > 0.5, idx, -1.0).astype(I32)
        for r in range(B_GROUP):
            h = g * B_GROUP + r
            outs.append(gates[:, 3 * h:3 * h + 1] * o_c[r:r + 1])
    oc_ref[0] = jnp.concatenate(outs, axis=1)


def _nsa_sample_select(q3, gate3, ckv, rel_bias, past_len):
    s = q3.shape[0]
    nc = past_len // CMP_BLOCK
    ns = -(-(past_len + 1) // SEL_BLOCK)
    nsp = -(-ns // LANES) * LANES
    cur = past_len // SEL_BLOCK
    bc = _bias_of(rel_bias, past_len - (jnp.arange(nc) * CMP_BLOCK + (CMP_BLOCK - 1)))
    pair = jnp.asarray(np.arange(nc)[:, None] // SEL_RATIO == np.arange(nsp)[None, :], F32)
    tri = jnp.asarray(np.arange(nsp)[:, None] < np.arange(nsp)[None, :], BF16)
    return pl.pallas_call(
        functools.partial(_nsa_sample_select_kernel, nsp=nsp, cur=cur, top_n=min(TOP_N, ns)),
        grid=(s,),
        in_specs=[pl.BlockSpec((1, 1, B_WIDTH), lambda i: (i, 0, 0)), pl.BlockSpec((1, 1, GATE_PAD), lambda i: (i, 0, 0)),
                  pl.BlockSpec((nc, KV_ROW), lambda i: (i, 0)), _full((B_HEADS, nc)), _full((nc, nsp)), _full((nsp, nsp))],
        out_specs=[pl.BlockSpec((1, 1, B_WIDTH), lambda i: (i, 0, 0)),
                   pl.BlockSpec((1, B_KV_HEADS, TOP_N, 1), lambda i: (i, 0, 0, 0))],
        out_shape=[jax.ShapeDtypeStruct((s, 1, B_WIDTH), F32), jax.ShapeDtypeStruct((s, B_KV_HEADS, TOP_N, 1), I32)],
        compiler_params=_cparams(("parallel",)),
        name="nsa_sample_select",
    )(q3, gate3, ckv, bc, pair, tri)


def _nsa_sample_attend_kernel(idx_ref, pt_ref, q_ref, gate_ref, oc_ref, ksel_ref, kwin_ref, win_ref, fpg_ref, bw_ref,
                              cache_ref, o_ref, buf, sem, *, cur, n_pages, past_len, page_len):
    i = pl.program_id(0)
    n_s = pl.num_programs(0)
    slot = i % 2
    halves = PAGE_BLOCKS
    n_slot = B_KV_HEADS * TOP_N

    def block_of(s_idx, j):
        return idx_ref[s_idx * n_slot + j]

    def copy(s_idx, sl, j):
        n = jnp.clip(block_of(s_idx, j), 0, cur - 1)
        page = pt_ref[s_idx * n_pages + n // halves]
        return pltpu.make_async_copy(cache_ref.at[page], buf.at[sl, j], sem.at[sl])

    def cached(s_idx, j):
        n = block_of(s_idx, j)
        return (n >= 0) & (n < cur)

    def fetch(s_idx, sl):
        for j in range(n_slot):
            @pl.when(cached(s_idx, j))
            def _():
                copy(s_idx, sl, j).start()

    @pl.when(i == 0)
    def _():
        fetch(0, 0)

    @pl.when(i + 1 < n_s)
    def _():
        fetch(i + 1, 1 - slot)

    for j in range(n_slot):
        @pl.when(cached(i, j))
        def _():
            copy(i, slot, j).wait()

        @pl.when(jnp.logical_not(cached(i, j)))
        def _():
            buf[slot, j] = jnp.zeros((KV_ROW, page_len), F32)
            buf[slot, j, :, 0:1] = ksel_ref[0]

    gates = jax.nn.sigmoid(gate_ref[0])
    t_l = lax.broadcasted_iota(I32, (1, page_len), 1)
    n_buf = win_ref.shape[2]
    j_w = lax.broadcasted_iota(I32, (1, n_buf), 1)
    d_w = n_buf - j_w
    ok_w = (d_w >= 0) & (d_w < WINDOW) & (past_len - d_w >= 0)
    f0 = fpg_ref[n_pages]
    outs = []
    for g in range(B_KV_HEADS):
        q4 = _group_q(q_ref, g)
        ksl = slice(g * HEAD_DIM, (g + 1) * HEAD_DIM)
        vsl = slice(KV_WIDTH + g * HEAD_DIM, KV_WIDTH + (g + 1) * HEAD_DIM)
        pieces = []
        for k in range(TOP_N):
            j = g * TOP_N + k
            n = block_of(i, j)
            pg = jnp.clip(n, 0, cur) // halves
            fb = fpg_ref[pg][g * B_GROUP:(g + 1) * B_GROUP, :]
            s_k = _dot(q4, _bf(buf[slot, j, ksl, :])) + fb
            ok = (n >= 0) & (t_l // SEL_BLOCK == n % halves) & (pg * page_len + t_l <= past_len)
            pieces.append(jnp.where(ok, s_k, NEG_INF))
        s = jnp.concatenate(pieces, axis=1)
        e = jnp.exp(s - jnp.max(s, axis=-1, keepdims=True))
        p = _bf(e / jnp.sum(e, axis=-1, keepdims=True))
        o_s = jnp.zeros((B_GROUP, HEAD_DIM), F32)
        for k in range(TOP_N):
            o_s = o_s + _dot_nt(p[:, k * page_len:(k + 1) * page_len], _bf(buf[slot, g * TOP_N + k, vsl, :]))
        kw = _bf(win_ref[0, ksl, :])
        vw = _bf(win_ref[0, vsl, :])
        s_w = jnp.where(ok_w, _dot(q4, kw) + bw_ref[g * B_GROUP:(g + 1) * B_GROUP, :], NEG_INF)
        k_new = _bf(kwin_ref[0, :, ksl])
        v_new = _bf(kwin_ref[0, :, vsl])
        s_n = (jnp.sum(q4.astype(F32) * k_new.astype(F32), axis=-1, keepdims=True)
               + f0[g * B_GROUP:(g + 1) * B_GROUP, 0:1])
        m = jnp.maximum(jnp.max(s_w, axis=-1, keepdims=True), s_n)
        e_w = jnp.exp(s_w - m)
        e_n = jnp.exp(s_n - m)
        den = jnp.sum(e_w, axis=-1, keepdims=True) + e_n
        o_w = _dot_nt(_bf(e_w / den), vw) + _bf(e_n / den).astype(F32) * v_new.astype(F32)
        for r in range(B_GROUP):
            h = g * B_GROUP + r
            outs.append(gates[:, 3 * h + 1:3 * h + 2] * o_s[r:r + 1] + gates[:, 3 * h + 2:3 * h + 3] * o_w[r:r + 1])
    o_ref[0] = oc_ref[0] + jnp.concatenate(outs, axis=1)


PAGE_BLOCKS = 2


def _nsa_sample_attend(idx, page_table, q3, gate3, oc3, ksel_t, kwin3, win_t, cache_t, rel_bias, past_len):
    s = q3.shape[0]
    n_pages = page_table.shape[1]
    cur = past_len // SEL_BLOCK
    n_buf = win_t.shape[2]
    page = cache_t.shape[2]
    assert page == PAGE_BLOCKS * SEL_BLOCK and past_len == n_pages * page
    keypos = jnp.arange(n_pages + 1)[:, None] * page + jnp.arange(page)[None, :]
    fpg = jnp.moveaxis(_bias_of(rel_bias, past_len - keypos), 0, 1)
    bw = _bias_of(rel_bias, n_buf - jnp.arange(n_buf))
    row3 = lambda w: pl.BlockSpec((1, 1, w), lambda i, *_: (i, 0, 0))
    grid_spec = pltpu.PrefetchScalarGridSpec(
        num_scalar_prefetch=2,
        grid=(s,),
        in_specs=[row3(B_WIDTH), row3(GATE_PAD), row3(B_WIDTH),
                  pl.BlockSpec((1, KV_ROW, 1), lambda i, *_: (i, 0, 0)), row3(KV_ROW),
                  pl.BlockSpec((1, KV_ROW, n_buf), lambda i, *_: (i, 0, 0)),
                  pl.BlockSpec((n_pages + 1, B_HEADS, page), lambda i, *_: (0, 0, 0)),
                  pl.BlockSpec((B_HEADS, n_buf), lambda i, *_: (0, 0)),
                  pl.BlockSpec(memory_space=pl.ANY)],
        out_specs=row3(B_WIDTH),
        scratch_shapes=[pltpu.VMEM((2, B_KV_HEADS * TOP_N, KV_ROW, page), F32), pltpu.SemaphoreType.DMA((2,))],
    )
    return pl.pallas_call(
        functools.partial(_nsa_sample_attend_kernel, cur=cur, n_pages=n_pages, past_len=past_len, page_len=page),
        grid_spec=grid_spec,
        out_shape=jax.ShapeDtypeStruct((s, 1, B_WIDTH), F32),
        compiler_params=_cparams(("arbitrary",)),
        name="nsa_sample_attend",
    )(idx.reshape(-1), page_table.reshape(-1), q3, gate3, oc3, ksel_t, kwin3, win_t, fpg, bw, cache_t)


def _merge_kernel(x_ref, yn_ref, bo_ref, ga_ref, yb_ref, pg_ref, lnw_ref, lnb_ref, wa_ref, wb_ref, wo_ref, nf_ref,
                  wr_ref, br_ref, x1_ref, h_ref, te_ref, tw_ref, *, a_transposed):
    if a_transposed:
        tm = x_ref.shape[0]
        yn = yn_ref[...].reshape(A_WIDTH, tm)
        y_a = ((yn * lnw_ref[...] + lnb_ref[...] + bo_ref[0]) * ga_ref[0]).T
        y_b = yb_ref[0].T
    else:
        y_a = (yn_ref[...] * lnw_ref[...] + lnb_ref[...] + bo_ref[...]) * ga_ref[...]
        y_b = yb_ref[...]
    g_a = jax.nn.sigmoid(pg_ref[:, :D_MODEL])
    g_b = jax.nn.sigmoid(pg_ref[:, D_MODEL:])
    m = g_a * _dot(_bf(y_a), wa_ref[...]) + g_b * _dot(_bf(y_b), wb_ref[...])
    x1 = x_ref[...] + _dot(_bf(m), wo_ref[...])
    x1_ref[...] = x1
    h = x1 * lax.rsqrt(jnp.mean(x1 * x1, axis=-1, keepdims=True) + NORM_EPS) * nf_ref[...]
    _store_rows(h_ref, h)
    logits = _dot(_bf(h), wr_ref[...]) + br_ref[...]
    lane = lax.broadcasted_iota(I32, logits.shape, 1).astype(F32)
    col = lax.broadcasted_iota(I32, (logits.shape[0], TOP_K), 1)
    vals, idxs = [], []
    for _ in range(TOP_K):
        m_k = jnp.max(logits, axis=-1, keepdims=True)
        i_k = jnp.min(jnp.where(logits == m_k, lane, float(N_EXPERTS)), axis=-1, keepdims=True)
        vals.append(m_k)
        idxs.append(i_k)
        logits = jnp.where(lane == i_k, -jnp.inf, logits)
    e = [jnp.exp(v - vals[0]) for v in vals]
    den = ((e[0] + e[1]) + e[2]) + e[3]
    te = jnp.zeros(col.shape, F32)
    tw = jnp.zeros(col.shape, F32)
    for k in range(TOP_K):
        te = jnp.where(col == k, idxs[k], te)
        tw = jnp.where(col == k, e[k] / den, tw)
    te_ref[...] = te.astype(I32)
    tw_ref[...] = tw


def _merge(x, yn, bonus, gate, yb, pg, ln_w, ln_b, w_a, w_b, w_o, norm_ffn, w_router, b_router, tm, seq=None):
    rows = x.shape[0]
    row = lambda w: pl.BlockSpec((tm, w), lambda i: (i, 0))
    vec = lambda z: z.reshape(1, -1).astype(F32)
    if seq is None:
        a_specs = [row(A_WIDTH)] * 3 + [row(B_WIDTH)] + [_full((1, A_WIDTH))] * 2
        lnw, lnb = vec(ln_w), vec(ln_b)
    else:
        per = seq // tm
        nspec = pl.BlockSpec((1, A_WIDTH, tm), lambda i: (i // per, 0, i % per))
        a_specs = [pl.BlockSpec((HEAD_DIM, 1, A_HEADS, tm), lambda i: (0, i // per, 0, i % per)), nspec, nspec, nspec,
                   _full((A_WIDTH, 1)), _full((A_WIDTH, 1))]
        lnw, lnb = (z[CH_MAJOR].reshape(-1, 1).astype(F32) for z in (ln_w, ln_b))
        w_a = w_a[CH_MAJOR]
    return pl.pallas_call(
        functools.partial(_merge_kernel, a_transposed=seq is not None),
        grid=(rows // tm,),
        in_specs=[row(D_MODEL)] + a_specs[:4] + [row(GATE_PROJ)] + a_specs[4:]
                 + [_full((A_WIDTH, D_MODEL)), _full((B_WIDTH, D_MODEL)),
                  _full((D_MODEL, D_MODEL)), _full((1, D_MODEL)), _full((D_MODEL, N_EXPERTS)), _full((1, N_EXPERTS))],
        out_specs=[row(D_MODEL), pl.BlockSpec((tm * ROW_TILE, LANES), lambda i: (i, 0)), row(TOP_K), row(TOP_K)],
        out_shape=[jax.ShapeDtypeStruct((rows, D_MODEL), F32), jax.ShapeDtypeStruct((rows * ROW_TILE, LANES), F32),
                   jax.ShapeDtypeStruct((rows, TOP_K), I32), jax.ShapeDtypeStruct((rows, TOP_K), F32)],
        compiler_params=_cparams(("parallel",)),
        name="merge_router",
    )(x, yn, bonus, gate, yb, pg, lnw, lnb, _bf(w_a), _bf(w_b), _bf(w_o), vec(norm_ffn),
      _bf(w_router), vec(b_router))


ROW_TILE = D_MODEL // LANES


def _load_rows(ref, n, first=0, every=1):
    return jnp.concatenate([ref[pl.ds(first * ROW_TILE + s, n, stride=every * ROW_TILE), :] for s in range(ROW_TILE)],
                           axis=1)


def _store_rows(ref, x):
    for s in range(ROW_TILE):
        ref[pl.ds(s, x.shape[0], stride=ROW_TILE), :] = x[:, s * LANES:(s + 1) * LANES]


def _dispatch_kernel(dest_ref, h_ref, xs_in_ref, xs_ref, sem, *, tm):
    del xs_in_ref
    for t in range(tm):
        for k in range(TOP_K):
            pltpu.make_async_copy(h_ref.at[t], xs_ref.at[dest_ref[0, 0, t * TOP_K + k]], sem).start()
    for t in range(tm * TOP_K):
        pltpu.make_async_copy(h_ref.at[0], xs_ref.at[0], sem).wait()


def _dispatch(dest, h_tiles, slots, tm):
    n = h_tiles.shape[0]
    return pl.pallas_call(
        functools.partial(_dispatch_kernel, tm=tm),
        grid=(n // tm,),
        in_specs=[pl.BlockSpec((1, 1, tm * TOP_K), lambda i: (i, 0, 0), memory_space=pltpu.SMEM),
                  pl.BlockSpec((tm, ROW_TILE, LANES), lambda i: (i, 0, 0)), pl.BlockSpec(memory_space=pl.ANY)],
        out_specs=pl.BlockSpec(memory_space=pl.ANY),
        out_shape=jax.ShapeDtypeStruct((slots, ROW_TILE, LANES), F32),
        scratch_shapes=[pltpu.SemaphoreType.DMA(())],
        input_output_aliases={2: 0},
        compiler_params=_cparams(("arbitrary",)),
        name="moe_dispatch",
    )(dest.reshape(n // tm, 1, tm * TOP_K), h_tiles, jnp.zeros((slots, ROW_TILE, LANES), F32))


def _moe_kernel(ce_ref, nu_ref, x_ref, wu_ref, bu_ref, wd_ref, bd_ref, o_ref, wu_bf, wd_bf, *, rows):
    c = pl.program_id(0)
    e = ce_ref[c]
    prev = ce_ref[jnp.maximum(c - 1, 0)]

    @pl.when((c == 0) | (e != prev))
    def _():
        wu_bf[...] = _bf(wu_ref[0])
        wd_bf[...] = _bf(wd_ref[0])

    @pl.when(c < nu_ref[0])
    def _():
        u = _dot(_bf(_load_rows(x_ref, rows)), wu_bf[...]) + bu_ref[0]
        glu = jnp.minimum(u[:, :D_FF], SWIGLU_LIMIT)
        lin = jnp.clip(u[:, D_FF:], -SWIGLU_LIMIT, SWIGLU_LIMIT)
        act = glu * jax.nn.sigmoid(SWIGLU_ALPHA * glu) * (lin + 1.0)
        _store_rows(o_ref, _dot(_bf(act), wd_bf[...]) + bd_ref[0])

    @pl.when(c >= nu_ref[0])
    def _():
        o_ref[...] = jnp.zeros_like(o_ref)


def _moe_experts(chunk_e, n_used, xs, chunk, w_up, b_up, w_down, b_down):
    n_chunks = xs.shape[0] // (chunk * ROW_TILE)
    rows = pl.BlockSpec((chunk * ROW_TILE, LANES), lambda c, ce, nu: (c, 0))
    grid_spec = pltpu.PrefetchScalarGridSpec(
        num_scalar_prefetch=2,
        grid=(n_chunks,),
        in_specs=[rows,
                  pl.BlockSpec((1, D_MODEL, 2 * D_FF), lambda c, ce, nu: (ce[c], 0, 0)),
                  pl.BlockSpec((1, 1, 2 * D_FF), lambda c, ce, nu: (ce[c], 0, 0)),
                  pl.BlockSpec((1, D_FF, D_MODEL), lambda c, ce, nu: (ce[c], 0, 0)),
                  pl.BlockSpec((1, 1, D_MODEL), lambda c, ce, nu: (ce[c], 0, 0))],
        out_specs=rows,
        scratch_shapes=[pltpu.VMEM((D_MODEL, 2 * D_FF), BF16), pltpu.VMEM((D_FF, D_MODEL), BF16)],
    )
    return pl.pallas_call(
        functools.partial(_moe_kernel, rows=chunk),
        grid_spec=grid_spec,
        out_shape=jax.ShapeDtypeStruct(xs.shape, F32),
        compiler_params=_cparams(("arbitrary",)),
        name="moe_experts",
    )(chunk_e, n_used, xs, w_up, b_up.reshape(N_EXPERTS, 1, -1), w_down, b_down.reshape(N_EXPERTS, 1, -1))


def _combine_kernel(dcur_ref, dnext_ref, x_ref, w_ref, g_ref, ys_ref, o_ref, buf, sem, *, tm, normalize):
    i = pl.program_id(0)
    slot = i % 2
    n_a = tm * TOP_K

    def copy(d_ref, sl, a):
        return pltpu.make_async_copy(ys_ref.at[d_ref[0, 0, a]], buf.at[sl, pl.ds(a * ROW_TILE, ROW_TILE)], sem.at[sl])

    def fetch(d_ref, sl):
        for a in range(n_a):
            copy(d_ref, sl, a).start()

    @pl.when(i == 0)
    def _():
        fetch(dcur_ref, 0)

    @pl.when(i + 1 < pl.num_programs(0))
    def _():
        fetch(dnext_ref, 1 - slot)

    for a in range(n_a):
        copy(dcur_ref, slot, a).wait()
    x = x_ref[...]
    for k in range(TOP_K):
        x = x + w_ref[:, k:k + 1] * _load_rows(buf.at[slot], tm, first=k, every=TOP_K)
    if normalize:
        x = x * lax.rsqrt(jnp.mean(x * x, axis=-1, keepdims=True) + NORM_EPS) * g_ref[...]
    o_ref[...] = x


def _combine(dest, x1, top_w, ys_tiles, g, tm, normalize):
    n = x1.shape[0]
    steps = n // tm
    d2 = dest.reshape(steps, 1, tm * TOP_K)
    row = lambda w: pl.BlockSpec((tm, w), lambda i: (i, 0))
    return pl.pallas_call(
        functools.partial(_combine_kernel, tm=tm, normalize=normalize),
        grid=(steps,),
        in_specs=[pl.BlockSpec((1, 1, tm * TOP_K), lambda i: (i, 0, 0), memory_space=pltpu.SMEM),
                  pl.BlockSpec((1, 1, tm * TOP_K), lambda i: (jnp.minimum(i + 1, steps - 1), 0, 0),
                               memory_space=pltpu.SMEM),
                  row(D_MODEL), row(TOP_K), _full((1, D_MODEL)), pl.BlockSpec(memory_space=pl.ANY)],
        out_specs=row(D_MODEL),
        out_shape=jax.ShapeDtypeStruct((n, D_MODEL), F32),
        scratch_shapes=[pltpu.VMEM((2, tm * TOP_K * ROW_TILE, LANES), F32), pltpu.SemaphoreType.DMA((2,))],
        compiler_params=_cparams(("arbitrary",)),
        name="moe_combine",
    )(d2, d2, x1, top_w, g.reshape(1, -1).astype(F32), ys_tiles)


def _moe_residual(x1, h_tiles, top_e, top_w, w_up, b_up, w_down, b_down, g, normalize):
    n = x1.shape[0]
    n_assign = n * TOP_K
    chunk = MOE_ROWS_MIN
    while chunk * 2 <= min(MOE_ROWS_MAX, n_assign // N_EXPERTS):
        chunk *= 2
    onehot = (top_e.reshape(-1, 1) == jnp.arange(N_EXPERTS, dtype=I32)[None, :]).astype(I32)
    csum = jnp.cumsum(onehot, axis=0)
    counts = csum[-1]
    padded = (counts + chunk - 1) // chunk * chunk
    pad_end = jnp.cumsum(padded)
    dest = jnp.sum(onehot * (csum - 1 + (pad_end - padded)[None, :]), axis=1).astype(I32)
    n_chunks = -(-(n_assign + N_EXPERTS * (chunk - 1)) // chunk)
    slots = n_chunks * chunk
    chunk_e = jnp.sum(pad_end[None, :] <= (jnp.arange(n_chunks) * chunk)[:, None], axis=1)
    chunk_e = jnp.minimum(chunk_e, N_EXPERTS - 1).astype(I32)
    n_used = (pad_end[-1] // chunk).astype(I32).reshape(1)
    tm = _row_tile(n, 256)
    xs = _dispatch(dest, h_tiles.reshape(n, ROW_TILE, LANES), slots, tm)
    ys = _moe_experts(chunk_e, n_used, xs.reshape(slots * ROW_TILE, LANES), chunk, w_up, b_up, w_down, b_down)
    return _combine(dest, x1, top_w, ys.reshape(slots, ROW_TILE, LANES), g, tm, normalize)


def _row_tile(rows, cap):
    tm = cap
    while rows % tm:
        tm //= 2
    return tm


def kernel(x_prompt, x_sample, cache_cmp_kv, cache_sel_kv, state_win_kv, state_rwkv, state_rwkv_shift, page_table,
           norm_attn, w_in, mu_shift, w0, w_decay_up, a0, w_iclr_up, w_gate_up, k_k, k_a, r_k, ln_x_w, ln_x_b,
           pe_cmp_k, pe_cmp_v, w_cmp_k1, w_cmp_k2, w_cmp_v1, w_cmp_v2, rel_bias, w_br_a, w_br_b, w_out,
           norm_ffn, w_router, b_router, w_up, b_up, w_down, b_down, norm_final):
    bp, tp, _ = x_prompt.shape
    bs, ts, _ = x_sample.shape
    depth = w_in.shape[0]
    past_len = page_table.shape[1] * cache_cmp_kv.shape[2]
    n_buf = state_win_kv.shape[2]
    assert ts == 1 and bp * A_HEADS == BH_LANES and bs % 8 == 0 and tp % LANES == 0
    xp = x_prompt.reshape(bp * tp, D_MODEL)
    xs = x_sample.reshape(bs, D_MODEL)
    new = {name: [] for name in ('cmp_p', 'sel_p', 'win_p', 'wkv_p', 'shift_p', 'cmp_s', 'sel_s', 'win_s', 'wkv_s', 'shift_s')}
    kv6 = lambda z, b, t: z.reshape(b, t, 2, B_KV_HEADS, HEAD_DIM)
    for l in range(depth):
        rw = (mu_shift[l], w0[l], w_decay_up[l], a0[l], w_iclr_up[l], w_gate_up[l], k_k[l], k_a[l], r_k[l])
        cw = (pe_cmp_k[l], pe_cmp_v[l], w_cmp_k1[l], w_cmp_k2[l], w_cmp_v1[l], w_cmp_v2[l])
        wa_nat, w_rest, w_t, w_n = _pack_w_in(w_in[l])
        g_attn = norm_attn[l].reshape(1, -1).astype(F32)
        last = l == depth - 1
        g_fin = norm_final if last else jnp.ones((D_MODEL,), F32)

        tq = _row_tile(tp, 256)
        pa_t, q_t, kvt_c, kvt_s, kvt_w, gt_t, kv_c, kv_s, kv_w, pg = _project_t(xp, g_attn, w_t, w_n, tq, tp)
        r, w, k2, v, kk, kb, gate, bonus = _rwkv_prep_t(pa_t, *rw, tq)
        r, w, k2, kk, kb = _key_tiles([z.reshape(N_JP, LANES, tp) for z in (r, w, k2, kk, kb)], LANES)
        v = _val_tiles(v.reshape(HEAD_DIM, BH_LANES, tp), LANES).reshape(1, tp, HEAD_DIM, LANES)
        s0 = jnp.zeros((1, N_IO, N_JP, 8, LANES), F32)
        yn, s_fin = _wkv_scan(r, w, k2, v, kk, kb, s0, _row_tile(tp, 64))
        yn = _val_untile(yn.reshape(tp * HEAD_DIM, LANES), LANES).reshape(HEAD_DIM, bp, A_HEADS, tp)
        ckv = _compress(kv_c, *cw, rows=_row_tile(tp, 2048))
        y_b = _nsa_prompt(q_t, gt_t, ckv, kv_s, kvt_s, kv_w, kvt_w, rel_bias)
        x1, h, top_e, top_w = _merge(xp, yn, bonus, gate, y_b, pg, ln_x_w[l], ln_x_b[l], w_br_a[l], w_br_b[l], w_out[l],
                                     norm_ffn[l], w_router[l], b_router[l], tq, seq=tp)
        xp_next = _moe_residual(x1, h, top_e, top_w, w_up[l], b_up[l], w_down[l], b_down[l], g_fin, last)
        n_win = min(WINDOW, tp)
        kv6t = lambda z: jnp.moveaxis(z.reshape(bp, 2, B_KV_HEADS, HEAD_DIM, tp), -1, 1)
        new['cmp_p'].append(kv6t(kvt_c))
        new['sel_p'].append(kv6t(kvt_s))
        new['win_p'].append(kv6t(kvt_w)[:, tp - n_win:])
        new['wkv_p'].append(_state_from_tiles(s_fin, bp))
        new['shift_p'].append(jnp.zeros((bp, A_PROJ), F32).at[:, A_PERM].set(pa_t[:, :, tp - 1]))

        s_a, q, kv_c, kv_s, kv_w, gt, pg = _project(xs, g_attn, wa_nat, w_rest, _row_tile(bs, 256))
        r, w, k2, v, kk, kb, gate, bonus = _rwkv_prep(s_a, state_rwkv_shift[l], *rw, _row_tile(bs, 256))
        yn, wkv_s = _wkv_step(r, w, k2, v, kk, kb, jnp.transpose(state_rwkv[l].astype(F32), (1, 2, 3, 0)))
        wkv_s = jnp.transpose(wkv_s, (3, 0, 1, 2))
        tok_minor = lambda z: jnp.moveaxis(z, 1, -1).reshape(z.shape[0], KV_ROW, z.shape[1])
        ckv = _compress_paged(tok_minor(cache_cmp_kv[l]), page_table, cw)
        q3, gt3 = q.reshape(bs, 1, B_WIDTH), gt.reshape(bs, 1, GATE_PAD)
        oc3, idx = _nsa_sample_select(q3, gt3, ckv, rel_bias, past_len)
        y_b = _nsa_sample_attend(idx, page_table, q3, gt3, oc3, kv_s.reshape(bs, KV_ROW, 1), kv_w.reshape(bs, 1, KV_ROW),
                                 tok_minor(state_win_kv[l]), tok_minor(cache_sel_kv[l]),
                                 rel_bias, past_len).reshape(bs, B_WIDTH)
        x1, h, top_e, top_w = _merge(xs, yn, bonus, gate, y_b, pg, ln_x_w[l], ln_x_b[l], w_br_a[l], w_br_b[l], w_out[l],
                                     norm_ffn[l], w_router[l], b_router[l], _row_tile(bs, 256))
        xs_next = _moe_residual(x1, h, top_e, top_w, w_up[l], b_up[l], w_down[l], b_down[l], g_fin, last)
        new['cmp_s'].append(kv6(kv_c, bs, 1))
        new['sel_s'].append(kv6(kv_s, bs, 1))
        new['win_s'].append(jnp.concatenate([state_win_kv[l], kv6(kv_w, bs, 1)], axis=1)[:, 1:])
        new['wkv_s'].append(wkv_s.astype(state_rwkv.dtype))
        new['shift_s'].append(s_a)
        xp, xs = xp_next, xs_next
    return (xp.reshape(bp, tp, D_MODEL), xs.reshape(bs, ts, D_MODEL),
            jnp.stack(new['cmp_p']), jnp.stack(new['sel_p']), jnp.stack(new['win_p']),
            jnp.stack(new['wkv_p']), jnp.stack(new['shift_p']),
            jnp.stack(new['cmp_s']), jnp.stack(new['sel_s']), jnp.stack(new['win_s']),
            jnp.stack(new['wkv_s']), jnp.stack(new['shift_s']))
```
